```python
import math
import jax, jax.numpy as jnp
from jax import lax
import numpy as np

D_MODEL = 1024
BATCH = 4
SEQ = 4096
DEPTH = 2
DEC_BATCH = 32
DEC_SEQ = 64
PAST_LEN = 2048

CHUNK = 64
Q_BLOCK = 128
ROPE_THETA = 500000.0
ROPE_FRACTION = 4
EPS = 1e-6
NEG_INF = -1e30

GLA_HEADS = 4
GLA_DK = 64
GLA_DV = 128
GLA_GATE_RANK = 16
GLA_GATE_TAU = 16.0
DIFF_HEADS = 4
DIFF_DH = 64
DIFF_DV = 2 * DIFF_DH
DSA_HEADS = 4
DSA_DH = 128
IDX_HEADS = 4
IDX_DH = 64
DSA_TOPK_MAX = 256
N_EXPERTS = 32
TOP_K = 4
D_FF = 1024
SWIGLU_LIMIT = 7.0
SWIGLU_ALPHA = 1.702
EXPERT_BLOCK = 128

N_BRANCHES = 3
GLA_WIDTH = GLA_HEADS * GLA_DV
DIFF_WIDTH = DIFF_HEADS * DIFF_DV
DSA_WIDTH = DSA_HEADS * DSA_DH

IN_SPLITS = (
    ('gla_q', GLA_HEADS * GLA_DK), ('gla_k', GLA_HEADS * GLA_DK), ('gla_v', GLA_HEADS * GLA_DV),
    ('gla_a', GLA_GATE_RANK), ('gla_r', GLA_HEADS * GLA_DV),
    ('diff_q', DIFF_HEADS * 2 * DIFF_DH), ('diff_k', DIFF_HEADS * 2 * DIFF_DH), ('diff_v', DIFF_HEADS * DIFF_DV),
    ('dsa_q', DSA_HEADS * DSA_DH), ('dsa_k', DSA_HEADS * DSA_DH), ('dsa_v', DSA_HEADS * DSA_DH),
    ('idx_q', IDX_HEADS * IDX_DH), ('idx_k', IDX_DH), ('idx_w', IDX_HEADS),
    ('gates', N_BRANCHES * D_MODEL),
)
IN_WIDTH = sum(s for _, s in IN_SPLITS)

kernel_name = 'hybrid_streaming_gla_diff_dsa_moe_step'


def rmsnorm(x, g):
    xf = x.astype(jnp.float32)
    y = xf * lax.rsqrt(jnp.mean(xf * xf, axis=-1, keepdims=True) + EPS)
    return (y * g.astype(jnp.float32)).astype(x.dtype)


def rope_partial(x, pos):
    d = x.shape[-1]
    rot = d // ROPE_FRACTION
    half = rot // 2
    inv_freq = ROPE_THETA ** (-jnp.arange(half, dtype=jnp.float32) / half)
    ang = pos.astype(jnp.float32)[:, None] * inv_freq[None, :]
    ang = ang.reshape((pos.shape[0],) + (1,) * (x.ndim - 3) + (half,))
    cos, sin = jnp.cos(ang), jnp.sin(ang)
    xr = x[..., :rot].astype(jnp.float32)
    x1, x2 = xr[..., :half], xr[..., half:]
    rotated = jnp.concatenate([x1 * cos - x2 * sin, x1 * sin + x2 * cos], axis=-1).astype(x.dtype)
    return jnp.concatenate([rotated, x[..., rot:]], axis=-1)


def split_cols(z):
    cols = {}
    off = 0
    for name, size in IN_SPLITS:
        cols[name] = z[..., off:off + size]
        off += size
    return cols


def map_query_blocks(fn, q_parts, qpos):
    T = qpos.shape[0]
    qb = Q_BLOCK if T % Q_BLOCK == 0 else T
    nb = T // qb
    parts = tuple(p.reshape((nb, qb) + p.shape[1:]) for p in q_parts)
    out = lax.map(lambda a: fn(*a), parts + (qpos.reshape(nb, qb),))
    return out.reshape((T,) + out.shape[2:])


def gla_scan(q, k, v, log_a, s0):
    B, T, H, dk = q.shape
    dv = v.shape[-1]
    csz = CHUNK if T % CHUNK == 0 else T
    nc = T // csz

    def to_chunks(a):
        return jnp.transpose(a.astype(jnp.float32).reshape(B, nc, csz, H, a.shape[-1]), (1, 0, 3, 2, 4))

    qc = to_chunks(q) * (dk ** -0.5)
    kc, vc, lc = to_chunks(k), to_chunks(v), to_chunks(log_a)
    tri = jnp.tril(jnp.ones((csz, csz), dtype=bool))

    def step(S, blk):
        qb, kb, vb, lb = blk
        b = jnp.cumsum(lb, axis=2)
        rel = jnp.where(tri[:, :, None], b[:, :, :, None, :] - b[:, :, None, :, :], -jnp.inf)
        att = jnp.einsum('bhtd,bhtsd,bhsd->bhts', qb, jnp.exp(rel), kb)
        o = (jnp.einsum('bhtd,bhde->bhte', qb * jnp.exp(b), S)
             + jnp.einsum('bhts,bhse->bhte', att, vb))
        b_end = b[:, :, -1, :]
        S = (jnp.exp(b_end)[..., None] * S
             + jnp.einsum('bhsd,bhse->bhde', kb * jnp.exp(b_end[:, :, None, :] - b), vb))
        return S, o

    S, o = lax.scan(step, s0, (qc, kc, vc, lc))
    o = jnp.transpose(o, (1, 0, 3, 2, 4)).reshape(B, T, H, dv)
    return o, S


def diff_attn_seq(q, k, v, qpos, kpos, lam):
    s = jnp.einsum('qhcd,khcd->hcqk', q, k).astype(jnp.float32) * (DIFF_DH ** -0.5)
    vis = (kpos[None, :] // CHUNK) <= (qpos[:, None] // CHUNK)
    p = jax.nn.softmax(jnp.where(vis, s, NEG_INF), axis=-1)
    w = p[:, 0] - lam * p[:, 1]
    return jnp.einsum('hqk,khe->qhe', w.astype(v.dtype), v)


def dsa_attn_seq(q, iq, iw, qpos, k, v, ik, kpos, n_sel):
    vis = (kpos[None, :] // CHUNK) <= (qpos[:, None] // CHUNK)
    logits = jnp.einsum('qid,kd->qik', iq, ik).astype(jnp.float32) * (IDX_DH ** -0.5)
    score = jnp.einsum('qi,qik->qk', iw.astype(jnp.float32), jax.nn.relu(logits))
    score = jnp.where(vis, score, NEG_INF)
    _, idx = lax.top_k(score, n_sel)
    valid = (kpos[idx] // CHUNK) <= (qpos[:, None] // CHUNK)
    k_sel = k[idx]
    v_sel = v[idx]
    s = jnp.einsum('qhd,qkhd->hqk', q, k_sel).astype(jnp.float32) * (DSA_DH ** -0.5)
    p = jax.nn.softmax(jnp.where(valid[None], s, NEG_INF), axis=-1)
    return jnp.einsum('hqk,qkhd->qhd', p.astype(v.dtype), v_sel)


def token_mixers(n, past, P, l):
    B, T, _ = n.shape
    past_len = 0 if past is None else past[0].shape[1]
    pos = past_len + jnp.arange(T, dtype=jnp.int32)
    kpos = jnp.arange(past_len + T, dtype=jnp.int32)
    cols = split_cols(n @ P['w_in'][l])

    gq = cols['gla_q'].reshape(B, T, GLA_HEADS, GLA_DK)
    gk = cols['gla_k'].reshape(B, T, GLA_HEADS, GLA_DK)
    gv = cols['gla_v'].reshape(B, T, GLA_HEADS, GLA_DV)
    gr = cols['gla_r'].reshape(B, T, GLA_HEADS, GLA_DV)
    log_a = jax.nn.log_sigmoid((cols['gla_a'] @ P['w_gla_a2'][l] + P['b_gla_a2'][l]).astype(jnp.float32)) / GLA_GATE_TAU
    log_a = log_a.reshape(B, T, GLA_HEADS, GLA_DK)
    if past is None:
        s0 = jnp.zeros((B, GLA_HEADS, GLA_DK, GLA_DV), jnp.float32)
    else:
        s0 = past[5].astype(jnp.float32)
    o_gla, s_gla = gla_scan(gq, gk, gv, log_a, s0)
    o_gla = (rmsnorm(o_gla, P['g_gla_out'][l]) * jax.nn.silu(gr.astype(jnp.float32))).astype(n.dtype)
    o_gla = o_gla.reshape(B, T, GLA_WIDTH)

    dq = rope_partial(rmsnorm(cols['diff_q'].reshape(B, T, DIFF_HEADS, 2, DIFF_DH), P['g_diff_q'][l]), pos)
    dk = rope_partial(rmsnorm(cols['diff_k'].reshape(B, T, DIFF_HEADS, 2, DIFF_DH), P['g_diff_k'][l]), pos)
    dv = cols['diff_v'].reshape(B, T, DIFF_HEADS, DIFF_DV)
    dk_all = dk if past is None else jnp.concatenate([past[0], dk], axis=1)
    dv_all = dv if past is None else jnp.concatenate([past[1], dv], axis=1)
    lam_init = 0.8 - 0.6 * math.exp(-0.3 * l)
    lam = (jnp.exp(jnp.sum(P['lambda_q1'][l].astype(jnp.float32) * P['lambda_k1'][l].astype(jnp.float32)))
           - jnp.exp(jnp.sum(P['lambda_q2'][l].astype(jnp.float32) * P['lambda_k2'][l].astype(jnp.float32)))
           + lam_init)
    o_diff = lax.map(
        lambda a: map_query_blocks(lambda qb, pb: diff_attn_seq(qb, a[1], a[2], pb, kpos, lam), (a[0],), pos),
        (dq, dk_all, dv_all))
    o_diff = (rmsnorm(o_diff, P['g_diff_out'][l]) * (1.0 - lam_init)).reshape(B, T, DIFF_WIDTH)

    cq = rope_partial(rmsnorm(cols['dsa_q'].reshape(B, T, DSA_HEADS, DSA_DH), P['g_dsa_q'][l]), pos)
    ck = rope_partial(rmsnorm(cols['dsa_k'].reshape(B, T, DSA_HEADS, DSA_DH), P['g_dsa_k'][l]), pos)
    cv = cols['dsa_v'].reshape(B, T, DSA_HEADS, DSA_DH)
    iq = rope_partial(cols['idx_q'].reshape(B, T, IDX_HEADS, IDX_DH), pos)
    ik = rope_partial(rmsnorm(cols['idx_k'], P['g_idx_k'][l]), pos)
    iw = cols['idx_w'] * (IDX_HEADS ** -0.5)
    ck_all = ck if past is None else jnp.concatenate([past[2], ck], axis=1)
    cv_all = cv if past is None else jnp.concatenate([past[3], cv], axis=1)
    ik_all = ik if past is None else jnp.concatenate([past[4], ik], axis=1)
    n_sel = min(DSA_TOPK_MAX, (past_len + T) // 4)
    o_dsa = lax.map(
        lambda a: map_query_blocks(
            lambda qb, iqb, iwb, pb: dsa_attn_seq(qb, iqb, iwb, pb, a[3], a[4], a[5], kpos, n_sel),
            (a[0], a[1], a[2]), pos),
        (cq, iq, iw, ck_all, cv_all, ik_all))
    o_dsa = o_dsa.reshape(B, T, DSA_WIDTH)

    gates = jax.nn.sigmoid(cols['gates'].astype(jnp.float32)).reshape(B, T, N_BRANCHES, D_MODEL).astype(n.dtype)
    merged = (gates[:, :, 0] * (o_gla @ P['w_branch_gla'][l])
              + gates[:, :, 1] * (o_diff @ P['w_branch_diff'][l])
              + gates[:, :, 2] * (o_dsa @ P['w_branch_dsa'][l]))
    return merged @ P['w_out'][l], (dk, dv, ck, cv, ik, s_gla.astype(n.dtype))


def expert_ffn(x, w1, b1, w2, b2):
    h = x @ w1 + b1
    g = jnp.minimum(h[:, 0::2], SWIGLU_LIMIT)
    u = jnp.clip(h[:, 1::2], -SWIGLU_LIMIT, SWIGLU_LIMIT)
    return (g * jax.nn.sigmoid(SWIGLU_ALPHA * g) * (u + 1.0)) @ w2 + b2


def moe(n, w_router, b_router, w1, b1, w2, b2):
    B, T, D = n.shape
    N = B * T
    xt = n.reshape(N, D)
    logits = (xt @ w_router + b_router).astype(jnp.float32)
    top_val, top_idx = lax.top_k(logits, TOP_K)
    top_w = jax.nn.softmax(top_val, axis=-1)
    NK = N * TOP_K
    flat_e = top_idx.reshape(NK).astype(jnp.int32)
    flat_tok = jnp.arange(NK, dtype=jnp.int32) // TOP_K
    flat_w = top_w.reshape(NK)
    order = jnp.argsort(flat_e)
    sorted_e = flat_e[order]
    counts = jnp.zeros((N_EXPERTS,), jnp.int32).at[flat_e].add(1)
    padded = (counts + EXPERT_BLOCK - 1) // EXPERT_BLOCK * EXPERT_BLOCK
    start = jnp.cumsum(counts) - counts
    pad_end = jnp.cumsum(padded)
    pad_start = pad_end - padded
    dest = pad_start[sorted_e] + jnp.arange(NK, dtype=jnp.int32) - start[sorted_e]
    n_rows = (-(-NK // EXPERT_BLOCK)) * EXPERT_BLOCK + N_EXPERTS * EXPERT_BLOCK
    n_blocks = n_rows // EXPERT_BLOCK
    row_tok = jnp.zeros((n_rows,), jnp.int32).at[dest].set(flat_tok[order])
    row_w = jnp.zeros((n_rows,), jnp.float32).at[dest].set(flat_w[order])
    block_e = jnp.minimum(
        jnp.searchsorted(pad_end, jnp.arange(n_blocks, dtype=jnp.int32) * EXPERT_BLOCK, side='right'),
        N_EXPERTS - 1)
    xs = xt[row_tok].reshape(n_blocks, EXPERT_BLOCK, D)
    ys = lax.map(lambda a: expert_ffn(a[0], w1[a[1]], b1[a[1]], w2[a[1]], b2[a[1]]), (xs, block_e))
    y = jax.ops.segment_sum(ys.reshape(n_rows, D).astype(jnp.float32) * row_w[:, None], row_tok, num_segments=N)
    return y.astype(n.dtype).reshape(B, T, D)


def run_trunk(x, c, past, P):
    B = c.shape[0]
    per_layer = []
    for l in range(DEPTH):
        m = (jax.nn.silu(c) @ P['w_ada'][l] + P['b_ada'][l]).reshape(B, 6, 1, D_MODEL)
        past_l = None if past is None else tuple(a[l] for a in past)
        h = rmsnorm(x, P['g_norm1'][l]) * (1.0 + m[:, 1]) + m[:, 0]
        mix, st = token_mixers(h, past_l, P, l)
        x = x + m[:, 2] * mix
        h = rmsnorm(x, P['g_norm2'][l]) * (1.0 + m[:, 4]) + m[:, 3]
        x = x + m[:, 5] * moe(h, P['w_router'][l], P['b_router'][l], P['w_mlp1'][l], P['b_mlp1'][l],
                              P['w_mlp2'][l], P['b_mlp2'][l])
        per_layer.append(st)
    stacked = tuple(jnp.stack([st[i] for st in per_layer]) for i in range(6))
    return x, stacked


def setup_inputs(seed: int = 0) -> dict:
    key = jax.random.key(seed)
    it = iter(jax.random.split(key, 48))

    def nrm(shape, scale):
        return jax.random.normal(next(it), shape, jnp.float32) * scale

    def gain(shape):
        return 1.0 + 0.05 * jax.random.normal(next(it), shape, jnp.float32)

    D = D_MODEL
    return {
        'x_prompt': nrm((BATCH, SEQ, D), 1.0),
        'x_sample': nrm((DEC_BATCH, DEC_SEQ, D), 1.0),
        'cache_diff_k': nrm((DEPTH, DEC_BATCH, PAST_LEN, DIFF_HEADS, 2, DIFF_DH), 1.0),
        'cache_diff_v': nrm((DEPTH, DEC_BATCH, PAST_LEN, DIFF_HEADS, DIFF_DV), 1.0),
        'cache_dsa_k': nrm((DEPTH, DEC_BATCH, PAST_LEN, DSA_HEADS, DSA_DH), 1.0),
        'cache_dsa_v': nrm((DEPTH, DEC_BATCH, PAST_LEN, DSA_HEADS, DSA_DH), 1.0),
        'cache_dsa_idx_k': nrm((DEPTH, DEC_BATCH, PAST_LEN, IDX_DH), 1.0),
        'state_gla': nrm((DEPTH, DEC_BATCH, GLA_HEADS, GLA_DK, GLA_DV), 1.0),
        'c_prompt': nrm((BATCH, D), 1.0),
        'c_sample': nrm((DEC_BATCH, D), 1.0),
        'w_ada': nrm((DEPTH, D, 6 * D), 0.5 * D ** -0.5),
        'b_ada': nrm((DEPTH, 6 * D), 0.02),
        'g_norm1': gain((DEPTH, D)),
        'g_norm2': gain((DEPTH, D)),
        'w_in': nrm((DEPTH, D, IN_WIDTH), D ** -0.5),
        'w_gla_a2': nrm((DEPTH, GLA_GATE_RANK, GLA_HEADS * GLA_DK), GLA_GATE_RANK ** -0.5),
        'b_gla_a2': nrm((DEPTH, GLA_HEADS * GLA_DK), 0.1),
        'g_gla_out': gain((DEPTH, GLA_DV)),
        'g_diff_q': gain((DEPTH, DIFF_DH)),
        'g_diff_k': gain((DEPTH, DIFF_DH)),
        'lambda_q1': nrm((DEPTH, DIFF_DH), 0.1),
        'lambda_k1': nrm((DEPTH, DIFF_DH), 0.1),
        'lambda_q2': nrm((DEPTH, DIFF_DH), 0.1),
        'lambda_k2': nrm((DEPTH, DIFF_DH), 0.1),
        'g_diff_out': gain((DEPTH, DIFF_DV)),
        'g_dsa_q': gain((DEPTH, DSA_DH)),
        'g_dsa_k': gain((DEPTH, DSA_DH)),
        'g_idx_k': gain((DEPTH, IDX_DH)),
        'w_branch_gla': nrm((DEPTH, GLA_WIDTH, D), GLA_WIDTH ** -0.5),
        'w_branch_diff': nrm((DEPTH, DIFF_WIDTH, D), DIFF_WIDTH ** -0.5),
        'w_branch_dsa': nrm((DEPTH, DSA_WIDTH, D), DSA_WIDTH ** -0.5),
        'w_out': nrm((DEPTH, D, D), D ** -0.5),
        'w_router': nrm((DEPTH, D, N_EXPERTS), D ** -0.5),
        'b_router': nrm((DEPTH, N_EXPERTS), 0.01),
        'w_mlp1': nrm((DEPTH, N_EXPERTS, D, 2 * D_FF), D ** -0.5),
        'b_mlp1': nrm((DEPTH, N_EXPERTS, 2 * D_FF), 0.02),
        'w_mlp2': nrm((DEPTH, N_EXPERTS, D_FF, D), D_FF ** -0.5),
        'b_mlp2': nrm((DEPTH, N_EXPERTS, D), 0.02),
    }


def reference(x_prompt, x_sample, cache_diff_k, cache_diff_v, cache_dsa_k, cache_dsa_v, cache_dsa_idx_k,
              state_gla, c_prompt, c_sample, w_ada, b_ada, g_norm1, g_norm2, w_in, w_gla_a2, b_gla_a2,
              g_gla_out, g_diff_q, g_diff_k, lambda_q1, lambda_k1, lambda_q2, lambda_k2, g_diff_out,
              g_dsa_q, g_dsa_k, g_idx_k, w_branch_gla, w_branch_diff, w_branch_dsa, w_out, w_router,
              b_router, w_mlp1, b_mlp1, w_mlp2, b_mlp2):
    P = dict(w_ada=w_ada, b_ada=b_ada, g_norm1=g_norm1, g_norm2=g_norm2, w_in=w_in, w_gla_a2=w_gla_a2,
             b_gla_a2=b_gla_a2, g_gla_out=g_gla_out, g_diff_q=g_diff_q, g_diff_k=g_diff_k,
             lambda_q1=lambda_q1, lambda_k1=lambda_k1, lambda_q2=lambda_q2, lambda_k2=lambda_k2,
             g_diff_out=g_diff_out, g_dsa_q=g_dsa_q, g_dsa_k=g_dsa_k, g_idx_k=g_idx_k,
             w_branch_gla=w_branch_gla, w_branch_diff=w_branch_diff, w_branch_dsa=w_branch_dsa,
             w_out=w_out, w_router=w_router, b_router=b_router, w_mlp1=w_mlp1, b_mlp1=b_mlp1,
             w_mlp2=w_mlp2, b_mlp2=b_mlp2)
    y_prompt, new_p = run_trunk(x_prompt, c_prompt, None, P)
    y_sample, new_s = run_trunk(
        x_sample, c_sample,
        (cache_diff_k, cache_diff_v, cache_dsa_k, cache_dsa_v, cache_dsa_idx_k, state_gla), P)
    p_diff_k, p_diff_v, p_dsa_k, p_dsa_v, p_idx_k, p_gla = new_p
    s_diff_k, s_diff_v, s_dsa_k, s_dsa_v, s_idx_k, s_gla = new_s
    return (y_prompt, y_sample, p_diff_k, p_diff_v, p_dsa_k, p_dsa_v, p_idx_k, p_gla,
            s_diff_k, s_diff_v, s_dsa_k, s_dsa_v, s_idx_k, s_gla)
```

```python
import functools
import math

import jax
import jax.numpy as jnp
from jax import lax
from jax.experimental import pallas as pl
from jax.experimental.pallas import tpu as pltpu

F32 = jnp.float32
BF16 = jnp.bfloat16
I32 = jnp.int32

D_MODEL = 1024
CHUNK = 64
ROPE_THETA = 500000.0
ROPE_FRACTION = 4
EPS = 1e-6
NEG_INF = -1e30

GLA_HEADS = 4
GLA_DK = 64
GLA_DV = 128
GLA_GATE_RANK = 16
GLA_GATE_TAU = 16.0
GLA_SUB = 16
DIFF_HEADS = 4
DIFF_DH = 64
DIFF_DV = 128
DSA_HEADS = 4
DSA_DH = 128
IDX_HEADS = 4
IDX_DH = 64
DSA_TOPK_MAX = 256
N_EXPERTS = 32
TOP_K = 4
D_FF = 1024
SWIGLU_LIMIT = 7.0
SWIGLU_ALPHA = 1.702

LANES = 128
ROW_TILE = 8
VMEM_LIMIT = 48 * 1024 * 1024

Z_GQK, Z_GV, Z_GR = 0, 512, 1024
Z_DQ, Z_DK, Z_DV = 1536, 2048, 2560
Z_CQ, Z_CK, Z_CV = 3072, 3584, 4096
Z_MISC = 4608
Z_GATES = 5120
Z_WIDTH = 8192
MISC_IK = 256
MISC_GA = 384

_SRC = {}
_off = 0
for _name, _size in (
        ('gla_q', 256), ('gla_k', 256), ('gla_v', 512), ('gla_a', 16), ('gla_r', 512),
        ('diff_q', 512), ('diff_k', 512), ('diff_v', 512),
        ('dsa_q', 512), ('dsa_k', 512), ('dsa_v', 512),
        ('idx_q', 256), ('idx_k', 64), ('idx_w', 4), ('gates', 3072)):
    _SRC[_name] = (_off, _size)
    _off += _size


def _cparams(sem, vmem=VMEM_LIMIT):
    return pltpu.CompilerParams(dimension_semantics=sem, vmem_limit_bytes=vmem)


def _dot(a, b):
    return jnp.dot(a, b, preferred_element_type=F32)


def _dot_nt(a, b):
    return lax.dot_general(a, b, (((1,), (1,)), ((), ())), preferred_element_type=F32)


def _dot_tn(a, b):
    return lax.dot_general(a, b, (((0,), (0,)), ((), ())), preferred_element_type=F32)


def _silu(x):
    return x * jax.nn.sigmoid(x)


def _row_tiling(B, T, target):
    if T >= target:
        assert T % target == 0
        return target, 1
    nb = 1
    for cand in range(1, B + 1):
        if B % cand == 0 and cand * T <= target:
            nb = cand
    return nb * T, nb


def _mod_rows(m, lo, tm, nb):
    return m[:, lo:lo + 1, :]


def _modulate(xn, m, shift_i, scale_i, nb):
    tm = xn.shape[0]
    if nb == 1:
        return xn * (1.0 + m[0, scale_i:scale_i + 1, :]) + m[0, shift_i:shift_i + 1, :]
    x3 = xn.reshape(nb, tm // nb, D_MODEL)
    h = x3 * (1.0 + m[:, scale_i:scale_i + 1, :]) + m[:, shift_i:shift_i + 1, :]
    return h.reshape(tm, D_MODEL)


def _gate_rows(y, m, gate_i, nb):
    tm = y.shape[0]
    if nb == 1:
        return y * m[0, gate_i:gate_i + 1, :]
    return (y.reshape(nb, tm // nb, D_MODEL) * m[:, gate_i:gate_i + 1, :]).reshape(tm, D_MODEL)


def _ada_kernel(c_ref, w_ref, b_ref, o_ref):
    s = _silu(c_ref[...])
    o_ref[...] = _dot(s.astype(BF16), w_ref[...].astype(BF16)) + b_ref[...]


def _ada(c, w, b):
    B = c.shape[0]
    out = pl.pallas_call(
        _ada_kernel,
        out_shape=jax.ShapeDtypeStruct((B, 6 * D_MODEL), F32),
        grid=(6,),
        in_specs=[pl.BlockSpec((B, D_MODEL), lambda j: (0, 0)),
                  pl.BlockSpec((D_MODEL, D_MODEL), lambda j: (0, j)),
                  pl.BlockSpec((1, D_MODEL), lambda j: (0, j))],
        out_specs=pl.BlockSpec((B, D_MODEL), lambda j: (0, j)),
        compiler_params=_cparams(("arbitrary",)),
        name="ada",
    )(c, w, b.reshape(1, -1))
    return out.reshape(B, 6, D_MODEL)


def _inproj_kernel(x_ref, m_ref, g_ref, w_ref, o_ref, h_scr, *, nb):
    @pl.when(pl.program_id(1) == 0)
    def _():
        x = x_ref[...]
        xn = x * lax.rsqrt(jnp.mean(x * x, axis=-1, keepdims=True) + EPS) * g_ref[...]
        h_scr[...] = _modulate(xn, m_ref[...], 0, 1, nb).astype(BF16)

    o_ref[...] = _dot(h_scr[...], w_ref[...])


def _inproj(x2, m, g, w, B, T):
    M = B * T
    tm, nb = _row_tiling(B, T, 1024)
    tn = 1024
    return pl.pallas_call(
        functools.partial(_inproj_kernel, nb=nb),
        out_shape=jax.ShapeDtypeStruct((M, Z_WIDTH), F32),
        grid=(M // tm, Z_WIDTH // tn),
        in_specs=[pl.BlockSpec((tm, D_MODEL), lambda i, j: (i, 0)),
                  pl.BlockSpec((nb, 6, D_MODEL), lambda i, j: ((i * tm) // (T * nb), 0, 0)),
                  pl.BlockSpec((1, D_MODEL), lambda i, j: (0, 0)),
                  pl.BlockSpec((D_MODEL, tn), lambda i, j: (0, j))],
        out_specs=pl.BlockSpec((tm, tn), lambda i, j: (i, j)),
        scratch_shapes=[pltpu.VMEM((tm, D_MODEL), BF16)],
        compiler_params=_cparams(("arbitrary", "arbitrary")),
        name="inproj",
    )(x2, m, g.reshape(1, -1), w)


def _rope_tables(pos, d):
    rot = d // ROPE_FRACTION
    half = rot // 2
    T = pos.shape[0]
    inv_freq = ROPE_THETA ** (-jnp.arange(half, dtype=F32) / half)
    ang = pos.astype(F32)[:, None] * inv_freq[None, :]
    cos, sin = jnp.cos(ang), jnp.sin(ang)
    c = jnp.concatenate([cos, cos, jnp.ones((T, d - rot), F32)], axis=1)
    a = jnp.concatenate([-sin, jnp.zeros((T, d - half), F32)], axis=1)
    b = jnp.concatenate([jnp.zeros((T, half), F32), sin, jnp.zeros((T, d - rot), F32)], axis=1)
    reps = LANES // d
    return tuple(jnp.tile(t, (1, reps)) for t in (c, a, b))


def _rope128(xs, tabs, half):
    c, a, b = tabs
    return xs * c + pltpu.roll(xs, LANES - half, 1) * a + pltpu.roll(xs, half, 1) * b


def _norm_rope(x, g, tabs, d, norm=True):
    tm, W = x.shape
    half = d // ROPE_FRACTION // 2
    lo = lax.broadcasted_iota(I32, (tm, LANES), 1) < 64
    outs = []
    for s in range(W // LANES):
        xs = x[:, s * LANES:(s + 1) * LANES]
        if norm:
            sq = xs * xs
            if d == LANES:
                r = lax.rsqrt(jnp.sum(sq, axis=1, keepdims=True) * (1.0 / d) + EPS)
            else:
                s_lo = jnp.sum(jnp.where(lo, sq, 0.0), axis=1, keepdims=True)
                s_hi = jnp.sum(jnp.where(lo, 0.0, sq), axis=1, keepdims=True)
                r = jnp.where(lo, lax.rsqrt(s_lo * (1.0 / d) + EPS), lax.rsqrt(s_hi * (1.0 / d) + EPS))
            xs = xs * r * g[:, s * LANES:(s + 1) * LANES]
        outs.append(_rope128(xs, tabs, half))
    return outs[0] if len(outs) == 1 else jnp.concatenate(outs, axis=1)


def _post_kernel(dq_ref, dk_ref, dv_ref, cq_ref, ck_ref, cv_ref, mi_ref,
                 c64_ref, a64_ref, b64_ref, c128_ref, a128_ref, b128_ref,
                 gdq_ref, gdk_ref, gcq_ref, gck_ref, gik_ref,
                 dq_o, dk32_o, dkb_o, dvb_o, cq_o, ck32_o, ckb_o, cvb_o, iq_o, ik32_o, ikr_o, iw_o):
    t64 = (c64_ref[...], a64_ref[...], b64_ref[...])
    t128 = (c128_ref[...], a128_ref[...], b128_ref[...])
    dq = _norm_rope(dq_ref[...], gdq_ref[...], t64, 64)
    dq_o[...] = (dq * (DIFF_DH ** -0.5)).astype(BF16)
    dk = _norm_rope(dk_ref[...], gdk_ref[...], t64, 64)
    dk32_o[...] = dk
    dkb_o[...] = dk.astype(BF16)
    dvb_o[...] = dv_ref[...].astype(BF16)
    cq = _norm_rope(cq_ref[...], gcq_ref[...], t128, 128)
    cq_o[...] = (cq * (DSA_DH ** -0.5)).astype(BF16)
    ck = _norm_rope(ck_ref[...], gck_ref[...], t128, 128)
    ck32_o[...] = ck
    ckb_o[...] = ck.astype(BF16)
    cvb_o[...] = cv_ref[...].astype(BF16)
    mi = mi_ref[...]
    iq = _norm_rope(mi[:, 0:256], None, t64, 64, norm=False)
    iq_o[...] = (iq * (IDX_DH ** -0.5)).astype(BF16)
    mk = mi[:, MISC_IK:MISC_IK + LANES]
    tm = mk.shape[0]
    lane = lax.broadcasted_iota(I32, (tm, LANES), 1)
    lo = lane < 64
    ssq = jnp.sum(jnp.where(lo, mk * mk, 0.0), axis=1, keepdims=True)
    ikn = mk * lax.rsqrt(ssq * (1.0 / IDX_DH) + EPS) * gik_ref[...]
    ik = _rope128(ikn, t64, IDX_DH // ROPE_FRACTION // 2)
    ik32_o[...] = ik[:, 0:IDX_DH]
    ik2 = jnp.where(lo, ik, pltpu.roll(ik, 64, 1))
    ikr_o[...] = jnp.concatenate([ik2, ik2], axis=1).astype(BF16)
    iw_o[...] = jnp.where(lane < IDX_HEADS, pltpu.roll(mk, 64, 1) * (IDX_HEADS ** -0.5), 0.0)


def _post(z, tabs64, tabs128, gains, B, T):
    M = B * T
    tm = min(T, 512)
    assert T % tm == 0
    npos = T // tm

    def zspec(off):
        return pl.BlockSpec((tm, 512), lambda i: (i, off // 512))

    tab_spec = pl.BlockSpec((tm, LANES), lambda i: (i % npos, 0))

    def gspec(w):
        return pl.BlockSpec((1, w), lambda i: (0, 0))

    def ospec(w):
        return pl.BlockSpec((tm, w), lambda i: (i, 0))

    outs = [(512, BF16), (512, F32), (512, BF16), (512, BF16),
            (512, BF16), (512, F32), (512, BF16), (512, BF16),
            (256, BF16), (IDX_DH, F32), (256, BF16), (LANES, F32)]
    return pl.pallas_call(
        _post_kernel,
        out_shape=[jax.ShapeDtypeStruct((M, w), dt) for w, dt in outs],
        grid=(M // tm,),
        in_specs=[zspec(Z_DQ), zspec(Z_DK), zspec(Z_DV), zspec(Z_CQ), zspec(Z_CK), zspec(Z_CV), zspec(Z_MISC)]
                 + [tab_spec] * 6 + [gspec(512)] * 4 + [gspec(LANES)],
        out_specs=[ospec(w) for w, _ in outs],
        compiler_params=_cparams(("arbitrary",)),
        name="post",
    )(z, z, z, z, z, z, z, *tabs64, *tabs128, *gains)


def _split3(x):
    hi = x.astype(BF16)
    r = x - hi.astype(F32)
    mid = r.astype(BF16)
    lo = (r - mid.astype(F32)).astype(BF16)
    return hi, mid, lo


def _gla_kernel(qk_ref, v_ref, r_ref, ga_ref, wa_ref, ba_ref, g_ref, s0_ref, o_ref, st_ref, st_scr, *, nct):
    ti = pl.program_id(1)

    @pl.when(ti == 0)
    def _():
        st_scr[...] = s0_ref[...]

    C = CHUNK
    W = GLA_HEADS * GLA_DK
    row = lax.broadcasted_iota(I32, (C, C), 0)
    col = lax.broadcasted_iota(I32, (C, C), 1)
    tri = col <= row
    tri_bf = tri.astype(BF16)
    lane = lax.broadcasted_iota(I32, (1, W), 1)
    hmask = [(lane // GLA_DK) == h for h in range(GLA_HEADS)]
    rowi = lax.broadcasted_iota(I32, (C, W), 0)
    wa = wa_ref[...]
    ba = ba_ref[...]
    g = g_ref[...]

    def chunk(c, carry):
        rows = pl.ds(pl.multiple_of(c * C, C), C)
        qk = qk_ref[rows, :]
        q = qk[:, :W] * (GLA_DK ** -0.5)
        k = qk[:, W:]
        v = v_ref[rows, :].astype(BF16)
        pre = _dot(ga_ref[rows, :].astype(BF16), wa) + ba
        la = (jnp.minimum(pre, 0.0) - jnp.log(1.0 + jnp.exp(-jnp.abs(pre)))) * (1.0 / GLA_GATE_TAU)
        hi, mid, lo = _split3(la)
        b = _dot(tri_bf, hi) + _dot(tri_bf, mid) + _dot(tri_bf, lo)
        st = st_scr[...]
        st_bf = st.astype(BF16)
        qe = q * jnp.exp(b)
        b_end = b[C - 1:C, :]
        kend = k * jnp.exp(b_end - b)
        att_parts = [[] for _ in range(GLA_HEADS)]
        for s in range(C // GLA_SUB):
            r0 = s * GLA_SUB
            br = b[r0:r0 + 1, :]
            qs = q[r0:r0 + GLA_SUB, :] * jnp.exp(b[r0:r0 + GLA_SUB, :] - br)
            ks = k * jnp.exp(br - b)
            if r0 + GLA_SUB < C:
                ks = jnp.where(rowi < r0 + GLA_SUB, ks, 0.0)
            ks = ks.astype(BF16)
            for h in range(GLA_HEADS):
                att_parts[h].append(_dot_nt(jnp.where(hmask[h], qs, 0.0).astype(BF16), ks))
        upd = None
        for h in range(GLA_HEADS):
            hs = slice(h * GLA_DV, (h + 1) * GLA_DV)
            att = jnp.where(tri, jnp.concatenate(att_parts[h], axis=0), 0.0)
            o = _dot(att.astype(BF16), v[:, hs]) + _dot_nt(jnp.where(hmask[h], qe, 0.0).astype(BF16), st_bf)
            u = _dot_tn(v[:, hs], jnp.where(hmask[h], kend, 0.0).astype(BF16))
            upd = u if upd is None else upd + u
            y = o * lax.rsqrt(jnp.mean(o * o, axis=-1, keepdims=True) + EPS) * g[:, hs]
            o_ref[rows, hs] = (y * _silu(r_ref[rows, hs])).astype(BF16)
        st_scr[...] = st * jnp.exp(b_end) + upd
        return carry

    lax.fori_loop(0, nct, chunk, 0)

    @pl.when(ti == pl.num_programs(1) - 1)
    def _():
        st_ref[...] = st_scr[...]


def _gla(z, wa, ba, g, s0t, B, T):
    M = B * T
    tt = min(T, 512)
    assert T % tt == 0 and tt % CHUNK == 0
    nt = T // tt
    W = GLA_HEADS * GLA_DK
    return pl.pallas_call(
        functools.partial(_gla_kernel, nct=tt // CHUNK),
        out_shape=[jax.ShapeDtypeStruct((M, 512), BF16),
                   jax.ShapeDtypeStruct((B, GLA_DV, W), F32)],
        grid=(B, nt),
        in_specs=[pl.BlockSpec((tt, 512), lambda b, t: (b * nt + t, Z_GQK // 512)),
                  pl.BlockSpec((tt, 512), lambda b, t: (b * nt + t, Z_GV // 512)),
                  pl.BlockSpec((tt, 512), lambda b, t: (b * nt + t, Z_GR // 512)),
                  pl.BlockSpec((tt, LANES), lambda b, t: (b * nt + t, (Z_MISC + MISC_GA) // LANES)),
                  pl.BlockSpec((LANES, W), lambda b, t: (0, 0)),
                  pl.BlockSpec((1, W), lambda b, t: (0, 0)),
                  pl.BlockSpec((1, 512), lambda b, t: (0, 0)),
                  pl.BlockSpec((None, GLA_DV, W), lambda b, t: (b, 0, 0))],
        out_specs=[pl.BlockSpec((tt, 512), lambda b, t: (b * nt + t, 0)),
                   pl.BlockSpec((None, GLA_DV, W), lambda b, t: (b, 0, 0))],
        scratch_shapes=[pltpu.VMEM((GLA_DV, W), F32)],
        compiler_params=_cparams(("arbitrary", "arbitrary")),
        name="gla",
    )(z, z, z, z, wa, ba, g, s0t)


def _diff_lambda(lam_ref, lam_init):
    lv = lam_ref[...]
    a = jnp.sum(lv[0:1, :] * lv[1:2, :], axis=1, keepdims=True)
    b = jnp.sum(lv[2:3, :] * lv[3:4, :], axis=1, keepdims=True)
    return jnp.exp(a) - jnp.exp(b) + lam_init


def _diff_finish(o0, o1, lam, g, lam_init):
    o = o0 - lam * o1
    return o * lax.rsqrt(jnp.mean(o * o, axis=-1, keepdims=True) + EPS) * g * (1.0 - lam_init)


def _diff_prompt_kernel(qi_ref, kj_ref, q_ref, k_ref, v_ref, lam_ref, g_ref, o_ref,
                        m_scr, l_scr, acc_scr, *, lam_init, tq):
    n = pl.program_id(2)
    qi = qi_ref[n]
    kj = kj_ref[n]

    @pl.when(kj == 0)
    def _():
        m_scr[...] = jnp.full(m_scr.shape, NEG_INF, F32)
        l_scr[...] = jnp.zeros(l_scr.shape, F32)
        acc_scr[...] = jnp.zeros(acc_scr.shape, F32)

    q = q_ref[...]
    k = k_ref[...]
    v = v_ref[...]
    lane = lax.broadcasted_iota(I32, (1, LANES), 1)
    qpos = qi * tq + lax.broadcasted_iota(I32, (tq, tq), 0)
    kpos = kj * tq + lax.broadcasted_iota(I32, (tq, tq), 1)
    vis = (kpos // CHUNK) <= (qpos // CHUNK)
    for c in range(2):
        qc = jnp.where((lane < DIFF_DH) == (c == 0), q, jnp.zeros_like(q))
        s = jnp.where(vis, _dot_nt(qc, k), NEG_INF)
        m_prev = m_scr[c]
        m_new = jnp.maximum(m_prev, jnp.max(s, axis=1, keepdims=True))
        alpha = jnp.exp(m_prev - m_new)
        p = jnp.exp(s - m_new)
        l_scr[c] = alpha * l_scr[c] + jnp.sum(p, axis=1, keepdims=True)
        acc_scr[c] = alpha * acc_scr[c] + _dot(p.astype(BF16), v)
        m_scr[c] = m_new

    @pl.when(kj == qi)
    def _():
        lam = _diff_lambda(lam_ref, lam_init)
        o_ref[...] = _diff_finish(acc_scr[0] / l_scr[0], acc_scr[1] / l_scr[1], lam, g_ref[...],
                                  lam_init).astype(BF16)


def _diff_prompt(dq, dk, dv, lamv, g, B, T, lam_init):
    M = B * T
    tq = min(T, 512)
    nq = T // tq
    pairs = [(i, j) for i in range(nq) for j in range(i + 1)]
    qi = jnp.asarray([p[0] for p in pairs], I32)
    kj = jnp.asarray([p[1] for p in pairs], I32)
    grid_spec = pltpu.PrefetchScalarGridSpec(
        num_scalar_prefetch=2,
        grid=(B, DIFF_HEADS, len(pairs)),
        in_specs=[pl.BlockSpec((tq, LANES), lambda b, h, n, qi, kj: (b * nq + qi[n], h)),
                  pl.BlockSpec((tq, LANES), lambda b, h, n, qi, kj: (b * nq + kj[n], h)),
                  pl.BlockSpec((tq, LANES), lambda b, h, n, qi, kj: (b * nq + kj[n], h)),
                  pl.BlockSpec((4, DIFF_DH), lambda b, h, n, qi, kj: (0, 0)),
                  pl.BlockSpec((1, LANES), lambda b, h, n, qi, kj: (0, 0))],
        out_specs=pl.BlockSpec((tq, LANES), lambda b, h, n, qi, kj: (b * nq + qi[n], h)),
        scratch_shapes=[pltpu.VMEM((2, tq, 1), F32), pltpu.VMEM((2, tq, 1), F32),
                        pltpu.VMEM((2, tq, LANES), F32)])
    return pl.pallas_call(
        functools.partial(_diff_prompt_kernel, lam_init=lam_init, tq=tq),
        out_shape=jax.ShapeDtypeStruct((M, 512), BF16),
        grid_spec=grid_spec,
        compiler_params=_cparams(("arbitrary", "arbitrary", "arbitrary")),
        name="diff_prompt",
    )(qi, kj, dq, dk, dv, lamv, g)


def _diff_sample_kernel(q_ref, kn_ref, vn_ref, kp_ref, vp_ref, lam_ref, g_ref, o_ref, *, lam_init, past_len):
    T = q_ref.shape[0]
    lam = _diff_lambda(lam_ref, lam_init)
    lane = lax.broadcasted_iota(I32, (1, LANES), 1)
    qpos = past_len + lax.broadcasted_iota(I32, (T, T), 0)
    kpos = past_len + lax.broadcasted_iota(I32, (T, T), 1)
    vis_new = (kpos // CHUNK) <= (qpos // CHUNK)
    for h in range(DIFF_HEADS):
        hs = slice(h * LANES, (h + 1) * LANES)
        q = q_ref[:, hs]
        kp = kp_ref[:, hs].astype(BF16)
        vp = vp_ref[:, hs].astype(BF16)
        kn = kn_ref[:, hs]
        vn = vn_ref[:, hs]
        outs = []
        for c in range(2):
            qc = jnp.where((lane < DIFF_DH) == (c == 0), q, jnp.zeros_like(q))
            sp = _dot_nt(qc, kp)
            sn = jnp.where(vis_new, _dot_nt(qc, kn), NEG_INF)
            m = jnp.maximum(jnp.max(sp, axis=1, keepdims=True), jnp.max(sn, axis=1, keepdims=True))
            pp = jnp.exp(sp - m)
            pn = jnp.exp(sn - m)
            l = jnp.sum(pp, axis=1, keepdims=True) + jnp.sum(pn, axis=1, keepdims=True)
            outs.append((_dot(pp.astype(BF16), vp) + _dot(pn.astype(BF16), vn)) / l)
        o_ref[:, hs] = _diff_finish(outs[0], outs[1], lam, g_ref[...], lam_init).astype(BF16)


def _diff_sample(dq, dk, dv, past_k, past_v, layer, lamv, g, B, T, lam_init):
    M = B * T
    P = past_k.shape[2]
    return pl.pallas_call(
        functools.partial(_diff_sample_kernel, lam_init=lam_init, past_len=P),
        out_shape=jax.ShapeDtypeStruct((M, 512), BF16),
        grid=(B,),
        in_specs=[pl.BlockSpec((T, 512), lambda b: (b, 0)),
                  pl.BlockSpec((T, 512), lambda b: (b, 0)),
                  pl.BlockSpec((T, 512), lambda b: (b, 0)),
                  pl.BlockSpec((None, None, P, 512), lambda b: (layer, b, 0, 0)),
                  pl.BlockSpec((None, None, P, 512), lambda b: (layer, b, 0, 0)),
                  pl.BlockSpec((4, DIFF_DH), lambda b: (0, 0)),
                  pl.BlockSpec((1, LANES), lambda b: (0, 0))],
        out_specs=pl.BlockSpec((T, 512), lambda b: (b, 0)),
        compiler_params=_cparams(("arbitrary",)),
        name="diff_sample",
    )(dq, dk, dv, past_k, past_v, lamv, g)


INT_MIN = -2 ** 31


def _idx_score(iq, iw, ikr):
    lane = lax.broadcasted_iota(I32, (1, IDX_HEADS * IDX_DH), 1)
    sc = None
    for i in range(IDX_HEADS):
        lg = _dot_nt(jnp.where((lane // IDX_DH) == i, iq, jnp.zeros_like(iq)), ikr)
        t = iw[:, i:i + 1] * jnp.maximum(lg, 0.0)
        sc = t if sc is None else sc + t
    return jnp.where(sc == 0.0, 0.0, sc)


def _order_key(score):
    bits = pltpu.bitcast(score, I32)
    return jnp.where(bits < 0, bits ^ 0x7FFFFFFF, bits)


def _count(mask):
    return jnp.sum(mask.astype(F32), axis=1, keepdims=True)


def _topk_select(key_refs, bases, n_sel, n_keys):
    tq = key_refs[0].shape[0]
    n = float(n_sel)

    def count_ge(cand):
        tot = None
        for kr in key_refs:
            c = _count(kr[...] >= cand)
            tot = c if tot is None else tot + c
        return tot

    t0 = jnp.where(count_ge(jnp.zeros((tq, 1), I32)) >= n, 0, INT_MIN).astype(I32)

    def vbody(i, t):
        cand = t | jnp.left_shift(jnp.int32(1), 30 - i)
        return jnp.where(count_ge(cand) >= n, cand, t)

    t = lax.fori_loop(0, 31, vbody, t0)

    c_gt = None
    for kr in key_refs:
        c = _count(kr[...] > t)
        c_gt = c if c_gt is None else c_gt + c
    need = n - c_gt
    nbits = max(1, int(n_keys).bit_length())

    def ibody(i, J):
        cand = J | jnp.left_shift(jnp.int32(1), nbits - 1 - i)
        f = None
        for kr, base in zip(key_refs, bases):
            kidx = base + lax.broadcasted_iota(I32, kr.shape, 1)
            c = _count((kr[...] == t) & (kidx < cand))
            f = c if f is None else f + c
        return jnp.where(f <= need, cand, J)

    J = lax.fori_loop(0, nbits, ibody, jnp.zeros((tq, 1), I32))
    return t, J


def _dsa_attend(q_ref, bias_refs, k_loads, v_loads, o_ref):
    for h in range(DSA_HEADS):
        hs = slice(h * DSA_DH, (h + 1) * DSA_DH)
        q = q_ref[:, hs]
        ss = [_dot_nt(q, kl(hs)) + br[...] for kl, br in zip(k_loads, bias_refs)]
        m = None
        for s in ss:
            ms = jnp.max(s, axis=1, keepdims=True)
            m = ms if m is None else jnp.maximum(m, ms)
        l = None
        o = None
        for s, vl in zip(ss, v_loads):
            p = jnp.exp(s - m)
            ls = jnp.sum(p, axis=1, keepdims=True)
            os_ = _dot(p.astype(BF16), vl(hs))
            l = ls if l is None else l + ls
            o = os_ if o is None else o + os_
        o_ref[:, hs] = (o / l).astype(BF16)


def _dsa_prompt_kernel(q_ref, iq_ref, iw_ref, k_ref, v_ref, ikr_ref, o_ref, key_scr, bias_scr,
                       *, tq, q_tile0, n_sel):
    tk = k_ref.shape[0]
    qt = q_tile0 + pl.program_id(1)
    score = _idx_score(iq_ref[...], iw_ref[...], ikr_ref[...])
    qpos = qt * tq + lax.broadcasted_iota(I32, (tq, tk), 0)
    kidx = lax.broadcasted_iota(I32, (tq, tk), 1)
    vis = (kidx // CHUNK) <= (qpos // CHUNK)
    key_scr[...] = _order_key(jnp.where(vis, score, NEG_INF))
    t, J = _topk_select([key_scr], [0], n_sel, tk)
    key = key_scr[...]
    sel = vis & ((key > t) | ((key == t) & (kidx < J)))
    bias_scr[...] = jnp.where(sel, 0.0, NEG_INF)
    _dsa_attend(q_ref, [bias_scr], [lambda hs: k_ref[:, hs]], [lambda hs: v_ref[:, hs]], o_ref)


def _dsa_prompt(cq, iq, iw, ck, cv, ikr, B, T):
    tq = min(T, 128)
    nq = T // tq
    n_sel = min(DSA_TOPK_MAX, T // 4)
    ng = 4 if nq % 4 == 0 else 1
    tpg = nq // ng
    outs = []
    for gi in range(ng):
        tk = (gi + 1) * tpg * tq
        nkb = T // tk if T % tk == 0 else None

        def kv_spec(w, tk=tk):
            return pl.BlockSpec((None, tk, w), lambda b, i: (b, 0, 0))

        def q_spec(w, gi=gi):
            return pl.BlockSpec((None, tq, w), lambda b, i: (b, gi * tpg + i, 0))

        out = pl.pallas_call(
            functools.partial(_dsa_prompt_kernel, tq=tq, q_tile0=gi * tpg, n_sel=n_sel),
            out_shape=jax.ShapeDtypeStruct((B, tpg * tq, 512), BF16),
            grid=(B, tpg),
            in_specs=[q_spec(512), q_spec(256), q_spec(LANES), kv_spec(512), kv_spec(512), kv_spec(256)],
            out_specs=pl.BlockSpec((None, tq, 512), lambda b, i: (b, i, 0)),
            scratch_shapes=[pltpu.VMEM((tq, tk), I32), pltpu.VMEM((tq, tk), F32)],
            compiler_params=_cparams(("arbitrary", "arbitrary"), 56 * 1024 * 1024),
            name=f"dsa_prompt_{gi}",
        )(cq.reshape(B, T, 512), iq.reshape(B, T, 256), iw.reshape(B, T, LANES),
          ck.reshape(B, T, 512), cv.reshape(B, T, 512), ikr.reshape(B, T, 256))
        outs.append(out)
    o = outs[0] if ng == 1 else jnp.concatenate(outs, axis=1)
    return o.reshape(B * T, 512)


def _dsa_sample_kernel(q_ref, iq_ref, iw_ref, kn_ref, vn_ref, ikrn_ref, kp_ref, vp_ref, ikp_ref, o_ref,
                       keyp_scr, keyn_scr, biasp_scr, biasn_scr, *, past_len, n_sel):
    T = q_ref.shape[0]
    P = past_len
    iq = iq_ref[...]
    iw = iw_ref[...]
    ikp = ikp_ref[...]
    lane = lax.broadcasted_iota(I32, (P, LANES), 1)
    ikp2 = jnp.concatenate([ikp, ikp], axis=1)
    ikrp = jnp.concatenate([ikp2, ikp2], axis=1).astype(BF16)
    del lane
    keyp_scr[...] = _order_key(_idx_score(iq, iw, ikrp))
    qpos = P + lax.broadcasted_iota(I32, (T, T), 0)
    kpos = P + lax.broadcasted_iota(I32, (T, T), 1)
    vis_n = (kpos // CHUNK) <= (qpos // CHUNK)
    keyn_scr[...] = _order_key(jnp.where(vis_n, _idx_score(iq, iw, ikrn_ref[...]), NEG_INF))
    t, J = _topk_select([keyp_scr, keyn_scr], [0, P], n_sel, P + T)
    kp_ = keyp_scr[...]
    kidx_p = lax.broadcasted_iota(I32, (T, P), 1)
    biasp_scr[...] = jnp.where((kp_ > t) | ((kp_ == t) & (kidx_p < J)), 0.0, NEG_INF)
    kn_ = keyn_scr[...]
    biasn_scr[...] = jnp.where(vis_n & ((kn_ > t) | ((kn_ == t) & (kpos < J))), 0.0, NEG_INF)
    _dsa_attend(q_ref, [biasp_scr, biasn_scr],
                [lambda hs: kp_ref[:, hs].astype(BF16), lambda hs: kn_ref[:, hs]],
                [lambda hs: vp_ref[:, hs].astype(BF16), lambda hs: vn_ref[:, hs]], o_ref)


def _dsa_sample(cq, iq, iw, ck, cv, ikr, past_k, past_v, past_ik, layer, B, T):
    M = B * T
    P = past_k.shape[2]
    n_sel = min(DSA_TOPK_MAX, (P + T) // 4)

    def rspec(w):
        return pl.BlockSpec((T, w), lambda b: (b, 0))

    def pspec(w):
        return pl.BlockSpec((None, None, P, w), lambda b: (layer, b, 0, 0))

    return pl.pallas_call(
        functools.partial(_dsa_sample_kernel, past_len=P, n_sel=n_sel),
        out_shape=jax.ShapeDtypeStruct((M, 512), BF16),
        grid=(B,),
        in_specs=[rspec(512), rspec(256), rspec(LANES), rspec(512), rspec(512), rspec(256),
                  pspec(512), pspec(512), pspec(IDX_DH)],
        out_specs=rspec(512),
        scratch_shapes=[pltpu.VMEM((T, P), I32), pltpu.VMEM((T, T), I32),
                        pltpu.VMEM((T, P), F32), pltpu.VMEM((T, T), F32)],
        compiler_params=_cparams(("arbitrary",)),
        name="dsa_sample",
    )(cq, iq, iw, ck, cv, ikr, past_k, past_v, past_ik)


def _merge_kernel(og_ref, od_ref, oc_ref, g0_ref, g1_ref, g2_ref, x_ref, m_ref,
                  wg_ref, wd_ref, wc_ref, wo_ref, gn_ref, wr_ref, br_ref,
                  x1_ref, h2_ref, ti_ref, tw_ref, *, nb):
    merged = (jax.nn.sigmoid(g0_ref[...]) * _dot(og_ref[...], wg_ref[...])
              + jax.nn.sigmoid(g1_ref[...]) * _dot(od_ref[...], wd_ref[...])
              + jax.nn.sigmoid(g2_ref[...]) * _dot(oc_ref[...], wc_ref[...]))
    mix = _dot(merged.astype(BF16), wo_ref[...])
    m = m_ref[...]
    x1 = x_ref[...] + _gate_rows(mix, m, 2, nb)
    x1_ref[...] = x1
    xn = x1 * lax.rsqrt(jnp.mean(x1 * x1, axis=-1, keepdims=True) + EPS) * gn_ref[...]
    h2 = _modulate(xn, m, 3, 4, nb)
    h2_ref[...] = h2
    lg = _dot(h2.astype(BF16), wr_ref[...]) + br_ref[...]
    tm = lg.shape[0]
    lane = lax.broadcasted_iota(I32, (tm, LANES), 1).astype(F32)
    vals, idxs = [], []
    for _ in range(TOP_K):
        mx = jnp.max(lg, axis=1, keepdims=True)
        ix = jnp.min(jnp.where(lg == mx, lane, float(LANES)), axis=1, keepdims=True)
        vals.append(mx)
        idxs.append(ix)
        lg = jnp.where(lane == ix, -jnp.inf, lg)
    es = [jnp.exp(v - vals[0]) for v in vals]
    den = es[0] + es[1] + es[2] + es[3]
    ti = jnp.zeros((tm, LANES), F32)
    tw = jnp.zeros((tm, LANES), F32)
    for r in range(TOP_K):
        ti = jnp.where(lane == float(r), idxs[r], ti)
        tw = jnp.where(lane == float(r), es[r] / den, tw)
    ti_ref[...] = ti.astype(I32)
    tw_ref[...] = tw


def _merge(og, od, oc, z, x2, m, wts, B, T):
    M = B * T
    tm, nb = _row_tiling(B, T, 512)
    wg, wd, wc, wo, gn, wr, br = wts

    def rspec(w):
        return pl.BlockSpec((tm, w), lambda i: (i, 0))

    def gspec(k):
        return pl.BlockSpec((tm, D_MODEL), lambda i: (i, Z_GATES // D_MODEL + k))

    def wspec(r, c):
        return pl.BlockSpec((r, c), lambda i: (0, 0))

    return pl.pallas_call(
        functools.partial(_merge_kernel, nb=nb),
        out_shape=[jax.ShapeDtypeStruct((M, D_MODEL), F32), jax.ShapeDtypeStruct((M, D_MODEL), F32),
                   jax.ShapeDtypeStruct((M, LANES), I32), jax.ShapeDtypeStruct((M, LANES), F32)],
        grid=(M // tm,),
        in_specs=[rspec(512), rspec(512), rspec(512), gspec(0), gspec(1), gspec(2), rspec(D_MODEL),
                  pl.BlockSpec((nb, 6, D_MODEL), lambda i: ((i * tm) // (T * nb), 0, 0)),
                  wspec(512, D_MODEL), wspec(512, D_MODEL), wspec(512, D_MODEL), wspec(D_MODEL, D_MODEL),
                  wspec(1, D_MODEL), wspec(D_MODEL, LANES), wspec(1, LANES)],
        out_specs=[rspec(D_MODEL), rspec(D_MODEL), rspec(LANES), rspec(LANES)],
        compiler_params=_cparams(("arbitrary",)),
        name="merge_router",
    )(og, od, oc, z, z, z, x2, m, wg, wd, wc, wo, gn, wr, br)


def _gather_rows(idx_ref, n, src_hbm, dst, sem):
    def copy(r):
        s = pl.multiple_of(idx_ref[0, 0, r] * ROW_TILE, ROW_TILE)
        d = pl.multiple_of(r * ROW_TILE, ROW_TILE)
        return pltpu.make_async_copy(src_hbm.at[pl.ds(s, ROW_TILE), :], dst.at[pl.ds(d, ROW_TILE), :], sem)

    def issue(r, c):
        copy(r).start()
        return c

    def wait(r, c):
        copy(r).wait()
        return c

    lax.fori_loop(0, n, issue, 0)
    lax.fori_loop(0, n, wait, 0)


def _tile_rows_to_2d(buf, base, n):
    return jnp.concatenate([buf[pl.ds(base + c, n, stride=ROW_TILE), :] for c in range(ROW_TILE)], axis=1)


def _ffn_kernel(be_ref, nu_ref, tok_ref, x_hbm, rw_ref, w1g_ref, b1g_ref, w1u_ref, b1u_ref, w2_ref, b2_ref,
                o_ref, xbuf, sem, *, bm):
    i = pl.program_id(0)

    @pl.when(i < nu_ref[0])
    def _():
        _gather_rows(tok_ref, bm, x_hbm, xbuf, sem)
        x = _tile_rows_to_2d(xbuf, 0, bm).astype(BF16)
        g = jnp.minimum(_dot(x, w1g_ref[...]) + b1g_ref[...], SWIGLU_LIMIT)
        u = jnp.clip(_dot(x, w1u_ref[...]) + b1u_ref[...], -SWIGLU_LIMIT, SWIGLU_LIMIT)
        a = g * jax.nn.sigmoid(SWIGLU_ALPHA * g) * (u + 1.0)
        y = _dot(a.astype(BF16), w2_ref[...]) + b2_ref[...]
        o_ref[...] = y * rw_ref[...]

    @pl.when(i >= nu_ref[0])
    def _():
        o_ref[...] = jnp.zeros(o_ref.shape, F32)


def _ffn(h2t, block_e, n_used, row_tok, row_w, w1g, b1g, w1u, b1u, w2, b2, bm):
    n_rows = row_tok.shape[0]
    nblk = n_rows // bm

    def wspec(r, c):
        return pl.BlockSpec((None, r, c), lambda i, be, nu: (be[i], 0, 0))

    grid_spec = pltpu.PrefetchScalarGridSpec(
        num_scalar_prefetch=2,
        grid=(nblk,),
        in_specs=[pl.BlockSpec((1, 1, bm), lambda i, be, nu: (i, 0, 0), memory_space=pltpu.SMEM),
                  pl.BlockSpec(memory_space=pl.ANY),
                  pl.BlockSpec((bm, 1), lambda i, be, nu: (i, 0)),
                  wspec(D_MODEL, D_FF), wspec(1, D_FF), wspec(D_MODEL, D_FF), wspec(1, D_FF),
                  wspec(D_FF, D_MODEL), wspec(1, D_MODEL)],
        out_specs=pl.BlockSpec((bm, D_MODEL), lambda i, be, nu: (i, 0)),
        scratch_shapes=[pltpu.VMEM((bm * ROW_TILE, LANES), F32), pltpu.SemaphoreType.DMA])
    return pl.pallas_call(
        functools.partial(_ffn_kernel, bm=bm),
        out_shape=jax.ShapeDtypeStruct((n_rows, D_MODEL), F32),
        grid_spec=grid_spec,
        compiler_params=_cparams(("arbitrary",)),
        name="moe_ffn",
    )(block_e, n_used, row_tok.reshape(nblk, 1, bm), h2t, row_w.reshape(n_rows, 1),
      w1g, b1g, w1u, b1u, w2, b2)


def _combine_kernel(pos_ref, ys_hbm, x_ref, m_ref, o_ref, ybuf, sem, *, tm, nb):
    _gather_rows(pos_ref, TOP_K * tm, ys_hbm, ybuf, sem)
    y = None
    for k in range(TOP_K):
        yk = _tile_rows_to_2d(ybuf, k * tm * ROW_TILE, tm)
        y = yk if y is None else y + yk
    o_ref[...] = x_ref[...] + _gate_rows(y, m_ref[...], 5, nb)


def _combine(pos_t, ys_t, x1, m, B, T):
    M = B * T
    tm, nb = _row_tiling(B, T, 256)
    nt = M // tm
    return pl.pallas_call(
        functools.partial(_combine_kernel, tm=tm, nb=nb),
        out_shape=jax.ShapeDtypeStruct((M, D_MODEL), F32),
        grid=(nt,),
        in_specs=[pl.BlockSpec((1, 1, TOP_K * tm), lambda i: (i, 0, 0), memory_space=pltpu.SMEM),
                  pl.BlockSpec(memory_space=pl.ANY),
                  pl.BlockSpec((tm, D_MODEL), lambda i: (i, 0)),
                  pl.BlockSpec((nb, 6, D_MODEL), lambda i: ((i * tm) // (T * nb), 0, 0))],
        out_specs=pl.BlockSpec((tm, D_MODEL), lambda i: (i, 0)),
        scratch_shapes=[pltpu.VMEM((TOP_K * tm * ROW_TILE, LANES), F32), pltpu.SemaphoreType.DMA],
        compiler_params=_cparams(("arbitrary",)),
        name="moe_combine",
    )(pos_t.reshape(nt, 1, TOP_K * tm), ys_t, x1, m)


def _route(top_i, top_w, bm):
    N = top_i.shape[0]
    NK = N * TOP_K
    flat_e = top_i.reshape(NK)
    onehot = (flat_e[:, None] == jnp.arange(N_EXPERTS, dtype=I32)[None, :]).astype(I32)
    csum = jnp.cumsum(onehot, axis=0)
    rank = jnp.take_along_axis(csum, flat_e[:, None], axis=1)[:, 0] - 1
    counts = csum[-1]
    padded = (counts + bm - 1) // bm * bm
    pad_end = jnp.cumsum(padded)
    pad_start = pad_end - padded
    pos = pad_start[flat_e] + rank
    n_rows = (-(-NK // bm)) * bm + N_EXPERTS * bm
    nblk = n_rows // bm
    flat_tok = jnp.arange(NK, dtype=I32) // TOP_K
    row_tok = jnp.zeros((n_rows,), I32).at[pos].set(flat_tok)
    row_w = jnp.zeros((n_rows,), F32).at[pos].set(top_w.reshape(NK))
    block_e = jnp.minimum(
        jnp.searchsorted(pad_end, jnp.arange(nblk, dtype=I32) * bm, side='right'), N_EXPERTS - 1).astype(I32)
    n_used = (pad_end[-1:] // bm).astype(I32)
    return pos.astype(I32), row_tok, row_w, block_e, n_used


def _moe(h2, top_i, top_w, x1, m, ew, B, T):
    M = B * T
    bm = 256 if M * TOP_K >= 256 * N_EXPERTS * 4 else 128
    pos, row_tok, row_w, block_e, n_used = _route(top_i, top_w, bm)
    h2t = h2.reshape(M * ROW_TILE, LANES)
    ys = _ffn(h2t, block_e, n_used, row_tok, row_w, *ew, bm)
    ys_t = ys.reshape(ys.shape[0] * ROW_TILE, LANES)
    tm, _ = _row_tiling(B, T, 256)
    pos_t = pos.reshape(M // tm, tm, TOP_K).transpose(0, 2, 1)
    return _combine(pos_t, ys_t, x1, m, B, T)


def _prep_layer(P, l):
    w_in = P['w_in'][l]

    def cols(name):
        o, s = _SRC[name]
        return w_in[:, o:o + s]

    zeros = lambda n: jnp.zeros((D_MODEL, n), F32)
    w_r = jnp.concatenate([
        cols('gla_q'), cols('gla_k'), cols('gla_v'), cols('gla_r'),
        cols('diff_q'), cols('diff_k'), cols('diff_v'),
        cols('dsa_q'), cols('dsa_k'), cols('dsa_v'),
        cols('idx_q'), cols('idx_k'), cols('idx_w'), zeros(60), cols('gla_a'), zeros(112),
        cols('gates')], axis=1).astype(BF16)
    assert w_r.shape[1] == Z_WIDTH
    W = GLA_HEADS * GLA_DK
    wa = jnp.zeros((LANES, W), F32).at[:GLA_GATE_RANK].set(P['w_gla_a2'][l]).astype(BF16)
    w1 = P['w_mlp1'][l]
    b1 = P['b_mlp1'][l]
    return dict(
        w_in=w_r, wa=wa, ba=P['b_gla_a2'][l].reshape(1, W),
        g_gla=jnp.tile(P['g_gla_out'][l], GLA_HEADS).reshape(1, 512),
        gains=(jnp.tile(P['g_diff_q'][l], 8).reshape(1, 512), jnp.tile(P['g_diff_k'][l], 8).reshape(1, 512),
               jnp.tile(P['g_dsa_q'][l], 4).reshape(1, 512), jnp.tile(P['g_dsa_k'][l], 4).reshape(1, 512),
               jnp.tile(P['g_idx_k'][l], 2).reshape(1, LANES)),
        lamv=jnp.stack([P['lambda_q1'][l], P['lambda_k1'][l], P['lambda_q2'][l], P['lambda_k2'][l]]),
        g_diff=P['g_diff_out'][l].reshape(1, LANES),
        merge=(P['w_branch_gla'][l].astype(BF16), P['w_branch_diff'][l].astype(BF16),
               P['w_branch_dsa'][l].astype(BF16), P['w_out'][l].astype(BF16),
               P['g_norm2'][l].reshape(1, D_MODEL),
               jnp.zeros((D_MODEL, LANES), F32).at[:, :N_EXPERTS].set(P['w_router'][l]).astype(BF16),
               jnp.full((1, LANES), NEG_INF, F32).at[0, :N_EXPERTS].set(P['b_router'][l])),
        experts=(w1[:, :, 0::2].astype(BF16), b1[:, None, 0::2], w1[:, :, 1::2].astype(BF16), b1[:, None, 1::2],
                 P['w_mlp2'][l].astype(BF16), P['b_mlp2'][l][:, None, :]),
    )


def _trunk(x, c, past, P, prepped):
    B, T, _ = x.shape
    M = B * T
    x2 = x.reshape(M, D_MODEL)
    past_len = 0 if past is None else past[0].shape[2]
    pos = past_len + jnp.arange(T, dtype=I32)
    tabs64 = _rope_tables(pos, 64)
    tabs128 = _rope_tables(pos, 128)
    W = GLA_HEADS * GLA_DK
    if past is not None:
        pdk = past[0].reshape(past[0].shape[:3] + (512,))
        pdv = past[1].reshape(past[1].shape[:3] + (512,))
        pck = past[2].reshape(past[2].shape[:3] + (512,))
        pcv = past[3].reshape(past[3].shape[:3] + (512,))
        pik = past[4]
        s0_all = past[5].transpose(0, 1, 4, 2, 3).reshape(past[5].shape[0], B, GLA_DV, W)
    per_layer = []
    for l, pp in enumerate(prepped):
        lam_init = 0.8 - 0.6 * math.exp(-0.3 * l)
        m = _ada(c, P['w_ada'][l], P['b_ada'][l])
        z = _inproj(x2, m, P['g_norm1'][l], pp['w_in'], B, T)
        (dq, dk32, dkb, dvb, cq, ck32, ckb, cvb, iq, ik32, ikr, iw) = _post(z, tabs64, tabs128, pp['gains'], B, T)
        s0t = jnp.zeros((B, GLA_DV, W), F32) if past is None else s0_all[l]
        o_gla, st = _gla(z, pp['wa'], pp['ba'], pp['g_gla'], s0t, B, T)
        if past is None:
            o_diff = _diff_prompt(dq, dkb, dvb, pp['lamv'], pp['g_diff'], B, T, lam_init)
            o_dsa = _dsa_prompt(cq, iq, iw, ckb, cvb, ikr, B, T)
        else:
            o_diff = _diff_sample(dq, dkb, dvb, pdk, pdv, l, pp['lamv'], pp['g_diff'], B, T, lam_init)
            o_dsa = _dsa_sample(cq, iq, iw, ckb, cvb, ikr, pck, pcv, pik, l, B, T)
        x1, h2, top_i, top_w = _merge(o_gla, o_diff, o_dsa, z, x2, m, pp['merge'], B, T)
        x2 = _moe(h2, top_i[:, :TOP_K], top_w[:, :TOP_K], x1, m, pp['experts'], B, T)
        s_gla = st.reshape(B, GLA_DV, GLA_HEADS, GLA_DK).transpose(0, 2, 3, 1)
        per_layer.append((dk32.reshape(B, T, DIFF_HEADS, 2, DIFF_DH),
                          z[:, Z_DV:Z_DV + 512].reshape(B, T, DIFF_HEADS, DIFF_DV),
                          ck32.reshape(B, T, DSA_HEADS, DSA_DH),
                          z[:, Z_CV:Z_CV + 512].reshape(B, T, DSA_HEADS, DSA_DH),
                          ik32.reshape(B, T, IDX_DH),
                          s_gla))
    stacked = tuple(jnp.stack([st[i] for st in per_layer]) for i in range(6))
    return x2.reshape(B, T, D_MODEL), stacked


def kernel(x_prompt, x_sample, cache_diff_k, cache_diff_v, cache_dsa_k, cache_dsa_v, cache_dsa_idx_k,
           state_gla, c_prompt, c_sample, w_ada, b_ada, g_norm1, g_norm2, w_in, w_gla_a2, b_gla_a2,
           g_gla_out, g_diff_q, g_diff_k, lambda_q1, lambda_k1, lambda_q2, lambda_k2, g_diff_out,
           g_dsa_q, g_dsa_k, g_idx_k, w_branch_gla, w_branch_diff, w_branch_dsa, w_out, w_router,
           b_router, w_mlp1, b_mlp1, w_mlp2, b_mlp2):
    P = dict(w_ada=w_ada, b_ada=b_ada, g_norm1=g_norm1, g_norm2=g_norm2, w_in=w_in, w_gla_a2=w_gla_a2,
             b_gla_a2=b_gla_a2, g_gla_out=g_gla_out, g_diff_q=g_diff_q, g_diff_k=g_diff_k,
             lambda_q1=lambda_q1, lambda_k1=lambda_k1, lambda_q2=lambda_q2, lambda_k2=lambda_k2,
             g_diff_out=g_diff_out, g_dsa_q=g_dsa_q, g_dsa_k=g_dsa_k, g_idx_k=g_idx_k,
             w_branch_gla=w_branch_gla, w_branch_diff=w_branch_diff, w_branch_dsa=w_branch_dsa,
             w_out=w_out, w_router=w_router, b_router=b_router, w_mlp1=w_mlp1, b_mlp1=b_mlp1,
             w_mlp2=w_mlp2, b_mlp2=b_mlp2)
    depth = w_in.shape[0]
    prepped = [_prep_layer(P, l) for l in range(depth)]
    y_prompt, new_p = _trunk(x_prompt, c_prompt, None, P, prepped)
    y_sample, new_s = _trunk(
        x_sample, c_sample,
        (cache_diff_k, cache_diff_v, cache_dsa_k, cache_dsa_v, cache_dsa_idx_k, state_gla), P, prepped)
    return (y_prompt, y_sample) + new_p + new_s
```

```python
import functools
import math

import jax
import jax.numpy as jnp
from jax import lax
from jax.experimental import pallas as pl
from jax.experimental.pallas import tpu as pltpu

F32 = jnp.float32
BF16 = jnp.bfloat16
I32 = jnp.int32

D_MODEL = 1024
CHUNK = 64
ROPE_THETA = 500000.0
ROPE_FRACTION = 4
EPS = 1e-6
NEG_INF = -1e30

GLA_HEADS = 4
GLA_DK = 64
GLA_DV = 128
GLA_GATE_RANK = 16
GLA_GATE_TAU = 16.0
GLA_SUB = 16
DIFF_HEADS = 4
DIFF_DH = 64
DIFF_DV = 128
DSA_HEADS = 4
DSA_DH = 128
IDX_HEADS = 4
IDX_DH = 64
DSA_TOPK_MAX = 256
N_EXPERTS = 32
TOP_K = 4
D_FF = 1024
SWIGLU_LIMIT = 7.0
SWIGLU_ALPHA = 1.702

LANES = 128
ROW_TILE = 8
VMEM_LIMIT = 48 * 1024 * 1024

Z_GQK, Z_GV, Z_GR = 0, 512, 1024
Z_DQ, Z_DK, Z_DV = 1536, 2048, 2560
Z_CQ, Z_CK, Z_CV = 3072, 3584, 4096
Z_MISC = 4608
Z_GATES = 5120
Z_WIDTH = 8192
MISC_IK = 256
MISC_GA = 384

_SRC = {}
_off = 0
for _name, _size in (
        ('gla_q', 256), ('gla_k', 256), ('gla_v', 512), ('gla_a', 16), ('gla_r', 512),
        ('diff_q', 512), ('diff_k', 512), ('diff_v', 512),
        ('dsa_q', 512), ('dsa_k', 512), ('dsa_v', 512),
        ('idx_q', 256), ('idx_k', 64), ('idx_w', 4), ('gates', 3072)):
    _SRC[_name] = (_off, _size)
    _off += _size


def _cparams(sem, vmem=VMEM_LIMIT):
    return pltpu.CompilerParams(dimension_semantics=sem, vmem_limit_bytes=vmem)


def _dot(a, b):
    return jnp.dot(a, b, preferred_element_type=F32)


def _dot_nt(a, b):
    return lax.dot_general(a, b, (((1,), (1,)), ((), ())), preferred_element_type=F32)


def _dot_tn(a, b):
    return lax.dot_general(a, b, (((0,), (0,)), ((), ())), preferred_element_type=F32)


def _silu(x):
    return x * jax.nn.sigmoid(x)


def _row_tiling(B, T, target):
    if T >= target:
        assert T % target == 0
        return target, 1
    nb = 1
    for cand in range(1, B + 1):
        if B % cand == 0 and cand * T <= target:
            nb = cand
    return nb * T, nb


def _mod_rows(m, lo, tm, nb):
    return m[:, lo:lo + 1, :]


def _modulate(xn, m, shift_i, scale_i, nb):
    tm = xn.shape[0]
    if nb == 1:
        return xn * (1.0 + m[0, scale_i:scale_i + 1, :]) + m[0, shift_i:shift_i + 1, :]
    x3 = xn.reshape(nb, tm // nb, D_MODEL)
    h = x3 * (1.0 + m[:, scale_i:scale_i + 1, :]) + m[:, shift_i:shift_i + 1, :]
    return h.reshape(tm, D_MODEL)


def _gate_rows(y, m, gate_i, nb):
    tm = y.shape[0]
    if nb == 1:
        return y * m[0, gate_i:gate_i + 1, :]
    return (y.reshape(nb, tm // nb, D_MODEL) * m[:, gate_i:gate_i + 1, :]).reshape(tm, D_MODEL)


def _ada_kernel(c_ref, w_ref, b_ref, o_ref):
    s = _silu(c_ref[...])
    o_ref[...] = _dot(s.astype(BF16), w_ref[...].astype(BF16)) + b_ref[...]


def _ada(c, w, b):
    B = c.shape[0]
    out = pl.pallas_call(
        _ada_kernel,
        out_shape=jax.ShapeDtypeStruct((B, 6 * D_MODEL), F32),
        grid=(6,),
        in_specs=[pl.BlockSpec((B, D_MODEL), lambda j: (0, 0)),
                  pl.BlockSpec((D_MODEL, D_MODEL), lambda j: (0, j)),
                  pl.BlockSpec((1, D_MODEL), lambda j: (0, j))],
        out_specs=pl.BlockSpec((B, D_MODEL), lambda j: (0, j)),
        compiler_params=_cparams(("arbitrary",)),
        name="ada",
    )(c, w, b.reshape(1, -1))
    return out.reshape(B, 6, D_MODEL)


def _inproj_kernel(x_ref, m_ref, g_ref, w_ref, o_ref, h_scr, *, nb):
    @pl.when(pl.program_id(1) == 0)
    def _():
        x = x_ref[...]
        xn = x * lax.rsqrt(jnp.mean(x * x, axis=-1, keepdims=True) + EPS) * g_ref[...]
        h_scr[...] = _modulate(xn, m_ref[...], 0, 1, nb).astype(BF16)

    o_ref[...] = _dot(h_scr[...], w_ref[...])


def _inproj(x2, m, g, w, B, T):
    M = B * T
    tm, nb = _row_tiling(B, T, 1024)
    tn = 1024
    return pl.pallas_call(
        functools.partial(_inproj_kernel, nb=nb),
        out_shape=jax.ShapeDtypeStruct((M, Z_WIDTH), F32),
        grid=(M // tm, Z_WIDTH // tn),
        in_specs=[pl.BlockSpec((tm, D_MODEL), lambda i, j: (i, 0)),
                  pl.BlockSpec((nb, 6, D_MODEL), lambda i, j: ((i * tm) // (T * nb), 0, 0)),
                  pl.BlockSpec((1, D_MODEL), lambda i, j: (0, 0)),
                  pl.BlockSpec((D_MODEL, tn), lambda i, j: (0, j))],
        out_specs=pl.BlockSpec((tm, tn), lambda i, j: (i, j)),
        scratch_shapes=[pltpu.VMEM((tm, D_MODEL), BF16)],
        compiler_params=_cparams(("arbitrary", "arbitrary")),
        name="inproj",
    )(x2, m, g.reshape(1, -1), w)


def _rope_tables(pos, d):
    rot = d // ROPE_FRACTION
    half = rot // 2
    T = pos.shape[0]
    inv_freq = ROPE_THETA ** (-jnp.arange(half, dtype=F32) / half)
    ang = pos.astype(F32)[:, None] * inv_freq[None, :]
    cos, sin = jnp.cos(ang), jnp.sin(ang)
    c = jnp.concatenate([cos, cos, jnp.ones((T, d - rot), F32)], axis=1)
    a = jnp.concatenate([-sin, jnp.zeros((T, d - half), F32)], axis=1)
    b = jnp.concatenate([jnp.zeros((T, half), F32), sin, jnp.zeros((T, d - rot), F32)], axis=1)
    reps = LANES // d
    return tuple(jnp.tile(t, (1, reps)) for t in (c, a, b))


def _rope128(xs, tabs, half):
    c, a, b = tabs
    return xs * c + pltpu.roll(xs, LANES - half, 1) * a + pltpu.roll(xs, half, 1) * b


def _norm_rope(x, g, tabs, d, norm=True):
    tm, W = x.shape
    half = d // ROPE_FRACTION // 2
    lo = lax.broadcasted_iota(I32, (tm, LANES), 1) < 64
    outs = []
    for s in range(W // LANES):
        xs = x[:, s * LANES:(s + 1) * LANES]
        if norm:
            sq = xs * xs
            if d == LANES:
                r = lax.rsqrt(jnp.sum(sq, axis=1, keepdims=True) * (1.0 / d) + EPS)
            else:
                s_lo = jnp.sum(jnp.where(lo, sq, 0.0), axis=1, keepdims=True)
                s_hi = jnp.sum(jnp.where(lo, 0.0, sq), axis=1, keepdims=True)
                r = jnp.where(lo, lax.rsqrt(s_lo * (1.0 / d) + EPS), lax.rsqrt(s_hi * (1.0 / d) + EPS))
            xs = xs * r * g[:, s * LANES:(s + 1) * LANES]
        outs.append(_rope128(xs, tabs, half))
    return outs[0] if len(outs) == 1 else jnp.concatenate(outs, axis=1)


def _post_kernel(dq_ref, dk_ref, dv_ref, cq_ref, ck_ref, cv_ref, mi_ref,
                 c64_ref, a64_ref, b64_ref, c128_ref, a128_ref, b128_ref,
                 gdq_ref, gdk_ref, gcq_ref, gck_ref, gik_ref,
                 dq_o, dk32_o, dkb_o, dvb_o, cq_o, ck32_o, ckb_o, cvb_o, iq_o, ik32_o, ikr_o, iw_o):
    t64 = (c64_ref[...], a64_ref[...], b64_ref[...])
    t128 = (c128_ref[...], a128_ref[...], b128_ref[...])
    dq = _norm_rope(dq_ref[...], gdq_ref[...], t64, 64)
    dq_o[...] = (dq * (DIFF_DH ** -0.5)).astype(BF16)
    dk = _norm_rope(dk_ref[...], gdk_ref[...], t64, 64)
    dk32_o[...] = dk
    dkb_o[...] = dk.astype(BF16)
    dvb_o[...] = dv_ref[...].astype(BF16)
    cq = _norm_rope(cq_ref[...], gcq_ref[...], t128, 128)
    cq_o[...] = (cq * (DSA_DH ** -0.5)).astype(BF16)
    ck = _norm_rope(ck_ref[...], gck_ref[...], t128, 128)
    ck32_o[...] = ck
    ckb_o[...] = ck.astype(BF16)
    cvb_o[...] = cv_ref[...].astype(BF16)
    mi = mi_ref[...]
    iq = _norm_rope(mi[:, 0:256], None, t64, 64, norm=False)
    iq_o[...] = (iq * (IDX_DH ** -0.5)).astype(BF16)
    mk = mi[:, MISC_IK:MISC_IK + LANES]
    tm = mk.shape[0]
    lane = lax.broadcasted_iota(I32, (tm, LANES), 1)
    lo = lane < 64
    ssq = jnp.sum(jnp.where(lo, mk * mk, 0.0), axis=1, keepdims=True)
    ikn = mk * lax.rsqrt(ssq * (1.0 / IDX_DH) + EPS) * gik_ref[...]
    ik = _rope128(ikn, t64, IDX_DH // ROPE_FRACTION // 2)
    ik32_o[...] = ik[:, 0:IDX_DH]
    ik2 = jnp.where(lo, ik, pltpu.roll(ik, 64, 1))
    ikr_o[...] = jnp.concatenate([ik2, ik2], axis=1).astype(BF16)
    iw_o[...] = jnp.where(lane < IDX_HEADS, pltpu.roll(mk, 64, 1) * (IDX_HEADS ** -0.5), 0.0)


def _post(z, tabs64, tabs128, gains, B, T):
    M = B * T
    tm = min(T, 512)
    assert T % tm == 0
    npos = T // tm

    def zspec(off):
        return pl.BlockSpec((tm, 512), lambda i: (i, off // 512))

    tab_spec = pl.BlockSpec((tm, LANES), lambda i: (i % npos, 0))

    def gspec(w):
        return pl.BlockSpec((1, w), lambda i: (0, 0))

    def ospec(w):
        return pl.BlockSpec((tm, w), lambda i: (i, 0))

    outs = [(512, BF16), (512, F32), (512, BF16), (512, BF16),
            (512, BF16), (512, F32), (512, BF16), (512, BF16),
            (256, BF16), (IDX_DH, F32), (256, BF16), (LANES, F32)]
    return pl.pallas_call(
        _post_kernel,
        out_shape=[jax.ShapeDtypeStruct((M, w), dt) for w, dt in outs],
        grid=(M // tm,),
        in_specs=[zspec(Z_DQ), zspec(Z_DK), zspec(Z_DV), zspec(Z_CQ), zspec(Z_CK), zspec(Z_CV), zspec(Z_MISC)]
                 + [tab_spec] * 6 + [gspec(512)] * 4 + [gspec(LANES)],
        out_specs=[ospec(w) for w, _ in outs],
        compiler_params=_cparams(("arbitrary",)),
        name="post",
    )(z, z, z, z, z, z, z, *tabs64, *tabs128, *gains)


def _split3(x):
    hi = x.astype(BF16)
    r = x - hi.astype(F32)
    mid = r.astype(BF16)
    lo = (r - mid.astype(F32)).astype(BF16)
    return hi, mid, lo


def _gla_kernel(qk_ref, v_ref, r_ref, ga_ref, wa_ref, ba_ref, g_ref, s0_ref, o_ref, st_ref, st_scr, *, nct):
    ti = pl.program_id(1)

    @pl.when(ti == 0)
    def _():
        st_scr[...] = s0_ref[...]

    C = CHUNK
    W = GLA_HEADS * GLA_DK
    row = lax.broadcasted_iota(I32, (C, C), 0)
    col = lax.broadcasted_iota(I32, (C, C), 1)
    tri = col <= row
    tri_bf = tri.astype(BF16)
    lane = lax.broadcasted_iota(I32, (1, W), 1)
    hmask = [(lane // GLA_DK) == h for h in range(GLA_HEADS)]
    rowi = lax.broadcasted_iota(I32, (C, W), 0)
    wa = wa_ref[...]
    ba = ba_ref[...]
    g = g_ref[...]

    def chunk(c, carry):
        rows = pl.ds(pl.multiple_of(c * C, C), C)
        qk = qk_ref[rows, :]
        q = qk[:, :W] * (GLA_DK ** -0.5)
        k = qk[:, W:]
        v = v_ref[rows, :].astype(BF16)
        pre = _dot(ga_ref[rows, :].astype(BF16), wa) + ba
        la = (jnp.minimum(pre, 0.0) - jnp.log(1.0 + jnp.exp(-jnp.abs(pre)))) * (1.0 / GLA_GATE_TAU)
        hi, mid, lo = _split3(la)
        b = _dot(tri_bf, hi) + _dot(tri_bf, mid) + _dot(tri_bf, lo)
        st = st_scr[...]
        st_bf = st.astype(BF16)
        qe = q * jnp.exp(b)
        b_end = b[C - 1:C, :]
        kend = k * jnp.exp(b_end - b)
        att_parts = [[] for _ in range(GLA_HEADS)]
        for s in range(C // GLA_SUB):
            r0 = s * GLA_SUB
            br = b[r0:r0 + 1, :]
            qs = q[r0:r0 + GLA_SUB, :] * jnp.exp(b[r0:r0 + GLA_SUB, :] - br)
            ks = k * jnp.exp(br - b)
            if r0 + GLA_SUB < C:
                ks = jnp.where(rowi < r0 + GLA_SUB, ks, 0.0)
            ks = ks.astype(BF16)
            for h in range(GLA_HEADS):
                att_parts[h].append(_dot_nt(jnp.where(hmask[h], qs, 0.0).astype(BF16), ks))
        upd = None
        for h in range(GLA_HEADS):
            hs = slice(h * GLA_DV, (h + 1) * GLA_DV)
            att = jnp.where(tri, jnp.concatenate(att_parts[h], axis=0), 0.0)
            o = _dot(att.astype(BF16), v[:, hs]) + _dot_nt(jnp.where(hmask[h], qe, 0.0).astype(BF16), st_bf)
            u = _dot_tn(v[:, hs], jnp.where(hmask[h], kend, 0.0).astype(BF16))
            upd = u if upd is None else upd + u
            y = o * lax.rsqrt(jnp.mean(o * o, axis=-1, keepdims=True) + EPS) * g[:, hs]
            o_ref[rows, hs] = (y * _silu(r_ref[rows, hs])).astype(BF16)
        st_scr[...] = st * jnp.exp(b_end) + upd
        return carry

    lax.fori_loop(0, nct, chunk, 0)

    @pl.when(ti == pl.num_programs(1) - 1)
    def _():
        st_ref[...] = st_scr[...]


def _gla(z, wa, ba, g, s0t, B, T):
    M = B * T
    tt = min(T, 512)
    assert T % tt == 0 and tt % CHUNK == 0
    nt = T // tt
    W = GLA_HEADS * GLA_DK
    return pl.pallas_call(
        functools.partial(_gla_kernel, nct=tt // CHUNK),
        out_shape=[jax.ShapeDtypeStruct((M, 512), BF16),
                   jax.ShapeDtypeStruct((B, GLA_DV, W), F32)],
        grid=(B, nt),
        in_specs=[pl.BlockSpec((tt, 512), lambda b, t: (b * nt + t, Z_GQK // 512)),
                  pl.BlockSpec((tt, 512), lambda b, t: (b * nt + t, Z_GV // 512)),
                  pl.BlockSpec((tt, 512), lambda b, t: (b * nt + t, Z_GR // 512)),
                  pl.BlockSpec((tt, LANES), lambda b, t: (b * nt + t, (Z_MISC + MISC_GA) // LANES)),
                  pl.BlockSpec((LANES, W), lambda b, t: (0, 0)),
                  pl.BlockSpec((1, W), lambda b, t: (0, 0)),
                  pl.BlockSpec((1, 512), lambda b, t: (0, 0)),
                  pl.BlockSpec((None, GLA_DV, W), lambda b, t: (b, 0, 0))],
        out_specs=[pl.BlockSpec((tt, 512), lambda b, t: (b * nt + t, 0)),
                   pl.BlockSpec((None, GLA_DV, W), lambda b, t: (b, 0, 0))],
        scratch_shapes=[pltpu.VMEM((GLA_DV, W), F32)],
        compiler_params=_cparams(("arbitrary", "arbitrary")),
        name="gla",
    )(z, z, z, z, wa, ba, g, s0t)


def _diff_lambda(lam_ref, lam_init):
    lv = lam_ref[...]
    a = jnp.sum(lv[0:1, :] * lv[1:2, :], axis=1, keepdims=True)
    b = jnp.sum(lv[2:3, :] * lv[3:4, :], axis=1, keepdims=True)
    return jnp.exp(a) - jnp.exp(b) + lam_init


def _diff_finish(o0, o1, lam, g, lam_init):
    o = o0 - lam * o1
    return o * lax.rsqrt(jnp.mean(o * o, axis=-1, keepdims=True) + EPS) * g * (1.0 - lam_init)


def _diff_prompt_kernel(qi_ref, kj_ref, q_ref, k_ref, v_ref, lam_ref, g_ref, o_ref,
                        m_scr, l_scr, acc_scr, *, lam_init, tq):
    n = pl.program_id(2)
    qi = qi_ref[n]
    kj = kj_ref[n]

    @pl.when(kj == 0)
    def _():
        m_scr[...] = jnp.full(m_scr.shape, NEG_INF, F32)
        l_scr[...] = jnp.zeros(l_scr.shape, F32)
        acc_scr[...] = jnp.zeros(acc_scr.shape, F32)

    q = q_ref[...]
    k = k_ref[...]
    v = v_ref[...]
    lane = lax.broadcasted_iota(I32, (1, LANES), 1)
    qpos = qi * tq + lax.broadcasted_iota(I32, (tq, tq), 0)
    kpos = kj * tq + lax.broadcasted_iota(I32, (tq, tq), 1)
    vis = (kpos // CHUNK) <= (qpos // CHUNK)
    for c in range(2):
        qc = jnp.where((lane < DIFF_DH) == (c == 0), q, jnp.zeros_like(q))
        s = jnp.where(vis, _dot_nt(qc, k), NEG_INF)
        m_prev = m_scr[c]
        m_new = jnp.maximum(m_prev, jnp.max(s, axis=1, keepdims=True))
        alpha = jnp.exp(m_prev - m_new)
        p = jnp.exp(s - m_new)
        l_scr[c] = alpha * l_scr[c] + jnp.sum(p, axis=1, keepdims=True)
        acc_scr[c] = alpha * acc_scr[c] + _dot(p.astype(BF16), v)
        m_scr[c] = m_new

    @pl.when(kj == qi)
    def _():
        lam = _diff_lambda(lam_ref, lam_init)
        o_ref[...] = _diff_finish(acc_scr[0] / l_scr[0], acc_scr[1] / l_scr[1], lam, g_ref[...],
                                  lam_init).astype(BF16)


def _diff_prompt(dq, dk, dv, lamv, g, B, T, lam_init):
    M = B * T
    tq = min(T, 512)
    nq = T // tq
    pairs = [(i, j) for i in range(nq) for j in range(i + 1)]
    qi = jnp.asarray([p[0] for p in pairs], I32)
    kj = jnp.asarray([p[1] for p in pairs], I32)
    grid_spec = pltpu.PrefetchScalarGridSpec(
        num_scalar_prefetch=2,
        grid=(B, DIFF_HEADS, len(pairs)),
        in_specs=[pl.BlockSpec((tq, LANES), lambda b, h, n, qi, kj: (b * nq + qi[n], h)),
                  pl.BlockSpec((tq, LANES), lambda b, h, n, qi, kj: (b * nq + kj[n], h)),
                  pl.BlockSpec((tq, LANES), lambda b, h, n, qi, kj: (b * nq + kj[n], h)),
                  pl.BlockSpec((4, DIFF_DH), lambda b, h, n, qi, kj: (0, 0)),
                  pl.BlockSpec((1, LANES), lambda b, h, n, qi, kj: (0, 0))],
        out_specs=pl.BlockSpec((tq, LANES), lambda b, h, n, qi, kj: (b * nq + qi[n], h)),
        scratch_shapes=[pltpu.VMEM((2, tq, 1), F32), pltpu.VMEM((2, tq, 1), F32),
                        pltpu.VMEM((2, tq, LANES), F32)])
    return pl.pallas_call(
        functools.partial(_diff_prompt_kernel, lam_init=lam_init, tq=tq),
        out_shape=jax.ShapeDtypeStruct((M, 512), BF16),
        grid_spec=grid_spec,
        compiler_params=_cparams(("arbitrary", "arbitrary", "arbitrary")),
        name="diff_prompt",
    )(qi, kj, dq, dk, dv, lamv, g)


def _diff_sample_kernel(q_ref, kn_ref, vn_ref, kp_ref, vp_ref, lam_ref, g_ref, o_ref, *, lam_init, past_len):
    T = q_ref.shape[0]
    lam = _diff_lambda(lam_ref, lam_init)
    lane = lax.broadcasted_iota(I32, (1, LANES), 1)
    qpos = past_len + lax.broadcasted_iota(I32, (T, T), 0)
    kpos = past_len + lax.broadcasted_iota(I32, (T, T), 1)
    vis_new = (kpos // CHUNK) <= (qpos // CHUNK)
    for h in range(DIFF_HEADS):
        hs = slice(h * LANES, (h + 1) * LANES)
        q = q_ref[:, hs]
        kp = kp_ref[:, hs].astype(BF16)
        vp = vp_ref[:, hs].astype(BF16)
        kn = kn_ref[:, hs]
        vn = vn_ref[:, hs]
        outs = []
        for c in range(2):
            qc = jnp.where((lane < DIFF_DH) == (c == 0), q, jnp.zeros_like(q))
            sp = _dot_nt(qc, kp)
            sn = jnp.where(vis_new, _dot_nt(qc, kn), NEG_INF)
            m = jnp.maximum(jnp.max(sp, axis=1, keepdims=True), jnp.max(sn, axis=1, keepdims=True))
            pp = jnp.exp(sp - m)
            pn = jnp.exp(sn - m)
            l = jnp.sum(pp, axis=1, keepdims=True) + jnp.sum(pn, axis=1, keepdims=True)
            outs.append((_dot(pp.astype(BF16), vp) + _dot(pn.astype(BF16), vn)) / l)
        o_ref[:, hs] = _diff_finish(outs[0], outs[1], lam, g_ref[...], lam_init).astype(BF16)


def _diff_sample(dq, dk, dv, past_k, past_v, layer, lamv, g, B, T, lam_init):
    M = B * T
    P = past_k.shape[2]
    return pl.pallas_call(
        functools.partial(_diff_sample_kernel, lam_init=lam_init, past_len=P),
        out_shape=jax.ShapeDtypeStruct((M, 512), BF16),
        grid=(B,),
        in_specs=[pl.BlockSpec((T, 512), lambda b: (b, 0)),
                  pl.BlockSpec((T, 512), lambda b: (b, 0)),
                  pl.BlockSpec((T, 512), lambda b: (b, 0)),
                  pl.BlockSpec((None, None, P, 512), lambda b: (layer, b, 0, 0)),
                  pl.BlockSpec((None, None, P, 512), lambda b: (layer, b, 0, 0)),
                  pl.BlockSpec((4, DIFF_DH), lambda b: (0, 0)),
                  pl.BlockSpec((1, LANES), lambda b: (0, 0))],
        out_specs=pl.BlockSpec((T, 512), lambda b: (b, 0)),
        compiler_params=_cparams(("arbitrary",)),
        name="diff_sample",
    )(dq, dk, dv, past_k, past_v, lamv, g)


INT_MIN = -2 ** 31


def _idx_score(iq, iw, ikr):
    lane = lax.broadcasted_iota(I32, (1, IDX_HEADS * IDX_DH), 1)
    sc = None
    for i in range(IDX_HEADS):
        lg = _dot_nt(jnp.where((lane // IDX_DH) == i, iq, jnp.zeros_like(iq)), ikr)
        t = iw[:, i:i + 1] * jnp.maximum(lg, 0.0)
        sc = t if sc is None else sc + t
    return jnp.where(sc == 0.0, 0.0, sc)


def _order_key(score):
    bits = pltpu.bitcast(score, I32)
    return jnp.where(bits < 0, bits ^ 0x7FFFFFFF, bits)


def _count(mask):
    return jnp.sum(mask.astype(F32), axis=1, keepdims=True)


def _topk_select(key_refs, bases, n_sel, n_keys):
    tq = key_refs[0].shape[0]
    n = float(n_sel)

    def count_ge(cand):
        tot = None
        for kr in key_refs:
            c = _count(kr[...] >= cand)
            tot = c if tot is None else tot + c
        return tot

    t0 = jnp.where(count_ge(jnp.zeros((tq, 1), I32)) >= n, 0, INT_MIN).astype(I32)

    def vbody(i, t):
        cand = t | jnp.left_shift(jnp.int32(1), 30 - i)
        return jnp.where(count_ge(cand) >= n, cand, t)

    t = lax.fori_loop(0, 31, vbody, t0)

    c_gt = None
    for kr in key_refs:
        c = _count(kr[...] > t)
        c_gt = c if c_gt is None else c_gt + c
    need = n - c_gt
    nbits = max(1, int(n_keys).bit_length())

    def ibody(i, J):
        cand = J | jnp.left_shift(jnp.int32(1), nbits - 1 - i)
        f = None
        for kr, base in zip(key_refs, bases):
            kidx = base + lax.broadcasted_iota(I32, kr.shape, 1)
            c = _count((kr[...] == t) & (kidx < cand))
            f = c if f is None else f + c
        return jnp.where(f <= need, cand, J)

    J = lax.fori_loop(0, nbits, ibody, jnp.zeros((tq, 1), I32))
    return t, J


def _dsa_attend(q_ref, bias_refs, k_loads, v_loads, o_ref):
    for h in range(DSA_HEADS):
        hs = slice(h * DSA_DH, (h + 1) * DSA_DH)
        q = q_ref[:, hs]
        ss = [_dot_nt(q, kl(hs)) + br[...] for kl, br in zip(k_loads, bias_refs)]
        m = None
        for s in ss:
            ms = jnp.max(s, axis=1, keepdims=True)
            m = ms if m is None else jnp.maximum(m, ms)
        l = None
        o = None
        for s, vl in zip(ss, v_loads):
            p = jnp.exp(s - m)
            ls = jnp.sum(p, axis=1, keepdims=True)
            os_ = _dot(p.astype(BF16), vl(hs))
            l = ls if l is None else l + ls
            o = os_ if o is None else o + os_
        o_ref[:, hs] = (o / l).astype(BF16)


def _dsa_prompt_kernel(q_ref, iq_ref, iw_ref, k_ref, v_ref, ikr_ref, o_ref, key_scr, bias_scr,
                       *, tq, q_tile0, n_sel):
    tk = k_ref.shape[0]
    qt = q_tile0 + pl.program_id(1)
    score = _idx_score(iq_ref[...], iw_ref[...], ikr_ref[...])
    qpos = qt * tq + lax.broadcasted_iota(I32, (tq, tk), 0)
    kidx = lax.broadcasted_iota(I32, (tq, tk), 1)
    vis = (kidx // CHUNK) <= (qpos // CHUNK)
    key_scr[...] = _order_key(jnp.where(vis, score, NEG_INF))
    t, J = _topk_select([key_scr], [0], n_sel, tk)
    key = key_scr[...]
    sel = vis & ((key > t) | ((key == t) & (kidx < J)))
    bias_scr[...] = jnp.where(sel, 0.0, NEG_INF)
    _dsa_attend(q_ref, [bias_scr], [lambda hs: k_ref[:, hs]], [lambda hs: v_ref[:, hs]], o_ref)


def _dsa_prompt(cq, iq, iw, ck, cv, ikr, B, T):
    tq = min(T, 128)
    nq = T // tq
    n_sel = min(DSA_TOPK_MAX, T // 4)
    ng = 4 if nq % 4 == 0 else 1
    tpg = nq // ng
    outs = []
    for gi in range(ng):
        tk = (gi + 1) * tpg * tq
        nkb = T // tk if T % tk == 0 else None

        def kv_spec(w, tk=tk):
            return pl.BlockSpec((None, tk, w), lambda b, i: (b, 0, 0))

        def q_spec(w, gi=gi):
            return pl.BlockSpec((None, tq, w), lambda b, i: (b, gi * tpg + i, 0))

        out = pl.pallas_call(
            functools.partial(_dsa_prompt_kernel, tq=tq, q_tile0=gi * tpg, n_sel=n_sel),
            out_shape=jax.ShapeDtypeStruct((B, tpg * tq, 512), BF16),
            grid=(B, tpg),
            in_specs=[q_spec(512), q_spec(256), q_spec(LANES), kv_spec(512), kv_spec(512), kv_spec(256)],
            out_specs=pl.BlockSpec((None, tq, 512), lambda b, i: (b, i, 0)),
            scratch_shapes=[pltpu.VMEM((tq, tk), I32), pltpu.VMEM((tq, tk), F32)],
            compiler_params=_cparams(("arbitrary", "arbitrary"), 56 * 1024 * 1024),
            name=f"dsa_prompt_{gi}",
        )(cq.reshape(B, T, 512), iq.reshape(B, T, 256), iw.reshape(B, T, LANES),
          ck.reshape(B, T, 512), cv.reshape(B, T, 512), ikr.reshape(B, T, 256))
        outs.append(out)
    o = outs[0] if ng == 1 else jnp.concatenate(outs, axis=1)
    return o.reshape(B * T, 512)


def _dsa_sample_kernel(q_ref, iq_ref, iw_ref, kn_ref, vn_ref, ikrn_ref, kp_ref, vp_ref, ikp_ref, o_ref,
                       keyp_scr, keyn_scr, biasp_scr, biasn_scr, *, past_len, n_sel):
    T = q_ref.shape[0]
    P = past_len
    iq = iq_ref[...]
    iw = iw_ref[...]
    ikp = ikp_ref[...]
    lane = lax.broadcasted_iota(I32, (P, LANES), 1)
    ikp2 = jnp.concatenate([ikp, ikp], axis=1)
    ikrp = jnp.concatenate([ikp2, ikp2], axis=1).astype(BF16)
    del lane
    keyp_scr[...] = _order_key(_idx_score(iq, iw, ikrp))
    qpos = P + lax.broadcasted_iota(I32, (T, T), 0)
    kpos = P + lax.broadcasted_iota(I32, (T, T), 1)
    vis_n = (kpos // CHUNK) <= (qpos // CHUNK)
    keyn_scr[...] = _order_key(jnp.where(vis_n, _idx_score(iq, iw, ikrn_ref[...]), NEG_INF))
    t, J = _topk_select([keyp_scr, keyn_scr], [0, P], n_sel, P + T)
    kp_ = keyp_scr[...]
    kidx_p = lax.broadcasted_iota(I32, (T, P), 1)
    biasp_scr[...] = jnp.where((kp_ > t) | ((kp_ == t) & (kidx_p < J)), 0.0, NEG_INF)
    kn_ = keyn_scr[...]
    biasn_scr[...] = jnp.where(vis_n & ((kn_ > t) | ((kn_ == t) & (kpos < J))), 0.0, NEG_INF)
    _dsa_attend(q_ref, [biasp_scr, biasn_scr],
                [lambda hs: kp_ref[:, hs].astype(BF16), lambda hs: kn_ref[:, hs]],
                [lambda hs: vp_ref[:, hs].astype(BF16), lambda hs: vn_ref[:, hs]], o_ref)


def _dsa_sample(cq, iq, iw, ck, cv, ikr, past_k, past_v, past_ik, layer, B, T):
    M = B * T
    P = past_k.shape[2]
    n_sel = min(DSA_TOPK_MAX, (P + T) // 4)

    def rspec(w):
        return pl.BlockSpec((T, w), lambda b: (b, 0))

    def pspec(w):
        return pl.BlockSpec((None, None, P, w), lambda b: (layer, b, 0, 0))

    return pl.pallas_call(
        functools.partial(_dsa_sample_kernel, past_len=P, n_sel=n_sel),
        out_shape=jax.ShapeDtypeStruct((M, 512), BF16),
        grid=(B,),
        in_specs=[rspec(512), rspec(256), rspec(LANES), rspec(512), rspec(512), rspec(256),
                  pspec(512), pspec(512), pspec(IDX_DH)],
        out_specs=rspec(512),
        scratch_shapes=[pltpu.VMEM((T, P), I32), pltpu.VMEM((T, T), I32),
                        pltpu.VMEM((T, P), F32), pltpu.VMEM((T, T), F32)],
        compiler_params=_cparams(("arbitrary",)),
        name="dsa_sample",
    )(cq, iq, iw, ck, cv, ikr, past_k, past_v, past_ik)


def _merge_kernel(og_ref, od_ref, oc_ref, g0_ref, g1_ref, g2_ref, x_ref, m_ref,
                  wg_ref, wd_ref, wc_ref, wo_ref, gn_ref, wr_ref, br_ref,
                  x1_ref, h2_ref, ti_ref, tw_ref, *, nb):
    merged = (jax.nn.sigmoid(g0_ref[...]) * _dot(og_ref[...], wg_ref[...])
              + jax.nn.sigmoid(g1_ref[...]) * _dot(od_ref[...], wd_ref[...])
              + jax.nn.sigmoid(g2_ref[...]) * _dot(oc_ref[...], wc_ref[...]))
    mix = _dot(merged.astype(BF16), wo_ref[...])
    m = m_ref[...]
    x1 = x_ref[...] + _gate_rows(mix, m, 2, nb)
    x1_ref[...] = x1
    xn = x1 * lax.rsqrt(jnp.mean(x1 * x1, axis=-1, keepdims=True) + EPS) * gn_ref[...]
    h2 = _modulate(xn, m, 3, 4, nb)
    h2_ref[...] = h2
    lg = _dot(h2.astype(BF16), wr_ref[...]) + br_ref[...]
    tm = lg.shape[0]
    lane = lax.broadcasted_iota(I32, (tm, LANES), 1).astype(F32)
    vals, idxs = [], []
    for _ in range(TOP_K):
        mx = jnp.max(lg, axis=1, keepdims=True)
        ix = jnp.min(jnp.where(lg == mx, lane, float(LANES)), axis=1, keepdims=True)
        vals.append(mx)
        idxs.append(ix)
        lg = jnp.where(lane == ix, -jnp.inf, lg)
    es = [jnp.exp(v - vals[0]) for v in vals]
    den = es[0] + es[1] + es[2] + es[3]
    ti = jnp.zeros((tm, LANES), F32)
    tw = jnp.zeros((tm, LANES), F32)
    for r in range(TOP_K):
        ti = jnp.where(lane == float(r), idxs[r], ti)
        tw = jnp.where(lane == float(r), es[r] / den, tw)
    ti_ref[...] = ti.astype(I32)
    tw_ref[...] = tw


def _merge(og, od, oc, z, x2, m, wts, B, T):
    M = B * T
    tm, nb = _row_tiling(B, T, 512)
    wg, wd, wc, wo, gn, wr, br = wts

    def rspec(w):
        return pl.BlockSpec((tm, w), lambda i: (i, 0))

    def gspec(k):
        return pl.BlockSpec((tm, D_MODEL), lambda i: (i, Z_GATES // D_MODEL + k))

    def wspec(r, c):
        return pl.BlockSpec((r, c), lambda i: (0, 0))

    return pl.pallas_call(
        functools.partial(_merge_kernel, nb=nb),
        out_shape=[jax.ShapeDtypeStruct((M, D_MODEL), F32), jax.ShapeDtypeStruct((M, D_MODEL), F32),
                   jax.ShapeDtypeStruct((M, LANES), I32), jax.ShapeDtypeStruct((M, LANES), F32)],
        grid=(M // tm,),
        in_specs=[rspec(512), rspec(512), rspec(512), gspec(0), gspec(1), gspec(2), rspec(D_MODEL),
                  pl.BlockSpec((nb, 6, D_MODEL), lambda i: ((i * tm) // (T * nb), 0, 0)),
                  wspec(512, D_MODEL), wspec(512, D_MODEL), wspec(512, D_MODEL), wspec(D_MODEL, D_MODEL),
                  wspec(1, D_MODEL), wspec(D_MODEL, LANES), wspec(1, LANES)],
        out_specs=[rspec(D_MODEL), rspec(D_MODEL), rspec(LANES), rspec(LANES)],
        compiler_params=_cparams(("arbitrary",)),
        name="merge_router",
    )(og, od, oc, z, z, z, x2, m, wg, wd, wc, wo, gn, wr, br)


DMA_UNROLL = 8


def _w1prep_kernel(w_ref, sel_ref, g_ref, u_ref):
    sel = sel_ref[...]
    for c in range(w_ref.shape[1] // 256):
        r = _dot(w_ref[:, c * 256:(c + 1) * 256].astype(BF16), sel)
        g_ref[:, c * LANES:(c + 1) * LANES] = r[:, :LANES].astype(BF16)
        u_ref[:, c * LANES:(c + 1) * LANES] = r[:, LANES:].astype(BF16)


def _w1prep(w1):
    E = w1.shape[0]
    tr = 512
    j = jnp.arange(256, dtype=I32)
    src = jnp.where(j < LANES, 2 * j, 2 * (j - LANES) + 1)
    sel = (jnp.arange(256, dtype=I32)[:, None] == src[None, :]).astype(BF16)
    return pl.pallas_call(
        _w1prep_kernel,
        out_shape=[jax.ShapeDtypeStruct((E, D_MODEL, D_FF), BF16)] * 2,
        grid=(E, D_MODEL // tr),
        in_specs=[pl.BlockSpec((None, tr, 2 * D_FF), lambda e, r: (e, r, 0)),
                  pl.BlockSpec((256, 256), lambda e, r: (0, 0))],
        out_specs=[pl.BlockSpec((None, tr, D_FF), lambda e, r: (e, r, 0))] * 2,
        compiler_params=_cparams(("arbitrary", "arbitrary")),
        name="w1prep",
    )(w1, sel)


def _dispatch_kernel(pos_ref, h_ref, xs_in, xs_out, sem, *, tm):
    del xs_in

    def body(j, c):
        for u in range(DMA_UNROLL):
            n = j * DMA_UNROLL + u
            pltpu.make_async_copy(h_ref.at[pl.ds(n // TOP_K, 1), :],
                                  xs_out.at[pl.ds(pos_ref[0, 0, n], 1), :], sem).start()
        return c

    lax.fori_loop(0, TOP_K * tm // DMA_UNROLL, body, 0)
    for _ in range(TOP_K):
        pltpu.make_async_copy(h_ref, xs_out.at[pl.ds(0, tm), :], sem).wait()


def _dispatch(h2, pos, n_rows, tm):
    M = h2.shape[0]
    nt = M // tm
    return pl.pallas_call(
        functools.partial(_dispatch_kernel, tm=tm),
        out_shape=jax.ShapeDtypeStruct((n_rows, D_MODEL), F32),
        grid=(nt,),
        in_specs=[pl.BlockSpec((1, 1, TOP_K * tm), lambda i: (i, 0, 0), memory_space=pltpu.SMEM),
                  pl.BlockSpec((tm, D_MODEL), lambda i: (i, 0)),
                  pl.BlockSpec(memory_space=pl.ANY)],
        out_specs=pl.BlockSpec(memory_space=pl.ANY),
        scratch_shapes=[pltpu.SemaphoreType.DMA],
        input_output_aliases={2: 0},
        compiler_params=_cparams(("arbitrary",)),
        name="moe_dispatch",
    )(pos.reshape(nt, 1, TOP_K * tm), h2, jnp.zeros((n_rows, D_MODEL), F32))


def _ffn_kernel(be_ref, nu_ref, x_ref, w1g_ref, b1g_ref, w1u_ref, b1u_ref, w2_ref, b2_ref, o_ref):
    @pl.when(pl.program_id(0) < nu_ref[0])
    def _():
        x = x_ref[...].astype(BF16)
        g = jnp.minimum(_dot(x, w1g_ref[...]) + b1g_ref[...], SWIGLU_LIMIT)
        u = jnp.clip(_dot(x, w1u_ref[...]) + b1u_ref[...], -SWIGLU_LIMIT, SWIGLU_LIMIT)
        a = g * jax.nn.sigmoid(SWIGLU_ALPHA * g) * (u + 1.0)
        o_ref[...] = _dot(a.astype(BF16), w2_ref[...]) + b2_ref[...]

    @pl.when(pl.program_id(0) >= nu_ref[0])
    def _():
        o_ref[...] = jnp.zeros(o_ref.shape, F32)


def _ffn(xs, block_e, n_used, w1g, b1g, w1u, b1u, w2, b2, bm):
    n_rows = xs.shape[0]
    nblk = n_rows // bm

    def wspec(r, c):
        return pl.BlockSpec((None, r, c), lambda i, be, nu: (be[i], 0, 0))

    grid_spec = pltpu.PrefetchScalarGridSpec(
        num_scalar_prefetch=2,
        grid=(nblk,),
        in_specs=[pl.BlockSpec((bm, D_MODEL), lambda i, be, nu: (jnp.minimum(i, nu[0] - 1), 0)),
                  wspec(D_MODEL, D_FF), wspec(1, D_FF), wspec(D_MODEL, D_FF), wspec(1, D_FF),
                  wspec(D_FF, D_MODEL), wspec(1, D_MODEL)],
        out_specs=pl.BlockSpec((bm, D_MODEL), lambda i, be, nu: (i, 0)))
    return pl.pallas_call(
        _ffn_kernel,
        out_shape=jax.ShapeDtypeStruct((n_rows, D_MODEL), F32),
        grid_spec=grid_spec,
        compiler_params=_cparams(("arbitrary",)),
        name="moe_ffn",
    )(block_e, n_used, xs, w1g, b1g, w1u, b1u, w2, b2)


def _combine_kernel(pos_ref, ys_hbm, tw_ref, x_ref, m_ref, o_ref, ybuf, sem, *, tm, nb):
    def body(j, c):
        for u in range(DMA_UNROLL):
            n = j * DMA_UNROLL + u
            pltpu.make_async_copy(ys_hbm.at[pl.ds(pos_ref[0, 0, n], 1), :], ybuf.at[pl.ds(n, 1), :], sem).start()
        return c

    lax.fori_loop(0, TOP_K * tm // DMA_UNROLL, body, 0)
    pltpu.make_async_copy(ys_hbm.at[pl.ds(0, TOP_K * tm), :], ybuf, sem).wait()
    tw = tw_ref[...]
    y = None
    for k in range(TOP_K):
        yk = tw[:, k:k + 1] * ybuf[k * tm:(k + 1) * tm, :]
        y = yk if y is None else y + yk
    o_ref[...] = x_ref[...] + _gate_rows(y, m_ref[...], 5, nb)


def _combine(pos_t, ys, top_w, x1, m, B, T, tm, nb):
    M = B * T
    nt = M // tm
    return pl.pallas_call(
        functools.partial(_combine_kernel, tm=tm, nb=nb),
        out_shape=jax.ShapeDtypeStruct((M, D_MODEL), F32),
        grid=(nt,),
        in_specs=[pl.BlockSpec((1, 1, TOP_K * tm), lambda i: (i, 0, 0), memory_space=pltpu.SMEM),
                  pl.BlockSpec(memory_space=pl.ANY),
                  pl.BlockSpec((tm, LANES), lambda i: (i, 0)),
                  pl.BlockSpec((tm, D_MODEL), lambda i: (i, 0)),
                  pl.BlockSpec((nb, 6, D_MODEL), lambda i: ((i * tm) // (T * nb), 0, 0))],
        out_specs=pl.BlockSpec((tm, D_MODEL), lambda i: (i, 0)),
        scratch_shapes=[pltpu.VMEM((TOP_K * tm, D_MODEL), F32), pltpu.SemaphoreType.DMA],
        compiler_params=_cparams(("arbitrary",)),
        name="moe_combine",
    )(pos_t.reshape(nt, 1, TOP_K * tm), ys, top_w, x1, m)


def _route(top_i, bm):
    N = top_i.shape[0]
    NK = N * TOP_K
    flat_e = top_i.reshape(NK)
    onehot = (flat_e[:, None] == jnp.arange(N_EXPERTS, dtype=I32)[None, :]).astype(I32)
    csum = jnp.cumsum(onehot, axis=0)
    counts = csum[-1]
    padded = (counts + bm - 1) // bm * bm
    pad_end = jnp.cumsum(padded)
    pad_start = pad_end - padded
    pos = jnp.sum(onehot * (csum - 1 + pad_start[None, :]), axis=1)
    n_rows = (-(-NK // bm)) * bm + N_EXPERTS * bm
    nblk = n_rows // bm
    block_e = jnp.minimum(
        jnp.searchsorted(pad_end, jnp.arange(nblk, dtype=I32) * bm, side='right'), N_EXPERTS - 1).astype(I32)
    n_used = (pad_end[-1:] // bm).astype(I32)
    return pos.astype(I32), n_rows, block_e, n_used


def _moe(h2, top_i, top_w, x1, m, ew, B, T):
    M = B * T
    bm = 256 if M * TOP_K >= 256 * N_EXPERTS * 4 else 128
    pos, n_rows, block_e, n_used = _route(top_i[:, :TOP_K], bm)
    tm, nb = _row_tiling(B, T, 256)
    xs = _dispatch(h2, pos, n_rows, tm)
    ys = _ffn(xs, block_e, n_used, *ew, bm)
    pos_t = pos.reshape(M // tm, tm, TOP_K).transpose(0, 2, 1)
    return _combine(pos_t, ys, top_w, x1, m, B, T, tm, nb)


def _prep_layer(P, l):
    w_in = P['w_in'][l]

    def cols(name):
        o, s = _SRC[name]
        return w_in[:, o:o + s]

    zeros = lambda n: jnp.zeros((D_MODEL, n), F32)
    w_r = jnp.concatenate([
        cols('gla_q'), cols('gla_k'), cols('gla_v'), cols('gla_r'),
        cols('diff_q'), cols('diff_k'), cols('diff_v'),
        cols('dsa_q'), cols('dsa_k'), cols('dsa_v'),
        cols('idx_q'), cols('idx_k'), cols('idx_w'), zeros(60), cols('gla_a'), zeros(112),
        cols('gates')], axis=1).astype(BF16)
    assert w_r.shape[1] == Z_WIDTH
    W = GLA_HEADS * GLA_DK
    wa = jnp.zeros((LANES, W), F32).at[:GLA_GATE_RANK].set(P['w_gla_a2'][l]).astype(BF16)
    w1g, w1u = _w1prep(P['w_mlp1'][l])
    b1 = P['b_mlp1'][l]
    return dict(
        w_in=w_r, wa=wa, ba=P['b_gla_a2'][l].reshape(1, W),
        g_gla=jnp.tile(P['g_gla_out'][l], GLA_HEADS).reshape(1, 512),
        gains=(jnp.tile(P['g_diff_q'][l], 8).reshape(1, 512), jnp.tile(P['g_diff_k'][l], 8).reshape(1, 512),
               jnp.tile(P['g_dsa_q'][l], 4).reshape(1, 512), jnp.tile(P['g_dsa_k'][l], 4).reshape(1, 512),
               jnp.tile(P['g_idx_k'][l], 2).reshape(1, LANES)),
        lamv=jnp.stack([P['lambda_q1'][l], P['lambda_k1'][l], P['lambda_q2'][l], P['lambda_k2'][l]]),
        g_diff=P['g_diff_out'][l].reshape(1, LANES),
        merge=(P['w_branch_gla'][l].astype(BF16), P['w_branch_diff'][l].astype(BF16),
               P['w_branch_dsa'][l].astype(BF16), P['w_out'][l].astype(BF16),
               P['g_norm2'][l].reshape(1, D_MODEL),
               jnp.zeros((D_MODEL, LANES), F32).at[:, :N_EXPERTS].set(P['w_router'][l]).astype(BF16),
               jnp.full((1, LANES), NEG_INF, F32).at[0, :N_EXPERTS].set(P['b_router'][l])),
        experts=(w1g, b1[:, None, 0::2], w1u, b1[:, None, 1::2],
                 P['w_mlp2'][l].astype(BF16), P['b_mlp2'][l][:, None, :]),
    )


def _trunk(x, c, past, P, prepped):
    B, T, _ = x.shape
    M = B * T
    x2 = x.reshape(M, D_MODEL)
    past_len = 0 if past is None else past[0].shape[2]
    pos = past_len + jnp.arange(T, dtype=I32)
    tabs64 = _rope_tables(pos, 64)
    tabs128 = _rope_tables(pos, 128)
    W = GLA_HEADS * GLA_DK
    if past is not None:
        pdk = past[0].reshape(past[0].shape[:3] + (512,))
        pdv = past[1].reshape(past[1].shape[:3] + (512,))
        pck = past[2].reshape(past[2].shape[:3] + (512,))
        pcv = past[3].reshape(past[3].shape[:3] + (512,))
        pik = past[4]
        s0_all = past[5].transpose(0, 1, 4, 2, 3).reshape(past[5].shape[0], B, GLA_DV, W)
    per_layer = []
    for l, pp in enumerate(prepped):
        lam_init = 0.8 - 0.6 * math.exp(-0.3 * l)
        m = _ada(c, P['w_ada'][l], P['b_ada'][l])
        z = _inproj(x2, m, P['g_norm1'][l], pp['w_in'], B, T)
        (dq, dk32, dkb, dvb, cq, ck32, ckb, cvb, iq, ik32, ikr, iw) = _post(z, tabs64, tabs128, pp['gains'], B, T)
        s0t = jnp.zeros((B, GLA_DV, W), F32) if past is None else s0_all[l]
        o_gla, st = _gla(z, pp['wa'], pp['ba'], pp['g_gla'], s0t, B, T)
        if past is None:
            o_diff = _diff_prompt(dq, dkb, dvb, pp['lamv'], pp['g_diff'], B, T, lam_init)
            o_dsa = _dsa_prompt(cq, iq, iw, ckb, cvb, ikr, B, T)
        else:
            o_diff = _diff_sample(dq, dkb, dvb, pdk, pdv, l, pp['lamv'], pp['g_diff'], B, T, lam_init)
            o_dsa = _dsa_sample(cq, iq, iw, ckb, cvb, ikr, pck, pcv, pik, l, B, T)
        x1, h2, top_i, top_w = _merge(o_gla, o_diff, o_dsa, z, x2, m, pp['merge'], B, T)
        x2 = _moe(h2, top_i, top_w, x1, m, pp['experts'], B, T)
        s_gla = st.reshape(B, GLA_DV, GLA_HEADS, GLA_DK).transpose(0, 2, 3, 1)
        per_layer.append((dk32.reshape(B, T, DIFF_HEADS, 2, DIFF_DH),
                          z[:, Z_DV:Z_DV + 512].reshape(B, T, DIFF_HEADS, DIFF_DV),
                          ck32.reshape(B, T, DSA_HEADS, DSA_DH),
                          z[:, Z_CV:Z_CV + 512].reshape(B, T, DSA_HEADS, DSA_DH),
                          ik32.reshape(B, T, IDX_DH),
                          s_gla))
    stacked = tuple(jnp.stack([st[i] for st in per_layer]) for i in range(6))
    return x2.reshape(B, T, D_MODEL), stacked


def kernel(x_prompt, x_sample, cache_diff_k, cache_diff_v, cache_dsa_k, cache_dsa_v, cache_dsa_idx_k,
           state_gla, c_prompt, c_sample, w_ada, b_ada, g_norm1, g_norm2, w_in, w_gla_a2, b_gla_a2,
           g_gla_out, g_diff_q, g_diff_k, lambda_q1, lambda_k1, lambda_q2, lambda_k2, g_diff_out,
           g_dsa_q, g_dsa_k, g_idx_k, w_branch_gla, w_branch_diff, w_branch_dsa, w_out, w_router,
           b_router, w_mlp1, b_mlp1, w_mlp2, b_mlp2):
    P = dict(w_ada=w_ada, b_ada=b_ada, g_norm1=g_norm1, g_norm2=g_norm2, w_in=w_in, w_gla_a2=w_gla_a2,
             b_gla_a2=b_gla_a2, g_gla_out=g_gla_out, g_diff_q=g_diff_q, g_diff_k=g_diff_k,
             lambda_q1=lambda_q1, lambda_k1=lambda_k1, lambda_q2=lambda_q2, lambda_k2=lambda_k2,
             g_diff_out=g_diff_out, g_dsa_q=g_dsa_q, g_dsa_k=g_dsa_k, g_idx_k=g_idx_k,
             w_branch_gla=w_branch_gla, w_branch_diff=w_branch_diff, w_branch_dsa=w_branch_dsa,
             w_out=w_out, w_router=w_router, b_router=b_router, w_mlp1=w_mlp1, b_mlp1=b_mlp1,
             w_mlp2=w_mlp2, b_mlp2=b_mlp2)
    depth = w_in.shape[0]
    prepped = [_prep_layer(P, l) for l in range(depth)]
    y_prompt, new_p = _trunk(x_prompt, c_prompt, None, P, prepped)
    y_sample, new_s = _trunk(
        x_sample, c_sample,
        (cache_diff_k, cache_diff_v, cache_dsa_k, cache_dsa_v, cache_dsa_idx_k, state_gla), P, prepped)
    return (y_prompt, y_sample) + new_p + new_s
```

```python
import functools
import math

import jax
import jax.numpy as jnp
from jax import lax
from jax.experimental import pallas as pl
from jax.experimental.pallas import tpu as pltpu

F32 = jnp.float32
BF16 = jnp.bfloat16
I32 = jnp.int32

D_MODEL = 1024
CHUNK = 64
ROPE_THETA = 500000.0
ROPE_FRACTION = 4
EPS = 1e-6
NEG_INF = -1e30

GLA_HEADS = 4
GLA_DK = 64
GLA_DV = 128
GLA_GATE_RANK = 16
GLA_GATE_TAU = 16.0
GLA_SUB = 16
DIFF_HEADS = 4
DIFF_DH = 64
DIFF_DV = 128
DSA_HEADS = 4
DSA_DH = 128
IDX_HEADS = 4
IDX_DH = 64
DSA_TOPK_MAX = 256
N_EXPERTS = 32
TOP_K = 4
D_FF = 1024
SWIGLU_LIMIT = 7.0
SWIGLU_ALPHA = 1.702

LANES = 128
ROW_TILE = 8
VMEM_LIMIT = 48 * 1024 * 1024

Z_GQK, Z_GV, Z_GR = 0, 512, 1024
Z_DQ, Z_DK, Z_DV = 1536, 2048, 2560
Z_CQ, Z_CK, Z_CV = 3072, 3584, 4096
Z_MISC = 4608
Z_GATES = 5120
Z_WIDTH = 8192
MISC_IK = 256
MISC_GA = 384

_SRC = {}
_off = 0
for _name, _size in (
        ('gla_q', 256), ('gla_k', 256), ('gla_v', 512), ('gla_a', 16), ('gla_r', 512),
        ('diff_q', 512), ('diff_k', 512), ('diff_v', 512),
        ('dsa_q', 512), ('dsa_k', 512), ('dsa_v', 512),
        ('idx_q', 256), ('idx_k', 64), ('idx_w', 4), ('gates', 3072)):
    _SRC[_name] = (_off, _size)
    _off += _size


def _cparams(sem, vmem=VMEM_LIMIT):
    return pltpu.CompilerParams(dimension_semantics=sem, vmem_limit_bytes=vmem)


def _dot(a, b):
    return jnp.dot(a, b, preferred_element_type=F32)


def _dot_nt(a, b):
    return lax.dot_general(a, b, (((1,), (1,)), ((), ())), preferred_element_type=F32)


def _dot_tn(a, b):
    return lax.dot_general(a, b, (((0,), (0,)), ((), ())), preferred_element_type=F32)


def _silu(x):
    return x * jax.nn.sigmoid(x)


def _row_tiling(B, T, target):
    if T >= target:
        assert T % target == 0
        return target, 1
    nb = 1
    for cand in range(1, B + 1):
        if B % cand == 0 and cand * T <= target:
            nb = cand
    return nb * T, nb


def _modulate(xn, m, shift_i, scale_i, nb):
    tm = xn.shape[0]
    if nb == 1:
        return xn * (1.0 + m[0, scale_i:scale_i + 1, :]) + m[0, shift_i:shift_i + 1, :]
    x3 = xn.reshape(nb, tm // nb, D_MODEL)
    h = x3 * (1.0 + m[:, scale_i:scale_i + 1, :]) + m[:, shift_i:shift_i + 1, :]
    return h.reshape(tm, D_MODEL)


def _gate_rows(y, m, gate_i, nb):
    tm = y.shape[0]
    if nb == 1:
        return y * m[0, gate_i:gate_i + 1, :]
    return (y.reshape(nb, tm // nb, D_MODEL) * m[:, gate_i:gate_i + 1, :]).reshape(tm, D_MODEL)


def _ada_kernel(c_ref, w_ref, b_ref, o_ref):
    s = _silu(c_ref[...])
    o_ref[...] = _dot(s.astype(BF16), w_ref[...].astype(BF16)) + b_ref[...]


def _ada(c, w, b):
    B = c.shape[0]
    out = pl.pallas_call(
        _ada_kernel,
        out_shape=jax.ShapeDtypeStruct((B, 6 * D_MODEL), F32),
        grid=(6,),
        in_specs=[pl.BlockSpec((B, D_MODEL), lambda j: (0, 0)),
                  pl.BlockSpec((D_MODEL, D_MODEL), lambda j: (0, j)),
                  pl.BlockSpec((1, D_MODEL), lambda j: (0, j))],
        out_specs=pl.BlockSpec((B, D_MODEL), lambda j: (0, j)),
        compiler_params=_cparams(("arbitrary",)),
        name="ada",
    )(c, w, b.reshape(1, -1))
    return out.reshape(B, 6, D_MODEL)


def _inproj_kernel(x_ref, m_ref, g_ref, w_ref, o_ref, h_scr, *, nb):
    @pl.when(pl.program_id(1) == 0)
    def _():
        x = x_ref[...]
        xn = x * lax.rsqrt(jnp.mean(x * x, axis=-1, keepdims=True) + EPS) * g_ref[...]
        h_scr[...] = _modulate(xn, m_ref[...], 0, 1, nb).astype(BF16)

    o_ref[...] = _dot(h_scr[...], w_ref[...])


def _inproj(x2, m, g, w, B, T):
    M = B * T
    tm, nb = _row_tiling(B, T, 1024)
    tn = 1024
    return pl.pallas_call(
        functools.partial(_inproj_kernel, nb=nb),
        out_shape=jax.ShapeDtypeStruct((M, Z_WIDTH), F32),
        grid=(M // tm, Z_WIDTH // tn),
        in_specs=[pl.BlockSpec((tm, D_MODEL), lambda i, j: (i, 0)),
                  pl.BlockSpec((nb, 6, D_MODEL), lambda i, j: ((i * tm) // (T * nb), 0, 0)),
                  pl.BlockSpec((1, D_MODEL), lambda i, j: (0, 0)),
                  pl.BlockSpec((D_MODEL, tn), lambda i, j: (0, j))],
        out_specs=pl.BlockSpec((tm, tn), lambda i, j: (i, j)),
        scratch_shapes=[pltpu.VMEM((tm, D_MODEL), BF16)],
        compiler_params=_cparams(("arbitrary", "arbitrary")),
        name="inproj",
    )(x2, m, g.reshape(1, -1), w)


def _rope_tables(pos, d):
    rot = d // ROPE_FRACTION
    half = rot // 2
    T = pos.shape[0]
    inv_freq = ROPE_THETA ** (-jnp.arange(half, dtype=F32) / half)
    ang = pos.astype(F32)[:, None] * inv_freq[None, :]
    cos, sin = jnp.cos(ang), jnp.sin(ang)
    c = jnp.concatenate([cos, cos, jnp.ones((T, d - rot), F32)], axis=1)
    a = jnp.concatenate([-sin, jnp.zeros((T, d - half), F32)], axis=1)
    b = jnp.concatenate([jnp.zeros((T, half), F32), sin, jnp.zeros((T, d - rot), F32)], axis=1)
    reps = LANES // d
    return tuple(jnp.tile(t, (1, reps)) for t in (c, a, b))


def _rope128(xs, tabs, half):
    c, a, b = tabs
    return xs * c + pltpu.roll(xs, LANES - half, 1) * a + pltpu.roll(xs, half, 1) * b


def _norm_rope(x, g, tabs, d, norm=True):
    tm, W = x.shape
    half = d // ROPE_FRACTION // 2
    lo = lax.broadcasted_iota(I32, (tm, LANES), 1) < 64
    outs = []
    for s in range(W // LANES):
        xs = x[:, s * LANES:(s + 1) * LANES]
        if norm:
            sq = xs * xs
            if d == LANES:
                r = lax.rsqrt(jnp.sum(sq, axis=1, keepdims=True) * (1.0 / d) + EPS)
            else:
                s_lo = jnp.sum(jnp.where(lo, sq, 0.0), axis=1, keepdims=True)
                s_hi = jnp.sum(jnp.where(lo, 0.0, sq), axis=1, keepdims=True)
                r = jnp.where(lo, lax.rsqrt(s_lo * (1.0 / d) + EPS), lax.rsqrt(s_hi * (1.0 / d) + EPS))
            xs = xs * r * g[:, s * LANES:(s + 1) * LANES]
        outs.append(_rope128(xs, tabs, half))
    return outs[0] if len(outs) == 1 else jnp.concatenate(outs, axis=1)


def _post_kernel(dq_ref, dk_ref, dv_ref, cq_ref, ck_ref, cv_ref, mi_ref,
                 c64_ref, a64_ref, b64_ref, c128_ref, a128_ref, b128_ref,
                 gdq_ref, gdk_ref, gcq_ref, gck_ref, gik_ref,
                 dq_o, dk32_o, dkb_o, dvb_o, cq_o, ck32_o, ckb_o, cvb_o, iq_o, ik32_o, ikr_o, iw_o,
                 dv32_o, cv32_o):
    t64 = (c64_ref[...], a64_ref[...], b64_ref[...])
    t128 = (c128_ref[...], a128_ref[...], b128_ref[...])
    dq = _norm_rope(dq_ref[...], gdq_ref[...], t64, 64)
    dq_o[...] = (dq * (DIFF_DH ** -0.5)).astype(BF16)
    dk = _norm_rope(dk_ref[...], gdk_ref[...], t64, 64)
    dk32_o[...] = dk
    dkb_o[...] = dk.astype(BF16)
    dv = dv_ref[...]
    dv32_o[...] = dv
    dvb_o[...] = dv.astype(BF16)
    cq = _norm_rope(cq_ref[...], gcq_ref[...], t128, 128)
    cq_o[...] = (cq * (DSA_DH ** -0.5)).astype(BF16)
    ck = _norm_rope(ck_ref[...], gck_ref[...], t128, 128)
    ck32_o[...] = ck
    ckb_o[...] = ck.astype(BF16)
    cv = cv_ref[...]
    cv32_o[...] = cv
    cvb_o[...] = cv.astype(BF16)
    mi = mi_ref[...]
    iq = _norm_rope(mi[:, 0:256], None, t64, 64, norm=False)
    iq_o[...] = (iq * (IDX_DH ** -0.5)).astype(BF16)
    mk = mi[:, MISC_IK:MISC_IK + LANES]
    tm = mk.shape[0]
    lane = lax.broadcasted_iota(I32, (tm, LANES), 1)
    lo = lane < 64
    ssq = jnp.sum(jnp.where(lo, mk * mk, 0.0), axis=1, keepdims=True)
    ikn = mk * lax.rsqrt(ssq * (1.0 / IDX_DH) + EPS) * gik_ref[...]
    ik = _rope128(ikn, t64, IDX_DH // ROPE_FRACTION // 2)
    ik32_o[...] = ik[:, 0:IDX_DH]
    ik2 = jnp.where(lo, ik, pltpu.roll(ik, 64, 1))
    ikr_o[...] = jnp.concatenate([ik2, ik2], axis=1).astype(BF16)
    iw_o[...] = jnp.where(lane < IDX_HEADS, pltpu.roll(mk, 64, 1) * (IDX_HEADS ** -0.5), 0.0)


def _post(z, tabs64, tabs128, gains, B, T):
    M = B * T
    tm = min(T, 512)
    assert T % tm == 0
    npos = T // tm

    def zspec(off):
        return pl.BlockSpec((tm, 512), lambda i: (i, off // 512))

    tab_spec = pl.BlockSpec((tm, LANES), lambda i: (i % npos, 0))

    def gspec(w):
        return pl.BlockSpec((1, w), lambda i: (0, 0))

    def ospec(w):
        return pl.BlockSpec((tm, w), lambda i: (i, 0))

    outs = [(512, BF16), (512, F32), (512, BF16), (512, BF16),
            (512, BF16), (512, F32), (512, BF16), (512, BF16),
            (256, BF16), (IDX_DH, F32), (256, BF16), (LANES, F32), (512, F32), (512, F32)]
    return pl.pallas_call(
        _post_kernel,
        out_shape=[jax.ShapeDtypeStruct((M, w), dt) for w, dt in outs],
        grid=(M // tm,),
        in_specs=[zspec(Z_DQ), zspec(Z_DK), zspec(Z_DV), zspec(Z_CQ), zspec(Z_CK), zspec(Z_CV), zspec(Z_MISC)]
                 + [tab_spec] * 6 + [gspec(512)] * 4 + [gspec(LANES)],
        out_specs=[ospec(w) for w, _ in outs],
        compiler_params=_cparams(("arbitrary",)),
        name="post",
    )(z, z, z, z, z, z, z, *tabs64, *tabs128, *gains)


def _split3(x):
    hi = x.astype(BF16)
    r = x - hi.astype(F32)
    mid = r.astype(BF16)
    lo = (r - mid.astype(F32)).astype(BF16)
    return hi, mid, lo


def _gla_kernel(qk_ref, v_ref, r_ref, ga_ref, wa_ref, ba_ref, g_ref, s0_ref, o_ref, st_ref, st_scr, *, nct):
    ti = pl.program_id(1)

    @pl.when(ti == 0)
    def _():
        st_scr[...] = s0_ref[...]

    C = CHUNK
    W = GLA_HEADS * GLA_DK
    row = lax.broadcasted_iota(I32, (C, C), 0)
    col = lax.broadcasted_iota(I32, (C, C), 1)
    tri = col <= row
    tri_bf = tri.astype(BF16)
    lane = lax.broadcasted_iota(I32, (1, W), 1)
    hmask = [(lane // GLA_DK) == h for h in range(GLA_HEADS)]
    rowi = lax.broadcasted_iota(I32, (C, W), 0)
    wa = wa_ref[...]
    ba = ba_ref[...]
    g = g_ref[...]

    def chunk(c, carry):
        rows = pl.ds(pl.multiple_of(c * C, C), C)
        qk = qk_ref[rows, :]
        q = qk[:, :W] * (GLA_DK ** -0.5)
        k = qk[:, W:]
        v = v_ref[rows, :].astype(BF16)
        pre = _dot(ga_ref[rows, :].astype(BF16), wa) + ba
        la = (jnp.minimum(pre, 0.0) - jnp.log(1.0 + jnp.exp(-jnp.abs(pre)))) * (1.0 / GLA_GATE_TAU)
        hi, mid, lo = _split3(la)
        b = _dot(tri_bf, hi) + _dot(tri_bf, mid) + _dot(tri_bf, lo)
        st = st_scr[...]
        st_bf = st.astype(BF16)
        qe = q * jnp.exp(b)
        b_end = b[C - 1:C, :]
        kend = k * jnp.exp(b_end - b)
        att_parts = [[] for _ in range(GLA_HEADS)]
        for s in range(C // GLA_SUB):
            r0 = s * GLA_SUB
            br = b[r0:r0 + 1, :]
            qs = q[r0:r0 + GLA_SUB, :] * jnp.exp(b[r0:r0 + GLA_SUB, :] - br)
            ks = k * jnp.exp(br - b)
            if r0 + GLA_SUB < C:
                ks = jnp.where(rowi < r0 + GLA_SUB, ks, 0.0)
            ks = ks.astype(BF16)
            for h in range(GLA_HEADS):
                att_parts[h].append(_dot_nt(jnp.where(hmask[h], qs, 0.0).astype(BF16), ks))
        upd = None
        for h in range(GLA_HEADS):
            hs = slice(h * GLA_DV, (h + 1) * GLA_DV)
            att = jnp.where(tri, jnp.concatenate(att_parts[h], axis=0), 0.0)
            o = _dot(att.astype(BF16), v[:, hs]) + _dot_nt(jnp.where(hmask[h], qe, 0.0).astype(BF16), st_bf)
            u = _dot_tn(v[:, hs], jnp.where(hmask[h], kend, 0.0).astype(BF16))
            upd = u if upd is None else upd + u
            y = o * lax.rsqrt(jnp.mean(o * o, axis=-1, keepdims=True) + EPS) * g[:, hs]
            o_ref[rows, hs] = (y * _silu(r_ref[rows, hs])).astype(BF16)
        st_scr[...] = st * jnp.exp(b_end) + upd
        return carry

    lax.fori_loop(0, nct, chunk, 0)

    @pl.when(ti == pl.num_programs(1) - 1)
    def _():
        st_ref[...] = st_scr[...]


def _gla(z, wa, ba, g, s0t, B, T):
    M = B * T
    tt = min(T, 512)
    assert T % tt == 0 and tt % CHUNK == 0
    nt = T // tt
    W = GLA_HEADS * GLA_DK
    return pl.pallas_call(
        functools.partial(_gla_kernel, nct=tt // CHUNK),
        out_shape=[jax.ShapeDtypeStruct((M, 512), BF16),
                   jax.ShapeDtypeStruct((B, GLA_DV, W), F32)],
        grid=(B, nt),
        in_specs=[pl.BlockSpec((tt, 512), lambda b, t: (b * nt + t, Z_GQK // 512)),
                  pl.BlockSpec((tt, 512), lambda b, t: (b * nt + t, Z_GV // 512)),
                  pl.BlockSpec((tt, 512), lambda b, t: (b * nt + t, Z_GR // 512)),
                  pl.BlockSpec((tt, LANES), lambda b, t: (b * nt + t, (Z_MISC + MISC_GA) // LANES)),
                  pl.BlockSpec((LANES, W), lambda b, t: (0, 0)),
                  pl.BlockSpec((1, W), lambda b, t: (0, 0)),
                  pl.BlockSpec((1, 512), lambda b, t: (0, 0)),
                  pl.BlockSpec((None, GLA_DV, W), lambda b, t: (b, 0, 0))],
        out_specs=[pl.BlockSpec((tt, 512), lambda b, t: (b * nt + t, 0)),
                   pl.BlockSpec((None, GLA_DV, W), lambda b, t: (b, 0, 0))],
        scratch_shapes=[pltpu.VMEM((GLA_DV, W), F32)],
        compiler_params=_cparams(("arbitrary", "arbitrary")),
        name="gla",
    )(z, z, z, z, wa, ba, g, s0t)


def _diff_lambda(lam_ref, lam_init):
    lv = lam_ref[...]
    a = jnp.sum(lv[0:1, :] * lv[1:2, :], axis=1, keepdims=True)
    b = jnp.sum(lv[2:3, :] * lv[3:4, :], axis=1, keepdims=True)
    return jnp.exp(a) - jnp.exp(b) + lam_init


def _diff_finish(o0, o1, lam, g, lam_init):
    o = o0 - lam * o1
    return o * lax.rsqrt(jnp.mean(o * o, axis=-1, keepdims=True) + EPS) * g * (1.0 - lam_init)


def _diff_prompt_kernel(qi_ref, kj_ref, q_ref, k_ref, v_ref, lam_ref, g_ref, o_ref,
                        q2_scr, m_scr, l_scr, acc_scr, *, lam_init, tq):
    n = pl.program_id(2)
    qi = qi_ref[n]
    kj = kj_ref[n]

    @pl.when(kj == 0)
    def _():
        q = q_ref[...]
        lane = lax.broadcasted_iota(I32, (1, LANES), 1)
        q2_scr[0:tq, :] = jnp.where(lane < DIFF_DH, q, jnp.zeros_like(q))
        q2_scr[tq:2 * tq, :] = jnp.where(lane < DIFF_DH, jnp.zeros_like(q), q)
        m_scr[...] = jnp.full(m_scr.shape, NEG_INF, F32)
        l_scr[...] = jnp.zeros(l_scr.shape, F32)
        acc_scr[...] = jnp.zeros(acc_scr.shape, F32)

    def step(diagonal):
        s = _dot_nt(q2_scr[...], k_ref[...])
        if diagonal:
            qc = lax.broadcasted_iota(I32, (tq, tq), 0) // CHUNK
            kc = lax.broadcasted_iota(I32, (tq, tq), 1) // CHUNK
            vis = kc <= qc
            s = jnp.where(jnp.concatenate([vis, vis], axis=0), s, NEG_INF)
        m_prev = m_scr[...]
        m_new = jnp.maximum(m_prev, jnp.max(s, axis=1, keepdims=True))
        alpha = jnp.exp(m_prev - m_new)
        p = jnp.exp(s - jnp.concatenate([m_new] * (tq // LANES), axis=1))
        l_scr[...] = alpha * l_scr[...] + jnp.sum(p, axis=1, keepdims=True)
        acc_scr[...] = alpha * acc_scr[...] + _dot(p.astype(BF16), v_ref[...])
        m_scr[...] = m_new

    @pl.when(kj < qi)
    def _():
        step(False)

    @pl.when(kj == qi)
    def _():
        step(True)
        lam = _diff_lambda(lam_ref, lam_init)
        o = acc_scr[...] / l_scr[...]
        o_ref[...] = _diff_finish(o[0:tq], o[tq:2 * tq], lam, g_ref[...], lam_init).astype(BF16)


def _diff_prompt(dq, dk, dv, lamv, g, B, T, lam_init):
    M = B * T
    tq = min(T, 512)
    nq = T // tq
    pairs = [(i, j) for i in range(nq) for j in range(i + 1)]
    qi = jnp.asarray([p[0] for p in pairs], I32)
    kj = jnp.asarray([p[1] for p in pairs], I32)
    grid_spec = pltpu.PrefetchScalarGridSpec(
        num_scalar_prefetch=2,
        grid=(B, DIFF_HEADS, len(pairs)),
        in_specs=[pl.BlockSpec((tq, LANES), lambda b, h, n, qi, kj: (b * nq + qi[n], h)),
                  pl.BlockSpec((tq, LANES), lambda b, h, n, qi, kj: (b * nq + kj[n], h)),
                  pl.BlockSpec((tq, LANES), lambda b, h, n, qi, kj: (b * nq + kj[n], h)),
                  pl.BlockSpec((4, DIFF_DH), lambda b, h, n, qi, kj: (0, 0)),
                  pl.BlockSpec((1, LANES), lambda b, h, n, qi, kj: (0, 0))],
        out_specs=pl.BlockSpec((tq, LANES), lambda b, h, n, qi, kj: (b * nq + qi[n], h)),
        scratch_shapes=[pltpu.VMEM((2 * tq, LANES), BF16), pltpu.VMEM((2 * tq, LANES), F32),
                        pltpu.VMEM((2 * tq, LANES), F32), pltpu.VMEM((2 * tq, LANES), F32)])
    return pl.pallas_call(
        functools.partial(_diff_prompt_kernel, lam_init=lam_init, tq=tq),
        out_shape=jax.ShapeDtypeStruct((M, 512), BF16),
        grid_spec=grid_spec,
        compiler_params=_cparams(("arbitrary", "arbitrary", "arbitrary")),
        name="diff_prompt",
    )(qi, kj, dq, dk, dv, lamv, g)


def _diff_sample_kernel(q_ref, kn_ref, vn_ref, kp_ref, vp_ref, lam_ref, g_ref, o_ref, *, lam_init, past_len):
    T = q_ref.shape[0]
    lam = _diff_lambda(lam_ref, lam_init)
    lane = lax.broadcasted_iota(I32, (1, LANES), 1)
    qpos = past_len + lax.broadcasted_iota(I32, (T, T), 0)
    kpos = past_len + lax.broadcasted_iota(I32, (T, T), 1)
    vis_new = (kpos // CHUNK) <= (qpos // CHUNK)
    for h in range(DIFF_HEADS):
        hs = slice(h * LANES, (h + 1) * LANES)
        q = q_ref[:, hs]
        kp = kp_ref[:, hs].astype(BF16)
        vp = vp_ref[:, hs].astype(BF16)
        kn = kn_ref[:, hs]
        vn = vn_ref[:, hs]
        outs = []
        for c in range(2):
            qc = jnp.where((lane < DIFF_DH) == (c == 0), q, jnp.zeros_like(q))
            sp = _dot_nt(qc, kp)
            sn = jnp.where(vis_new, _dot_nt(qc, kn), NEG_INF)
            m = jnp.maximum(jnp.max(sp, axis=1, keepdims=True), jnp.max(sn, axis=1, keepdims=True))
            pp = jnp.exp(sp - m)
            pn = jnp.exp(sn - m)
            l = jnp.sum(pp, axis=1, keepdims=True) + jnp.sum(pn, axis=1, keepdims=True)
            outs.append((_dot(pp.astype(BF16), vp) + _dot(pn.astype(BF16), vn)) / l)
        o_ref[:, hs] = _diff_finish(outs[0], outs[1], lam, g_ref[...], lam_init).astype(BF16)


def _diff_sample(dq, dk, dv, past_k, past_v, layer, lamv, g, B, T, lam_init):
    M = B * T
    P = past_k.shape[2]
    return pl.pallas_call(
        functools.partial(_diff_sample_kernel, lam_init=lam_init, past_len=P),
        out_shape=jax.ShapeDtypeStruct((M, 512), BF16),
        grid=(B,),
        in_specs=[pl.BlockSpec((T, 512), lambda b: (b, 0)),
                  pl.BlockSpec((T, 512), lambda b: (b, 0)),
                  pl.BlockSpec((T, 512), lambda b: (b, 0)),
                  pl.BlockSpec((None, None, P, 512), lambda b: (layer, b, 0, 0)),
                  pl.BlockSpec((None, None, P, 512), lambda b: (layer, b, 0, 0)),
                  pl.BlockSpec((4, DIFF_DH), lambda b: (0, 0)),
                  pl.BlockSpec((1, LANES), lambda b: (0, 0))],
        out_specs=pl.BlockSpec((T, 512), lambda b: (b, 0)),
        compiler_params=_cparams(("arbitrary",)),
        name="diff_sample",
    )(dq, dk, dv, past_k, past_v, lamv, g)


INT_MIN = -2 ** 31


def _idx_score(iq, iw, ikr):
    lane = lax.broadcasted_iota(I32, (1, IDX_HEADS * IDX_DH), 1)
    sc = None
    for i in range(IDX_HEADS):
        lg = _dot_nt(jnp.where((lane // IDX_DH) == i, iq, jnp.zeros_like(iq)), ikr)
        t = iw[:, i:i + 1] * jnp.maximum(lg, 0.0)
        sc = t if sc is None else sc + t
    return jnp.where(sc == 0.0, 0.0, sc)


def _order_key(score):
    bits = pltpu.bitcast(score, I32)
    return jnp.where(bits < 0, bits ^ 0x7FFFFFFF, bits)


def _count(mask):
    return jnp.sum(mask.astype(F32), axis=1, keepdims=True)


def _topk_select(key_refs, bases, n_sel, n_keys):
    tq = key_refs[0].shape[0]
    n = float(n_sel)

    def count_ge(cand):
        tot = None
        for kr in key_refs:
            c = _count(kr[...] >= cand)
            tot = c if tot is None else tot + c
        return tot

    c0 = count_ge(jnp.zeros((tq, 1), I32))
    nonneg = c0 >= n
    t0 = jnp.where(nonneg, 0, INT_MIN).astype(I32)
    cnt0 = jnp.where(nonneg, c0, float(n_keys))

    def unsettled(cnt):
        return jnp.max(jnp.abs(cnt - n)) > 0.0

    def vcond(st):
        i, _, _, go = st
        return jnp.logical_and(i < 31, go)

    def vbody(st):
        i, t, cnt, _ = st
        cand = t | jnp.left_shift(jnp.int32(1), 30 - i)
        c = count_ge(cand)
        take = c >= n
        cnt = jnp.where(take, c, cnt)
        return i + 1, jnp.where(take, cand, t), cnt, unsettled(cnt)

    _, t, cnt, ties = lax.while_loop(vcond, vbody, (jnp.int32(0), t0, cnt0, unsettled(cnt0)))

    nbits = max(1, int(n_keys).bit_length())
    j_all = jnp.full((tq, 1), (1 << nbits) - 1, I32)

    def tie_bound():
        c_gt = None
        for kr in key_refs:
            c = _count(kr[...] > t)
            c_gt = c if c_gt is None else c_gt + c
        need = n - c_gt

        def ibody(i, J):
            cand = J | jnp.left_shift(jnp.int32(1), nbits - 1 - i)
            f = None
            for kr, base in zip(key_refs, bases):
                kidx = base + lax.broadcasted_iota(I32, kr.shape, 1)
                c = _count((kr[...] == t) & (kidx < cand))
                f = c if f is None else f + c
            return jnp.where(f <= need, cand, J)

        return lax.fori_loop(0, nbits, ibody, jnp.zeros((tq, 1), I32))

    J = lax.cond(ties, tie_bound, lambda: j_all)
    return t, J


def _dsa_attend(q_ref, bias_refs, k_loads, v_loads, o_ref):
    for h in range(DSA_HEADS):
        hs = slice(h * DSA_DH, (h + 1) * DSA_DH)
        q = q_ref[:, hs]
        ss = [_dot_nt(q, kl(hs)) + br[...] for kl, br in zip(k_loads, bias_refs)]
        m = None
        for s in ss:
            ms = jnp.max(s, axis=1, keepdims=True)
            m = ms if m is None else jnp.maximum(m, ms)
        l = None
        o = None
        for s, vl in zip(ss, v_loads):
            p = jnp.exp(s - m)
            ls = jnp.sum(p, axis=1, keepdims=True)
            os_ = _dot(p.astype(BF16), vl(hs))
            l = ls if l is None else l + ls
            o = os_ if o is None else o + os_
        o_ref[:, hs] = (o / l).astype(BF16)


def _dsa_prompt_kernel(q_ref, iq_ref, iw_ref, k_ref, v_ref, ikr_ref, o_ref, key_scr, bias_scr,
                       *, tq, q_tile0, n_sel):
    tk = k_ref.shape[0]
    qt = q_tile0 + pl.program_id(1)
    score = _idx_score(iq_ref[...], iw_ref[...], ikr_ref[...])
    qpos = qt * tq + lax.broadcasted_iota(I32, (tq, tk), 0)
    kidx = lax.broadcasted_iota(I32, (tq, tk), 1)
    vis = (kidx // CHUNK) <= (qpos // CHUNK)
    key_scr[...] = _order_key(jnp.where(vis, score, NEG_INF))
    t, J = _topk_select([key_scr], [0], n_sel, tk)
    key = key_scr[...]
    sel = vis & ((key > t) | ((key == t) & (kidx < J)))
    bias_scr[...] = jnp.where(sel, 0.0, NEG_INF)
    _dsa_attend(q_ref, [bias_scr], [lambda hs: k_ref[:, hs]], [lambda hs: v_ref[:, hs]], o_ref)


def _dsa_prompt(cq, iq, iw, ck, cv, ikr, B, T):
    tq = min(T, 128)
    nq = T // tq
    n_sel = min(DSA_TOPK_MAX, T // 4)
    ng = 4 if nq % 4 == 0 else 1
    tpg = nq // ng
    outs = []
    for gi in range(ng):
        tk = (gi + 1) * tpg * tq
        nkb = T // tk if T % tk == 0 else None

        def kv_spec(w, tk=tk):
            return pl.BlockSpec((None, tk, w), lambda b, i: (b, 0, 0))

        def q_spec(w, gi=gi):
            return pl.BlockSpec((None, tq, w), lambda b, i: (b, gi * tpg + i, 0))

        out = pl.pallas_call(
            functools.partial(_dsa_prompt_kernel, tq=tq, q_tile0=gi * tpg, n_sel=n_sel),
            out_shape=jax.ShapeDtypeStruct((B, tpg * tq, 512), BF16),
            grid=(B, tpg),
            in_specs=[q_spec(512), q_spec(256), q_spec(LANES), kv_spec(512), kv_spec(512), kv_spec(256)],
            out_specs=pl.BlockSpec((None, tq, 512), lambda b, i: (b, i, 0)),
            scratch_shapes=[pltpu.VMEM((tq, tk), I32), pltpu.VMEM((tq, tk), F32)],
            compiler_params=_cparams(("arbitrary", "arbitrary"), 56 * 1024 * 1024),
            name=f"dsa_prompt_{gi}",
        )(cq.reshape(B, T, 512), iq.reshape(B, T, 256), iw.reshape(B, T, LANES),
          ck.reshape(B, T, 512), cv.reshape(B, T, 512), ikr.reshape(B, T, 256))
        outs.append(out)
    o = outs[0] if ng == 1 else jnp.concatenate(outs, axis=1)
    return o.reshape(B * T, 512)


def _dsa_sample_kernel(q_ref, iq_ref, iw_ref, kn_ref, vn_ref, ikrn_ref, kp_ref, vp_ref, ikp_ref, o_ref,
                       keyp_scr, keyn_scr, biasp_scr, biasn_scr, *, past_len, n_sel):
    T = q_ref.shape[0]
    P = past_len
    iq = iq_ref[...]
    iw = iw_ref[...]
    ikp = ikp_ref[...]
    lane = lax.broadcasted_iota(I32, (P, LANES), 1)
    ikp2 = jnp.concatenate([ikp, ikp], axis=1)
    ikrp = jnp.concatenate([ikp2, ikp2], axis=1).astype(BF16)
    del lane
    keyp_scr[...] = _order_key(_idx_score(iq, iw, ikrp))
    qpos = P + lax.broadcasted_iota(I32, (T, T), 0)
    kpos = P + lax.broadcasted_iota(I32, (T, T), 1)
    vis_n = (kpos // CHUNK) <= (qpos // CHUNK)
    keyn_scr[...] = _order_key(jnp.where(vis_n, _idx_score(iq, iw, ikrn_ref[...]), NEG_INF))
    t, J = _topk_select([keyp_scr, keyn_scr], [0, P], n_sel, P + T)
    kp_ = keyp_scr[...]
    kidx_p = lax.broadcasted_iota(I32, (T, P), 1)
    biasp_scr[...] = jnp.where((kp_ > t) | ((kp_ == t) & (kidx_p < J)), 0.0, NEG_INF)
    kn_ = keyn_scr[...]
    biasn_scr[...] = jnp.where(vis_n & ((kn_ > t) | ((kn_ == t) & (kpos < J))), 0.0, NEG_INF)
    _dsa_attend(q_ref, [biasp_scr, biasn_scr],
                [lambda hs: kp_ref[:, hs].astype(BF16), lambda hs: kn_ref[:, hs]],
                [lambda hs: vp_ref[:, hs].astype(BF16), lambda hs: vn_ref[:, hs]], o_ref)


def _dsa_sample(cq, iq, iw, ck, cv, ikr, past_k, past_v, past_ik, layer, B, T):
    M = B * T
    P = past_k.shape[2]
    n_sel = min(DSA_TOPK_MAX, (P + T) // 4)

    def rspec(w):
        return pl.BlockSpec((T, w), lambda b: (b, 0))

    def pspec(w):
        return pl.BlockSpec((None, None, P, w), lambda b: (layer, b, 0, 0))

    return pl.pallas_call(
        functools.partial(_dsa_sample_kernel, past_len=P, n_sel=n_sel),
        out_shape=jax.ShapeDtypeStruct((M, 512), BF16),
        grid=(B,),
        in_specs=[rspec(512), rspec(256), rspec(LANES), rspec(512), rspec(512), rspec(256),
                  pspec(512), pspec(512), pspec(IDX_DH)],
        out_specs=rspec(512),
        scratch_shapes=[pltpu.VMEM((T, P), I32), pltpu.VMEM((T, T), I32),
                        pltpu.VMEM((T, P), F32), pltpu.VMEM((T, T), F32)],
        compiler_params=_cparams(("arbitrary",)),
        name="dsa_sample",
    )(cq, iq, iw, ck, cv, ikr, past_k, past_v, past_ik)


def _merge_kernel(og_ref, od_ref, oc_ref, g0_ref, g1_ref, g2_ref, x_ref, m_ref,
                  wg_ref, wd_ref, wc_ref, wo_ref, gn_ref, wr_ref, br_ref,
                  x1_ref, h2_ref, ti_ref, tw_ref, rk_ref, cnt_ref, run_scr, *, nb):
    merged = (jax.nn.sigmoid(g0_ref[...]) * _dot(og_ref[...], wg_ref[...])
              + jax.nn.sigmoid(g1_ref[...]) * _dot(od_ref[...], wd_ref[...])
              + jax.nn.sigmoid(g2_ref[...]) * _dot(oc_ref[...], wc_ref[...]))
    mix = _dot(merged.astype(BF16), wo_ref[...])
    m = m_ref[...]
    x1 = x_ref[...] + _gate_rows(mix, m, 2, nb)
    x1_ref[...] = x1
    xn = x1 * lax.rsqrt(jnp.mean(x1 * x1, axis=-1, keepdims=True) + EPS) * gn_ref[...]
    h2 = _modulate(xn, m, 3, 4, nb)
    h2_ref[...] = h2
    lg = _dot(h2.astype(BF16), wr_ref[...]) + br_ref[...]
    tm = lg.shape[0]
    lane = lax.broadcasted_iota(I32, (tm, LANES), 1).astype(F32)
    vals, idxs = [], []
    for _ in range(TOP_K):
        mx = jnp.max(lg, axis=1, keepdims=True)
        ix = jnp.min(jnp.where(lg == mx, lane, float(LANES)), axis=1, keepdims=True)
        vals.append(mx)
        idxs.append(ix)
        lg = jnp.where(lane == ix, -jnp.inf, lg)
    es = [jnp.exp(v - vals[0]) for v in vals]
    den = es[0] + es[1] + es[2] + es[3]
    @pl.when(pl.program_id(0) == 0)
    def _():
        run_scr[...] = jnp.zeros(run_scr.shape, F32)

    hot = [lane == ix for ix in idxs]
    tot = (hot[0].astype(F32) + hot[1].astype(F32)) + (hot[2].astype(F32) + hot[3].astype(F32))
    strict = (lax.broadcasted_iota(I32, (tm, tm), 1) < lax.broadcasted_iota(I32, (tm, tm), 0)).astype(BF16)
    before = run_scr[...] + _dot(strict, tot.astype(BF16))
    ti = jnp.zeros((tm, LANES), F32)
    tw = jnp.zeros((tm, LANES), F32)
    rk = jnp.zeros((tm, LANES), F32)
    for r in range(TOP_K):
        ti = jnp.where(lane == float(r), idxs[r], ti)
        tw = jnp.where(lane == float(r), es[r] / den, tw)
        rk = jnp.where(lane == float(r), jnp.sum(jnp.where(hot[r], before, 0.0), axis=1, keepdims=True), rk)
    ti_ref[...] = ti.astype(I32)
    tw_ref[...] = tw
    rk_ref[...] = rk.astype(I32)
    run_new = run_scr[...] + jnp.sum(tot, axis=0, keepdims=True)
    run_scr[...] = run_new
    cnt_ref[...] = run_new.astype(I32)


def _merge(og, od, oc, z, x2, m, wts, B, T):
    M = B * T
    tm, nb = _row_tiling(B, T, 512)
    wg, wd, wc, wo, gn, wr, br = wts

    def rspec(w):
        return pl.BlockSpec((tm, w), lambda i: (i, 0))

    def gspec(k):
        return pl.BlockSpec((tm, D_MODEL), lambda i: (i, Z_GATES // D_MODEL + k))

    def wspec(r, c):
        return pl.BlockSpec((r, c), lambda i: (0, 0))

    return pl.pallas_call(
        functools.partial(_merge_kernel, nb=nb),
        out_shape=[jax.ShapeDtypeStruct((M, D_MODEL), F32), jax.ShapeDtypeStruct((M, D_MODEL), F32),
                   jax.ShapeDtypeStruct((M, LANES), I32), jax.ShapeDtypeStruct((M, LANES), F32),
                   jax.ShapeDtypeStruct((M, LANES), I32), jax.ShapeDtypeStruct((1, LANES), I32)],
        grid=(M // tm,),
        in_specs=[rspec(512), rspec(512), rspec(512), gspec(0), gspec(1), gspec(2), rspec(D_MODEL),
                  pl.BlockSpec((nb, 6, D_MODEL), lambda i: ((i * tm) // (T * nb), 0, 0)),
                  wspec(512, D_MODEL), wspec(512, D_MODEL), wspec(512, D_MODEL), wspec(D_MODEL, D_MODEL),
                  wspec(1, D_MODEL), wspec(D_MODEL, LANES), wspec(1, LANES)],
        out_specs=[rspec(D_MODEL), rspec(D_MODEL), rspec(LANES), rspec(LANES), rspec(LANES),
                   pl.BlockSpec((1, LANES), lambda i: (0, 0))],
        scratch_shapes=[pltpu.VMEM((1, LANES), F32)],
        compiler_params=_cparams(("arbitrary",)),
        name="merge_router",
    )(og, od, oc, z, z, z, x2, m, wg, wd, wc, wo, gn, wr, br)


DMA_UNROLL = 8


def _w1prep_kernel(w_ref, sel_ref, g_ref, u_ref):
    sel = sel_ref[...]
    for c in range(w_ref.shape[1] // 256):
        r = _dot(w_ref[:, c * 256:(c + 1) * 256].astype(BF16), sel)
        g_ref[:, c * LANES:(c + 1) * LANES] = r[:, :LANES].astype(BF16)
        u_ref[:, c * LANES:(c + 1) * LANES] = r[:, LANES:].astype(BF16)


def _w1prep(w1):
    E = w1.shape[0]
    tr = 512
    j = jnp.arange(256, dtype=I32)
    src = jnp.where(j < LANES, 2 * j, 2 * (j - LANES) + 1)
    sel = (jnp.arange(256, dtype=I32)[:, None] == src[None, :]).astype(BF16)
    return pl.pallas_call(
        _w1prep_kernel,
        out_shape=[jax.ShapeDtypeStruct((E, D_MODEL, D_FF), BF16)] * 2,
        grid=(E, D_MODEL // tr),
        in_specs=[pl.BlockSpec((None, tr, 2 * D_FF), lambda e, r: (e, r, 0)),
                  pl.BlockSpec((256, 256), lambda e, r: (0, 0))],
        out_specs=[pl.BlockSpec((None, tr, D_FF), lambda e, r: (e, r, 0))] * 2,
        compiler_params=_cparams(("arbitrary", "arbitrary")),
        name="w1prep",
    )(w1, sel)


def _dispatch_kernel(pos_ref, h_ref, xs_in, xs_out, sem, *, tm):
    del xs_in

    def body(j, c):
        for u in range(DMA_UNROLL):
            n = j * DMA_UNROLL + u
            pltpu.make_async_copy(h_ref.at[pl.ds(n // TOP_K, 1), :],
                                  xs_out.at[pl.ds(pos_ref[0, 0, n], 1), :], sem).start()
        return c

    lax.fori_loop(0, TOP_K * tm // DMA_UNROLL, body, 0)
    for _ in range(TOP_K):
        pltpu.make_async_copy(h_ref, xs_out.at[pl.ds(0, tm), :], sem).wait()


def _dispatch(h2, pos, n_rows, tm):
    M = h2.shape[0]
    nt = M // tm
    return pl.pallas_call(
        functools.partial(_dispatch_kernel, tm=tm),
        out_shape=jax.ShapeDtypeStruct((n_rows, D_MODEL), F32),
        grid=(nt,),
        in_specs=[pl.BlockSpec((1, 1, TOP_K * tm), lambda i: (i, 0, 0), memory_space=pltpu.SMEM),
                  pl.BlockSpec((tm, D_MODEL), lambda i: (i, 0)),
                  pl.BlockSpec(memory_space=pl.ANY)],
        out_specs=pl.BlockSpec(memory_space=pl.ANY),
        scratch_shapes=[pltpu.SemaphoreType.DMA],
        input_output_aliases={2: 0},
        compiler_params=_cparams(("arbitrary",)),
        name="moe_dispatch",
    )(pos.reshape(nt, 1, TOP_K * tm), h2, jnp.zeros((n_rows, D_MODEL), F32))


def _ffn_kernel(be_ref, nu_ref, x_ref, w1g_ref, b1g_ref, w1u_ref, b1u_ref, w2_ref, b2_ref, o_ref):
    @pl.when(pl.program_id(0) < nu_ref[0])
    def _():
        x = x_ref[...].astype(BF16)
        g = jnp.minimum(_dot(x, w1g_ref[...]) + b1g_ref[...], SWIGLU_LIMIT)
        u = jnp.clip(_dot(x, w1u_ref[...]) + b1u_ref[...], -SWIGLU_LIMIT, SWIGLU_LIMIT)
        a = g * jax.nn.sigmoid(SWIGLU_ALPHA * g) * (u + 1.0)
        o_ref[...] = _dot(a.astype(BF16), w2_ref[...]) + b2_ref[...]

    @pl.when(pl.program_id(0) >= nu_ref[0])
    def _():
        o_ref[...] = jnp.zeros(o_ref.shape, F32)


def _ffn(xs, block_e, n_used, w1g, b1g, w1u, b1u, w2, b2, bm):
    n_rows = xs.shape[0]
    nblk = n_rows // bm

    def wspec(r, c):
        return pl.BlockSpec((None, r, c), lambda i, be, nu: (be[i], 0, 0))

    grid_spec = pltpu.PrefetchScalarGridSpec(
        num_scalar_prefetch=2,
        grid=(nblk,),
        in_specs=[pl.BlockSpec((bm, D_MODEL), lambda i, be, nu: (jnp.minimum(i, nu[0] - 1), 0)),
                  wspec(D_MODEL, D_FF), wspec(1, D_FF), wspec(D_MODEL, D_FF), wspec(1, D_FF),
                  wspec(D_FF, D_MODEL), wspec(1, D_MODEL)],
        out_specs=pl.BlockSpec((bm, D_MODEL), lambda i, be, nu: (i, 0)))
    return pl.pallas_call(
        _ffn_kernel,
        out_shape=jax.ShapeDtypeStruct((n_rows, D_MODEL), F32),
        grid_spec=grid_spec,
        compiler_params=_cparams(("arbitrary",)),
        name="moe_ffn",
    )(block_e, n_used, xs, w1g, b1g, w1u, b1u, w2, b2)


def _combine_kernel(pos_ref, ys_hbm, tw_ref, x_ref, m_ref, o_ref, ybuf, sem, *, tm, nb):
    def body(j, c):
        for u in range(DMA_UNROLL):
            n = j * DMA_UNROLL + u
            pltpu.make_async_copy(ys_hbm.at[pl.ds(pos_ref[0, 0, n], 1), :], ybuf.at[pl.ds(n, 1), :], sem).start()
        return c

    lax.fori_loop(0, TOP_K * tm // DMA_UNROLL, body, 0)
    pltpu.make_async_copy(ys_hbm.at[pl.ds(0, TOP_K * tm), :], ybuf, sem).wait()
    tw = tw_ref[...]
    y = None
    for k in range(TOP_K):
        yk = tw[:, k:k + 1] * ybuf[k * tm:(k + 1) * tm, :]
        y = yk if y is None else y + yk
    o_ref[...] = x_ref[...] + _gate_rows(y, m_ref[...], 5, nb)


def _combine(pos_t, ys, top_w, x1, m, B, T, tm, nb):
    M = B * T
    nt = M // tm
    return pl.pallas_call(
        functools.partial(_combine_kernel, tm=tm, nb=nb),
        out_shape=jax.ShapeDtypeStruct((M, D_MODEL), F32),
        grid=(nt,),
        in_specs=[pl.BlockSpec((1, 1, TOP_K * tm), lambda i: (i, 0, 0), memory_space=pltpu.SMEM),
                  pl.BlockSpec(memory_space=pl.ANY),
                  pl.BlockSpec((tm, LANES), lambda i: (i, 0)),
                  pl.BlockSpec((tm, D_MODEL), lambda i: (i, 0)),
                  pl.BlockSpec((nb, 6, D_MODEL), lambda i: ((i * tm) // (T * nb), 0, 0))],
        out_specs=pl.BlockSpec((tm, D_MODEL), lambda i: (i, 0)),
        scratch_shapes=[pltpu.VMEM((TOP_K * tm, D_MODEL), F32), pltpu.SemaphoreType.DMA],
        compiler_params=_cparams(("arbitrary",)),
        name="moe_combine",
    )(pos_t.reshape(nt, 1, TOP_K * tm), ys, top_w, x1, m)


def _route(top_i, rank, counts, bm):
    N = top_i.shape[0]
    NK = N * TOP_K
    padded = (counts + bm - 1) // bm * bm
    pad_end = jnp.cumsum(padded)
    pad_start = pad_end - padded
    onehot = top_i[:, :, None] == jnp.arange(N_EXPERTS, dtype=I32)[None, None, :]
    pos = rank + jnp.sum(jnp.where(onehot, pad_start[None, None, :], 0), axis=2)
    n_rows = (-(-NK // bm)) * bm + N_EXPERTS * bm
    nblk = n_rows // bm
    starts = jnp.arange(nblk, dtype=I32) * bm
    block_e = jnp.minimum(jnp.sum((pad_end[None, :] <= starts[:, None]).astype(I32), axis=1), N_EXPERTS - 1)
    n_used = (pad_end[-1:] // bm).astype(I32)
    return pos.reshape(NK).astype(I32), n_rows, block_e.astype(I32), n_used


def _moe(h2, top_i, top_w, rank, counts, x1, m, ew, B, T):
    M = B * T
    bm = 256 if M * TOP_K >= 256 * N_EXPERTS * 4 else 128
    pos, n_rows, block_e, n_used = _route(top_i[:, :TOP_K], rank[:, :TOP_K], counts[0, :N_EXPERTS], bm)
    tm, nb = _row_tiling(B, T, 256)
    xs = _dispatch(h2, pos, n_rows, tm)
    ys = _ffn(xs, block_e, n_used, *ew, bm)
    pos_t = pos.reshape(M // tm, tm, TOP_K).transpose(0, 2, 1)
    return _combine(pos_t, ys, top_w, x1, m, B, T, tm, nb)


def _prep_layer(P, l):
    w_in = P['w_in'][l]

    def cols(name):
        o, s = _SRC[name]
        return w_in[:, o:o + s]

    zeros = lambda n: jnp.zeros((D_MODEL, n), F32)
    w_r = jnp.concatenate([
        cols('gla_q'), cols('gla_k'), cols('gla_v'), cols('gla_r'),
        cols('diff_q'), cols('diff_k'), cols('diff_v'),
        cols('dsa_q'), cols('dsa_k'), cols('dsa_v'),
        cols('idx_q'), cols('idx_k'), cols('idx_w'), zeros(60), cols('gla_a'), zeros(112),
        cols('gates')], axis=1).astype(BF16)
    assert w_r.shape[1] == Z_WIDTH
    W = GLA_HEADS * GLA_DK
    wa = jnp.zeros((LANES, W), F32).at[:GLA_GATE_RANK].set(P['w_gla_a2'][l]).astype(BF16)
    w1g, w1u = _w1prep(P['w_mlp1'][l])
    b1 = P['b_mlp1'][l]
    return dict(
        w_in=w_r, wa=wa, ba=P['b_gla_a2'][l].reshape(1, W),
        g_gla=jnp.tile(P['g_gla_out'][l], GLA_HEADS).reshape(1, 512),
        gains=(jnp.tile(P['g_diff_q'][l], 8).reshape(1, 512), jnp.tile(P['g_diff_k'][l], 8).reshape(1, 512),
               jnp.tile(P['g_dsa_q'][l], 4).reshape(1, 512), jnp.tile(P['g_dsa_k'][l], 4).reshape(1, 512),
               jnp.tile(P['g_idx_k'][l], 2).reshape(1, LANES)),
        lamv=jnp.stack([P['lambda_q1'][l], P['lambda_k1'][l], P['lambda_q2'][l], P['lambda_k2'][l]]),
        g_diff=P['g_diff_out'][l].reshape(1, LANES),
        merge=(P['w_branch_gla'][l].astype(BF16), P['w_branch_diff'][l].astype(BF16),
               P['w_branch_dsa'][l].astype(BF16), P['w_out'][l].astype(BF16),
               P['g_norm2'][l].reshape(1, D_MODEL),
               jnp.zeros((D_MODEL, LANES), F32).at[:, :N_EXPERTS].set(P['w_router'][l]).astype(BF16),
               jnp.full((1, LANES), NEG_INF, F32).at[0, :N_EXPERTS].set(P['b_router'][l])),
        experts=(w1g, b1[:, None, 0::2], w1u, b1[:, None, 1::2],
                 P['w_mlp2'][l].astype(BF16), P['b_mlp2'][l][:, None, :]),
    )


def _trunk(x, c, past, P, prepped):
    B, T, _ = x.shape
    M = B * T
    x2 = x.reshape(M, D_MODEL)
    past_len = 0 if past is None else past[0].shape[2]
    pos = past_len + jnp.arange(T, dtype=I32)
    tabs64 = _rope_tables(pos, 64)
    tabs128 = _rope_tables(pos, 128)
    W = GLA_HEADS * GLA_DK
    if past is not None:
        pdk = past[0].reshape(past[0].shape[:3] + (512,))
        pdv = past[1].reshape(past[1].shape[:3] + (512,))
        pck = past[2].reshape(past[2].shape[:3] + (512,))
        pcv = past[3].reshape(past[3].shape[:3] + (512,))
        pik = past[4]
        s0_all = past[5].transpose(0, 1, 4, 2, 3).reshape(past[5].shape[0], B, GLA_DV, W)
    per_layer = []
    for l, pp in enumerate(prepped):
        lam_init = 0.8 - 0.6 * math.exp(-0.3 * l)
        m = _ada(c, P['w_ada'][l], P['b_ada'][l])
        z = _inproj(x2, m, P['g_norm1'][l], pp['w_in'], B, T)
        (dq, dk32, dkb, dvb, cq, ck32, ckb, cvb, iq, ik32, ikr, iw, dv32, cv32) = _post(
            z, tabs64, tabs128, pp['gains'], B, T)
        s0t = jnp.zeros((B, GLA_DV, W), F32) if past is None else s0_all[l]
        o_gla, st = _gla(z, pp['wa'], pp['ba'], pp['g_gla'], s0t, B, T)
        if past is None:
            o_diff = _diff_prompt(dq, dkb, dvb, pp['lamv'], pp['g_diff'], B, T, lam_init)
            o_dsa = _dsa_prompt(cq, iq, iw, ckb, cvb, ikr, B, T)
        else:
            o_diff = _diff_sample(dq, dkb, dvb, pdk, pdv, l, pp['lamv'], pp['g_diff'], B, T, lam_init)
            o_dsa = _dsa_sample(cq, iq, iw, ckb, cvb, ikr, pck, pcv, pik, l, B, T)
        x1, h2, top_i, top_w, rank, counts = _merge(o_gla, o_diff, o_dsa, z, x2, m, pp['merge'], B, T)
        x2 = _moe(h2, top_i, top_w, rank, counts, x1, m, pp['experts'], B, T)
        s_gla = st.reshape(B, GLA_DV, GLA_HEADS, GLA_DK).transpose(0, 2, 3, 1)
        per_layer.append((dk32.reshape(B, T, DIFF_HEADS, 2, DIFF_DH),
                          dv32.reshape(B, T, DIFF_HEADS, DIFF_DV),
                          ck32.reshape(B, T, DSA_HEADS, DSA_DH),
                          cv32.reshape(B, T, DSA_HEADS, DSA_DH),
                          ik32.reshape(B, T, IDX_DH),
                          s_gla))
    stacked = tuple(jnp.stack([st[i] for st in per_layer]) for i in range(6))
    return x2.reshape(B, T, D_MODEL), stacked


def kernel(x_prompt, x_sample, cache_diff_k, cache_diff_v, cache_dsa_k, cache_dsa_v, cache_dsa_idx_k,
           state_gla, c_prompt, c_sample, w_ada, b_ada, g_norm1, g_norm2, w_in, w_gla_a2, b_gla_a2,
           g_gla_out, g_diff_q, g_diff_k, lambda_q1, lambda_k1, lambda_q2, lambda_k2, g_diff_out,
           g_dsa_q, g_dsa_k, g_idx_k, w_branch_gla, w_branch_diff, w_branch_dsa, w_out, w_router,
           b_router, w_mlp1, b_mlp1, w_mlp2, b_mlp2):
    P = dict(w_ada=w_ada, b_ada=b_ada, g_norm1=g_norm1, g_norm2=g_norm2, w_in=w_in, w_gla_a2=w_gla_a2,
             b_gla_a2=b_gla_a2, g_gla_out=g_gla_out, g_diff_q=g_diff_q, g_diff_k=g_diff_k,
             lambda_q1=lambda_q1, lambda_k1=lambda_k1, lambda_q2=lambda_q2, lambda_k2=lambda_k2,
             g_diff_out=g_diff_out, g_dsa_q=g_dsa_q, g_dsa_k=g_dsa_k, g_idx_k=g_idx_k,
             w_branch_gla=w_branch_gla, w_branch_diff=w_branch_diff, w_branch_dsa=w_branch_dsa,
             w_out=w_out, w_router=w_router, b_router=b_router, w_mlp1=w_mlp1, b_mlp1=b_mlp1,
             w_mlp2=w_mlp2, b_mlp2=b_mlp2)
    depth = w_in.shape[0]
    prepped = [_prep_layer(P, l) for l in range(depth)]
    y_prompt, new_p = _trunk(x_prompt, c_prompt, None, P, prepped)
    y_sample, new_s = _trunk(
        x_sample, c_sample,
        (cache_diff_k, cache_diff_v, cache_dsa_k, cache_dsa_v, cache_dsa_idx_k, state_gla), P, prepped)
    return (y_prompt, y_sample) + new_p + new_s
```

```python
import functools
import math

import jax
import jax.numpy as jnp
from jax import lax
from jax.experimental import pallas as pl
from jax.experimental.pallas import tpu as pltpu

F32 = jnp.float32
BF16 = jnp.bfloat16
I32 = jnp.int32

D_MODEL = 1024
CHUNK = 64
ROPE_THETA = 500000.0
ROPE_FRACTION = 4
EPS = 1e-6
NEG_INF = -1e30

GLA_HEADS = 4
GLA_DK = 64
GLA_DV = 128
GLA_GATE_RANK = 16
GLA_GATE_TAU = 16.0
GLA_SUB = 16
DIFF_HEADS = 4
DIFF_DH = 64
DIFF_DV = 128
DSA_HEADS = 4
DSA_DH = 128
IDX_HEADS = 4
IDX_DH = 64
DSA_TOPK_MAX = 256
N_EXPERTS = 32
TOP_K = 4
D_FF = 1024
SWIGLU_LIMIT = 7.0
SWIGLU_ALPHA = 1.702

LANES = 128
ROW_TILE = 8
VMEM_LIMIT = 48 * 1024 * 1024

Z_GQK, Z_GV, Z_GR = 0, 512, 1024
Z_DQ, Z_DK, Z_DV = 1536, 2048, 2560
Z_CQ, Z_CK, Z_CV = 3072, 3584, 4096
Z_MISC = 4608
Z_GATES = 5120
Z_WIDTH = 8192
MISC_IK = 256
MISC_GA = 384

_SRC = {}
_off = 0
for _name, _size in (
        ('gla_q', 256), ('gla_k', 256), ('gla_v', 512), ('gla_a', 16), ('gla_r', 512),
        ('diff_q', 512), ('diff_k', 512), ('diff_v', 512),
        ('dsa_q', 512), ('dsa_k', 512), ('dsa_v', 512),
        ('idx_q', 256), ('idx_k', 64), ('idx_w', 4), ('gates', 3072)):
    _SRC[_name] = (_off, _size)
    _off += _size


def _cparams(sem, vmem=VMEM_LIMIT):
    return pltpu.CompilerParams(dimension_semantics=sem, vmem_limit_bytes=vmem)


def _dot(a, b):
    return jnp.dot(a, b, preferred_element_type=F32)


def _dot_nt(a, b):
    return lax.dot_general(a, b, (((1,), (1,)), ((), ())), preferred_element_type=F32)


def _dot_tn(a, b):
    return lax.dot_general(a, b, (((0,), (0,)), ((), ())), preferred_element_type=F32)


def _silu(x):
    return x * jax.nn.sigmoid(x)


def _row_tiling(B, T, target):
    if T >= target:
        assert T % target == 0
        return target, 1
    nb = 1
    for cand in range(1, B + 1):
        if B % cand == 0 and cand * T <= target:
            nb = cand
    return nb * T, nb


def _modulate(xn, m, shift_i, scale_i, nb):
    tm = xn.shape[0]
    if nb == 1:
        return xn * (1.0 + m[0, scale_i:scale_i + 1, :]) + m[0, shift_i:shift_i + 1, :]
    x3 = xn.reshape(nb, tm // nb, D_MODEL)
    h = x3 * (1.0 + m[:, scale_i:scale_i + 1, :]) + m[:, shift_i:shift_i + 1, :]
    return h.reshape(tm, D_MODEL)


def _gate_rows(y, m, gate_i, nb):
    tm = y.shape[0]
    if nb == 1:
        return y * m[0, gate_i:gate_i + 1, :]
    return (y.reshape(nb, tm // nb, D_MODEL) * m[:, gate_i:gate_i + 1, :]).reshape(tm, D_MODEL)


def _ada_kernel(c_ref, w_ref, b_ref, o_ref):
    s = _silu(c_ref[...])
    o_ref[...] = _dot(s.astype(BF16), w_ref[...].astype(BF16)) + b_ref[...]


def _ada(c, w, b, layer):
    B = c.shape[0]
    out = pl.pallas_call(
        _ada_kernel,
        out_shape=jax.ShapeDtypeStruct((B, 6 * D_MODEL), F32),
        grid=(6,),
        in_specs=[pl.BlockSpec((B, D_MODEL), lambda j: (0, 0)),
                  pl.BlockSpec((None, D_MODEL, D_MODEL), lambda j: (layer, 0, j)),
                  pl.BlockSpec((None, 1, D_MODEL), lambda j: (layer, 0, j))],
        out_specs=pl.BlockSpec((B, D_MODEL), lambda j: (0, j)),
        compiler_params=_cparams(("arbitrary",)),
        name="ada",
    )(c, w, b.reshape(b.shape[0], 1, -1))
    return out.reshape(B, 6, D_MODEL)


def _inproj_kernel(x_ref, m_ref, g_ref, w_ref, o_ref, h_scr, *, nb):
    @pl.when(pl.program_id(1) == 0)
    def _():
        x = x_ref[...]
        xn = x * lax.rsqrt(jnp.mean(x * x, axis=-1, keepdims=True) + EPS) * g_ref[...]
        h_scr[...] = _modulate(xn, m_ref[...], 0, 1, nb).astype(BF16)

    o_ref[...] = _dot_nt(h_scr[...], w_ref[...])


def _inproj(x2, m, g, wt, B, T):
    M = B * T
    tm, nb = _row_tiling(B, T, 1024)
    tn = 1024
    return pl.pallas_call(
        functools.partial(_inproj_kernel, nb=nb),
        out_shape=jax.ShapeDtypeStruct((M, Z_WIDTH), F32),
        grid=(M // tm, Z_WIDTH // tn),
        in_specs=[pl.BlockSpec((tm, D_MODEL), lambda i, j: (i, 0)),
                  pl.BlockSpec((nb, 6, D_MODEL), lambda i, j: ((i * tm) // (T * nb), 0, 0)),
                  pl.BlockSpec((1, D_MODEL), lambda i, j: (0, 0)),
                  pl.BlockSpec((tn, D_MODEL), lambda i, j: (j, 0))],
        out_specs=pl.BlockSpec((tm, tn), lambda i, j: (i, j)),
        scratch_shapes=[pltpu.VMEM((tm, D_MODEL), BF16)],
        compiler_params=_cparams(("arbitrary", "arbitrary")),
        name="inproj",
    )(x2, m, g.reshape(1, -1), wt)


def _rope_tables(pos, d):
    rot = d // ROPE_FRACTION
    half = rot // 2
    T = pos.shape[0]
    inv_freq = ROPE_THETA ** (-jnp.arange(half, dtype=F32) / half)
    ang = pos.astype(F32)[:, None] * inv_freq[None, :]
    cos, sin = jnp.cos(ang), jnp.sin(ang)
    c = jnp.concatenate([cos, cos, jnp.ones((T, d - rot), F32)], axis=1)
    a = jnp.concatenate([-sin, jnp.zeros((T, d - half), F32)], axis=1)
    b = jnp.concatenate([jnp.zeros((T, half), F32), sin, jnp.zeros((T, d - rot), F32)], axis=1)
    reps = LANES // d
    return tuple(jnp.tile(t, (1, reps)) for t in (c, a, b))


def _rope128(xs, tabs, half):
    c, a, b = tabs
    return xs * c + pltpu.roll(xs, LANES - half, 1) * a + pltpu.roll(xs, half, 1) * b


def _norm_rope(x, g, tabs, d, norm=True):
    tm, W = x.shape
    half = d // ROPE_FRACTION // 2
    lo = lax.broadcasted_iota(I32, (tm, LANES), 1) < 64
    outs = []
    for s in range(W // LANES):
        xs = x[:, s * LANES:(s + 1) * LANES]
        if norm:
            sq = xs * xs
            if d == LANES:
                r = lax.rsqrt(jnp.sum(sq, axis=1, keepdims=True) * (1.0 / d) + EPS)
            else:
                s_lo = jnp.sum(jnp.where(lo, sq, 0.0), axis=1, keepdims=True)
                s_hi = jnp.sum(jnp.where(lo, 0.0, sq), axis=1, keepdims=True)
                r = jnp.where(lo, lax.rsqrt(s_lo * (1.0 / d) + EPS), lax.rsqrt(s_hi * (1.0 / d) + EPS))
            xs = xs * r * g[:, s * LANES:(s + 1) * LANES]
        outs.append(_rope128(xs, tabs, half))
    return outs[0] if len(outs) == 1 else jnp.concatenate(outs, axis=1)


def _post_kernel(dq_ref, dk_ref, dv_ref, cq_ref, ck_ref, cv_ref, mi_ref,
                 c64_ref, a64_ref, b64_ref, c128_ref, a128_ref, b128_ref,
                 gdq_ref, gdk_ref, gcq_ref, gck_ref, gik_ref,
                 dq_o, dk32_o, dkb_o, dvb_o, cq_o, ck32_o, ckb_o, cvb_o, iq_o, ik32_o, ikr_o, iw_o,
                 dv32_o, cv32_o):
    t64 = (c64_ref[...], a64_ref[...], b64_ref[...])
    t128 = (c128_ref[...], a128_ref[...], b128_ref[...])
    dq = _norm_rope(dq_ref[...], gdq_ref[...], t64, 64)
    dq_o[...] = (dq * (DIFF_DH ** -0.5)).astype(BF16)
    dk = _norm_rope(dk_ref[...], gdk_ref[...], t64, 64)
    dk32_o[...] = dk
    dkb_o[...] = dk.astype(BF16)
    dv = dv_ref[...]
    dv32_o[...] = dv
    dvb_o[...] = dv.astype(BF16)
    cq = _norm_rope(cq_ref[...], gcq_ref[...], t128, 128)
    cq_o[...] = (cq * (DSA_DH ** -0.5)).astype(BF16)
    ck = _norm_rope(ck_ref[...], gck_ref[...], t128, 128)
    ck32_o[...] = ck
    ckb_o[...] = ck.astype(BF16)
    cv = cv_ref[...]
    cv32_o[...] = cv
    cvb_o[...] = cv.astype(BF16)
    mi = mi_ref[...]
    iq = _norm_rope(mi[:, 0:256], None, t64, 64, norm=False)
    iq_o[...] = (iq * (IDX_DH ** -0.5)).astype(BF16)
    mk = mi[:, MISC_IK:MISC_IK + LANES]
    tm = mk.shape[0]
    lane = lax.broadcasted_iota(I32, (tm, LANES), 1)
    lo = lane < 64
    ssq = jnp.sum(jnp.where(lo, mk * mk, 0.0), axis=1, keepdims=True)
    ikn = mk * lax.rsqrt(ssq * (1.0 / IDX_DH) + EPS) * gik_ref[...]
    ik = _rope128(ikn, t64, IDX_DH // ROPE_FRACTION // 2)
    ik32_o[...] = ik[:, 0:IDX_DH]
    ik2 = jnp.where(lo, ik, pltpu.roll(ik, 64, 1))
    ikr_o[...] = jnp.concatenate([ik2, ik2], axis=1).astype(BF16)
    iw_o[...] = jnp.where(lane < IDX_HEADS, pltpu.roll(mk, 64, 1) * (IDX_HEADS ** -0.5), 0.0)


def _post(z, tabs64, tabs128, gains, B, T):
    M = B * T
    tm = min(T, 512)
    assert T % tm == 0
    npos = T // tm

    def zspec(off):
        return pl.BlockSpec((tm, 512), lambda i: (i, off // 512))

    tab_spec = pl.BlockSpec((tm, LANES), lambda i: (i % npos, 0))

    def gspec(w):
        return pl.BlockSpec((1, w), lambda i: (0, 0))

    def ospec(w):
        return pl.BlockSpec((tm, w), lambda i: (i, 0))

    outs = [(512, BF16), (512, F32), (512, BF16), (512, BF16),
            (512, BF16), (512, F32), (512, BF16), (512, BF16),
            (256, BF16), (IDX_DH, F32), (256, BF16), (LANES, F32), (512, F32), (512, F32)]
    return pl.pallas_call(
        _post_kernel,
        out_shape=[jax.ShapeDtypeStruct((M, w), dt) for w, dt in outs],
        grid=(M // tm,),
        in_specs=[zspec(Z_DQ), zspec(Z_DK), zspec(Z_DV), zspec(Z_CQ), zspec(Z_CK), zspec(Z_CV), zspec(Z_MISC)]
                 + [tab_spec] * 6 + [gspec(512)] * 4 + [gspec(LANES)],
        out_specs=[ospec(w) for w, _ in outs],
        compiler_params=_cparams(("arbitrary",)),
        name="post",
    )(z, z, z, z, z, z, z, *tabs64, *tabs128, *gains)


def _split3(x):
    hi = x.astype(BF16)
    r = x - hi.astype(F32)
    mid = r.astype(BF16)
    lo = (r - mid.astype(F32)).astype(BF16)
    return hi, mid, lo


def _gla_kernel(qk_ref, v_ref, r_ref, ga_ref, wa_ref, ba_ref, g_ref, s0_ref, o_ref, st_ref, st_scr, *, nct):
    ti = pl.program_id(1)

    @pl.when(ti == 0)
    def _():
        st_scr[...] = s0_ref[...]

    C = CHUNK
    W = GLA_HEADS * GLA_DK
    row = lax.broadcasted_iota(I32, (C, C), 0)
    col = lax.broadcasted_iota(I32, (C, C), 1)
    tri = col <= row
    tri_bf = tri.astype(BF16)
    lane = lax.broadcasted_iota(I32, (1, W), 1)
    hmask = [(lane // GLA_DK) == h for h in range(GLA_HEADS)]
    rowi = lax.broadcasted_iota(I32, (C, W), 0)
    wa = wa_ref[...]
    ba = ba_ref[...]
    g = g_ref[...]

    def chunk(c, carry):
        rows = pl.ds(pl.multiple_of(c * C, C), C)
        qk = qk_ref[rows, :]
        q = qk[:, :W] * (GLA_DK ** -0.5)
        k = qk[:, W:]
        v = v_ref[rows, :].astype(BF16)
        pre = _dot(ga_ref[rows, :].astype(BF16), wa) + ba
        la = (jnp.minimum(pre, 0.0) - jnp.log(1.0 + jnp.exp(-jnp.abs(pre)))) * (1.0 / GLA_GATE_TAU)
        hi, mid, lo = _split3(la)
        b = _dot(tri_bf, hi) + _dot(tri_bf, mid) + _dot(tri_bf, lo)
        st = st_scr[...]
        st_bf = st.astype(BF16)
        qe = q * jnp.exp(b)
        b_end = b[C - 1:C, :]
        kend = k * jnp.exp(b_end - b)
        att_parts = [[] for _ in range(GLA_HEADS)]
        for s in range(C // GLA_SUB):
            r0 = s * GLA_SUB
            br = b[r0:r0 + 1, :]
            qs = q[r0:r0 + GLA_SUB, :] * jnp.exp(b[r0:r0 + GLA_SUB, :] - br)
            ks = k * jnp.exp(br - b)
            if r0 + GLA_SUB < C:
                ks = jnp.where(rowi < r0 + GLA_SUB, ks, 0.0)
            ks = ks.astype(BF16)
            for h in range(GLA_HEADS):
                att_parts[h].append(_dot_nt(jnp.where(hmask[h], qs, 0.0).astype(BF16), ks))
        upd = None
        for h in range(GLA_HEADS):
            hs = slice(h * GLA_DV, (h + 1) * GLA_DV)
            att = jnp.where(tri, jnp.concatenate(att_parts[h], axis=0), 0.0)
            o = _dot(att.astype(BF16), v[:, hs]) + _dot_nt(jnp.where(hmask[h], qe, 0.0).astype(BF16), st_bf)
            u = _dot_tn(v[:, hs], jnp.where(hmask[h], kend, 0.0).astype(BF16))
            upd = u if upd is None else upd + u
            y = o * lax.rsqrt(jnp.mean(o * o, axis=-1, keepdims=True) + EPS) * g[:, hs]
            o_ref[rows, hs] = (y * _silu(r_ref[rows, hs])).astype(BF16)
        st_scr[...] = st * jnp.exp(b_end) + upd
        return carry

    lax.fori_loop(0, nct, chunk, 0)

    @pl.when(ti == pl.num_programs(1) - 1)
    def _():
        st_ref[...] = st_scr[...]


def _gla(z, wa, ba, g, s0t, B, T):
    M = B * T
    tt = min(T, 512)
    assert T % tt == 0 and tt % CHUNK == 0
    nt = T // tt
    W = GLA_HEADS * GLA_DK
    return pl.pallas_call(
        functools.partial(_gla_kernel, nct=tt // CHUNK),
        out_shape=[jax.ShapeDtypeStruct((M, 512), BF16),
                   jax.ShapeDtypeStruct((B, GLA_DV, W), F32)],
        grid=(B, nt),
        in_specs=[pl.BlockSpec((tt, 512), lambda b, t: (b * nt + t, Z_GQK // 512)),
                  pl.BlockSpec((tt, 512), lambda b, t: (b * nt + t, Z_GV // 512)),
                  pl.BlockSpec((tt, 512), lambda b, t: (b * nt + t, Z_GR // 512)),
                  pl.BlockSpec((tt, LANES), lambda b, t: (b * nt + t, (Z_MISC + MISC_GA) // LANES)),
                  pl.BlockSpec((LANES, W), lambda b, t: (0, 0)),
                  pl.BlockSpec((1, W), lambda b, t: (0, 0)),
                  pl.BlockSpec((1, 512), lambda b, t: (0, 0)),
                  pl.BlockSpec((None, GLA_DV, W), lambda b, t: (b, 0, 0))],
        out_specs=[pl.BlockSpec((tt, 512), lambda b, t: (b * nt + t, 0)),
                   pl.BlockSpec((None, GLA_DV, W), lambda b, t: (b, 0, 0))],
        scratch_shapes=[pltpu.VMEM((GLA_DV, W), F32)],
        compiler_params=_cparams(("arbitrary", "arbitrary")),
        name="gla",
    )(z, z, z, z, wa, ba, g, s0t)


def _diff_lambda(lam_ref, lam_init):
    lv = lam_ref[...]
    a = jnp.sum(lv[0:1, :] * lv[1:2, :], axis=1, keepdims=True)
    b = jnp.sum(lv[2:3, :] * lv[3:4, :], axis=1, keepdims=True)
    return jnp.exp(a) - jnp.exp(b) + lam_init


def _diff_finish(o0, o1, lam, g, lam_init):
    o = o0 - lam * o1
    return o * lax.rsqrt(jnp.mean(o * o, axis=-1, keepdims=True) + EPS) * g * (1.0 - lam_init)


def _diff_prompt_kernel(qi_ref, kj_ref, q_ref, k_ref, v_ref, lam_ref, g_ref, o_ref,
                        q2_scr, m_scr, l_scr, acc_scr, *, lam_init, tq):
    n = pl.program_id(2)
    qi = qi_ref[n]
    kj = kj_ref[n]

    @pl.when(kj == 0)
    def _():
        q = q_ref[...]
        lane = lax.broadcasted_iota(I32, (1, LANES), 1)
        q2_scr[0:tq, :] = jnp.where(lane < DIFF_DH, q, jnp.zeros_like(q))
        q2_scr[tq:2 * tq, :] = jnp.where(lane < DIFF_DH, jnp.zeros_like(q), q)
        m_scr[...] = jnp.full(m_scr.shape, NEG_INF, F32)
        l_scr[...] = jnp.zeros(l_scr.shape, F32)
        acc_scr[...] = jnp.zeros(acc_scr.shape, F32)

    def step(diagonal):
        s = _dot_nt(q2_scr[...], k_ref[...])
        if diagonal:
            qc = lax.broadcasted_iota(I32, (tq, tq), 0) // CHUNK
            kc = lax.broadcasted_iota(I32, (tq, tq), 1) // CHUNK
            vis = kc <= qc
            s = jnp.where(jnp.concatenate([vis, vis], axis=0), s, NEG_INF)
        m_prev = m_scr[...]
        m_new = jnp.maximum(m_prev, jnp.max(s, axis=1, keepdims=True))
        alpha = jnp.exp(m_prev - m_new)
        p = jnp.exp(s - jnp.concatenate([m_new] * (tq // LANES), axis=1))
        l_scr[...] = alpha * l_scr[...] + jnp.sum(p, axis=1, keepdims=True)
        acc_scr[...] = alpha * acc_scr[...] + _dot(p.astype(BF16), v_ref[...])
        m_scr[...] = m_new

    @pl.when(kj < qi)
    def _():
        step(False)

    @pl.when(kj == qi)
    def _():
        step(True)
        lam = _diff_lambda(lam_ref, lam_init)
        o = acc_scr[...] / l_scr[...]
        o_ref[...] = _diff_finish(o[0:tq], o[tq:2 * tq], lam, g_ref[...], lam_init).astype(BF16)


def _diff_prompt(dq, dk, dv, lamv, g, B, T, lam_init):
    M = B * T
    tq = min(T, 512)
    nq = T // tq
    pairs = [(i, j) for i in range(nq) for j in range(i + 1)]
    qi = jnp.asarray([p[0] for p in pairs], I32)
    kj = jnp.asarray([p[1] for p in pairs], I32)
    grid_spec = pltpu.PrefetchScalarGridSpec(
        num_scalar_prefetch=2,
        grid=(B, DIFF_HEADS, len(pairs)),
        in_specs=[pl.BlockSpec((tq, LANES), lambda b, h, n, qi, kj: (b * nq + qi[n], h)),
                  pl.BlockSpec((tq, LANES), lambda b, h, n, qi, kj: (b * nq + kj[n], h)),
                  pl.BlockSpec((tq, LANES), lambda b, h, n, qi, kj: (b * nq + kj[n], h)),
                  pl.BlockSpec((4, DIFF_DH), lambda b, h, n, qi, kj: (0, 0)),
                  pl.BlockSpec((1, LANES), lambda b, h, n, qi, kj: (0, 0))],
        out_specs=pl.BlockSpec((tq, LANES), lambda b, h, n, qi, kj: (b * nq + qi[n], h)),
        scratch_shapes=[pltpu.VMEM((2 * tq, LANES), BF16), pltpu.VMEM((2 * tq, LANES), F32),
                        pltpu.VMEM((2 * tq, LANES), F32), pltpu.VMEM((2 * tq, LANES), F32)])
    return pl.pallas_call(
        functools.partial(_diff_prompt_kernel, lam_init=lam_init, tq=tq),
        out_shape=jax.ShapeDtypeStruct((M, 512), BF16),
        grid_spec=grid_spec,
        compiler_params=_cparams(("arbitrary", "arbitrary", "arbitrary")),
        name="diff_prompt",
    )(qi, kj, dq, dk, dv, lamv, g)


def _diff_sample_kernel(q_ref, kn_ref, vn_ref, kp_ref, vp_ref, lam_ref, g_ref, o_ref, *, lam_init, past_len):
    T = q_ref.shape[0]
    lam = _diff_lambda(lam_ref, lam_init)
    lane = lax.broadcasted_iota(I32, (1, LANES), 1)
    qpos = past_len + lax.broadcasted_iota(I32, (T, T), 0)
    kpos = past_len + lax.broadcasted_iota(I32, (T, T), 1)
    vis_new = (kpos // CHUNK) <= (qpos // CHUNK)
    P = past_len
    for h in range(DIFF_HEADS):
        hs = slice(h * LANES, (h + 1) * LANES)
        q = q_ref[:, hs]
        kpt = kp_ref[h].reshape(2 * DIFF_DH, P).astype(BF16)
        vp = vp_ref[pl.ds(h, P, stride=DIFF_HEADS), :].astype(BF16)
        kn = kn_ref[:, hs]
        vn = vn_ref[:, hs]
        outs = []
        for c in range(2):
            qc = jnp.where((lane < DIFF_DH) == (c == 0), q, jnp.zeros_like(q))
            sp = _dot(qc, kpt)
            sn = jnp.where(vis_new, _dot_nt(qc, kn), NEG_INF)
            m = jnp.maximum(jnp.max(sp, axis=1, keepdims=True), jnp.max(sn, axis=1, keepdims=True))
            pp = jnp.exp(sp - m)
            pn = jnp.exp(sn - m)
            l = jnp.sum(pp, axis=1, keepdims=True) + jnp.sum(pn, axis=1, keepdims=True)
            outs.append((_dot(pp.astype(BF16), vp) + _dot(pn.astype(BF16), vn)) / l)
        o_ref[:, hs] = _diff_finish(outs[0], outs[1], lam, g_ref[...], lam_init).astype(BF16)


def _diff_sample(dq, dk, dv, past_kt, past_v, layer, lamv, g, B, T, lam_init):
    M = B * T
    P = past_kt.shape[-1]
    return pl.pallas_call(
        functools.partial(_diff_sample_kernel, lam_init=lam_init, past_len=P),
        out_shape=jax.ShapeDtypeStruct((M, 512), BF16),
        grid=(B,),
        in_specs=[pl.BlockSpec((T, 512), lambda b: (b, 0)),
                  pl.BlockSpec((T, 512), lambda b: (b, 0)),
                  pl.BlockSpec((T, 512), lambda b: (b, 0)),
                  pl.BlockSpec((None, None, DIFF_HEADS, 2, DIFF_DH, P), lambda b: (layer, b, 0, 0, 0, 0)),
                  pl.BlockSpec((None, None, P * DIFF_HEADS, DIFF_DV), lambda b: (layer, b, 0, 0)),
                  pl.BlockSpec((4, DIFF_DH), lambda b: (0, 0)),
                  pl.BlockSpec((1, LANES), lambda b: (0, 0))],
        out_specs=pl.BlockSpec((T, 512), lambda b: (b, 0)),
        compiler_params=_cparams(("arbitrary",)),
        name="diff_sample",
    )(dq, dk, dv, past_kt, past_v, lamv, g)


INT_MIN = -2 ** 31


def _idx_score(iq, iw, ikr, transposed=False):
    lane = lax.broadcasted_iota(I32, (1, IDX_HEADS * IDX_DH), 1)
    sc = None
    for i in range(IDX_HEADS):
        iqi = jnp.where((lane // IDX_DH) == i, iq, jnp.zeros_like(iq))
        lg = _dot(iqi, ikr) if transposed else _dot_nt(iqi, ikr)
        t = iw[:, i:i + 1] * jnp.maximum(lg, 0.0)
        sc = t if sc is None else sc + t
    return jnp.where(sc == 0.0, 0.0, sc)


def _order_key(score):
    bits = pltpu.bitcast(score, I32)
    return jnp.where(bits < 0, bits ^ 0x7FFFFFFF, bits)


def _count(mask):
    return jnp.sum(mask.astype(F32), axis=1, keepdims=True)


def _topk_select(key_refs, bases, n_sel, n_keys):
    tq = key_refs[0].shape[0]
    n = float(n_sel)

    def count_ge(cand):
        tot = None
        for kr in key_refs:
            c = _count(kr[...] >= cand)
            tot = c if tot is None else tot + c
        return tot

    c0 = count_ge(jnp.zeros((tq, 1), I32))
    nonneg = c0 >= n
    t0 = jnp.where(nonneg, 0, INT_MIN).astype(I32)
    cnt0 = jnp.where(nonneg, c0, float(n_keys))

    def unsettled(cnt):
        return jnp.max(jnp.abs(cnt - n)) > 0.0

    def vcond(st):
        i, _, _, go = st
        return jnp.logical_and(i < 31, go)

    def vbody(st):
        i, t, cnt, _ = st
        cand = t | jnp.left_shift(jnp.int32(1), 30 - i)
        c = count_ge(cand)
        take = c >= n
        cnt = jnp.where(take, c, cnt)
        return i + 1, jnp.where(take, cand, t), cnt, unsettled(cnt)

    _, t, cnt, ties = lax.while_loop(vcond, vbody, (jnp.int32(0), t0, cnt0, unsettled(cnt0)))

    nbits = max(1, int(n_keys).bit_length())
    j_all = jnp.full((tq, 1), (1 << nbits) - 1, I32)

    def tie_bound():
        c_gt = None
        for kr in key_refs:
            c = _count(kr[...] > t)
            c_gt = c if c_gt is None else c_gt + c
        need = n - c_gt

        def ibody(i, J):
            cand = J | jnp.left_shift(jnp.int32(1), nbits - 1 - i)
            f = None
            for kr, base in zip(key_refs, bases):
                kidx = base + lax.broadcasted_iota(I32, kr.shape, 1)
                c = _count((kr[...] == t) & (kidx < cand))
                f = c if f is None else f + c
            return jnp.where(f <= need, cand, J)

        return lax.fori_loop(0, nbits, ibody, jnp.zeros((tq, 1), I32))

    J = lax.cond(ties, tie_bound, lambda: j_all)
    return t, J


def _dsa_attend(q_ref, bias_refs, k_loads, v_loads, o_ref):
    for h in range(DSA_HEADS):
        hs = slice(h * DSA_DH, (h + 1) * DSA_DH)
        q = q_ref[:, hs]
        ss = [_dot_nt(q, kl(hs)) + br[...] for kl, br in zip(k_loads, bias_refs)]
        m = None
        for s in ss:
            ms = jnp.max(s, axis=1, keepdims=True)
            m = ms if m is None else jnp.maximum(m, ms)
        l = None
        o = None
        for s, vl in zip(ss, v_loads):
            p = jnp.exp(s - m)
            ls = jnp.sum(p, axis=1, keepdims=True)
            os_ = _dot(p.astype(BF16), vl(hs))
            l = ls if l is None else l + ls
            o = os_ if o is None else o + os_
        o_ref[:, hs] = (o / l).astype(BF16)


def _dsa_prompt_kernel(q_ref, iq_ref, iw_ref, k_ref, v_ref, ikr_ref, o_ref, key_scr, bias_scr,
                       *, tq, q_tile0, n_sel):
    tk = k_ref.shape[0]
    qt = q_tile0 + pl.program_id(1)
    score = _idx_score(iq_ref[...], iw_ref[...], ikr_ref[...])
    qpos = qt * tq + lax.broadcasted_iota(I32, (tq, tk), 0)
    kidx = lax.broadcasted_iota(I32, (tq, tk), 1)
    vis = (kidx // CHUNK) <= (qpos // CHUNK)
    key_scr[...] = _order_key(jnp.where(vis, score, NEG_INF))
    t, J = _topk_select([key_scr], [0], n_sel, tk)
    key = key_scr[...]
    sel = vis & ((key > t) | ((key == t) & (kidx < J)))
    bias_scr[...] = jnp.where(sel, 0.0, NEG_INF)
    _dsa_attend(q_ref, [bias_scr], [lambda hs: k_ref[:, hs]], [lambda hs: v_ref[:, hs]], o_ref)


def _dsa_prompt(cq, iq, iw, ck, cv, ikr, B, T):
    tq = min(T, 128)
    nq = T // tq
    n_sel = min(DSA_TOPK_MAX, T // 4)
    ng = 8 if nq % 8 == 0 else (4 if nq % 4 == 0 else 1)
    tpg = nq // ng
    outs = []
    for gi in range(ng):
        tk = (gi + 1) * tpg * tq
        nkb = T // tk if T % tk == 0 else None

        def kv_spec(w, tk=tk):
            return pl.BlockSpec((None, tk, w), lambda b, i: (b, 0, 0))

        def q_spec(w, gi=gi):
            return pl.BlockSpec((None, tq, w), lambda b, i: (b, gi * tpg + i, 0))

        out = pl.pallas_call(
            functools.partial(_dsa_prompt_kernel, tq=tq, q_tile0=gi * tpg, n_sel=n_sel),
            out_shape=jax.ShapeDtypeStruct((B, tpg * tq, 512), BF16),
            grid=(B, tpg),
            in_specs=[q_spec(512), q_spec(256), q_spec(LANES), kv_spec(512), kv_spec(512), kv_spec(256)],
            out_specs=pl.BlockSpec((None, tq, 512), lambda b, i: (b, i, 0)),
            scratch_shapes=[pltpu.VMEM((tq, tk), I32), pltpu.VMEM((tq, tk), F32)],
            compiler_params=_cparams(("arbitrary", "arbitrary"), 56 * 1024 * 1024),
            name=f"dsa_prompt_{gi}",
        )(cq.reshape(B, T, 512), iq.reshape(B, T, 256), iw.reshape(B, T, LANES),
          ck.reshape(B, T, 512), cv.reshape(B, T, 512), ikr.reshape(B, T, 256))
        outs.append(out)
    o = outs[0] if ng == 1 else jnp.concatenate(outs, axis=1)
    return o.reshape(B * T, 512)


def _dsa_sample_kernel(q_ref, iq_ref, iw_ref, kn_ref, vn_ref, ikrn_ref, kp_ref, vp_ref, ikp_ref, o_ref,
                       keyp_scr, keyn_scr, biasp_scr, biasn_scr, *, past_len, n_sel):
    T = q_ref.shape[0]
    P = past_len
    iq = iq_ref[...]
    iw = iw_ref[...]
    ikt = ikp_ref[...].astype(BF16)
    ikrp = jnp.concatenate([ikt] * IDX_HEADS, axis=0)
    keyp_scr[...] = _order_key(_idx_score(iq, iw, ikrp, transposed=True))
    qpos = P + lax.broadcasted_iota(I32, (T, T), 0)
    kpos = P + lax.broadcasted_iota(I32, (T, T), 1)
    vis_n = (kpos // CHUNK) <= (qpos // CHUNK)
    keyn_scr[...] = _order_key(jnp.where(vis_n, _idx_score(iq, iw, ikrn_ref[...]), NEG_INF))
    t, J = _topk_select([keyp_scr, keyn_scr], [0, P], n_sel, P + T)
    kp_ = keyp_scr[...]
    kidx_p = lax.broadcasted_iota(I32, (T, P), 1)
    biasp_scr[...] = jnp.where((kp_ > t) | ((kp_ == t) & (kidx_p < J)), 0.0, NEG_INF)
    kn_ = keyn_scr[...]
    biasn_scr[...] = jnp.where(vis_n & ((kn_ > t) | ((kn_ == t) & (kpos < J))), 0.0, NEG_INF)
    def past_head(ref):
        return lambda hs: ref[pl.ds(hs.start // DSA_DH, P, stride=DSA_HEADS), :].astype(BF16)

    _dsa_attend(q_ref, [biasp_scr, biasn_scr],
                [past_head(kp_ref), lambda hs: kn_ref[:, hs]],
                [past_head(vp_ref), lambda hs: vn_ref[:, hs]], o_ref)


def _dsa_sample(cq, iq, iw, ck, cv, ikr, past_k, past_v, past_ikt, layer, B, T):
    M = B * T
    P = past_ikt.shape[-1]
    n_sel = min(DSA_TOPK_MAX, (P + T) // 4)

    def rspec(w):
        return pl.BlockSpec((T, w), lambda b: (b, 0))

    def pspec(r, w):
        return pl.BlockSpec((None, None, r, w), lambda b: (layer, b, 0, 0))

    return pl.pallas_call(
        functools.partial(_dsa_sample_kernel, past_len=P, n_sel=n_sel),
        out_shape=jax.ShapeDtypeStruct((M, 512), BF16),
        grid=(B,),
        in_specs=[rspec(512), rspec(256), rspec(LANES), rspec(512), rspec(512), rspec(256),
                  pspec(P * DSA_HEADS, DSA_DH), pspec(P * DSA_HEADS, DSA_DH), pspec(IDX_DH, P)],
        out_specs=rspec(512),
        scratch_shapes=[pltpu.VMEM((T, P), I32), pltpu.VMEM((T, T), I32),
                        pltpu.VMEM((T, P), F32), pltpu.VMEM((T, T), F32)],
        compiler_params=_cparams(("arbitrary",)),
        name="dsa_sample",
    )(cq, iq, iw, ck, cv, ikr, past_k, past_v, past_ikt)


def _merge_kernel(og_ref, od_ref, oc_ref, g0_ref, g1_ref, g2_ref, x_ref, m_ref,
                  wg_ref, wd_ref, wc_ref, wo_ref, gn_ref, wr_ref, br_ref,
                  x1_ref, h2_ref, ti_ref, tw_ref, rk_ref, cnt_ref, run_scr, *, nb):
    merged = (jax.nn.sigmoid(g0_ref[...]) * _dot(og_ref[...], wg_ref[...])
              + jax.nn.sigmoid(g1_ref[...]) * _dot(od_ref[...], wd_ref[...])
              + jax.nn.sigmoid(g2_ref[...]) * _dot(oc_ref[...], wc_ref[...]))
    mix = _dot(merged.astype(BF16), wo_ref[...])
    m = m_ref[...]
    x1 = x_ref[...] + _gate_rows(mix, m, 2, nb)
    x1_ref[...] = x1
    xn = x1 * lax.rsqrt(jnp.mean(x1 * x1, axis=-1, keepdims=True) + EPS) * gn_ref[...]
    h2 = _modulate(xn, m, 3, 4, nb)
    _rows_to_tiles(h2_ref, h2)
    lg = _dot(h2.astype(BF16), wr_ref[...]) + br_ref[...]
    tm = lg.shape[0]
    lane = lax.broadcasted_iota(I32, (tm, LANES), 1).astype(F32)
    vals, idxs = [], []
    for _ in range(TOP_K):
        mx = jnp.max(lg, axis=1, keepdims=True)
        ix = jnp.min(jnp.where(lg == mx, lane, float(LANES)), axis=1, keepdims=True)
        vals.append(mx)
        idxs.append(ix)
        lg = jnp.where(lane == ix, -jnp.inf, lg)
    es = [jnp.exp(v - vals[0]) for v in vals]
    den = es[0] + es[1] + es[2] + es[3]
    @pl.when(pl.program_id(0) == 0)
    def _():
        run_scr[...] = jnp.zeros(run_scr.shape, F32)

    hot = [lane == ix for ix in idxs]
    tot = (hot[0].astype(F32) + hot[1].astype(F32)) + (hot[2].astype(F32) + hot[3].astype(F32))
    strict = (lax.broadcasted_iota(I32, (tm, tm), 1) < lax.broadcasted_iota(I32, (tm, tm), 0)).astype(BF16)
    before = run_scr[...] + _dot(strict, tot.astype(BF16))
    ti = jnp.zeros((tm, LANES), F32)
    tw = jnp.zeros((tm, LANES), F32)
    rk = jnp.zeros((tm, LANES), F32)
    for r in range(TOP_K):
        ti = jnp.where(lane == float(r), idxs[r], ti)
        tw = jnp.where(lane == float(r), es[r] / den, tw)
        rk = jnp.where(lane == float(r), jnp.sum(jnp.where(hot[r], before, 0.0), axis=1, keepdims=True), rk)
    ti_ref[...] = ti.astype(I32)
    tw_ref[...] = tw
    rk_ref[...] = rk.astype(I32)
    run_new = run_scr[...] + jnp.sum(tot, axis=0, keepdims=True)
    run_scr[...] = run_new
    cnt_ref[...] = run_new.astype(I32)


def _merge(og, od, oc, z, x2, m, wts, B, T):
    M = B * T
    tm, nb = _row_tiling(B, T, 512)
    wg, wd, wc, wo, gn, wr, br = wts

    def rspec(w):
        return pl.BlockSpec((tm, w), lambda i: (i, 0))

    def gspec(k):
        return pl.BlockSpec((tm, D_MODEL), lambda i: (i, Z_GATES // D_MODEL + k))

    def wspec(r, c):
        return pl.BlockSpec((r, c), lambda i: (0, 0))

    return pl.pallas_call(
        functools.partial(_merge_kernel, nb=nb),
        out_shape=[jax.ShapeDtypeStruct((M, D_MODEL), F32), jax.ShapeDtypeStruct((M * ROW_TILE, LANES), F32),
                   jax.ShapeDtypeStruct((M, LANES), I32), jax.ShapeDtypeStruct((M, LANES), F32),
                   jax.ShapeDtypeStruct((M, LANES), I32), jax.ShapeDtypeStruct((1, LANES), I32)],
        grid=(M // tm,),
        in_specs=[rspec(512), rspec(512), rspec(512), gspec(0), gspec(1), gspec(2), rspec(D_MODEL),
                  pl.BlockSpec((nb, 6, D_MODEL), lambda i: ((i * tm) // (T * nb), 0, 0)),
                  wspec(512, D_MODEL), wspec(512, D_MODEL), wspec(512, D_MODEL), wspec(D_MODEL, D_MODEL),
                  wspec(1, D_MODEL), wspec(D_MODEL, LANES), wspec(1, LANES)],
        out_specs=[rspec(D_MODEL), pl.BlockSpec((tm * ROW_TILE, LANES), lambda i: (i, 0)),
                   rspec(LANES), rspec(LANES), rspec(LANES),
                   pl.BlockSpec((1, LANES), lambda i: (0, 0))],
        scratch_shapes=[pltpu.VMEM((1, LANES), F32)],
        compiler_params=_cparams(("arbitrary",)),
        name="merge_router",
    )(og, od, oc, z, z, z, x2, m, wg, wd, wc, wo, gn, wr, br)


DMA_UNROLL = 8


def _w1prep_kernel(w_ref, sel_ref, g_ref, u_ref):
    sel = sel_ref[...]
    for c in range(w_ref.shape[1] // 256):
        r = _dot(w_ref[:, c * 256:(c + 1) * 256].astype(BF16), sel)
        g_ref[:, c * LANES:(c + 1) * LANES] = r[:, :LANES].astype(BF16)
        u_ref[:, c * LANES:(c + 1) * LANES] = r[:, LANES:].astype(BF16)


def _w1prep(w1_all, layer):
    E = w1_all.shape[1]
    tr = 512
    j = jnp.arange(256, dtype=I32)
    src = jnp.where(j < LANES, 2 * j, 2 * (j - LANES) + 1)
    sel = (jnp.arange(256, dtype=I32)[:, None] == src[None, :]).astype(BF16)
    return pl.pallas_call(
        _w1prep_kernel,
        out_shape=[jax.ShapeDtypeStruct((E, D_MODEL, D_FF), BF16)] * 2,
        grid=(E, D_MODEL // tr),
        in_specs=[pl.BlockSpec((None, None, tr, 2 * D_FF), lambda e, r: (layer, e, r, 0)),
                  pl.BlockSpec((256, 256), lambda e, r: (0, 0))],
        out_specs=[pl.BlockSpec((None, tr, D_FF), lambda e, r: (e, r, 0))] * 2,
        compiler_params=_cparams(("arbitrary", "arbitrary")),
        name="w1prep",
    )(w1_all, sel)


def _rows_to_tiles(ref, x):
    n = x.shape[0]
    for c in range(ROW_TILE):
        ref[pl.ds(c, n, stride=ROW_TILE), :] = x[:, c * LANES:(c + 1) * LANES]


def _tiles_to_rows(ref, base, n):
    return jnp.concatenate([ref[pl.ds(base + c, n, stride=ROW_TILE), :] for c in range(ROW_TILE)], axis=1)


def _tile_rows(ref, r):
    return ref.at[pl.ds(pl.multiple_of(r * ROW_TILE, ROW_TILE), ROW_TILE), :]


def _dispatch_kernel(pos_ref, h_ref, xs_in, xs_out, sem, *, tm):
    del xs_in

    def body(j, c):
        for u in range(DMA_UNROLL):
            n = j * DMA_UNROLL + u
            pltpu.make_async_copy(_tile_rows(h_ref, n // TOP_K), _tile_rows(xs_out, pos_ref[0, 0, n]), sem).start()
        return c

    lax.fori_loop(0, TOP_K * tm // DMA_UNROLL, body, 0)
    for _ in range(TOP_K):
        pltpu.make_async_copy(h_ref, xs_out.at[pl.ds(0, tm * ROW_TILE), :], sem).wait()


def _dispatch(h2t, pos, n_rows, tm):
    M = h2t.shape[0] // ROW_TILE
    nt = M // tm
    return pl.pallas_call(
        functools.partial(_dispatch_kernel, tm=tm),
        out_shape=jax.ShapeDtypeStruct((n_rows * ROW_TILE, LANES), F32),
        grid=(nt,),
        in_specs=[pl.BlockSpec((1, 1, TOP_K * tm), lambda i: (i, 0, 0), memory_space=pltpu.SMEM),
                  pl.BlockSpec((tm * ROW_TILE, LANES), lambda i: (i, 0)),
                  pl.BlockSpec(memory_space=pl.ANY)],
        out_specs=pl.BlockSpec(memory_space=pl.ANY),
        scratch_shapes=[pltpu.SemaphoreType.DMA],
        input_output_aliases={2: 0},
        compiler_params=_cparams(("arbitrary",)),
        name="moe_dispatch",
    )(pos.reshape(nt, 1, TOP_K * tm), h2t, jnp.zeros((n_rows * ROW_TILE, LANES), F32))


def _ffn_kernel(be_ref, nu_ref, x_ref, w1g_ref, b1g_ref, w1u_ref, b1u_ref, w2_ref, b2_ref, o_ref, *, bm):
    @pl.when(pl.program_id(0) < nu_ref[0])
    def _():
        x = _tiles_to_rows(x_ref, 0, bm).astype(BF16)
        g = jnp.minimum(_dot(x, w1g_ref[...]) + b1g_ref[...], SWIGLU_LIMIT)
        u = jnp.clip(_dot(x, w1u_ref[...]) + b1u_ref[...], -SWIGLU_LIMIT, SWIGLU_LIMIT)
        a = g * jax.nn.sigmoid(SWIGLU_ALPHA * g) * (u + 1.0)
        _rows_to_tiles(o_ref, _dot(a.astype(BF16), w2_ref[...]) + b2_ref[...])

    @pl.when(pl.program_id(0) >= nu_ref[0])
    def _():
        o_ref[...] = jnp.zeros(o_ref.shape, F32)


def _ffn(xs, block_e, n_used, w1g, b1g, w1u, b1u, w2_all, b2, layer, bm):
    n_rows = xs.shape[0] // ROW_TILE
    nblk = n_rows // bm

    def wspec(r, c):
        return pl.BlockSpec((None, r, c), lambda i, be, nu: (be[i], 0, 0))

    grid_spec = pltpu.PrefetchScalarGridSpec(
        num_scalar_prefetch=2,
        grid=(nblk,),
        in_specs=[pl.BlockSpec((bm * ROW_TILE, LANES), lambda i, be, nu: (jnp.minimum(i, nu[0] - 1), 0)),
                  wspec(D_MODEL, D_FF), wspec(1, D_FF), wspec(D_MODEL, D_FF), wspec(1, D_FF),
                  pl.BlockSpec((None, None, D_FF, D_MODEL), lambda i, be, nu: (layer, be[i], 0, 0)),
                  wspec(1, D_MODEL)],
        out_specs=pl.BlockSpec((bm * ROW_TILE, LANES), lambda i, be, nu: (i, 0)))
    return pl.pallas_call(
        functools.partial(_ffn_kernel, bm=bm),
        out_shape=jax.ShapeDtypeStruct((n_rows * ROW_TILE, LANES), F32),
        grid_spec=grid_spec,
        compiler_params=_cparams(("arbitrary",)),
        name="moe_ffn",
    )(block_e, n_used, xs, w1g, b1g, w1u, b1u, w2_all, b2)


def _combine_kernel(pos_ref, ys_hbm, tw_ref, x_ref, m_ref, o_ref, ybuf, sem, *, tm, nb):
    def body(j, c):
        for u in range(DMA_UNROLL):
            n = j * DMA_UNROLL + u
            pltpu.make_async_copy(_tile_rows(ys_hbm, pos_ref[0, 0, n]), _tile_rows(ybuf, n), sem).start()
        return c

    lax.fori_loop(0, TOP_K * tm // DMA_UNROLL, body, 0)
    pltpu.make_async_copy(ys_hbm.at[pl.ds(0, TOP_K * tm * ROW_TILE), :], ybuf, sem).wait()
    tw = tw_ref[...]
    y = None
    for k in range(TOP_K):
        yk = tw[:, k:k + 1] * _tiles_to_rows(ybuf, k * tm * ROW_TILE, tm)
        y = yk if y is None else y + yk
    o_ref[...] = x_ref[...] + _gate_rows(y, m_ref[...], 5, nb)


def _combine(pos_t, ys, top_w, x1, m, B, T, tm, nb):
    M = B * T
    nt = M // tm
    return pl.pallas_call(
        functools.partial(_combine_kernel, tm=tm, nb=nb),
        out_shape=jax.ShapeDtypeStruct((M, D_MODEL), F32),
        grid=(nt,),
        in_specs=[pl.BlockSpec((1, 1, TOP_K * tm), lambda i: (i, 0, 0), memory_space=pltpu.SMEM),
                  pl.BlockSpec(memory_space=pl.ANY),
                  pl.BlockSpec((tm, LANES), lambda i: (i, 0)),
                  pl.BlockSpec((tm, D_MODEL), lambda i: (i, 0)),
                  pl.BlockSpec((nb, 6, D_MODEL), lambda i: ((i * tm) // (T * nb), 0, 0))],
        out_specs=pl.BlockSpec((tm, D_MODEL), lambda i: (i, 0)),
        scratch_shapes=[pltpu.VMEM((TOP_K * tm * ROW_TILE, LANES), F32), pltpu.SemaphoreType.DMA],
        compiler_params=_cparams(("arbitrary",)),
        name="moe_combine",
    )(pos_t.reshape(nt, 1, TOP_K * tm), ys, top_w, x1, m)


def _route(top_i, rank, counts, bm):
    N = top_i.shape[0]
    NK = N * TOP_K
    padded = (counts + bm - 1) // bm * bm
    pad_end = jnp.cumsum(padded)
    pad_start = pad_end - padded
    onehot = top_i[:, :, None] == jnp.arange(N_EXPERTS, dtype=I32)[None, None, :]
    pos = rank + jnp.sum(jnp.where(onehot, pad_start[None, None, :], 0), axis=2)
    n_rows = (-(-NK // bm)) * bm + N_EXPERTS * bm
    nblk = n_rows // bm
    starts = jnp.arange(nblk, dtype=I32) * bm
    block_e = jnp.minimum(jnp.sum((pad_end[None, :] <= starts[:, None]).astype(I32), axis=1), N_EXPERTS - 1)
    n_used = (pad_end[-1:] // bm).astype(I32)
    return pos.reshape(NK).astype(I32), n_rows, block_e.astype(I32), n_used


def _moe(h2, top_i, top_w, rank, counts, x1, m, ew, B, T):
    M = B * T
    bm = 256 if M * TOP_K >= 256 * N_EXPERTS * 4 else 128
    pos, n_rows, block_e, n_used = _route(top_i[:, :TOP_K], rank[:, :TOP_K], counts[0, :N_EXPERTS], bm)
    tm, nb = _row_tiling(B, T, 256)
    xs = _dispatch(h2, pos, n_rows, tm)
    ys = _ffn(xs, block_e, n_used, *ew, bm)
    pos_t = pos.reshape(M // tm, tm, TOP_K).transpose(0, 2, 1)
    return _combine(pos_t, ys, top_w, x1, m, B, T, tm, nb)


def _prep_layer(P, l, w2_all):
    w_t = P['w_in'][l].T

    def rows(name):
        o, s = _SRC[name]
        return w_t[o:o + s]

    zeros = lambda n: jnp.zeros((n, D_MODEL), F32)
    w_r = jnp.concatenate([
        rows('gla_q'), rows('gla_k'), rows('gla_v'), rows('gla_r'),
        rows('diff_q'), rows('diff_k'), rows('diff_v'),
        rows('dsa_q'), rows('dsa_k'), rows('dsa_v'),
        rows('idx_q'), rows('idx_k'), rows('idx_w'), zeros(60), rows('gla_a'), zeros(112),
        rows('gates')], axis=0).astype(BF16)
    assert w_r.shape[0] == Z_WIDTH
    W = GLA_HEADS * GLA_DK
    wa = jnp.zeros((LANES, W), F32).at[:GLA_GATE_RANK].set(P['w_gla_a2'][l]).astype(BF16)
    w1g, w1u = _w1prep(P['w_mlp1'], l)
    b1 = P['b_mlp1'][l]
    return dict(
        w_in=w_r, wa=wa, ba=P['b_gla_a2'][l].reshape(1, W),
        g_gla=jnp.tile(P['g_gla_out'][l], GLA_HEADS).reshape(1, 512),
        gains=(jnp.tile(P['g_diff_q'][l], 8).reshape(1, 512), jnp.tile(P['g_diff_k'][l], 8).reshape(1, 512),
               jnp.tile(P['g_dsa_q'][l], 4).reshape(1, 512), jnp.tile(P['g_dsa_k'][l], 4).reshape(1, 512),
               jnp.tile(P['g_idx_k'][l], 2).reshape(1, LANES)),
        lamv=jnp.stack([P['lambda_q1'][l], P['lambda_k1'][l], P['lambda_q2'][l], P['lambda_k2'][l]]),
        g_diff=P['g_diff_out'][l].reshape(1, LANES),
        merge=(P['w_branch_gla'][l].astype(BF16), P['w_branch_diff'][l].astype(BF16),
               P['w_branch_dsa'][l].astype(BF16), P['w_out'][l].astype(BF16),
               P['g_norm2'][l].reshape(1, D_MODEL),
               jnp.zeros((D_MODEL, LANES), F32).at[:, :N_EXPERTS].set(P['w_router'][l]).astype(BF16),
               jnp.full((1, LANES), NEG_INF, F32).at[0, :N_EXPERTS].set(P['b_router'][l])),
        experts=(w1g, b1[:, None, 0::2], w1u, b1[:, None, 1::2], w2_all, P['b_mlp2'][l][:, None, :], l),
    )


def _trunk(x, c, past, P, prepped):
    B, T, _ = x.shape
    M = B * T
    x2 = x.reshape(M, D_MODEL)
    past_len = 0 if past is None else past[0].shape[2]
    pos = past_len + jnp.arange(T, dtype=I32)
    tabs64 = _rope_tables(pos, 64)
    tabs128 = _rope_tables(pos, 128)
    W = GLA_HEADS * GLA_DK
    if past is not None:
        L, _, PL = past[0].shape[:3]
        pdk = past[0].transpose(0, 1, 3, 4, 5, 2)
        pdv = past[1].reshape(L, B, PL * DIFF_HEADS, DIFF_DV)
        pck = past[2].reshape(L, B, PL * DSA_HEADS, DSA_DH)
        pcv = past[3].reshape(L, B, PL * DSA_HEADS, DSA_DH)
        pik = past[4].transpose(0, 1, 3, 2)
        s0_all = past[5].transpose(0, 1, 4, 2, 3).reshape(past[5].shape[0], B, GLA_DV, W)
    per_layer = []
    for l, pp in enumerate(prepped):
        lam_init = 0.8 - 0.6 * math.exp(-0.3 * l)
        m = _ada(c, P['w_ada'], P['b_ada'], l)
        z = _inproj(x2, m, P['g_norm1'][l], pp['w_in'], B, T)
        (dq, dk32, dkb, dvb, cq, ck32, ckb, cvb, iq, ik32, ikr, iw, dv32, cv32) = _post(
            z, tabs64, tabs128, pp['gains'], B, T)
        s0t = jnp.zeros((B, GLA_DV, W), F32) if past is None else s0_all[l]
        o_gla, st = _gla(z, pp['wa'], pp['ba'], pp['g_gla'], s0t, B, T)
        if past is None:
            o_diff = _diff_prompt(dq, dkb, dvb, pp['lamv'], pp['g_diff'], B, T, lam_init)
            o_dsa = _dsa_prompt(cq, iq, iw, ckb, cvb, ikr, B, T)
        else:
            o_diff = _diff_sample(dq, dkb, dvb, pdk, pdv, l, pp['lamv'], pp['g_diff'], B, T, lam_init)
            o_dsa = _dsa_sample(cq, iq, iw, ckb, cvb, ikr, pck, pcv, pik, l, B, T)
        x1, h2, top_i, top_w, rank, counts = _merge(o_gla, o_diff, o_dsa, z, x2, m, pp['merge'], B, T)
        x2 = _moe(h2, top_i, top_w, rank, counts, x1, m, pp['experts'], B, T)
        s_gla = st.reshape(B, GLA_DV, GLA_HEADS, GLA_DK).transpose(0, 2, 3, 1)
        per_layer.append((dk32.reshape(B, T, DIFF_HEADS, 2, DIFF_DH),
                          dv32.reshape(B, T, DIFF_HEADS, DIFF_DV),
                          ck32.reshape(B, T, DSA_HEADS, DSA_DH),
                          cv32.reshape(B, T, DSA_HEADS, DSA_DH),
                          ik32.reshape(B, T, IDX_DH),
                          s_gla))
    stacked = tuple(jnp.stack([st[i] for st in per_layer]) for i in range(6))
    return x2.reshape(B, T, D_MODEL), stacked


def kernel(x_prompt, x_sample, cache_diff_k, cache_diff_v, cache_dsa_k, cache_dsa_v, cache_dsa_idx_k,
           state_gla, c_prompt, c_sample, w_ada, b_ada, g_norm1, g_norm2, w_in, w_gla_a2, b_gla_a2,
           g_gla_out, g_diff_q, g_diff_k, lambda_q1, lambda_k1, lambda_q2, lambda_k2, g_diff_out,
           g_dsa_q, g_dsa_k, g_idx_k, w_branch_gla, w_branch_diff, w_branch_dsa, w_out, w_router,
           b_router, w_mlp1, b_mlp1, w_mlp2, b_mlp2):
    P = dict(w_ada=w_ada, b_ada=b_ada, g_norm1=g_norm1, g_norm2=g_norm2, w_in=w_in, w_gla_a2=w_gla_a2,
             b_gla_a2=b_gla_a2, g_gla_out=g_gla_out, g_diff_q=g_diff_q, g_diff_k=g_diff_k,
             lambda_q1=lambda_q1, lambda_k1=lambda_k1, lambda_q2=lambda_q2, lambda_k2=lambda_k2,
             g_diff_out=g_diff_out, g_dsa_q=g_dsa_q, g_dsa_k=g_dsa_k, g_idx_k=g_idx_k,
             w_branch_gla=w_branch_gla, w_branch_diff=w_branch_diff, w_branch_dsa=w_branch_dsa,
             w_out=w_out, w_router=w_router, b_router=b_router, w_mlp1=w_mlp1, b_mlp1=b_mlp1,
             w_mlp2=w_mlp2, b_mlp2=b_mlp2)
    depth = w_in.shape[0]
    w2_all = w_mlp2.astype(BF16)
    prepped = [_prep_layer(P, l, w2_all) for l in range(depth)]
    y_prompt, new_p = _trunk(x_prompt, c_prompt, None, P, prepped)
    y_sample, new_s = _trunk(
        x_sample, c_sample,
        (cache_diff_k, cache_diff_v, cache_dsa_k, cache_dsa_v, cache_dsa_idx_k, state_gla), P, prepped)
    return (y_prompt, y_sample) + new_p + new_s
```

```python
import functools
import math

import numpy as np
import jax
import jax.numpy as jnp
from jax import lax
from jax.experimental import pallas as pl
from jax.experimental.pallas import tpu as pltpu

F32 = jnp.float32
BF16 = jnp.bfloat16
I32 = jnp.int32

D_MODEL = 1024
CHUNK = 64
ROPE_THETA = 500000.0
ROPE_FRACTION = 4
EPS = 1e-6
NEG_INF = -1e30

GLA_HEADS = 4
GLA_DK = 64
GLA_DV = 128
GLA_GATE_RANK = 16
GLA_GATE_TAU = 16.0
GLA_SUB = 16
DIFF_HEADS = 4
DIFF_DH = 64
DIFF_DV = 128
DSA_HEADS = 4
DSA_DH = 128
IDX_HEADS = 4
IDX_DH = 64
DSA_TOPK_MAX = 256
N_EXPERTS = 32
TOP_K = 4
D_FF = 1024
SWIGLU_LIMIT = 7.0
SWIGLU_ALPHA = 1.702

LANES = 128
ROW_TILE = 8
VMEM_LIMIT = 48 * 1024 * 1024

Z_GQK, Z_GV, Z_GR = 0, 512, 1024
Z_DQ, Z_DK, Z_DV = 1536, 2048, 2560
Z_CQ, Z_CK, Z_CV = 3072, 3584, 4096
Z_MISC = 4608
Z_GATES = 5120
Z_WIDTH = 8192
MISC_IK = 256
MISC_GA = 384

_SRC = {}
_off = 0
for _name, _size in (
        ('gla_q', 256), ('gla_k', 256), ('gla_v', 512), ('gla_a', 16), ('gla_r', 512),
        ('diff_q', 512), ('diff_k', 512), ('diff_v', 512),
        ('dsa_q', 512), ('dsa_k', 512), ('dsa_v', 512),
        ('idx_q', 256), ('idx_k', 64), ('idx_w', 4), ('gates', 3072)):
    _SRC[_name] = (_off, _size)
    _off += _size


def _cparams(sem, vmem=VMEM_LIMIT):
    return pltpu.CompilerParams(dimension_semantics=sem, vmem_limit_bytes=vmem)


def _dot(a, b):
    return jnp.dot(a, b, preferred_element_type=F32)


def _dot_nt(a, b):
    return lax.dot_general(a, b, (((1,), (1,)), ((), ())), preferred_element_type=F32)


def _dot_tn(a, b):
    return lax.dot_general(a, b, (((0,), (0,)), ((), ())), preferred_element_type=F32)


def _silu(x):
    return x * jax.nn.sigmoid(x)


def _row_tiling(B, T, target):
    if T >= target:
        assert T % target == 0
        return target, 1
    nb = 1
    for cand in range(1, B + 1):
        if B % cand == 0 and cand * T <= target:
            nb = cand
    return nb * T, nb


def _modulate(xn, m, shift_i, scale_i, nb):
    tm = xn.shape[0]
    if nb == 1:
        return xn * (1.0 + m[0, scale_i:scale_i + 1, :]) + m[0, shift_i:shift_i + 1, :]
    x3 = xn.reshape(nb, tm // nb, D_MODEL)
    h = x3 * (1.0 + m[:, scale_i:scale_i + 1, :]) + m[:, shift_i:shift_i + 1, :]
    return h.reshape(tm, D_MODEL)


def _gate_rows(y, m, gate_i, nb):
    tm = y.shape[0]
    if nb == 1:
        return y * m[0, gate_i:gate_i + 1, :]
    return (y.reshape(nb, tm // nb, D_MODEL) * m[:, gate_i:gate_i + 1, :]).reshape(tm, D_MODEL)


def _ada_kernel(c_ref, w_ref, b_ref, o_ref):
    s = _silu(c_ref[...])
    o_ref[...] = _dot(s.astype(BF16), w_ref[...].astype(BF16)) + b_ref[...]


def _ada(c, w, b, layer):
    B = c.shape[0]
    out = pl.pallas_call(
        _ada_kernel,
        out_shape=jax.ShapeDtypeStruct((B, 6 * D_MODEL), F32),
        grid=(6,),
        in_specs=[pl.BlockSpec((B, D_MODEL), lambda j: (0, 0)),
                  pl.BlockSpec((None, D_MODEL, D_MODEL), lambda j: (layer, 0, j)),
                  pl.BlockSpec((None, 1, D_MODEL), lambda j: (layer, 0, j))],
        out_specs=pl.BlockSpec((B, D_MODEL), lambda j: (0, j)),
        compiler_params=_cparams(("arbitrary",)),
        name="ada",
    )(c, w, b.reshape(b.shape[0], 1, -1))
    return out.reshape(B, 6, D_MODEL)


def _inproj_kernel(x_ref, m_ref, g_ref, w_ref, o_ref, h_scr, *, nb):
    @pl.when(pl.program_id(1) == 0)
    def _():
        x = x_ref[...]
        xn = x * lax.rsqrt(jnp.mean(x * x, axis=-1, keepdims=True) + EPS) * g_ref[...]
        h_scr[...] = _modulate(xn, m_ref[...], 0, 1, nb).astype(BF16)

    o_ref[...] = _dot_nt(h_scr[...], w_ref[...])


def _inproj(x2, m, g, wt, B, T):
    M = B * T
    tm, nb = _row_tiling(B, T, 1024)
    tn = 1024
    return pl.pallas_call(
        functools.partial(_inproj_kernel, nb=nb),
        out_shape=jax.ShapeDtypeStruct((M, Z_WIDTH), F32),
        grid=(M // tm, Z_WIDTH // tn),
        in_specs=[pl.BlockSpec((tm, D_MODEL), lambda i, j: (i, 0)),
                  pl.BlockSpec((nb, 6, D_MODEL), lambda i, j: ((i * tm) // (T * nb), 0, 0)),
                  pl.BlockSpec((1, D_MODEL), lambda i, j: (0, 0)),
                  pl.BlockSpec((tn, D_MODEL), lambda i, j: (j, 0))],
        out_specs=pl.BlockSpec((tm, tn), lambda i, j: (i, j)),
        scratch_shapes=[pltpu.VMEM((tm, D_MODEL), BF16)],
        compiler_params=_cparams(("arbitrary", "arbitrary")),
        name="inproj",
    )(x2, m, g.reshape(1, -1), wt)


def _rope_tables(pos, d):
    rot = d // ROPE_FRACTION
    half = rot // 2
    T = pos.shape[0]
    inv_freq = ROPE_THETA ** (-jnp.arange(half, dtype=F32) / half)
    ang = pos.astype(F32)[:, None] * inv_freq[None, :]
    cos, sin = jnp.cos(ang), jnp.sin(ang)
    c = jnp.concatenate([cos, cos, jnp.ones((T, d - rot), F32)], axis=1)
    a = jnp.concatenate([-sin, jnp.zeros((T, d - half), F32)], axis=1)
    b = jnp.concatenate([jnp.zeros((T, half), F32), sin, jnp.zeros((T, d - rot), F32)], axis=1)
    reps = LANES // d
    return tuple(jnp.tile(t, (1, reps)) for t in (c, a, b))


def _rope128(xs, tabs, half):
    c, a, b = tabs
    return xs * c + pltpu.roll(xs, LANES - half, 1) * a + pltpu.roll(xs, half, 1) * b


def _norm_rope(x, g, tabs, d, norm=True):
    tm, W = x.shape
    half = d // ROPE_FRACTION // 2
    lo = lax.broadcasted_iota(I32, (tm, LANES), 1) < 64
    outs = []
    for s in range(W // LANES):
        xs = x[:, s * LANES:(s + 1) * LANES]
        if norm:
            sq = xs * xs
            if d == LANES:
                r = lax.rsqrt(jnp.sum(sq, axis=1, keepdims=True) * (1.0 / d) + EPS)
            else:
                s_lo = jnp.sum(jnp.where(lo, sq, 0.0), axis=1, keepdims=True)
                s_hi = jnp.sum(jnp.where(lo, 0.0, sq), axis=1, keepdims=True)
                r = jnp.where(lo, lax.rsqrt(s_lo * (1.0 / d) + EPS), lax.rsqrt(s_hi * (1.0 / d) + EPS))
            xs = xs * r * g[:, s * LANES:(s + 1) * LANES]
        outs.append(_rope128(xs, tabs, half))
    return outs[0] if len(outs) == 1 else jnp.concatenate(outs, axis=1)


def _post_kernel(dq_ref, dk_ref, dv_ref, cq_ref, ck_ref, cv_ref, mi_ref,
                 c64_ref, a64_ref, b64_ref, c128_ref, a128_ref, b128_ref,
                 gdq_ref, gdk_ref, gcq_ref, gck_ref, gik_ref,
                 dq_o, dk32_o, dkb_o, dvb_o, cq_o, ck32_o, ckb_o, cvb_o, iq_o, ik32_o, ikr_o, iw_o,
                 dv32_o, cv32_o):
    t64 = (c64_ref[...], a64_ref[...], b64_ref[...])
    t128 = (c128_ref[...], a128_ref[...], b128_ref[...])
    dq = _norm_rope(dq_ref[...], gdq_ref[...], t64, 64)
    dq_o[...] = (dq * (DIFF_DH ** -0.5)).astype(BF16)
    dk = _norm_rope(dk_ref[...], gdk_ref[...], t64, 64)
    dk32_o[...] = dk
    dkb_o[...] = dk.astype(BF16)
    dv = dv_ref[...]
    dv32_o[...] = dv
    dvb_o[...] = dv.astype(BF16)
    cq = _norm_rope(cq_ref[...], gcq_ref[...], t128, 128)
    cq_o[...] = (cq * (DSA_DH ** -0.5)).astype(BF16)
    ck = _norm_rope(ck_ref[...], gck_ref[...], t128, 128)
    ck32_o[...] = ck
    ckb_o[...] = ck.astype(BF16)
    cv = cv_ref[...]
    cv32_o[...] = cv
    cvb_o[...] = cv.astype(BF16)
    mi = mi_ref[...]
    iq = _norm_rope(mi[:, 0:256], None, t64, 64, norm=False)
    iq_o[...] = (iq * (IDX_DH ** -0.5)).astype(BF16)
    mk = mi[:, MISC_IK:MISC_IK + LANES]
    tm = mk.shape[0]
    lane = lax.broadcasted_iota(I32, (tm, LANES), 1)
    lo = lane < 64
    ssq = jnp.sum(jnp.where(lo, mk * mk, 0.0), axis=1, keepdims=True)
    ikn = mk * lax.rsqrt(ssq * (1.0 / IDX_DH) + EPS) * gik_ref[...]
    ik = _rope128(ikn, t64, IDX_DH // ROPE_FRACTION // 2)
    ik32_o[...] = ik[:, 0:IDX_DH]
    ik2 = jnp.where(lo, ik, pltpu.roll(ik, 64, 1))
    ikr_o[...] = jnp.concatenate([ik2, ik2], axis=1).astype(BF16)
    iw_o[...] = jnp.where(lane < IDX_HEADS, pltpu.roll(mk, 64, 1) * (IDX_HEADS ** -0.5), 0.0)


def _post(z, tabs64, tabs128, gains, B, T):
    M = B * T
    tm = min(T, 512)
    assert T % tm == 0
    npos = T // tm

    def zspec(off):
        return pl.BlockSpec((tm, 512), lambda i: (i, off // 512))

    tab_spec = pl.BlockSpec((tm, LANES), lambda i: (i % npos, 0))

    def gspec(w):
        return pl.BlockSpec((1, w), lambda i: (0, 0))

    def ospec(w):
        return pl.BlockSpec((tm, w), lambda i: (i, 0))

    outs = [(512, BF16), (512, F32), (512, BF16), (512, BF16),
            (512, BF16), (512, F32), (512, BF16), (512, BF16),
            (256, BF16), (IDX_DH, F32), (256, BF16), (LANES, F32), (512, F32), (512, F32)]
    return pl.pallas_call(
        _post_kernel,
        out_shape=[jax.ShapeDtypeStruct((M, w), dt) for w, dt in outs],
        grid=(M // tm,),
        in_specs=[zspec(Z_DQ), zspec(Z_DK), zspec(Z_DV), zspec(Z_CQ), zspec(Z_CK), zspec(Z_CV), zspec(Z_MISC)]
                 + [tab_spec] * 6 + [gspec(512)] * 4 + [gspec(LANES)],
        out_specs=[ospec(w) for w, _ in outs],
        compiler_params=_cparams(("arbitrary",)),
        name="post",
    )(z, z, z, z, z, z, z, *tabs64, *tabs128, *gains)


def _split3(x):
    hi = x.astype(BF16)
    r = x - hi.astype(F32)
    mid = r.astype(BF16)
    lo = (r - mid.astype(F32)).astype(BF16)
    return hi, mid, lo


def _gla_kernel(qk_ref, v_ref, r_ref, ga_ref, wa_ref, ba_ref, g_ref, s0_ref, o_ref, st_ref, st_scr, *, nct):
    ti = pl.program_id(1)

    @pl.when(ti == 0)
    def _():
        st_scr[...] = s0_ref[...]

    C = CHUNK
    W = GLA_HEADS * GLA_DK
    row = lax.broadcasted_iota(I32, (C, C), 0)
    col = lax.broadcasted_iota(I32, (C, C), 1)
    tri = col <= row
    tri_bf = tri.astype(BF16)
    lane = lax.broadcasted_iota(I32, (1, W), 1)
    hmask = [(lane // GLA_DK) == h for h in range(GLA_HEADS)]
    rowi = lax.broadcasted_iota(I32, (C, W), 0)
    wa = wa_ref[...]
    ba = ba_ref[...]
    g = g_ref[...]

    def chunk(c, carry):
        rows = pl.ds(pl.multiple_of(c * C, C), C)
        qk = qk_ref[rows, :]
        q = qk[:, :W] * (GLA_DK ** -0.5)
        k = qk[:, W:]
        v = v_ref[rows, :].astype(BF16)
        pre = _dot(ga_ref[rows, :].astype(BF16), wa) + ba
        la = (jnp.minimum(pre, 0.0) - jnp.log(1.0 + jnp.exp(-jnp.abs(pre)))) * (1.0 / GLA_GATE_TAU)
        hi, mid, lo = _split3(la)
        b = _dot(tri_bf, hi) + _dot(tri_bf, mid) + _dot(tri_bf, lo)
        st = st_scr[...]
        st_bf = st.astype(BF16)
        qe = q * jnp.exp(b)
        b_end = b[C - 1:C, :]
        kend = k * jnp.exp(b_end - b)
        att_parts = [[] for _ in range(GLA_HEADS)]
        for s in range(C // GLA_SUB):
            r0 = s * GLA_SUB
            br = b[r0:r0 + 1, :]
            qs = q[r0:r0 + GLA_SUB, :] * jnp.exp(b[r0:r0 + GLA_SUB, :] - br)
            ks = k * jnp.exp(br - b)
            if r0 + GLA_SUB < C:
                ks = jnp.where(rowi < r0 + GLA_SUB, ks, 0.0)
            ks = ks.astype(BF16)
            for h in range(GLA_HEADS):
                att_parts[h].append(_dot_nt(jnp.where(hmask[h], qs, 0.0).astype(BF16), ks))
        upd = None
        for h in range(GLA_HEADS):
            hs = slice(h * GLA_DV, (h + 1) * GLA_DV)
            att = jnp.where(tri, jnp.concatenate(att_parts[h], axis=0), 0.0)
            o = _dot(att.astype(BF16), v[:, hs]) + _dot_nt(jnp.where(hmask[h], qe, 0.0).astype(BF16), st_bf)
            u = _dot_tn(v[:, hs], jnp.where(hmask[h], kend, 0.0).astype(BF16))
            upd = u if upd is None else upd + u
            y = o * lax.rsqrt(jnp.mean(o * o, axis=-1, keepdims=True) + EPS) * g[:, hs]
            o_ref[rows, hs] = (y * _silu(r_ref[rows, hs])).astype(BF16)
        st_scr[...] = st * jnp.exp(b_end) + upd
        return carry

    lax.fori_loop(0, nct, chunk, 0)

    @pl.when(ti == pl.num_programs(1) - 1)
    def _():
        st_ref[...] = st_scr[...]


def _gla(z, wa, ba, g, s0t, B, T):
    M = B * T
    tt = min(T, 512)
    assert T % tt == 0 and tt % CHUNK == 0
    nt = T // tt
    W = GLA_HEADS * GLA_DK
    return pl.pallas_call(
        functools.partial(_gla_kernel, nct=tt // CHUNK),
        out_shape=[jax.ShapeDtypeStruct((M, 512), BF16),
                   jax.ShapeDtypeStruct((B, GLA_DV, W), F32)],
        grid=(B, nt),
        in_specs=[pl.BlockSpec((tt, 512), lambda b, t: (b * nt + t, Z_GQK // 512)),
                  pl.BlockSpec((tt, 512), lambda b, t: (b * nt + t, Z_GV // 512)),
                  pl.BlockSpec((tt, 512), lambda b, t: (b * nt + t, Z_GR // 512)),
                  pl.BlockSpec((tt, LANES), lambda b, t: (b * nt + t, (Z_MISC + MISC_GA) // LANES)),
                  pl.BlockSpec((LANES, W), lambda b, t: (0, 0)),
                  pl.BlockSpec((1, W), lambda b, t: (0, 0)),
                  pl.BlockSpec((1, 512), lambda b, t: (0, 0)),
                  pl.BlockSpec((None, GLA_DV, W), lambda b, t: (b, 0, 0))],
        out_specs=[pl.BlockSpec((tt, 512), lambda b, t: (b * nt + t, 0)),
                   pl.BlockSpec((None, GLA_DV, W), lambda b, t: (b, 0, 0))],
        scratch_shapes=[pltpu.VMEM((GLA_DV, W), F32)],
        compiler_params=_cparams(("arbitrary", "arbitrary")),
        name="gla",
    )(z, z, z, z, wa, ba, g, s0t)


def _diff_lambda(lam_ref, lam_init):
    lv = lam_ref[...]
    a = jnp.sum(lv[0:1, :] * lv[1:2, :], axis=1, keepdims=True)
    b = jnp.sum(lv[2:3, :] * lv[3:4, :], axis=1, keepdims=True)
    return jnp.exp(a) - jnp.exp(b) + lam_init


def _diff_finish(o0, o1, lam, g, lam_init):
    o = o0 - lam * o1
    return o * lax.rsqrt(jnp.mean(o * o, axis=-1, keepdims=True) + EPS) * g * (1.0 - lam_init)


def _diff_prompt_kernel(qi_ref, kj_ref, q_ref, k_ref, v_ref, lam_ref, g_ref, o_ref,
                        q2_scr, m_scr, l_scr, acc_scr, *, lam_init, tq):
    n = pl.program_id(2)
    qi = qi_ref[n]
    kj = kj_ref[n]

    @pl.when(kj == 0)
    def _():
        q = q_ref[...]
        lane = lax.broadcasted_iota(I32, (1, LANES), 1)
        q2_scr[0:tq, :] = jnp.where(lane < DIFF_DH, q, jnp.zeros_like(q))
        q2_scr[tq:2 * tq, :] = jnp.where(lane < DIFF_DH, jnp.zeros_like(q), q)
        m_scr[...] = jnp.full(m_scr.shape, NEG_INF, F32)
        l_scr[...] = jnp.zeros(l_scr.shape, F32)
        acc_scr[...] = jnp.zeros(acc_scr.shape, F32)

    def step(diagonal):
        s = _dot_nt(q2_scr[...], k_ref[...])
        if diagonal:
            qc = lax.broadcasted_iota(I32, (tq, tq), 0) // CHUNK
            kc = lax.broadcasted_iota(I32, (tq, tq), 1) // CHUNK
            vis = kc <= qc
            s = jnp.where(jnp.concatenate([vis, vis], axis=0), s, NEG_INF)
        m_prev = m_scr[...]
        m_new = jnp.maximum(m_prev, jnp.max(s, axis=1, keepdims=True))
        alpha = jnp.exp(m_prev - m_new)
        p = jnp.exp(s - jnp.concatenate([m_new] * (tq // LANES), axis=1))
        l_scr[...] = alpha * l_scr[...] + jnp.sum(p, axis=1, keepdims=True)
        acc_scr[...] = alpha * acc_scr[...] + _dot(p.astype(BF16), v_ref[...])
        m_scr[...] = m_new

    @pl.when(kj < qi)
    def _():
        step(False)

    @pl.when(kj == qi)
    def _():
        step(True)
        lam = _diff_lambda(lam_ref, lam_init)
        o = acc_scr[...] / l_scr[...]
        o_ref[...] = _diff_finish(o[0:tq], o[tq:2 * tq], lam, g_ref[...], lam_init).astype(BF16)


def _diff_prompt(dq, dk, dv, lamv, g, B, T, lam_init):
    M = B * T
    tq = min(T, 512)
    nq = T // tq
    pairs = [(i, j) for i in range(nq) for j in range(i + 1)]
    qi = jnp.asarray([p[0] for p in pairs], I32)
    kj = jnp.asarray([p[1] for p in pairs], I32)
    grid_spec = pltpu.PrefetchScalarGridSpec(
        num_scalar_prefetch=2,
        grid=(B, DIFF_HEADS, len(pairs)),
        in_specs=[pl.BlockSpec((tq, LANES), lambda b, h, n, qi, kj: (b * nq + qi[n], h)),
                  pl.BlockSpec((tq, LANES), lambda b, h, n, qi, kj: (b * nq + kj[n], h)),
                  pl.BlockSpec((tq, LANES), lambda b, h, n, qi, kj: (b * nq + kj[n], h)),
                  pl.BlockSpec((4, DIFF_DH), lambda b, h, n, qi, kj: (0, 0)),
                  pl.BlockSpec((1, LANES), lambda b, h, n, qi, kj: (0, 0))],
        out_specs=pl.BlockSpec((tq, LANES), lambda b, h, n, qi, kj: (b * nq + qi[n], h)),
        scratch_shapes=[pltpu.VMEM((2 * tq, LANES), BF16), pltpu.VMEM((2 * tq, LANES), F32),
                        pltpu.VMEM((2 * tq, LANES), F32), pltpu.VMEM((2 * tq, LANES), F32)])
    return pl.pallas_call(
        functools.partial(_diff_prompt_kernel, lam_init=lam_init, tq=tq),
        out_shape=jax.ShapeDtypeStruct((M, 512), BF16),
        grid_spec=grid_spec,
        compiler_params=_cparams(("arbitrary", "arbitrary", "arbitrary")),
        name="diff_prompt",
    )(qi, kj, dq, dk, dv, lamv, g)


def _diff_sample_kernel(q_ref, kn_ref, vn_ref, kp_ref, vp_ref, lam_ref, g_ref, o_ref, *, lam_init, past_len):
    T = q_ref.shape[0]
    lam = _diff_lambda(lam_ref, lam_init)
    lane = lax.broadcasted_iota(I32, (1, LANES), 1)
    qpos = past_len + lax.broadcasted_iota(I32, (T, T), 0)
    kpos = past_len + lax.broadcasted_iota(I32, (T, T), 1)
    vis_new = (kpos // CHUNK) <= (qpos // CHUNK)
    P = past_len
    for h in range(DIFF_HEADS):
        hs = slice(h * LANES, (h + 1) * LANES)
        q = q_ref[:, hs]
        kpt = kp_ref[h].reshape(2 * DIFF_DH, P).astype(BF16)
        vp = vp_ref[pl.ds(h, P, stride=DIFF_HEADS), :].astype(BF16)
        kn = kn_ref[:, hs]
        vn = vn_ref[:, hs]
        outs = []
        for c in range(2):
            qc = jnp.where((lane < DIFF_DH) == (c == 0), q, jnp.zeros_like(q))
            sp = _dot(qc, kpt)
            sn = jnp.where(vis_new, _dot_nt(qc, kn), NEG_INF)
            m = jnp.maximum(jnp.max(sp, axis=1, keepdims=True), jnp.max(sn, axis=1, keepdims=True))
            pp = jnp.exp(sp - m)
            pn = jnp.exp(sn - m)
            l = jnp.sum(pp, axis=1, keepdims=True) + jnp.sum(pn, axis=1, keepdims=True)
            outs.append((_dot(pp.astype(BF16), vp) + _dot(pn.astype(BF16), vn)) / l)
        o_ref[:, hs] = _diff_finish(outs[0], outs[1], lam, g_ref[...], lam_init).astype(BF16)


def _diff_sample(dq, dk, dv, past_kt, past_v, layer, lamv, g, B, T, lam_init):
    M = B * T
    P = past_kt.shape[-1]
    return pl.pallas_call(
        functools.partial(_diff_sample_kernel, lam_init=lam_init, past_len=P),
        out_shape=jax.ShapeDtypeStruct((M, 512), BF16),
        grid=(B,),
        in_specs=[pl.BlockSpec((T, 512), lambda b: (b, 0)),
                  pl.BlockSpec((T, 512), lambda b: (b, 0)),
                  pl.BlockSpec((T, 512), lambda b: (b, 0)),
                  pl.BlockSpec((None, None, DIFF_HEADS, 2, DIFF_DH, P), lambda b: (layer, b, 0, 0, 0, 0)),
                  pl.BlockSpec((None, None, P * DIFF_HEADS, DIFF_DV), lambda b: (layer, b, 0, 0)),
                  pl.BlockSpec((4, DIFF_DH), lambda b: (0, 0)),
                  pl.BlockSpec((1, LANES), lambda b: (0, 0))],
        out_specs=pl.BlockSpec((T, 512), lambda b: (b, 0)),
        compiler_params=_cparams(("arbitrary",)),
        name="diff_sample",
    )(dq, dk, dv, past_kt, past_v, lamv, g)


INT_MIN = -2 ** 31


def _idx_score(iq, iw, ikr, transposed=False):
    lane = lax.broadcasted_iota(I32, (1, IDX_HEADS * IDX_DH), 1)
    sc = None
    for i in range(IDX_HEADS):
        iqi = jnp.where((lane // IDX_DH) == i, iq, jnp.zeros_like(iq))
        lg = _dot(iqi, ikr) if transposed else _dot_nt(iqi, ikr)
        t = iw[:, i:i + 1] * jnp.maximum(lg, 0.0)
        sc = t if sc is None else sc + t
    return jnp.where(sc == 0.0, 0.0, sc)


def _order_key(score):
    bits = pltpu.bitcast(score, I32)
    return jnp.where(bits < 0, bits ^ 0x7FFFFFFF, bits)


def _count(mask):
    return jnp.sum(mask.astype(F32), axis=1, keepdims=True)


def _f32_key(x):
    b = int(np.float32(x).view(np.int32))
    return b ^ 0x7FFFFFFF if b < 0 else b


def _key_value(k):
    return pltpu.bitcast(jnp.where(k < 0, k ^ 0x7FFFFFFF, k), F32)


def _topk_threshold(key_refs, smax, smin, n_sel, n_keys, probe_masked):
    tq = key_refs[0].shape[0]
    n = float(n_sel)

    def count_ge(cand):
        tot = None
        for kr in key_refs:
            c = _count(kr[...] >= cand)
            tot = c if tot is None else tot + c
        return tot

    def update(st, cand):
        lo, cnt_lo, hi, cnt_hi = st
        c = count_ge(cand)
        ge = c >= n
        up = ge & (cand > lo)
        dn = jnp.logical_not(ge) & (cand < hi)
        return (jnp.where(up, cand, lo), jnp.where(up, c, cnt_lo),
                jnp.where(dn, cand, hi), jnp.where(dn, c, cnt_hi))

    def settled(st):
        lo, cnt_lo, hi, _ = st
        return (cnt_lo == n) | ((hi - 1) == lo)

    def unsettled_any(st):
        return jnp.max(jnp.where(settled(st), 0.0, 1.0)) > 0.0

    col = lambda v, dt: jnp.full((tq, 1), v, dt)
    kmax = _order_key(smax)
    hi0 = jnp.where(kmax == 2 ** 31 - 1, kmax, kmax + 1)
    st = (col(INT_MIN, I32), col(float(n_keys), F32), hi0, col(0.0, F32))
    fixed = ((_f32_key(NEG_INF), _f32_key(NEG_INF) + 1) if probe_masked else ()) + (0, 1)
    for c in fixed:
        st = update(st, col(c, I32))
    st = update(st, _order_key(smin))
    log_n = math.log(n)

    def cond(carry):
        it, _, go = carry
        return jnp.logical_and(it < 100, go)

    def body(carry):
        it, st, _ = carry
        lo, cnt_lo, hi, cnt_hi = st
        a = jnp.log(cnt_lo + 0.5)
        frac = (a - log_n) / jnp.maximum(a - jnp.log(cnt_hi + 0.5), 1e-9)
        v_lo = _key_value(lo)
        guess = _order_key(v_lo + frac * (_key_value(hi) - v_lo))
        mid = (lo >> 1) + (hi >> 1) + (lo & hi & 1)
        cand = jnp.minimum(jnp.maximum(jnp.where(lax.rem(it, 3) == 2, mid, guess), lo + 1), hi - 1)
        st = update(st, jnp.where(settled(st), lo, cand))
        return it + 1, st, unsettled_any(st)

    _, st, _ = lax.while_loop(cond, body, (jnp.int32(0), st, unsettled_any(st)))
    lo, cnt_lo, hi, cnt_hi = st
    tied = ((hi - 1) == lo) & (cnt_lo > n)
    return lo, jnp.where(tied, n - cnt_hi, float(n_keys))


def _row_extremes(score, vis):
    if vis is None:
        return jnp.max(score, axis=1, keepdims=True), jnp.min(score, axis=1, keepdims=True)
    return (jnp.max(jnp.where(vis, score, -jnp.inf), axis=1, keepdims=True),
            jnp.min(jnp.where(vis, score, jnp.inf), axis=1, keepdims=True))


def _selection_bias(key_refs, bias_refs, vis_fns, smax, smin, n_sel, n_keys, probe_masked):
    t, need = _topk_threshold(key_refs, smax, smin, n_sel, n_keys, probe_masked)
    tq = key_refs[0].shape[0]
    base = jnp.zeros((tq, 1), F32)
    for kr, br, vis_fn in zip(key_refs, bias_refs, vis_fns):
        W = kr.shape[1]
        bw = min(W, LANES)
        assert W % bw == 0
        tri = (lax.broadcasted_iota(I32, (bw, bw), 0) <= lax.broadcasted_iota(I32, (bw, bw), 1)).astype(BF16)
        ones_tri = jnp.concatenate([jnp.ones((bw, bw), BF16), tri], axis=0)
        earlier = jnp.zeros((tq, bw), F32)
        for j in range(W // bw):
            cs = slice(j * bw, (j + 1) * bw)
            key = kr[:, cs]
            tie = key == t
            tb = jnp.where(tie, 1.0, 0.0)
            rank = base + _dot(jnp.concatenate([earlier, tb], axis=1).astype(BF16), ones_tri)
            sel = (key > t) | (tie & (rank <= need))
            if vis_fn is not None:
                sel = sel & vis_fn(j * bw, bw)
            br[:, cs] = jnp.where(sel, 0.0, NEG_INF)
            earlier = earlier + tb
        base = base + jnp.sum(earlier, axis=1, keepdims=True)


def _dsa_attend(q_ref, bias_refs, k_loads, v_loads, o_ref):
    for h in range(DSA_HEADS):
        hs = slice(h * DSA_DH, (h + 1) * DSA_DH)
        q = q_ref[:, hs]
        ss = [_dot_nt(q, kl(hs)) + br[...] for kl, br in zip(k_loads, bias_refs)]
        m = None
        for s in ss:
            ms = jnp.max(s, axis=1, keepdims=True)
            m = ms if m is None else jnp.maximum(m, ms)
        l = None
        o = None
        for s, vl in zip(ss, v_loads):
            p = jnp.exp(s - m)
            ls = jnp.sum(p, axis=1, keepdims=True)
            os_ = _dot(p.astype(BF16), vl(hs))
            l = ls if l is None else l + ls
            o = os_ if o is None else o + os_
        o_ref[:, hs] = (o / l).astype(BF16)


def _dsa_prompt_kernel(q_ref, iq_ref, iw_ref, k_ref, v_ref, ikr_ref, o_ref, key_scr, bias_scr,
                       *, tq, q_tile0, n_sel, probe_masked):
    tk = k_ref.shape[0]
    qt = q_tile0 + pl.program_id(1)
    score = _idx_score(iq_ref[...], iw_ref[...], ikr_ref[...])
    qpos = qt * tq + lax.broadcasted_iota(I32, (tq, tk), 0)
    kidx = lax.broadcasted_iota(I32, (tq, tk), 1)
    vis = (kidx // CHUNK) <= (qpos // CHUNK)
    key_scr[...] = _order_key(jnp.where(vis, score, NEG_INF))
    smax, smin = _row_extremes(score, vis)
    qchunk = (qt * tq + lax.broadcasted_iota(I32, (tq, 1), 0)) // CHUNK

    def vis_block(off, bw):
        return ((off + lax.broadcasted_iota(I32, (tq, bw), 1)) // CHUNK) <= qchunk

    _selection_bias([key_scr], [bias_scr], [vis_block], smax, smin, n_sel, tk, probe_masked)
    _dsa_attend(q_ref, [bias_scr], [lambda hs: k_ref[:, hs]], [lambda hs: v_ref[:, hs]], o_ref)


def _dsa_prompt(cq, iq, iw, ck, cv, ikr, B, T):
    tq = min(T, 128)
    nq = T // tq
    n_sel = min(DSA_TOPK_MAX, T // 4)
    ng = 8 if nq % 8 == 0 else (4 if nq % 4 == 0 else 1)
    tpg = nq // ng
    outs = []
    for gi in range(ng):
        tk = (gi + 1) * tpg * tq
        nkb = T // tk if T % tk == 0 else None

        def kv_spec(w, tk=tk):
            return pl.BlockSpec((None, tk, w), lambda b, i: (b, 0, 0))

        def q_spec(w, gi=gi):
            return pl.BlockSpec((None, tq, w), lambda b, i: (b, gi * tpg + i, 0))

        out = pl.pallas_call(
            functools.partial(_dsa_prompt_kernel, tq=tq, q_tile0=gi * tpg, n_sel=n_sel,
                              probe_masked=gi * tpg * tq + CHUNK < n_sel),
            out_shape=jax.ShapeDtypeStruct((B, tpg * tq, 512), BF16),
            grid=(B, tpg),
            in_specs=[q_spec(512), q_spec(256), q_spec(LANES), kv_spec(512), kv_spec(512), kv_spec(256)],
            out_specs=pl.BlockSpec((None, tq, 512), lambda b, i: (b, i, 0)),
            scratch_shapes=[pltpu.VMEM((tq, tk), I32), pltpu.VMEM((tq, tk), F32)],
            compiler_params=_cparams(("arbitrary", "arbitrary"), 56 * 1024 * 1024),
            name=f"dsa_prompt_{gi}",
        )(cq.reshape(B, T, 512), iq.reshape(B, T, 256), iw.reshape(B, T, LANES),
          ck.reshape(B, T, 512), cv.reshape(B, T, 512), ikr.reshape(B, T, 256))
        outs.append(out)
    o = outs[0] if ng == 1 else jnp.concatenate(outs, axis=1)
    return o.reshape(B * T, 512)


def _dsa_sample_kernel(q_ref, iq_ref, iw_ref, kn_ref, vn_ref, ikrn_ref, kp_ref, vp_ref, ikp_ref, o_ref,
                       keyp_scr, keyn_scr, biasp_scr, biasn_scr, *, past_len, n_sel):
    T = q_ref.shape[0]
    P = past_len
    iq = iq_ref[...]
    iw = iw_ref[...]
    ikt = ikp_ref[...].astype(BF16)
    ikrp = jnp.concatenate([ikt] * IDX_HEADS, axis=0)
    score_p = _idx_score(iq, iw, ikrp, transposed=True)
    keyp_scr[...] = _order_key(score_p)
    pmax, pmin = _row_extremes(score_p, None)
    qpos = P + lax.broadcasted_iota(I32, (T, T), 0)
    kpos = P + lax.broadcasted_iota(I32, (T, T), 1)
    vis_n = (kpos // CHUNK) <= (qpos // CHUNK)
    score_n = _idx_score(iq, iw, ikrn_ref[...])
    keyn_scr[...] = _order_key(jnp.where(vis_n, score_n, NEG_INF))
    nmax, nmin = _row_extremes(score_n, vis_n)
    _selection_bias([keyp_scr, keyn_scr], [biasp_scr, biasn_scr], [None, lambda off, bw: vis_n[:, off:off + bw]],
                    jnp.maximum(pmax, nmax), jnp.minimum(pmin, nmin),
                    n_sel, P + T, probe_masked=P + min(T, CHUNK) < n_sel)

    def past_head(ref):
        return lambda hs: ref[pl.ds(hs.start // DSA_DH, P, stride=DSA_HEADS), :].astype(BF16)

    _dsa_attend(q_ref, [biasp_scr, biasn_scr],
                [past_head(kp_ref), lambda hs: kn_ref[:, hs]],
                [past_head(vp_ref), lambda hs: vn_ref[:, hs]], o_ref)


def _dsa_sample(cq, iq, iw, ck, cv, ikr, past_k, past_v, past_ikt, layer, B, T):
    M = B * T
    P = past_ikt.shape[-1]
    n_sel = min(DSA_TOPK_MAX, (P + T) // 4)

    def rspec(w):
        return pl.BlockSpec((T, w), lambda b: (b, 0))

    def pspec(r, w):
        return pl.BlockSpec((None, None, r, w), lambda b: (layer, b, 0, 0))

    return pl.pallas_call(
        functools.partial(_dsa_sample_kernel, past_len=P, n_sel=n_sel),
        out_shape=jax.ShapeDtypeStruct((M, 512), BF16),
        grid=(B,),
        in_specs=[rspec(512), rspec(256), rspec(LANES), rspec(512), rspec(512), rspec(256),
                  pspec(P * DSA_HEADS, DSA_DH), pspec(P * DSA_HEADS, DSA_DH), pspec(IDX_DH, P)],
        out_specs=rspec(512),
        scratch_shapes=[pltpu.VMEM((T, P), I32), pltpu.VMEM((T, T), I32),
                        pltpu.VMEM((T, P), F32), pltpu.VMEM((T, T), F32)],
        compiler_params=_cparams(("arbitrary",)),
        name="dsa_sample",
    )(cq, iq, iw, ck, cv, ikr, past_k, past_v, past_ikt)


def _merge_kernel(og_ref, od_ref, oc_ref, g0_ref, g1_ref, g2_ref, x_ref, m_ref,
                  wg_ref, wd_ref, wc_ref, wo_ref, gn_ref, wr_ref, br_ref,
                  x1_ref, h2_ref, ti_ref, tw_ref, rk_ref, cnt_ref, run_scr, *, nb):
    merged = (jax.nn.sigmoid(g0_ref[...]) * _dot(og_ref[...], wg_ref[...])
              + jax.nn.sigmoid(g1_ref[...]) * _dot(od_ref[...], wd_ref[...])
              + jax.nn.sigmoid(g2_ref[...]) * _dot(oc_ref[...], wc_ref[...]))
    mix = _dot(merged.astype(BF16), wo_ref[...])
    m = m_ref[...]
    x1 = x_ref[...] + _gate_rows(mix, m, 2, nb)
    x1_ref[...] = x1
    xn = x1 * lax.rsqrt(jnp.mean(x1 * x1, axis=-1, keepdims=True) + EPS) * gn_ref[...]
    h2 = _modulate(xn, m, 3, 4, nb)
    _rows_to_tiles(h2_ref, h2)
    lg = _dot(h2.astype(BF16), wr_ref[...]) + br_ref[...]
    tm = lg.shape[0]
    lane = lax.broadcasted_iota(I32, (tm, LANES), 1).astype(F32)
    vals, idxs = [], []
    for _ in range(TOP_K):
        mx = jnp.max(lg, axis=1, keepdims=True)
        ix = jnp.min(jnp.where(lg == mx, lane, float(LANES)), axis=1, keepdims=True)
        vals.append(mx)
        idxs.append(ix)
        lg = jnp.where(lane == ix, -jnp.inf, lg)
    es = [jnp.exp(v - vals[0]) for v in vals]
    den = es[0] + es[1] + es[2] + es[3]
    @pl.when(pl.program_id(0) == 0)
    def _():
        run_scr[...] = jnp.zeros(run_scr.shape, F32)

    hot = [lane == ix for ix in idxs]
    tot = (hot[0].astype(F32) + hot[1].astype(F32)) + (hot[2].astype(F32) + hot[3].astype(F32))
    strict = (lax.broadcasted_iota(I32, (tm, tm), 1) < lax.broadcasted_iota(I32, (tm, tm), 0)).astype(BF16)
    before = run_scr[...] + _dot(strict, tot.astype(BF16))
    ti = jnp.zeros((tm, LANES), F32)
    tw = jnp.zeros((tm, LANES), F32)
    rk = jnp.zeros((tm, LANES), F32)
    for r in range(TOP_K):
        ti = jnp.where(lane == float(r), idxs[r], ti)
        tw = jnp.where(lane == float(r), es[r] / den, tw)
        rk = jnp.where(lane == float(r), jnp.sum(jnp.where(hot[r], before, 0.0), axis=1, keepdims=True), rk)
    ti_ref[...] = ti.astype(I32)
    tw_ref[...] = tw
    rk_ref[...] = rk.astype(I32)
    run_new = run_scr[...] + jnp.sum(tot, axis=0, keepdims=True)
    run_scr[...] = run_new
    cnt_ref[...] = run_new.astype(I32)


def _merge(og, od, oc, z, x2, m, wts, B, T):
    M = B * T
    tm, nb = _row_tiling(B, T, 512)
    wg, wd, wc, wo, gn, wr, br = wts

    def rspec(w):
        return pl.BlockSpec((tm, w), lambda i: (i, 0))

    def gspec(k):
        return pl.BlockSpec((tm, D_MODEL), lambda i: (i, Z_GATES // D_MODEL + k))

    def wspec(r, c):
        return pl.BlockSpec((r, c), lambda i: (0, 0))

    return pl.pallas_call(
        functools.partial(_merge_kernel, nb=nb),
        out_shape=[jax.ShapeDtypeStruct((M, D_MODEL), F32), jax.ShapeDtypeStruct((M * ROW_TILE, LANES), F32),
                   jax.ShapeDtypeStruct((M, LANES), I32), jax.ShapeDtypeStruct((M, LANES), F32),
                   jax.ShapeDtypeStruct((M, LANES), I32), jax.ShapeDtypeStruct((1, LANES), I32)],
        grid=(M // tm,),
        in_specs=[rspec(512), rspec(512), rspec(512), gspec(0), gspec(1), gspec(2), rspec(D_MODEL),
                  pl.BlockSpec((nb, 6, D_MODEL), lambda i: ((i * tm) // (T * nb), 0, 0)),
                  wspec(512, D_MODEL), wspec(512, D_MODEL), wspec(512, D_MODEL), wspec(D_MODEL, D_MODEL),
                  wspec(1, D_MODEL), wspec(D_MODEL, LANES), wspec(1, LANES)],
        out_specs=[rspec(D_MODEL), pl.BlockSpec((tm * ROW_TILE, LANES), lambda i: (i, 0)),
                   rspec(LANES), rspec(LANES), rspec(LANES),
                   pl.BlockSpec((1, LANES), lambda i: (0, 0))],
        scratch_shapes=[pltpu.VMEM((1, LANES), F32)],
        compiler_params=_cparams(("arbitrary",)),
        name="merge_router",
    )(og, od, oc, z, z, z, x2, m, wg, wd, wc, wo, gn, wr, br)


DMA_UNROLL = 8


def _w1prep_kernel(w_ref, sel_ref, g_ref, u_ref):
    sel = sel_ref[...]
    for c in range(w_ref.shape[1] // 256):
        r = _dot(w_ref[:, c * 256:(c + 1) * 256].astype(BF16), sel)
        g_ref[:, c * LANES:(c + 1) * LANES] = r[:, :LANES].astype(BF16)
        u_ref[:, c * LANES:(c + 1) * LANES] = r[:, LANES:].astype(BF16)


def _w1prep(w1_all, layer):
    E = w1_all.shape[1]
    tr = 512
    j = jnp.arange(256, dtype=I32)
    src = jnp.where(j < LANES, 2 * j, 2 * (j - LANES) + 1)
    sel = (jnp.arange(256, dtype=I32)[:, None] == src[None, :]).astype(BF16)
    return pl.pallas_call(
        _w1prep_kernel,
        out_shape=[jax.ShapeDtypeStruct((E, D_MODEL, D_FF), BF16)] * 2,
        grid=(E, D_MODEL // tr),
        in_specs=[pl.BlockSpec((None, None, tr, 2 * D_FF), lambda e, r: (layer, e, r, 0)),
                  pl.BlockSpec((256, 256), lambda e, r: (0, 0))],
        out_specs=[pl.BlockSpec((None, tr, D_FF), lambda e, r: (e, r, 0))] * 2,
        compiler_params=_cparams(("arbitrary", "arbitrary")),
        name="w1prep",
    )(w1_all, sel)


def _rows_to_tiles(ref, x):
    n = x.shape[0]
    for c in range(ROW_TILE):
        ref[pl.ds(c, n, stride=ROW_TILE), :] = x[:, c * LANES:(c + 1) * LANES]


def _tiles_to_rows(ref, base, n):
    return jnp.concatenate([ref[pl.ds(base + c, n, stride=ROW_TILE), :] for c in range(ROW_TILE)], axis=1)


def _tile_rows(ref, r):
    return ref.at[pl.ds(pl.multiple_of(r * ROW_TILE, ROW_TILE), ROW_TILE), :]


def _dispatch_kernel(pos_ref, h_ref, xs_in, xs_out, sem, *, tm):
    del xs_in

    def body(j, c):
        for u in range(DMA_UNROLL):
            n = j * DMA_UNROLL + u
            pltpu.make_async_copy(_tile_rows(h_ref, n // TOP_K), _tile_rows(xs_out, pos_ref[0, 0, n]), sem).start()
        return c

    lax.fori_loop(0, TOP_K * tm // DMA_UNROLL, body, 0)
    for _ in range(TOP_K):
        pltpu.make_async_copy(h_ref, xs_out.at[pl.ds(0, tm * ROW_TILE), :], sem).wait()


def _dispatch(h2t, pos, n_rows, tm):
    M = h2t.shape[0] // ROW_TILE
    nt = M // tm
    return pl.pallas_call(
        functools.partial(_dispatch_kernel, tm=tm),
        out_shape=jax.ShapeDtypeStruct((n_rows * ROW_TILE, LANES), F32),
        grid=(nt,),
        in_specs=[pl.BlockSpec((1, 1, TOP_K * tm), lambda i: (i, 0, 0), memory_space=pltpu.SMEM),
                  pl.BlockSpec((tm * ROW_TILE, LANES), lambda i: (i, 0)),
                  pl.BlockSpec(memory_space=pl.ANY)],
        out_specs=pl.BlockSpec(memory_space=pl.ANY),
        scratch_shapes=[pltpu.SemaphoreType.DMA],
        input_output_aliases={2: 0},
        compiler_params=_cparams(("arbitrary",)),
        name="moe_dispatch",
    )(pos.reshape(nt, 1, TOP_K * tm), h2t, jnp.zeros((n_rows * ROW_TILE, LANES), F32))


def _ffn_kernel(be_ref, nu_ref, x_ref, w1g_ref, b1g_ref, w1u_ref, b1u_ref, w2_ref, b2_ref, o_ref, *, bm):
    @pl.when(pl.program_id(0) < nu_ref[0])
    def _():
        x = _tiles_to_rows(x_ref, 0, bm).astype(BF16)
        g = jnp.minimum(_dot(x, w1g_ref[...]) + b1g_ref[...], SWIGLU_LIMIT)
        u = jnp.clip(_dot(x, w1u_ref[...]) + b1u_ref[...], -SWIGLU_LIMIT, SWIGLU_LIMIT)
        a = g * jax.nn.sigmoid(SWIGLU_ALPHA * g) * (u + 1.0)
        _rows_to_tiles(o_ref, _dot(a.astype(BF16), w2_ref[...]) + b2_ref[...])

    @pl.when(pl.program_id(0) >= nu_ref[0])
    def _():
        o_ref[...] = jnp.zeros(o_ref.shape, F32)


def _ffn(xs, block_e, n_used, w1g, b1g, w1u, b1u, w2_all, b2, layer, bm):
    n_rows = xs.shape[0] // ROW_TILE
    nblk = n_rows // bm

    def wspec(r, c):
        return pl.BlockSpec((None, r, c), lambda i, be, nu: (be[i], 0, 0))

    grid_spec = pltpu.PrefetchScalarGridSpec(
        num_scalar_prefetch=2,
        grid=(nblk,),
        in_specs=[pl.BlockSpec((bm * ROW_TILE, LANES), lambda i, be, nu: (jnp.minimum(i, nu[0] - 1), 0)),
                  wspec(D_MODEL, D_FF), wspec(1, D_FF), wspec(D_MODEL, D_FF), wspec(1, D_FF),
                  pl.BlockSpec((None, None, D_FF, D_MODEL), lambda i, be, nu: (layer, be[i], 0, 0)),
                  wspec(1, D_MODEL)],
        out_specs=pl.BlockSpec((bm * ROW_TILE, LANES), lambda i, be, nu: (i, 0)))
    return pl.pallas_call(
        functools.partial(_ffn_kernel, bm=bm),
        out_shape=jax.ShapeDtypeStruct((n_rows * ROW_TILE, LANES), F32),
        grid_spec=grid_spec,
        compiler_params=_cparams(("arbitrary",)),
        name="moe_ffn",
    )(block_e, n_used, xs, w1g, b1g, w1u, b1u, w2_all, b2)


def _combine_kernel(pos_ref, ys_hbm, tw_ref, x_ref, m_ref, o_ref, ybuf, sem, *, tm, nb):
    def body(j, c):
        for u in range(DMA_UNROLL):
            n = j * DMA_UNROLL + u
            pltpu.make_async_copy(_tile_rows(ys_hbm, pos_ref[0, 0, n]), _tile_rows(ybuf, n), sem).start()
        return c

    lax.fori_loop(0, TOP_K * tm // DMA_UNROLL, body, 0)
    pltpu.make_async_copy(ys_hbm.at[pl.ds(0, TOP_K * tm * ROW_TILE), :], ybuf, sem).wait()
    tw = tw_ref[...]
    y = None
    for k in range(TOP_K):
        yk = tw[:, k:k + 1] * _tiles_to_rows(ybuf, k * tm * ROW_TILE, tm)
        y = yk if y is None else y + yk
    o_ref[...] = x_ref[...] + _gate_rows(y, m_ref[...], 5, nb)


def _combine(pos_t, ys, top_w, x1, m, B, T, tm, nb):
    M = B * T
    nt = M // tm
    return pl.pallas_call(
        functools.partial(_combine_kernel, tm=tm, nb=nb),
        out_shape=jax.ShapeDtypeStruct((M, D_MODEL), F32),
        grid=(nt,),
        in_specs=[pl.BlockSpec((1, 1, TOP_K * tm), lambda i: (i, 0, 0), memory_space=pltpu.SMEM),
                  pl.BlockSpec(memory_space=pl.ANY),
                  pl.BlockSpec((tm, LANES), lambda i: (i, 0)),
                  pl.BlockSpec((tm, D_MODEL), lambda i: (i, 0)),
                  pl.BlockSpec((nb, 6, D_MODEL), lambda i: ((i * tm) // (T * nb), 0, 0))],
        out_specs=pl.BlockSpec((tm, D_MODEL), lambda i: (i, 0)),
        scratch_shapes=[pltpu.VMEM((TOP_K * tm * ROW_TILE, LANES), F32), pltpu.SemaphoreType.DMA],
        compiler_params=_cparams(("arbitrary",)),
        name="moe_combine",
    )(pos_t.reshape(nt, 1, TOP_K * tm), ys, top_w, x1, m)


def _route(top_i, rank, counts, bm):
    N = top_i.shape[0]
    NK = N * TOP_K
    padded = (counts + bm - 1) // bm * bm
    pad_end = jnp.cumsum(padded)
    pad_start = pad_end - padded
    onehot = top_i[:, :, None] == jnp.arange(N_EXPERTS, dtype=I32)[None, None, :]
    pos = rank + jnp.sum(jnp.where(onehot, pad_start[None, None, :], 0), axis=2)
    n_rows = (-(-NK // bm)) * bm + N_EXPERTS * bm
    nblk = n_rows // bm
    starts = jnp.arange(nblk, dtype=I32) * bm
    block_e = jnp.minimum(jnp.sum((pad_end[None, :] <= starts[:, None]).astype(I32), axis=1), N_EXPERTS - 1)
    n_used = (pad_end[-1:] // bm).astype(I32)
    return pos.reshape(NK).astype(I32), n_rows, block_e.astype(I32), n_used


def _moe(h2, top_i, top_w, rank, counts, x1, m, ew, B, T):
    M = B * T
    bm = 256 if M * TOP_K >= 256 * N_EXPERTS * 4 else 128
    pos, n_rows, block_e, n_used = _route(top_i[:, :TOP_K], rank[:, :TOP_K], counts[0, :N_EXPERTS], bm)
    tm, nb = _row_tiling(B, T, 256)
    xs = _dispatch(h2, pos, n_rows, tm)
    ys = _ffn(xs, block_e, n_used, *ew, bm)
    pos_t = pos.reshape(M // tm, tm, TOP_K).transpose(0, 2, 1)
    return _combine(pos_t, ys, top_w, x1, m, B, T, tm, nb)


def _prep_layer(P, l, w2_all):
    w_t = P['w_in'][l].T

    def rows(name):
        o, s = _SRC[name]
        return w_t[o:o + s]

    zeros = lambda n: jnp.zeros((n, D_MODEL), F32)
    w_r = jnp.concatenate([
        rows('gla_q'), rows('gla_k'), rows('gla_v'), rows('gla_r'),
        rows('diff_q'), rows('diff_k'), rows('diff_v'),
        rows('dsa_q'), rows('dsa_k'), rows('dsa_v'),
        rows('idx_q'), rows('idx_k'), rows('idx_w'), zeros(60), rows('gla_a'), zeros(112),
        rows('gates')], axis=0).astype(BF16)
    assert w_r.shape[0] == Z_WIDTH
    W = GLA_HEADS * GLA_DK
    wa = jnp.zeros((LANES, W), F32).at[:GLA_GATE_RANK].set(P['w_gla_a2'][l]).astype(BF16)
    w1g, w1u = _w1prep(P['w_mlp1'], l)
    b1 = P['b_mlp1'][l]
    return dict(
        w_in=w_r, wa=wa, ba=P['b_gla_a2'][l].reshape(1, W),
        g_gla=jnp.tile(P['g_gla_out'][l], GLA_HEADS).reshape(1, 512),
        gains=(jnp.tile(P['g_diff_q'][l], 8).reshape(1, 512), jnp.tile(P['g_diff_k'][l], 8).reshape(1, 512),
               jnp.tile(P['g_dsa_q'][l], 4).reshape(1, 512), jnp.tile(P['g_dsa_k'][l], 4).reshape(1, 512),
               jnp.tile(P['g_idx_k'][l], 2).reshape(1, LANES)),
        lamv=jnp.stack([P['lambda_q1'][l], P['lambda_k1'][l], P['lambda_q2'][l], P['lambda_k2'][l]]),
        g_diff=P['g_diff_out'][l].reshape(1, LANES),
        merge=(P['w_branch_gla'][l].astype(BF16), P['w_branch_diff'][l].astype(BF16),
               P['w_branch_dsa'][l].astype(BF16), P['w_out'][l].astype(BF16),
               P['g_norm2'][l].reshape(1, D_MODEL),
               jnp.zeros((D_MODEL, LANES), F32).at[:, :N_EXPERTS].set(P['w_router'][l]).astype(BF16),
               jnp.full((1, LANES), NEG_INF, F32).at[0, :N_EXPERTS].set(P['b_router'][l])),
        experts=(w1g, b1[:, None, 0::2], w1u, b1[:, None, 1::2], w2_all, P['b_mlp2'][l][:, None, :], l),
    )


def _trunk(x, c, past, P, prepped):
    B, T, _ = x.shape
    M = B * T
    x2 = x.reshape(M, D_MODEL)
    past_len = 0 if past is None else past[0].shape[2]
    pos = past_len + jnp.arange(T, dtype=I32)
    tabs64 = _rope_tables(pos, 64)
    tabs128 = _rope_tables(pos, 128)
    W = GLA_HEADS * GLA_DK
    if past is not None:
        L, _, PL = past[0].shape[:3]
        pdk = past[0].transpose(0, 1, 3, 4, 5, 2)
        pdv = past[1].reshape(L, B, PL * DIFF_HEADS, DIFF_DV)
        pck = past[2].reshape(L, B, PL * DSA_HEADS, DSA_DH)
        pcv = past[3].reshape(L, B, PL * DSA_HEADS, DSA_DH)
        pik = past[4].transpose(0, 1, 3, 2)
        s0_all = past[5].transpose(0, 1, 4, 2, 3).reshape(past[5].shape[0], B, GLA_DV, W)
    per_layer = []
    for l, pp in enumerate(prepped):
        lam_init = 0.8 - 0.6 * math.exp(-0.3 * l)
        m = _ada(c, P['w_ada'], P['b_ada'], l)
        z = _inproj(x2, m, P['g_norm1'][l], pp['w_in'], B, T)
        (dq, dk32, dkb, dvb, cq, ck32, ckb, cvb, iq, ik32, ikr, iw, dv32, cv32) = _post(
            z, tabs64, tabs128, pp['gains'], B, T)
        s0t = jnp.zeros((B, GLA_DV, W), F32) if past is None else s0_all[l]
        o_gla, st = _gla(z, pp['wa'], pp['ba'], pp['g_gla'], s0t, B, T)
        if past is None:
            o_diff = _diff_prompt(dq, dkb, dvb, pp['lamv'], pp['g_diff'], B, T, lam_init)
            o_dsa = _dsa_prompt(cq, iq, iw, ckb, cvb, ikr, B, T)
        else:
            o_diff = _diff_sample(dq, dkb, dvb, pdk, pdv, l, pp['lamv'], pp['g_diff'], B, T, lam_init)
            o_dsa = _dsa_sample(cq, iq, iw, ckb, cvb, ikr, pck, pcv, pik, l, B, T)
        x1, h2, top_i, top_w, rank, counts = _merge(o_gla, o_diff, o_dsa, z, x2, m, pp['merge'], B, T)
        x2 = _moe(h2, top_i, top_w, rank, counts, x1, m, pp['experts'], B, T)
        s_gla = st.reshape(B, GLA_DV, GLA_HEADS, GLA_DK).transpose(0, 2, 3, 1)
        per_layer.append((dk32.reshape(B, T, DIFF_HEADS, 2, DIFF_DH),
                          dv32.reshape(B, T, DIFF_HEADS, DIFF_DV),
                          ck32.reshape(B, T, DSA_HEADS, DSA_DH),
                          cv32.reshape(B, T, DSA_HEADS, DSA_DH),
                          ik32.reshape(B, T, IDX_DH),
                          s_gla))
    stacked = tuple(jnp.stack([st[i] for st in per_layer]) for i in range(6))
    return x2.reshape(B, T, D_MODEL), stacked


def kernel(x_prompt, x_sample, cache_diff_k, cache_diff_v, cache_dsa_k, cache_dsa_v, cache_dsa_idx_k,
           state_gla, c_prompt, c_sample, w_ada, b_ada, g_norm1, g_norm2, w_in, w_gla_a2, b_gla_a2,
           g_gla_out, g_diff_q, g_diff_k, lambda_q1, lambda_k1, lambda_q2, lambda_k2, g_diff_out,
           g_dsa_q, g_dsa_k, g_idx_k, w_branch_gla, w_branch_diff, w_branch_dsa, w_out, w_router,
           b_router, w_mlp1, b_mlp1, w_mlp2, b_mlp2):
    P = dict(w_ada=w_ada, b_ada=b_ada, g_norm1=g_norm1, g_norm2=g_norm2, w_in=w_in, w_gla_a2=w_gla_a2,
             b_gla_a2=b_gla_a2, g_gla_out=g_gla_out, g_diff_q=g_diff_q, g_diff_k=g_diff_k,
             lambda_q1=lambda_q1, lambda_k1=lambda_k1, lambda_q2=lambda_q2, lambda_k2=lambda_k2,
             g_diff_out=g_diff_out, g_dsa_q=g_dsa_q, g_dsa_k=g_dsa_k, g_idx_k=g_idx_k,
             w_branch_gla=w_branch_gla, w_branch_diff=w_branch_diff, w_branch_dsa=w_branch_dsa,
             w_out=w_out, w_router=w_router, b_router=b_router, w_mlp1=w_mlp1, b_mlp1=b_mlp1,
             w_mlp2=w_mlp2, b_mlp2=b_mlp2)
    depth = w_in.shape[0]
    w2_all = w_mlp2.astype(BF16)
    prepped = [_prep_layer(P, l, w2_all) for l in range(depth)]
    y_prompt, new_p = _trunk(x_prompt, c_prompt, None, P, prepped)
    y_sample, new_s = _trunk(
        x_sample, c_sample,
        (cache_diff_k, cache_diff_v, cache_dsa_k, cache_dsa_v, cache_dsa_idx_k, state_gla), P, prepped)
    return (y_prompt, y_sample) + new_p + new_s
```

```python
import functools
import math

import numpy as np
import jax
import jax.numpy as jnp
from jax import lax
from jax.experimental import pallas as pl
from jax.experimental.pallas import tpu as pltpu

F32 = jnp.float32
BF16 = jnp.bfloat16
I32 = jnp.int32

D_MODEL = 1024
CHUNK = 64
ROPE_THETA = 500000.0
ROPE_FRACTION = 4
EPS = 1e-6
NEG_INF = -1e30

GLA_HEADS = 4
GLA_DK = 64
GLA_DV = 128
GLA_GATE_RANK = 16
GLA_GATE_TAU = 16.0
GLA_SUB = 16
DIFF_HEADS = 4
DIFF_DH = 64
DIFF_DV = 128
DSA_HEADS = 4
DSA_DH = 128
IDX_HEADS = 4
IDX_DH = 64
DSA_TOPK_MAX = 256
N_EXPERTS = 32
TOP_K = 4
D_FF = 1024
SWIGLU_LIMIT = 7.0
SWIGLU_ALPHA = 1.702

LANES = 128
ROW_TILE = 8
VMEM_LIMIT = 48 * 1024 * 1024

Z_GQK, Z_GV, Z_GR = 0, 512, 1024
Z_DQ, Z_DK, Z_DV = 1536, 2048, 2560
Z_CQ, Z_CK, Z_CV = 3072, 3584, 4096
Z_MISC = 4608
Z_GATES = 5120
Z_WIDTH = 8192
MISC_IK = 256
MISC_GA = 384

_SRC = {}
_off = 0
for _name, _size in (
        ('gla_q', 256), ('gla_k', 256), ('gla_v', 512), ('gla_a', 16), ('gla_r', 512),
        ('diff_q', 512), ('diff_k', 512), ('diff_v', 512),
        ('dsa_q', 512), ('dsa_k', 512), ('dsa_v', 512),
        ('idx_q', 256), ('idx_k', 64), ('idx_w', 4), ('gates', 3072)):
    _SRC[_name] = (_off, _size)
    _off += _size


def _cparams(sem, vmem=VMEM_LIMIT):
    return pltpu.CompilerParams(dimension_semantics=sem, vmem_limit_bytes=vmem)


def _dot(a, b):
    return jnp.dot(a, b, preferred_element_type=F32)


def _dot_nt(a, b):
    return lax.dot_general(a, b, (((1,), (1,)), ((), ())), preferred_element_type=F32)


def _dot_tn(a, b):
    return lax.dot_general(a, b, (((0,), (0,)), ((), ())), preferred_element_type=F32)


def _silu(x):
    return x * jax.nn.sigmoid(x)


def _row_tiling(B, T, target):
    if T >= target:
        assert T % target == 0
        return target, 1
    nb = 1
    for cand in range(1, B + 1):
        if B % cand == 0 and cand * T <= target:
            nb = cand
    return nb * T, nb


def _modulate(xn, m, shift_i, scale_i, nb):
    tm = xn.shape[0]
    if nb == 1:
        return xn * (1.0 + m[0, scale_i:scale_i + 1, :]) + m[0, shift_i:shift_i + 1, :]
    x3 = xn.reshape(nb, tm // nb, D_MODEL)
    h = x3 * (1.0 + m[:, scale_i:scale_i + 1, :]) + m[:, shift_i:shift_i + 1, :]
    return h.reshape(tm, D_MODEL)


def _gate_rows(y, m, gate_i, nb):
    tm = y.shape[0]
    if nb == 1:
        return y * m[0, gate_i:gate_i + 1, :]
    return (y.reshape(nb, tm // nb, D_MODEL) * m[:, gate_i:gate_i + 1, :]).reshape(tm, D_MODEL)


def _ada_kernel(c_ref, w_ref, b_ref, o_ref):
    s = _silu(c_ref[...])
    o_ref[...] = _dot(s.astype(BF16), w_ref[...].astype(BF16)) + b_ref[...]


def _ada(c, w, b, layer):
    B = c.shape[0]
    out = pl.pallas_call(
        _ada_kernel,
        out_shape=jax.ShapeDtypeStruct((B, 6 * D_MODEL), F32),
        grid=(6,),
        in_specs=[pl.BlockSpec((B, D_MODEL), lambda j: (0, 0)),
                  pl.BlockSpec((None, D_MODEL, D_MODEL), lambda j: (layer, 0, j)),
                  pl.BlockSpec((None, 1, D_MODEL), lambda j: (layer, 0, j))],
        out_specs=pl.BlockSpec((B, D_MODEL), lambda j: (0, j)),
        compiler_params=_cparams(("arbitrary",)),
        name="ada",
    )(c, w, b.reshape(b.shape[0], 1, -1))
    return out.reshape(B, 6, D_MODEL)


def _inproj_kernel(x_ref, m_ref, g_ref, w_ref, o_ref, h_scr, *, nb):
    @pl.when(pl.program_id(1) == 0)
    def _():
        x = x_ref[...]
        xn = x * lax.rsqrt(jnp.mean(x * x, axis=-1, keepdims=True) + EPS) * g_ref[...]
        h_scr[...] = _modulate(xn, m_ref[...], 0, 1, nb).astype(BF16)

    o_ref[...] = _dot_nt(h_scr[...], w_ref[...])


def _inproj(x2, m, g, wt, B, T):
    M = B * T
    tm, nb = _row_tiling(B, T, 1024)
    tn = 1024
    return pl.pallas_call(
        functools.partial(_inproj_kernel, nb=nb),
        out_shape=jax.ShapeDtypeStruct((M, Z_WIDTH), F32),
        grid=(M // tm, Z_WIDTH // tn),
        in_specs=[pl.BlockSpec((tm, D_MODEL), lambda i, j: (i, 0)),
                  pl.BlockSpec((nb, 6, D_MODEL), lambda i, j: ((i * tm) // (T * nb), 0, 0)),
                  pl.BlockSpec((1, D_MODEL), lambda i, j: (0, 0)),
                  pl.BlockSpec((tn, D_MODEL), lambda i, j: (j, 0))],
        out_specs=pl.BlockSpec((tm, tn), lambda i, j: (i, j)),
        scratch_shapes=[pltpu.VMEM((tm, D_MODEL), BF16)],
        compiler_params=_cparams(("arbitrary", "arbitrary")),
        name="inproj",
    )(x2, m, g.reshape(1, -1), wt)


def _rope_tables(pos, d):
    rot = d // ROPE_FRACTION
    half = rot // 2
    T = pos.shape[0]
    inv_freq = ROPE_THETA ** (-jnp.arange(half, dtype=F32) / half)
    ang = pos.astype(F32)[:, None] * inv_freq[None, :]
    cos, sin = jnp.cos(ang), jnp.sin(ang)
    c = jnp.concatenate([cos, cos, jnp.ones((T, d - rot), F32)], axis=1)
    a = jnp.concatenate([-sin, jnp.zeros((T, d - half), F32)], axis=1)
    b = jnp.concatenate([jnp.zeros((T, half), F32), sin, jnp.zeros((T, d - rot), F32)], axis=1)
    reps = LANES // d
    return tuple(jnp.tile(t, (1, reps)) for t in (c, a, b))


def _rope128(xs, tabs, half):
    c, a, b = tabs
    return xs * c + pltpu.roll(xs, LANES - half, 1) * a + pltpu.roll(xs, half, 1) * b


def _norm_rope(x, g, tabs, d, norm=True):
    tm, W = x.shape
    half = d // ROPE_FRACTION // 2
    lo = lax.broadcasted_iota(I32, (tm, LANES), 1) < 64
    outs = []
    for s in range(W // LANES):
        xs = x[:, s * LANES:(s + 1) * LANES]
        if norm:
            sq = xs * xs
            if d == LANES:
                r = lax.rsqrt(jnp.sum(sq, axis=1, keepdims=True) * (1.0 / d) + EPS)
            else:
                s_lo = jnp.sum(jnp.where(lo, sq, 0.0), axis=1, keepdims=True)
                s_hi = jnp.sum(jnp.where(lo, 0.0, sq), axis=1, keepdims=True)
                r = jnp.where(lo, lax.rsqrt(s_lo * (1.0 / d) + EPS), lax.rsqrt(s_hi * (1.0 / d) + EPS))
            xs = xs * r * g[:, s * LANES:(s + 1) * LANES]
        outs.append(_rope128(xs, tabs, half))
    return outs[0] if len(outs) == 1 else jnp.concatenate(outs, axis=1)


def _store_head_rows(ref, x):
    tm = x.shape[0]
    for h in range(4):
        ref[pl.ds(h, tm, stride=4), :] = x[:, h * LANES:(h + 1) * LANES]


def _post_kernel(dq_ref, dk_ref, dv_ref, cq_ref, ck_ref, cv_ref, mi_ref,
                 c64_ref, a64_ref, b64_ref, c128_ref, a128_ref, b128_ref,
                 gdq_ref, gdk_ref, gcq_ref, gck_ref, gik_ref,
                 dq_o, dk32_o, dkb_o, dvb_o, cq_o, ck32_o, ckb_o, cvb_o, iq_o, ik32_o, ikr_o, iw_o,
                 dv32_o, cv32_o):
    t64 = (c64_ref[...], a64_ref[...], b64_ref[...])
    t128 = (c128_ref[...], a128_ref[...], b128_ref[...])
    dq = _norm_rope(dq_ref[...], gdq_ref[...], t64, 64)
    dq_o[...] = (dq * (DIFF_DH ** -0.5)).astype(BF16)
    dk = _norm_rope(dk_ref[...], gdk_ref[...], t64, 64)
    dk32_o[...] = dk
    dkb_o[...] = dk.astype(BF16)
    dv = dv_ref[...]
    _store_head_rows(dv32_o, dv)
    dvb_o[...] = dv.astype(BF16)
    cq = _norm_rope(cq_ref[...], gcq_ref[...], t128, 128)
    cq_o[...] = (cq * (DSA_DH ** -0.5)).astype(BF16)
    ck = _norm_rope(ck_ref[...], gck_ref[...], t128, 128)
    _store_head_rows(ck32_o, ck)
    ckb_o[...] = ck.astype(BF16)
    cv = cv_ref[...]
    _store_head_rows(cv32_o, cv)
    cvb_o[...] = cv.astype(BF16)
    mi = mi_ref[...]
    iq = _norm_rope(mi[:, 0:256], None, t64, 64, norm=False)
    iq_o[...] = (iq * (IDX_DH ** -0.5)).astype(BF16)
    mk = mi[:, MISC_IK:MISC_IK + LANES]
    tm = mk.shape[0]
    lane = lax.broadcasted_iota(I32, (tm, LANES), 1)
    lo = lane < 64
    ssq = jnp.sum(jnp.where(lo, mk * mk, 0.0), axis=1, keepdims=True)
    ikn = mk * lax.rsqrt(ssq * (1.0 / IDX_DH) + EPS) * gik_ref[...]
    ik = _rope128(ikn, t64, IDX_DH // ROPE_FRACTION // 2)
    ik32_o[...] = ik[:, 0:IDX_DH]
    ik2 = jnp.where(lo, ik, pltpu.roll(ik, 64, 1))
    ikr_o[...] = jnp.concatenate([ik2, ik2], axis=1).astype(BF16)
    iw_o[...] = jnp.where(lane < IDX_HEADS, pltpu.roll(mk, 64, 1) * (IDX_HEADS ** -0.5), 0.0)


def _post(z, tabs64, tabs128, gains, B, T):
    M = B * T
    tm = min(T, 512)
    assert T % tm == 0
    npos = T // tm

    def zspec(off):
        return pl.BlockSpec((tm, 512), lambda i: (i, off // 512))

    tab_spec = pl.BlockSpec((tm, LANES), lambda i: (i % npos, 0))

    def gspec(w):
        return pl.BlockSpec((1, w), lambda i: (0, 0))

    HR = "head rows"
    outs = [(512, BF16), (512, F32), (512, BF16), (512, BF16),
            (512, BF16), (HR, F32), (512, BF16), (512, BF16),
            (256, BF16), (IDX_DH, F32), (256, BF16), (LANES, F32), (HR, F32), (HR, F32)]

    def oshape(w, dt):
        return jax.ShapeDtypeStruct((M * 4, LANES) if w is HR else (M, w), dt)

    def ospec(w):
        return pl.BlockSpec((tm * 4, LANES) if w is HR else (tm, w), lambda i: (i, 0))

    return pl.pallas_call(
        _post_kernel,
        out_shape=[oshape(w, dt) for w, dt in outs],
        grid=(M // tm,),
        in_specs=[zspec(Z_DQ), zspec(Z_DK), zspec(Z_DV), zspec(Z_CQ), zspec(Z_CK), zspec(Z_CV), zspec(Z_MISC)]
                 + [tab_spec] * 6 + [gspec(512)] * 4 + [gspec(LANES)],
        out_specs=[ospec(w) for w, _ in outs],
        compiler_params=_cparams(("arbitrary",)),
        name="post",
    )(z, z, z, z, z, z, z, *tabs64, *tabs128, *gains)


def _split3(x):
    hi = x.astype(BF16)
    r = x - hi.astype(F32)
    mid = r.astype(BF16)
    lo = (r - mid.astype(F32)).astype(BF16)
    return hi, mid, lo


def _gla_kernel(qk_ref, v_ref, r_ref, ga_ref, wa_ref, ba_ref, g_ref, s0_ref, o_ref, st_ref, st_scr, *, nct):
    ti = pl.program_id(1)

    @pl.when(ti == 0)
    def _():
        st_scr[...] = s0_ref[...]

    C = CHUNK
    W = GLA_HEADS * GLA_DK
    row = lax.broadcasted_iota(I32, (C, C), 0)
    col = lax.broadcasted_iota(I32, (C, C), 1)
    tri = col <= row
    tri_bf = tri.astype(BF16)
    lane = lax.broadcasted_iota(I32, (1, W), 1)
    hmask = [(lane // GLA_DK) == h for h in range(GLA_HEADS)]
    rowi = lax.broadcasted_iota(I32, (C, W), 0)
    wa = wa_ref[...]
    ba = ba_ref[...]
    g = g_ref[...]

    def chunk(c, carry):
        rows = pl.ds(pl.multiple_of(c * C, C), C)
        qk = qk_ref[rows, :]
        q = qk[:, :W] * (GLA_DK ** -0.5)
        k = qk[:, W:]
        v = v_ref[rows, :].astype(BF16)
        pre = _dot(ga_ref[rows, :].astype(BF16), wa) + ba
        la = (jnp.minimum(pre, 0.0) - jnp.log(1.0 + jnp.exp(-jnp.abs(pre)))) * (1.0 / GLA_GATE_TAU)
        hi, mid, lo = _split3(la)
        b = _dot(tri_bf, hi) + _dot(tri_bf, mid) + _dot(tri_bf, lo)
        st = st_scr[...]
        st_bf = st.astype(BF16)
        qe = q * jnp.exp(b)
        b_end = b[C - 1:C, :]
        kend = k * jnp.exp(b_end - b)
        att_parts = [[] for _ in range(GLA_HEADS)]
        for s in range(C // GLA_SUB):
            r0 = s * GLA_SUB
            br = b[r0:r0 + 1, :]
            qs = q[r0:r0 + GLA_SUB, :] * jnp.exp(b[r0:r0 + GLA_SUB, :] - br)
            ks = k * jnp.exp(br - b)
            if r0 + GLA_SUB < C:
                ks = jnp.where(rowi < r0 + GLA_SUB, ks, 0.0)
            ks = ks.astype(BF16)
            for h in range(GLA_HEADS):
                att_parts[h].append(_dot_nt(jnp.where(hmask[h], qs, 0.0).astype(BF16), ks))
        upd = None
        for h in range(GLA_HEADS):
            hs = slice(h * GLA_DV, (h + 1) * GLA_DV)
            att = jnp.where(tri, jnp.concatenate(att_parts[h], axis=0), 0.0)
            o = _dot(att.astype(BF16), v[:, hs]) + _dot_nt(jnp.where(hmask[h], qe, 0.0).astype(BF16), st_bf)
            u = _dot_tn(v[:, hs], jnp.where(hmask[h], kend, 0.0).astype(BF16))
            upd = u if upd is None else upd + u
            y = o * lax.rsqrt(jnp.mean(o * o, axis=-1, keepdims=True) + EPS) * g[:, hs]
            o_ref[rows, hs] = (y * _silu(r_ref[rows, hs])).astype(BF16)
        st_scr[...] = st * jnp.exp(b_end) + upd
        return carry

    lax.fori_loop(0, nct, chunk, 0)

    @pl.when(ti == pl.num_programs(1) - 1)
    def _():
        st_ref[...] = st_scr[...]


def _gla(z, wa, ba, g, s0t, B, T):
    M = B * T
    tt = min(T, 512)
    assert T % tt == 0 and tt % CHUNK == 0
    nt = T // tt
    W = GLA_HEADS * GLA_DK
    return pl.pallas_call(
        functools.partial(_gla_kernel, nct=tt // CHUNK),
        out_shape=[jax.ShapeDtypeStruct((M, 512), BF16),
                   jax.ShapeDtypeStruct((B, GLA_DV, W), F32)],
        grid=(B, nt),
        in_specs=[pl.BlockSpec((tt, 512), lambda b, t: (b * nt + t, Z_GQK // 512)),
                  pl.BlockSpec((tt, 512), lambda b, t: (b * nt + t, Z_GV // 512)),
                  pl.BlockSpec((tt, 512), lambda b, t: (b * nt + t, Z_GR // 512)),
                  pl.BlockSpec((tt, LANES), lambda b, t: (b * nt + t, (Z_MISC + MISC_GA) // LANES)),
                  pl.BlockSpec((LANES, W), lambda b, t: (0, 0)),
                  pl.BlockSpec((1, W), lambda b, t: (0, 0)),
                  pl.BlockSpec((1, 512), lambda b, t: (0, 0)),
                  pl.BlockSpec((None, GLA_DV, W), lambda b, t: (b, 0, 0))],
        out_specs=[pl.BlockSpec((tt, 512), lambda b, t: (b * nt + t, 0)),
                   pl.BlockSpec((None, GLA_DV, W), lambda b, t: (b, 0, 0))],
        scratch_shapes=[pltpu.VMEM((GLA_DV, W), F32)],
        compiler_params=_cparams(("arbitrary", "arbitrary")),
        name="gla",
    )(z, z, z, z, wa, ba, g, s0t)


def _diff_lambda(lam_ref, lam_init):
    lv = lam_ref[...]
    a = jnp.sum(lv[0:1, :] * lv[1:2, :], axis=1, keepdims=True)
    b = jnp.sum(lv[2:3, :] * lv[3:4, :], axis=1, keepdims=True)
    return jnp.exp(a) - jnp.exp(b) + lam_init


def _diff_finish(o0, o1, lam, g, lam_init):
    o = o0 - lam * o1
    return o * lax.rsqrt(jnp.mean(o * o, axis=-1, keepdims=True) + EPS) * g * (1.0 - lam_init)


def _diff_prompt_kernel(qi_ref, kj_ref, q_ref, k_ref, v_ref, lam_ref, g_ref, o_ref,
                        q2_scr, m_scr, l_scr, acc_scr, *, lam_init, tq):
    n = pl.program_id(2)
    qi = qi_ref[n]
    kj = kj_ref[n]

    @pl.when(kj == 0)
    def _():
        q = q_ref[...]
        lane = lax.broadcasted_iota(I32, (1, LANES), 1)
        q2_scr[0:tq, :] = jnp.where(lane < DIFF_DH, q, jnp.zeros_like(q))
        q2_scr[tq:2 * tq, :] = jnp.where(lane < DIFF_DH, jnp.zeros_like(q), q)
        m_scr[...] = jnp.full(m_scr.shape, NEG_INF, F32)
        l_scr[...] = jnp.zeros(l_scr.shape, F32)
        acc_scr[...] = jnp.zeros(acc_scr.shape, F32)

    def step(diagonal):
        s = _dot_nt(q2_scr[...], k_ref[...])
        if diagonal:
            qc = lax.broadcasted_iota(I32, (tq, tq), 0) // CHUNK
            kc = lax.broadcasted_iota(I32, (tq, tq), 1) // CHUNK
            vis = kc <= qc
            s = jnp.where(jnp.concatenate([vis, vis], axis=0), s, NEG_INF)
        m_prev = m_scr[...]
        m_new = jnp.maximum(m_prev, jnp.max(s, axis=1, keepdims=True))
        alpha = jnp.exp(m_prev - m_new)
        p = jnp.exp(s - jnp.concatenate([m_new] * (tq // LANES), axis=1))
        l_scr[...] = alpha * l_scr[...] + jnp.sum(p, axis=1, keepdims=True)
        acc_scr[...] = alpha * acc_scr[...] + _dot(p.astype(BF16), v_ref[...])
        m_scr[...] = m_new

    @pl.when(kj < qi)
    def _():
        step(False)

    @pl.when(kj == qi)
    def _():
        step(True)
        lam = _diff_lambda(lam_ref, lam_init)
        o = acc_scr[...] / l_scr[...]
        o_ref[...] = _diff_finish(o[0:tq], o[tq:2 * tq], lam, g_ref[...], lam_init).astype(BF16)


def _diff_prompt(dq, dk, dv, lamv, g, B, T, lam_init):
    M = B * T
    tq = min(T, 512)
    nq = T // tq
    pairs = [(i, j) for i in range(nq) for j in range(i + 1)]
    qi = jnp.asarray([p[0] for p in pairs], I32)
    kj = jnp.asarray([p[1] for p in pairs], I32)
    grid_spec = pltpu.PrefetchScalarGridSpec(
        num_scalar_prefetch=2,
        grid=(B, DIFF_HEADS, len(pairs)),
        in_specs=[pl.BlockSpec((tq, LANES), lambda b, h, n, qi, kj: (b * nq + qi[n], h)),
                  pl.BlockSpec((tq, LANES), lambda b, h, n, qi, kj: (b * nq + kj[n], h)),
                  pl.BlockSpec((tq, LANES), lambda b, h, n, qi, kj: (b * nq + kj[n], h)),
                  pl.BlockSpec((4, DIFF_DH), lambda b, h, n, qi, kj: (0, 0)),
                  pl.BlockSpec((1, LANES), lambda b, h, n, qi, kj: (0, 0))],
        out_specs=pl.BlockSpec((tq, LANES), lambda b, h, n, qi, kj: (b * nq + qi[n], h)),
        scratch_shapes=[pltpu.VMEM((2 * tq, LANES), BF16), pltpu.VMEM((2 * tq, LANES), F32),
                        pltpu.VMEM((2 * tq, LANES), F32), pltpu.VMEM((2 * tq, LANES), F32)])
    return pl.pallas_call(
        functools.partial(_diff_prompt_kernel, lam_init=lam_init, tq=tq),
        out_shape=jax.ShapeDtypeStruct((M, 512), BF16),
        grid_spec=grid_spec,
        compiler_params=_cparams(("arbitrary", "arbitrary", "arbitrary")),
        name="diff_prompt",
    )(qi, kj, dq, dk, dv, lamv, g)


def _diff_sample_kernel(q_ref, kn_ref, vn_ref, kp_ref, vp_ref, lam_ref, g_ref, o_ref, *, lam_init, past_len):
    T = q_ref.shape[0]
    lam = _diff_lambda(lam_ref, lam_init)
    lane = lax.broadcasted_iota(I32, (1, LANES), 1)
    qpos = past_len + lax.broadcasted_iota(I32, (T, T), 0)
    kpos = past_len + lax.broadcasted_iota(I32, (T, T), 1)
    vis_new = (kpos // CHUNK) <= (qpos // CHUNK)
    P = past_len
    for h in range(DIFF_HEADS):
        hs = slice(h * LANES, (h + 1) * LANES)
        q = q_ref[:, hs]
        kpt = kp_ref[h].reshape(2 * DIFF_DH, P).astype(BF16)
        vp = vp_ref[pl.ds(h, P, stride=DIFF_HEADS), :].astype(BF16)
        kn = kn_ref[:, hs]
        vn = vn_ref[:, hs]
        outs = []
        for c in range(2):
            qc = jnp.where((lane < DIFF_DH) == (c == 0), q, jnp.zeros_like(q))
            sp = _dot(qc, kpt)
            sn = jnp.where(vis_new, _dot_nt(qc, kn), NEG_INF)
            m = jnp.maximum(jnp.max(sp, axis=1, keepdims=True), jnp.max(sn, axis=1, keepdims=True))
            pp = jnp.exp(sp - m)
            pn = jnp.exp(sn - m)
            l = jnp.sum(pp, axis=1, keepdims=True) + jnp.sum(pn, axis=1, keepdims=True)
            outs.append((_dot(pp.astype(BF16), vp) + _dot(pn.astype(BF16), vn)) / l)
        o_ref[:, hs] = _diff_finish(outs[0], outs[1], lam, g_ref[...], lam_init).astype(BF16)


def _diff_sample(dq, dk, dv, past_kt, past_v, layer, lamv, g, B, T, lam_init):
    M = B * T
    P = past_kt.shape[-1]
    return pl.pallas_call(
        functools.partial(_diff_sample_kernel, lam_init=lam_init, past_len=P),
        out_shape=jax.ShapeDtypeStruct((M, 512), BF16),
        grid=(B,),
        in_specs=[pl.BlockSpec((T, 512), lambda b: (b, 0)),
                  pl.BlockSpec((T, 512), lambda b: (b, 0)),
                  pl.BlockSpec((T, 512), lambda b: (b, 0)),
                  pl.BlockSpec((None, None, DIFF_HEADS, 2, DIFF_DH, P), lambda b: (layer, b, 0, 0, 0, 0)),
                  pl.BlockSpec((None, None, P * DIFF_HEADS, DIFF_DV), lambda b: (layer, b, 0, 0)),
                  pl.BlockSpec((4, DIFF_DH), lambda b: (0, 0)),
                  pl.BlockSpec((1, LANES), lambda b: (0, 0))],
        out_specs=pl.BlockSpec((T, 512), lambda b: (b, 0)),
        compiler_params=_cparams(("arbitrary",)),
        name="diff_sample",
    )(dq, dk, dv, past_kt, past_v, lamv, g)


INT_MIN = -2 ** 31


def _idx_score(iq, iw, ikr, transposed=False):
    lane = lax.broadcasted_iota(I32, (1, IDX_HEADS * IDX_DH), 1)
    sc = None
    for i in range(IDX_HEADS):
        iqi = jnp.where((lane // IDX_DH) == i, iq, jnp.zeros_like(iq))
        lg = _dot(iqi, ikr) if transposed else _dot_nt(iqi, ikr)
        t = iw[:, i:i + 1] * jnp.maximum(lg, 0.0)
        sc = t if sc is None else sc + t
    return jnp.where(sc == 0.0, 0.0, sc)


def _order_key(score):
    bits = pltpu.bitcast(score, I32)
    return jnp.where(bits < 0, bits ^ 0x7FFFFFFF, bits)


def _count(mask):
    return jnp.sum(mask.astype(F32), axis=1, keepdims=True)


def _f32_key(x):
    b = int(np.float32(x).view(np.int32))
    return b ^ 0x7FFFFFFF if b < 0 else b


def _key_value(k):
    return pltpu.bitcast(jnp.where(k < 0, k ^ 0x7FFFFFFF, k), F32)


def _topk_threshold(key_refs, smax, smin, n_sel, n_keys, probe_masked):
    tq = key_refs[0].shape[0]
    n = float(n_sel)

    def count_ge(cand):
        tot = None
        for kr in key_refs:
            c = _count(kr[...] >= cand)
            tot = c if tot is None else tot + c
        return tot

    def update(st, cand):
        lo, cnt_lo, hi, cnt_hi = st
        c = count_ge(cand)
        ge = c >= n
        up = ge & (cand > lo)
        dn = jnp.logical_not(ge) & (cand < hi)
        return (jnp.where(up, cand, lo), jnp.where(up, c, cnt_lo),
                jnp.where(dn, cand, hi), jnp.where(dn, c, cnt_hi))

    def settled(st):
        lo, cnt_lo, hi, _ = st
        return (cnt_lo == n) | ((hi - 1) == lo)

    def unsettled_any(st):
        return jnp.max(jnp.where(settled(st), 0.0, 1.0)) > 0.0

    col = lambda v, dt: jnp.full((tq, 1), v, dt)
    kmax = _order_key(smax)
    hi0 = jnp.where(kmax == 2 ** 31 - 1, kmax, kmax + 1)
    st = (col(INT_MIN, I32), col(float(n_keys), F32), hi0, col(0.0, F32))
    fixed = ((_f32_key(NEG_INF), _f32_key(NEG_INF) + 1) if probe_masked else ()) + (0, 1)
    for c in fixed:
        st = update(st, col(c, I32))
    st = update(st, _order_key(smin))
    log_n = math.log(n)

    def cond(carry):
        it, _, go = carry
        return jnp.logical_and(it < 100, go)

    def body(carry):
        it, st, _ = carry
        lo, cnt_lo, hi, cnt_hi = st
        a = jnp.log(cnt_lo + 0.5)
        frac = (a - log_n) / jnp.maximum(a - jnp.log(cnt_hi + 0.5), 1e-9)
        v_lo = _key_value(lo)
        guess = _order_key(v_lo + frac * (_key_value(hi) - v_lo))
        mid = (lo >> 1) + (hi >> 1) + (lo & hi & 1)
        cand = jnp.minimum(jnp.maximum(jnp.where(lax.rem(it, 3) == 2, mid, guess), lo + 1), hi - 1)
        st = update(st, jnp.where(settled(st), lo, cand))
        return it + 1, st, unsettled_any(st)

    _, st, _ = lax.while_loop(cond, body, (jnp.int32(0), st, unsettled_any(st)))
    lo, cnt_lo, hi, cnt_hi = st
    tied = ((hi - 1) == lo) & (cnt_lo > n)
    return lo, jnp.where(tied, n - cnt_hi, float(n_keys))


def _row_extremes(score, vis):
    if vis is None:
        return jnp.max(score, axis=1, keepdims=True), jnp.min(score, axis=1, keepdims=True)
    return (jnp.max(jnp.where(vis, score, -jnp.inf), axis=1, keepdims=True),
            jnp.min(jnp.where(vis, score, jnp.inf), axis=1, keepdims=True))


def _selection_bias(key_refs, bias_refs, vis_fns, smax, smin, n_sel, n_keys, probe_masked):
    t, need = _topk_threshold(key_refs, smax, smin, n_sel, n_keys, probe_masked)
    tq = key_refs[0].shape[0]
    base = jnp.zeros((tq, 1), F32)
    for kr, br, vis_fn in zip(key_refs, bias_refs, vis_fns):
        W = kr.shape[1]
        bw = min(W, LANES)
        assert W % bw == 0
        tri = (lax.broadcasted_iota(I32, (bw, bw), 0) <= lax.broadcasted_iota(I32, (bw, bw), 1)).astype(BF16)
        ones_tri = jnp.concatenate([jnp.ones((bw, bw), BF16), tri], axis=0)
        earlier = jnp.zeros((tq, bw), F32)
        for j in range(W // bw):
            cs = slice(j * bw, (j + 1) * bw)
            key = kr[:, cs]
            tie = key == t
            tb = jnp.where(tie, 1.0, 0.0)
            rank = base + _dot(jnp.concatenate([earlier, tb], axis=1).astype(BF16), ones_tri)
            sel = (key > t) | (tie & (rank <= need))
            if vis_fn is not None:
                sel = sel & vis_fn(j * bw, bw)
            br[:, cs] = jnp.where(sel, 0.0, NEG_INF)
            earlier = earlier + tb
        base = base + jnp.sum(earlier, axis=1, keepdims=True)


def _dsa_attend(q_ref, bias_refs, k_loads, v_loads, o_ref):
    for h in range(DSA_HEADS):
        hs = slice(h * DSA_DH, (h + 1) * DSA_DH)
        q = q_ref[:, hs]
        ss = [_dot_nt(q, kl(hs)) + br[...] for kl, br in zip(k_loads, bias_refs)]
        m = None
        for s in ss:
            ms = jnp.max(s, axis=1, keepdims=True)
            m = ms if m is None else jnp.maximum(m, ms)
        l = None
        o = None
        for s, vl in zip(ss, v_loads):
            p = jnp.exp(s - m)
            ls = jnp.sum(p, axis=1, keepdims=True)
            os_ = _dot(p.astype(BF16), vl(hs))
            l = ls if l is None else l + ls
            o = os_ if o is None else o + os_
        o_ref[:, hs] = (o / l).astype(BF16)


def _dsa_prompt_kernel(q_ref, iq_ref, iw_ref, k_ref, v_ref, ikr_ref, o_ref, key_scr, bias_scr,
                       *, tq, q_tile0, n_sel, probe_masked):
    tk = k_ref.shape[0]
    qt = q_tile0 + pl.program_id(1)
    score = _idx_score(iq_ref[...], iw_ref[...], ikr_ref[...])
    qpos = qt * tq + lax.broadcasted_iota(I32, (tq, tk), 0)
    kidx = lax.broadcasted_iota(I32, (tq, tk), 1)
    vis = (kidx // CHUNK) <= (qpos // CHUNK)
    key_scr[...] = _order_key(jnp.where(vis, score, NEG_INF))
    smax, smin = _row_extremes(score, vis)
    qchunk = (qt * tq + lax.broadcasted_iota(I32, (tq, 1), 0)) // CHUNK

    def vis_block(off, bw):
        return ((off + lax.broadcasted_iota(I32, (tq, bw), 1)) // CHUNK) <= qchunk

    _selection_bias([key_scr], [bias_scr], [vis_block], smax, smin, n_sel, tk, probe_masked)
    _dsa_attend(q_ref, [bias_scr], [lambda hs: k_ref[:, hs]], [lambda hs: v_ref[:, hs]], o_ref)


def _dsa_prompt(cq, iq, iw, ck, cv, ikr, B, T):
    tq = min(T, 128)
    nq = T // tq
    n_sel = min(DSA_TOPK_MAX, T // 4)
    ng = 8 if nq % 8 == 0 else (4 if nq % 4 == 0 else 1)
    tpg = nq // ng
    outs = []
    for gi in range(ng):
        tk = (gi + 1) * tpg * tq
        nkb = T // tk if T % tk == 0 else None

        def kv_spec(w, tk=tk):
            return pl.BlockSpec((None, tk, w), lambda b, i: (b, 0, 0))

        def q_spec(w, gi=gi):
            return pl.BlockSpec((None, tq, w), lambda b, i: (b, gi * tpg + i, 0))

        out = pl.pallas_call(
            functools.partial(_dsa_prompt_kernel, tq=tq, q_tile0=gi * tpg, n_sel=n_sel,
                              probe_masked=gi * tpg * tq + CHUNK < n_sel),
            out_shape=jax.ShapeDtypeStruct((B, tpg * tq, 512), BF16),
            grid=(B, tpg),
            in_specs=[q_spec(512), q_spec(256), q_spec(LANES), kv_spec(512), kv_spec(512), kv_spec(256)],
            out_specs=pl.BlockSpec((None, tq, 512), lambda b, i: (b, i, 0)),
            scratch_shapes=[pltpu.VMEM((tq, tk), I32), pltpu.VMEM((tq, tk), F32)],
            compiler_params=_cparams(("arbitrary", "arbitrary"), 56 * 1024 * 1024),
            name=f"dsa_prompt_{gi}",
        )(cq.reshape(B, T, 512), iq.reshape(B, T, 256), iw.reshape(B, T, LANES),
          ck.reshape(B, T, 512), cv.reshape(B, T, 512), ikr.reshape(B, T, 256))
        outs.append(out)
    o = outs[0] if ng == 1 else jnp.concatenate(outs, axis=1)
    return o.reshape(B * T, 512)


def _dsa_sample_kernel(q_ref, iq_ref, iw_ref, kn_ref, vn_ref, ikrn_ref, kp_ref, vp_ref, ikp_ref, o_ref,
                       keyp_scr, keyn_scr, biasp_scr, biasn_scr, *, past_len, n_sel):
    T = q_ref.shape[0]
    P = past_len
    iq = iq_ref[...]
    iw = iw_ref[...]
    ikt = ikp_ref[...].astype(BF16)
    ikrp = jnp.concatenate([ikt] * IDX_HEADS, axis=0)
    score_p = _idx_score(iq, iw, ikrp, transposed=True)
    keyp_scr[...] = _order_key(score_p)
    pmax, pmin = _row_extremes(score_p, None)
    qpos = P + lax.broadcasted_iota(I32, (T, T), 0)
    kpos = P + lax.broadcasted_iota(I32, (T, T), 1)
    vis_n = (kpos // CHUNK) <= (qpos // CHUNK)
    score_n = _idx_score(iq, iw, ikrn_ref[...])
    keyn_scr[...] = _order_key(jnp.where(vis_n, score_n, NEG_INF))
    nmax, nmin = _row_extremes(score_n, vis_n)
    _selection_bias([keyp_scr, keyn_scr], [biasp_scr, biasn_scr], [None, lambda off, bw: vis_n[:, off:off + bw]],
                    jnp.maximum(pmax, nmax), jnp.minimum(pmin, nmin),
                    n_sel, P + T, probe_masked=P + min(T, CHUNK) < n_sel)

    def past_head(ref):
        return lambda hs: ref[pl.ds(hs.start // DSA_DH, P, stride=DSA_HEADS), :].astype(BF16)

    _dsa_attend(q_ref, [biasp_scr, biasn_scr],
                [past_head(kp_ref), lambda hs: kn_ref[:, hs]],
                [past_head(vp_ref), lambda hs: vn_ref[:, hs]], o_ref)


def _dsa_sample(cq, iq, iw, ck, cv, ikr, past_k, past_v, past_ikt, layer, B, T):
    M = B * T
    P = past_ikt.shape[-1]
    n_sel = min(DSA_TOPK_MAX, (P + T) // 4)

    def rspec(w):
        return pl.BlockSpec((T, w), lambda b: (b, 0))

    def pspec(r, w):
        return pl.BlockSpec((None, None, r, w), lambda b: (layer, b, 0, 0))

    return pl.pallas_call(
        functools.partial(_dsa_sample_kernel, past_len=P, n_sel=n_sel),
        out_shape=jax.ShapeDtypeStruct((M, 512), BF16),
        grid=(B,),
        in_specs=[rspec(512), rspec(256), rspec(LANES), rspec(512), rspec(512), rspec(256),
                  pspec(P * DSA_HEADS, DSA_DH), pspec(P * DSA_HEADS, DSA_DH), pspec(IDX_DH, P)],
        out_specs=rspec(512),
        scratch_shapes=[pltpu.VMEM((T, P), I32), pltpu.VMEM((T, T), I32),
                        pltpu.VMEM((T, P), F32), pltpu.VMEM((T, T), F32)],
        compiler_params=_cparams(("arbitrary",)),
        name="dsa_sample",
    )(cq, iq, iw, ck, cv, ikr, past_k, past_v, past_ikt)


def _merge_kernel(og_ref, od_ref, oc_ref, g0_ref, g1_ref, g2_ref, x_ref, m_ref,
                  wg_ref, wd_ref, wc_ref, wo_ref, gn_ref, wr_ref, br_ref,
                  x1_ref, h2_ref, ti_ref, tw_ref, rk_ref, cnt_ref, run_scr, *, nb):
    merged = (jax.nn.sigmoid(g0_ref[...]) * _dot(og_ref[...], wg_ref[...])
              + jax.nn.sigmoid(g1_ref[...]) * _dot(od_ref[...], wd_ref[...])
              + jax.nn.sigmoid(g2_ref[...]) * _dot(oc_ref[...], wc_ref[...]))
    mix = _dot(merged.astype(BF16), wo_ref[...])
    m = m_ref[...]
    x1 = x_ref[...] + _gate_rows(mix, m, 2, nb)
    x1_ref[...] = x1
    xn = x1 * lax.rsqrt(jnp.mean(x1 * x1, axis=-1, keepdims=True) + EPS) * gn_ref[...]
    h2 = _modulate(xn, m, 3, 4, nb)
    _rows_to_tiles(h2_ref, h2)
    lg = _dot(h2.astype(BF16), wr_ref[...]) + br_ref[...]
    tm = lg.shape[0]
    lane = lax.broadcasted_iota(I32, (tm, LANES), 1).astype(F32)
    vals, idxs = [], []
    for _ in range(TOP_K):
        mx = jnp.max(lg, axis=1, keepdims=True)
        ix = jnp.min(jnp.where(lg == mx, lane, float(LANES)), axis=1, keepdims=True)
        vals.append(mx)
        idxs.append(ix)
        lg = jnp.where(lane == ix, -jnp.inf, lg)
    es = [jnp.exp(v - vals[0]) for v in vals]
    den = es[0] + es[1] + es[2] + es[3]
    @pl.when(pl.program_id(0) == 0)
    def _():
        run_scr[...] = jnp.zeros(run_scr.shape, F32)

    hot = [lane == ix for ix in idxs]
    tot = (hot[0].astype(F32) + hot[1].astype(F32)) + (hot[2].astype(F32) + hot[3].astype(F32))
    strict = (lax.broadcasted_iota(I32, (tm, tm), 1) < lax.broadcasted_iota(I32, (tm, tm), 0)).astype(BF16)
    before = run_scr[...] + _dot(strict, tot.astype(BF16))
    ti = jnp.zeros((tm, LANES), F32)
    tw = jnp.zeros((tm, LANES), F32)
    rk = jnp.zeros((tm, LANES), F32)
    for r in range(TOP_K):
        ti = jnp.where(lane == float(r), idxs[r], ti)
        tw = jnp.where(lane == float(r), es[r] / den, tw)
        rk = jnp.where(lane == float(r), jnp.sum(jnp.where(hot[r], before, 0.0), axis=1, keepdims=True), rk)
    ti_ref[...] = ti.astype(I32)
    tw_ref[...] = tw
    rk_ref[...] = rk.astype(I32)
    run_new = run_scr[...] + jnp.sum(tot, axis=0, keepdims=True)
    run_scr[...] = run_new
    cnt_ref[...] = run_new.astype(I32)


def _merge(og, od, oc, z, x2, m, wts, B, T):
    M = B * T
    tm, nb = _row_tiling(B, T, 512)
    wg, wd, wc, wo, gn, wr, br = wts

    def rspec(w):
        return pl.BlockSpec((tm, w), lambda i: (i, 0))

    def gspec(k):
        return pl.BlockSpec((tm, D_MODEL), lambda i: (i, Z_GATES // D_MODEL + k))

    def wspec(r, c):
        return pl.BlockSpec((r, c), lambda i: (0, 0))

    return pl.pallas_call(
        functools.partial(_merge_kernel, nb=nb),
        out_shape=[jax.ShapeDtypeStruct((M, D_MODEL), F32), jax.ShapeDtypeStruct((M * ROW_TILE, LANES), F32),
                   jax.ShapeDtypeStruct((M, LANES), I32), jax.ShapeDtypeStruct((M, LANES), F32),
                   jax.ShapeDtypeStruct((M, LANES), I32), jax.ShapeDtypeStruct((1, LANES), I32)],
        grid=(M // tm,),
        in_specs=[rspec(512), rspec(512), rspec(512), gspec(0), gspec(1), gspec(2), rspec(D_MODEL),
                  pl.BlockSpec((nb, 6, D_MODEL), lambda i: ((i * tm) // (T * nb), 0, 0)),
                  wspec(512, D_MODEL), wspec(512, D_MODEL), wspec(512, D_MODEL), wspec(D_MODEL, D_MODEL),
                  wspec(1, D_MODEL), wspec(D_MODEL, LANES), wspec(1, LANES)],
        out_specs=[rspec(D_MODEL), pl.BlockSpec((tm * ROW_TILE, LANES), lambda i: (i, 0)),
                   rspec(LANES), rspec(LANES), rspec(LANES),
                   pl.BlockSpec((1, LANES), lambda i: (0, 0))],
        scratch_shapes=[pltpu.VMEM((1, LANES), F32)],
        compiler_params=_cparams(("arbitrary",)),
        name="merge_router",
    )(og, od, oc, z, z, z, x2, m, wg, wd, wc, wo, gn, wr, br)


DMA_UNROLL = 8


def _w1prep_kernel(w_ref, sel_ref, g_ref, u_ref):
    sel = sel_ref[...]
    for c in range(w_ref.shape[1] // 256):
        r = _dot(w_ref[:, c * 256:(c + 1) * 256].astype(BF16), sel)
        g_ref[:, c * LANES:(c + 1) * LANES] = r[:, :LANES].astype(BF16)
        u_ref[:, c * LANES:(c + 1) * LANES] = r[:, LANES:].astype(BF16)


def _w1prep(w1_all, layer):
    E = w1_all.shape[1]
    tr = 512
    j = jnp.arange(256, dtype=I32)
    src = jnp.where(j < LANES, 2 * j, 2 * (j - LANES) + 1)
    sel = (jnp.arange(256, dtype=I32)[:, None] == src[None, :]).astype(BF16)
    return pl.pallas_call(
        _w1prep_kernel,
        out_shape=[jax.ShapeDtypeStruct((E, D_MODEL, D_FF), BF16)] * 2,
        grid=(E, D_MODEL // tr),
        in_specs=[pl.BlockSpec((None, None, tr, 2 * D_FF), lambda e, r: (layer, e, r, 0)),
                  pl.BlockSpec((256, 256), lambda e, r: (0, 0))],
        out_specs=[pl.BlockSpec((None, tr, D_FF), lambda e, r: (e, r, 0))] * 2,
        compiler_params=_cparams(("arbitrary", "arbitrary")),
        name="w1prep",
    )(w1_all, sel)


def _rows_to_tiles(ref, x):
    n = x.shape[0]
    for c in range(ROW_TILE):
        ref[pl.ds(c, n, stride=ROW_TILE), :] = x[:, c * LANES:(c + 1) * LANES]


def _tiles_to_rows(ref, base, n):
    return jnp.concatenate([ref[pl.ds(base + c, n, stride=ROW_TILE), :] for c in range(ROW_TILE)], axis=1)


def _tile_rows(ref, r):
    return ref.at[pl.ds(pl.multiple_of(r * ROW_TILE, ROW_TILE), ROW_TILE), :]


def _dispatch_kernel(pos_ref, h_ref, xs_in, xs_out, sem, *, tm):
    del xs_in

    tokens = DMA_UNROLL // TOP_K

    def body(j, c):
        for tt in range(tokens):
            src = _tile_rows(h_ref, j * tokens + tt)
            for k in range(TOP_K):
                dst = _tile_rows(xs_out, pos_ref[0, 0, (j * tokens + tt) * TOP_K + k])
                pltpu.make_async_copy(src, dst, sem).start()
        return c

    lax.fori_loop(0, tm // tokens, body, 0)
    for _ in range(TOP_K):
        pltpu.make_async_copy(h_ref, xs_out.at[pl.ds(0, tm * ROW_TILE), :], sem).wait()


def _dispatch(h2t, pos, n_rows, tm):
    M = h2t.shape[0] // ROW_TILE
    nt = M // tm
    return pl.pallas_call(
        functools.partial(_dispatch_kernel, tm=tm),
        out_shape=jax.ShapeDtypeStruct((n_rows * ROW_TILE, LANES), F32),
        grid=(nt,),
        in_specs=[pl.BlockSpec((1, 1, TOP_K * tm), lambda i: (i, 0, 0), memory_space=pltpu.SMEM),
                  pl.BlockSpec((tm * ROW_TILE, LANES), lambda i: (i, 0)),
                  pl.BlockSpec(memory_space=pl.ANY)],
        out_specs=pl.BlockSpec(memory_space=pl.ANY),
        scratch_shapes=[pltpu.SemaphoreType.DMA],
        input_output_aliases={2: 0},
        compiler_params=_cparams(("arbitrary",)),
        name="moe_dispatch",
    )(pos.reshape(nt, 1, TOP_K * tm), h2t, jnp.zeros((n_rows * ROW_TILE, LANES), F32))


def _ffn_kernel(be_ref, nu_ref, x_ref, w1g_ref, b1g_ref, w1u_ref, b1u_ref, w2_ref, b2_ref, o_ref, *, bm):
    @pl.when(pl.program_id(0) < nu_ref[0])
    def _():
        x = _tiles_to_rows(x_ref, 0, bm).astype(BF16)
        g = jnp.minimum(_dot(x, w1g_ref[...]) + b1g_ref[...], SWIGLU_LIMIT)
        u = jnp.clip(_dot(x, w1u_ref[...]) + b1u_ref[...], -SWIGLU_LIMIT, SWIGLU_LIMIT)
        a = g * jax.nn.sigmoid(SWIGLU_ALPHA * g) * (u + 1.0)
        _rows_to_tiles(o_ref, _dot(a.astype(BF16), w2_ref[...]) + b2_ref[...])

    @pl.when(pl.program_id(0) >= nu_ref[0])
    def _():
        o_ref[...] = jnp.zeros(o_ref.shape, F32)


def _ffn(xs, block_e, n_used, w1g, b1g, w1u, b1u, w2_all, b2, layer, bm):
    n_rows = xs.shape[0] // ROW_TILE
    nblk = n_rows // bm

    def wspec(r, c):
        return pl.BlockSpec((None, r, c), lambda i, be, nu: (be[i], 0, 0))

    grid_spec = pltpu.PrefetchScalarGridSpec(
        num_scalar_prefetch=2,
        grid=(nblk,),
        in_specs=[pl.BlockSpec((bm * ROW_TILE, LANES), lambda i, be, nu: (jnp.minimum(i, nu[0] - 1), 0)),
                  wspec(D_MODEL, D_FF), wspec(1, D_FF), wspec(D_MODEL, D_FF), wspec(1, D_FF),
                  pl.BlockSpec((None, None, D_FF, D_MODEL), lambda i, be, nu: (layer, be[i], 0, 0)),
                  wspec(1, D_MODEL)],
        out_specs=pl.BlockSpec((bm * ROW_TILE, LANES), lambda i, be, nu: (i, 0)))
    return pl.pallas_call(
        functools.partial(_ffn_kernel, bm=bm),
        out_shape=jax.ShapeDtypeStruct((n_rows * ROW_TILE, LANES), F32),
        grid_spec=grid_spec,
        compiler_params=_cparams(("arbitrary",)),
        name="moe_ffn",
    )(block_e, n_used, xs, w1g, b1g, w1u, b1u, w2_all, b2)


def _combine_kernel(pos_ref, ys_hbm, tw_ref, x_ref, m_ref, o_ref, ybuf, sem, *, tm, nb):
    def body(j, c):
        for u in range(DMA_UNROLL):
            n = j * DMA_UNROLL + u
            pltpu.make_async_copy(_tile_rows(ys_hbm, pos_ref[0, 0, n]), _tile_rows(ybuf, n), sem).start()
        return c

    lax.fori_loop(0, TOP_K * tm // DMA_UNROLL, body, 0)
    pltpu.make_async_copy(ys_hbm.at[pl.ds(0, TOP_K * tm * ROW_TILE), :], ybuf, sem).wait()
    tw = tw_ref[...]
    y = None
    for k in range(TOP_K):
        yk = tw[:, k:k + 1] * _tiles_to_rows(ybuf, k * tm * ROW_TILE, tm)
        y = yk if y is None else y + yk
    o_ref[...] = x_ref[...] + _gate_rows(y, m_ref[...], 5, nb)


def _combine(pos_t, ys, top_w, x1, m, B, T, tm, nb):
    M = B * T
    nt = M // tm
    return pl.pallas_call(
        functools.partial(_combine_kernel, tm=tm, nb=nb),
        out_shape=jax.ShapeDtypeStruct((M, D_MODEL), F32),
        grid=(nt,),
        in_specs=[pl.BlockSpec((1, 1, TOP_K * tm), lambda i: (i, 0, 0), memory_space=pltpu.SMEM),
                  pl.BlockSpec(memory_space=pl.ANY),
                  pl.BlockSpec((tm, LANES), lambda i: (i, 0)),
                  pl.BlockSpec((tm, D_MODEL), lambda i: (i, 0)),
                  pl.BlockSpec((nb, 6, D_MODEL), lambda i: ((i * tm) // (T * nb), 0, 0))],
        out_specs=pl.BlockSpec((tm, D_MODEL), lambda i: (i, 0)),
        scratch_shapes=[pltpu.VMEM((TOP_K * tm * ROW_TILE, LANES), F32), pltpu.SemaphoreType.DMA],
        compiler_params=_cparams(("arbitrary",)),
        name="moe_combine",
    )(pos_t.reshape(nt, 1, TOP_K * tm), ys, top_w, x1, m)


def _route(top_i, rank, counts, bm):
    N = top_i.shape[0]
    NK = N * TOP_K
    padded = (counts + bm - 1) // bm * bm
    pad_end = jnp.cumsum(padded)
    pad_start = pad_end - padded
    onehot = top_i[:, :, None] == jnp.arange(N_EXPERTS, dtype=I32)[None, None, :]
    pos = rank + jnp.sum(jnp.where(onehot, pad_start[None, None, :], 0), axis=2)
    n_rows = (-(-NK // bm)) * bm + N_EXPERTS * bm
    nblk = n_rows // bm
    starts = jnp.arange(nblk, dtype=I32) * bm
    block_e = jnp.minimum(jnp.sum((pad_end[None, :] <= starts[:, None]).astype(I32), axis=1), N_EXPERTS - 1)
    n_used = (pad_end[-1:] // bm).astype(I32)
    return pos.reshape(NK).astype(I32), n_rows, block_e.astype(I32), n_used


def _moe(h2, top_i, top_w, rank, counts, x1, m, ew, B, T):
    M = B * T
    bm = 256 if M * TOP_K >= 256 * N_EXPERTS * 4 else 128
    pos, n_rows, block_e, n_used = _route(top_i[:, :TOP_K], rank[:, :TOP_K], counts[0, :N_EXPERTS], bm)
    tm, nb = _row_tiling(B, T, 256)
    xs = _dispatch(h2, pos, n_rows, tm)
    ys = _ffn(xs, block_e, n_used, *ew, bm)
    pos_t = pos.reshape(M // tm, tm, TOP_K).transpose(0, 2, 1)
    return _combine(pos_t, ys, top_w, x1, m, B, T, tm, nb)


def _prep_layer(P, l, w2_all):
    w_t = P['w_in'][l].T

    def rows(name):
        o, s = _SRC[name]
        return w_t[o:o + s]

    zeros = lambda n: jnp.zeros((n, D_MODEL), F32)
    w_r = jnp.concatenate([
        rows('gla_q'), rows('gla_k'), rows('gla_v'), rows('gla_r'),
        rows('diff_q'), rows('diff_k'), rows('diff_v'),
        rows('dsa_q'), rows('dsa_k'), rows('dsa_v'),
        rows('idx_q'), rows('idx_k'), rows('idx_w'), zeros(60), rows('gla_a'), zeros(112),
        rows('gates')], axis=0).astype(BF16)
    assert w_r.shape[0] == Z_WIDTH
    W = GLA_HEADS * GLA_DK
    wa = jnp.zeros((LANES, W), F32).at[:GLA_GATE_RANK].set(P['w_gla_a2'][l]).astype(BF16)
    w1g, w1u = _w1prep(P['w_mlp1'], l)
    b1 = P['b_mlp1'][l]
    return dict(
        w_in=w_r, wa=wa, ba=P['b_gla_a2'][l].reshape(1, W),
        g_gla=jnp.tile(P['g_gla_out'][l], GLA_HEADS).reshape(1, 512),
        gains=(jnp.tile(P['g_diff_q'][l], 8).reshape(1, 512), jnp.tile(P['g_diff_k'][l], 8).reshape(1, 512),
               jnp.tile(P['g_dsa_q'][l], 4).reshape(1, 512), jnp.tile(P['g_dsa_k'][l], 4).reshape(1, 512),
               jnp.tile(P['g_idx_k'][l], 2).reshape(1, LANES)),
        lamv=jnp.stack([P['lambda_q1'][l], P['lambda_k1'][l], P['lambda_q2'][l], P['lambda_k2'][l]]),
        g_diff=P['g_diff_out'][l].reshape(1, LANES),
        merge=(P['w_branch_gla'][l].astype(BF16), P['w_branch_diff'][l].astype(BF16),
               P['w_branch_dsa'][l].astype(BF16), P['w_out'][l].astype(BF16),
               P['g_norm2'][l].reshape(1, D_MODEL),
               jnp.zeros((D_MODEL, LANES), F32).at[:, :N_EXPERTS].set(P['w_router'][l]).astype(BF16),
               jnp.full((1, LANES), NEG_INF, F32).at[0, :N_EXPERTS].set(P['b_router'][l])),
        experts=(w1g, b1[:, None, 0::2], w1u, b1[:, None, 1::2], w2_all, P['b_mlp2'][l][:, None, :], l),
    )


def _trunk(x, c, past, P, prepped):
    B, T, _ = x.shape
    M = B * T
    x2 = x.reshape(M, D_MODEL)
    past_len = 0 if past is None else past[0].shape[2]
    pos = past_len + jnp.arange(T, dtype=I32)
    tabs64 = _rope_tables(pos, 64)
    tabs128 = _rope_tables(pos, 128)
    W = GLA_HEADS * GLA_DK
    if past is not None:
        L, _, PL = past[0].shape[:3]
        pdk = past[0].transpose(0, 1, 3, 4, 5, 2)
        pdv = past[1].reshape(L, B, PL * DIFF_HEADS, DIFF_DV)
        pck = past[2].reshape(L, B, PL * DSA_HEADS, DSA_DH)
        pcv = past[3].reshape(L, B, PL * DSA_HEADS, DSA_DH)
        pik = past[4].transpose(0, 1, 3, 2)
        s0_all = past[5].transpose(0, 1, 4, 2, 3).reshape(past[5].shape[0], B, GLA_DV, W)
    per_layer = []
    for l, pp in enumerate(prepped):
        lam_init = 0.8 - 0.6 * math.exp(-0.3 * l)
        m = _ada(c, P['w_ada'], P['b_ada'], l)
        z = _inproj(x2, m, P['g_norm1'][l], pp['w_in'], B, T)
        (dq, dk32, dkb, dvb, cq, ck32, ckb, cvb, iq, ik32, ikr, iw, dv32, cv32) = _post(
            z, tabs64, tabs128, pp['gains'], B, T)
        s0t = jnp.zeros((B, GLA_DV, W), F32) if past is None else s0_all[l]
        o_gla, st = _gla(z, pp['wa'], pp['ba'], pp['g_gla'], s0t, B, T)
        if past is None:
            o_diff = _diff_prompt(dq, dkb, dvb, pp['lamv'], pp['g_diff'], B, T, lam_init)
            o_dsa = _dsa_prompt(cq, iq, iw, ckb, cvb, ikr, B, T)
        else:
            o_diff = _diff_sample(dq, dkb, dvb, pdk, pdv, l, pp['lamv'], pp['g_diff'], B, T, lam_init)
            o_dsa = _dsa_sample(cq, iq, iw, ckb, cvb, ikr, pck, pcv, pik, l, B, T)
        x1, h2, top_i, top_w, rank, counts = _merge(o_gla, o_diff, o_dsa, z, x2, m, pp['merge'], B, T)
        x2 = _moe(h2, top_i, top_w, rank, counts, x1, m, pp['experts'], B, T)
        s_gla = st.reshape(B, GLA_DV, GLA_HEADS, GLA_DK).transpose(0, 2, 3, 1)
        per_layer.append((dk32.reshape(B, T, DIFF_HEADS, 2, DIFF_DH),
                          dv32.reshape(B, T, DIFF_HEADS, DIFF_DV),
                          ck32.reshape(B, T, DSA_HEADS, DSA_DH),
                          cv32.reshape(B, T, DSA_HEADS, DSA_DH),
                          ik32.reshape(B, T, IDX_DH),
                          s_gla))
    stacked = tuple(jnp.stack([st[i] for st in per_layer]) for i in range(6))
    return x2.reshape(B, T, D_MODEL), stacked


def kernel(x_prompt, x_sample, cache_diff_k, cache_diff_v, cache_dsa_k, cache_dsa_v, cache_dsa_idx_k,
           state_gla, c_prompt, c_sample, w_ada, b_ada, g_norm1, g_norm2, w_in, w_gla_a2, b_gla_a2,
           g_gla_out, g_diff_q, g_diff_k, lambda_q1, lambda_k1, lambda_q2, lambda_k2, g_diff_out,
           g_dsa_q, g_dsa_k, g_idx_k, w_branch_gla, w_branch_diff, w_branch_dsa, w_out, w_router,
           b_router, w_mlp1, b_mlp1, w_mlp2, b_mlp2):
    P = dict(w_ada=w_ada, b_ada=b_ada, g_norm1=g_norm1, g_norm2=g_norm2, w_in=w_in, w_gla_a2=w_gla_a2,
             b_gla_a2=b_gla_a2, g_gla_out=g_gla_out, g_diff_q=g_diff_q, g_diff_k=g_diff_k,
             lambda_q1=lambda_q1, lambda_k1=lambda_k1, lambda_q2=lambda_q2, lambda_k2=lambda_k2,
             g_diff_out=g_diff_out, g_dsa_q=g_dsa_q, g_dsa_k=g_dsa_k, g_idx_k=g_idx_k,
             w_branch_gla=w_branch_gla, w_branch_diff=w_branch_diff, w_branch_dsa=w_branch_dsa,
             w_out=w_out, w_router=w_router, b_router=b_router, w_mlp1=w_mlp1, b_mlp1=b_mlp1,
             w_mlp2=w_mlp2, b_mlp2=b_mlp2)
    depth = w_in.shape[0]
    w2_all = w_mlp2.astype(BF16)
    prepped = [_prep_layer(P, l, w2_all) for l in range(depth)]
    y_prompt, new_p = _trunk(x_prompt, c_prompt, None, P, prepped)
    y_sample, new_s = _trunk(
        x_sample, c_sample,
        (cache_diff_k, cache_diff_v, cache_dsa_k, cache_dsa_v, cache_dsa_idx_k, state_gla), P, prepped)
    return (y_prompt, y_sample) + new_p + new_s
```

```python
import functools
import math

import numpy as np
import jax
import jax.numpy as jnp
from jax import lax
from jax.experimental import pallas as pl
from jax.experimental.pallas import tpu as pltpu

F32 = jnp.float32
BF16 = jnp.bfloat16
I32 = jnp.int32

D_MODEL = 1024
CHUNK = 64
ROPE_THETA = 500000.0
ROPE_FRACTION = 4
EPS = 1e-6
NEG_INF = -1e30

GLA_HEADS = 4
GLA_DK = 64
GLA_DV = 128
GLA_GATE_RANK = 16
GLA_GATE_TAU = 16.0
GLA_SUB = 16
DIFF_HEADS = 4
DIFF_DH = 64
DIFF_DV = 128
DSA_HEADS = 4
DSA_DH = 128
IDX_HEADS = 4
IDX_DH = 64
DSA_TOPK_MAX = 256
N_EXPERTS = 32
TOP_K = 4
D_FF = 1024
SWIGLU_LIMIT = 7.0
SWIGLU_ALPHA = 1.702

LANES = 128
ROW_TILE = 8
VMEM_LIMIT = 48 * 1024 * 1024

Z_GQK, Z_GV, Z_GR = 0, 512, 1024
Z_DQ, Z_DK, Z_DV = 1536, 2048, 2560
Z_CQ, Z_CK, Z_CV = 3072, 3584, 4096
Z_MISC = 4608
Z_GATES = 5120
Z_WIDTH = 8192
MISC_IK = 256
MISC_GA = 384

_SRC = {}
_off = 0
for _name, _size in (
        ('gla_q', 256), ('gla_k', 256), ('gla_v', 512), ('gla_a', 16), ('gla_r', 512),
        ('diff_q', 512), ('diff_k', 512), ('diff_v', 512),
        ('dsa_q', 512), ('dsa_k', 512), ('dsa_v', 512),
        ('idx_q', 256), ('idx_k', 64), ('idx_w', 4), ('gates', 3072)):
    _SRC[_name] = (_off, _size)
    _off += _size


def _cparams(sem, vmem=VMEM_LIMIT):
    return pltpu.CompilerParams(dimension_semantics=sem, vmem_limit_bytes=vmem)


def _dot(a, b):
    return jnp.dot(a, b, preferred_element_type=F32)


def _dot_nt(a, b):
    return lax.dot_general(a, b, (((1,), (1,)), ((), ())), preferred_element_type=F32)


def _dot_tn(a, b):
    return lax.dot_general(a, b, (((0,), (0,)), ((), ())), preferred_element_type=F32)


def _silu(x):
    return x * jax.nn.sigmoid(x)


def _row_tiling(B, T, target):
    if T >= target:
        assert T % target == 0
        return target, 1
    nb = 1
    for cand in range(1, B + 1):
        if B % cand == 0 and cand * T <= target:
            nb = cand
    return nb * T, nb


def _modulate(xn, m, shift_i, scale_i, nb):
    tm = xn.shape[0]
    if nb == 1:
        return xn * (1.0 + m[0, scale_i:scale_i + 1, :]) + m[0, shift_i:shift_i + 1, :]
    x3 = xn.reshape(nb, tm // nb, D_MODEL)
    h = x3 * (1.0 + m[:, scale_i:scale_i + 1, :]) + m[:, shift_i:shift_i + 1, :]
    return h.reshape(tm, D_MODEL)


def _gate_rows(y, m, gate_i, nb):
    tm = y.shape[0]
    if nb == 1:
        return y * m[0, gate_i:gate_i + 1, :]
    return (y.reshape(nb, tm // nb, D_MODEL) * m[:, gate_i:gate_i + 1, :]).reshape(tm, D_MODEL)


def _ada_kernel(c_ref, w_ref, b_ref, o_ref):
    s = _silu(c_ref[...])
    o_ref[...] = _dot(s.astype(BF16), w_ref[...].astype(BF16)) + b_ref[...]


def _ada(c, w, b, layer):
    B = c.shape[0]
    out = pl.pallas_call(
        _ada_kernel,
        out_shape=jax.ShapeDtypeStruct((B, 6 * D_MODEL), F32),
        grid=(6,),
        in_specs=[pl.BlockSpec((B, D_MODEL), lambda j: (0, 0)),
                  pl.BlockSpec((None, D_MODEL, D_MODEL), lambda j: (layer, 0, j)),
                  pl.BlockSpec((None, 1, D_MODEL), lambda j: (layer, 0, j))],
        out_specs=pl.BlockSpec((B, D_MODEL), lambda j: (0, j)),
        compiler_params=_cparams(("arbitrary",)),
        name="ada",
    )(c, w, b.reshape(b.shape[0], 1, -1))
    return out.reshape(B, 6, D_MODEL)


def _inproj_kernel(x_ref, m_ref, g_ref, w_ref, o_ref, h_scr, *, nb):
    @pl.when(pl.program_id(1) == 0)
    def _():
        x = x_ref[...]
        xn = x * lax.rsqrt(jnp.mean(x * x, axis=-1, keepdims=True) + EPS) * g_ref[...]
        h_scr[...] = _modulate(xn, m_ref[...], 0, 1, nb).astype(BF16)

    o_ref[...] = _dot_nt(h_scr[...], w_ref[...])


def _inproj(x2, m, g, wt, B, T):
    M = B * T
    tm, nb = _row_tiling(B, T, 1024)
    tn = 1024
    return pl.pallas_call(
        functools.partial(_inproj_kernel, nb=nb),
        out_shape=jax.ShapeDtypeStruct((M, Z_WIDTH), F32),
        grid=(M // tm, Z_WIDTH // tn),
        in_specs=[pl.BlockSpec((tm, D_MODEL), lambda i, j: (i, 0)),
                  pl.BlockSpec((nb, 6, D_MODEL), lambda i, j: ((i * tm) // (T * nb), 0, 0)),
                  pl.BlockSpec((1, D_MODEL), lambda i, j: (0, 0)),
                  pl.BlockSpec((tn, D_MODEL), lambda i, j: (j, 0))],
        out_specs=pl.BlockSpec((tm, tn), lambda i, j: (i, j)),
        scratch_shapes=[pltpu.VMEM((tm, D_MODEL), BF16)],
        compiler_params=_cparams(("arbitrary", "arbitrary")),
        name="inproj",
    )(x2, m, g.reshape(1, -1), wt)


def _rope_tables(pos, d):
    rot = d // ROPE_FRACTION
    half = rot // 2
    T = pos.shape[0]
    inv_freq = ROPE_THETA ** (-jnp.arange(half, dtype=F32) / half)
    ang = pos.astype(F32)[:, None] * inv_freq[None, :]
    cos, sin = jnp.cos(ang), jnp.sin(ang)
    c = jnp.concatenate([cos, cos, jnp.ones((T, d - rot), F32)], axis=1)
    a = jnp.concatenate([-sin, jnp.zeros((T, d - half), F32)], axis=1)
    b = jnp.concatenate([jnp.zeros((T, half), F32), sin, jnp.zeros((T, d - rot), F32)], axis=1)
    reps = LANES // d
    return tuple(jnp.tile(t, (1, reps)) for t in (c, a, b))


def _rope128(xs, tabs, half):
    c, a, b = tabs
    return xs * c + pltpu.roll(xs, LANES - half, 1) * a + pltpu.roll(xs, half, 1) * b


def _norm_rope(x, g, tabs, d, norm=True):
    tm, W = x.shape
    half = d // ROPE_FRACTION // 2
    lo = lax.broadcasted_iota(I32, (tm, LANES), 1) < 64
    outs = []
    for s in range(W // LANES):
        xs = x[:, s * LANES:(s + 1) * LANES]
        if norm:
            sq = xs * xs
            if d == LANES:
                r = lax.rsqrt(jnp.sum(sq, axis=1, keepdims=True) * (1.0 / d) + EPS)
            else:
                s_lo = jnp.sum(jnp.where(lo, sq, 0.0), axis=1, keepdims=True)
                s_hi = jnp.sum(jnp.where(lo, 0.0, sq), axis=1, keepdims=True)
                r = jnp.where(lo, lax.rsqrt(s_lo * (1.0 / d) + EPS), lax.rsqrt(s_hi * (1.0 / d) + EPS))
            xs = xs * r * g[:, s * LANES:(s + 1) * LANES]
        outs.append(_rope128(xs, tabs, half))
    return outs[0] if len(outs) == 1 else jnp.concatenate(outs, axis=1)


def _store_head_rows(ref, x):
    tm = x.shape[0]
    for h in range(4):
        ref[pl.ds(h, tm, stride=4), :] = x[:, h * LANES:(h + 1) * LANES]


def _post_kernel(dq_ref, dk_ref, dv_ref, cq_ref, ck_ref, cv_ref, mi_ref,
                 c64_ref, a64_ref, b64_ref, c128_ref, a128_ref, b128_ref,
                 gdq_ref, gdk_ref, gcq_ref, gck_ref, gik_ref,
                 dq_o, dk32_o, dkb_o, dvb_o, cq_o, ck32_o, ckb_o, cvb_o, iq_o, ik32_o, ikr_o, iw_o,
                 dv32_o, cv32_o):
    t64 = (c64_ref[...], a64_ref[...], b64_ref[...])
    t128 = (c128_ref[...], a128_ref[...], b128_ref[...])
    dq = _norm_rope(dq_ref[...], gdq_ref[...], t64, 64)
    dq_o[...] = (dq * (DIFF_DH ** -0.5)).astype(BF16)
    dk = _norm_rope(dk_ref[...], gdk_ref[...], t64, 64)
    dk32_o[...] = dk
    dkb_o[...] = dk.astype(BF16)
    dv = dv_ref[...]
    _store_head_rows(dv32_o, dv)
    dvb_o[...] = dv.astype(BF16)
    cq = _norm_rope(cq_ref[...], gcq_ref[...], t128, 128)
    cq_o[...] = (cq * (DSA_DH ** -0.5)).astype(BF16)
    ck = _norm_rope(ck_ref[...], gck_ref[...], t128, 128)
    _store_head_rows(ck32_o, ck)
    ckb_o[...] = ck.astype(BF16)
    cv = cv_ref[...]
    _store_head_rows(cv32_o, cv)
    cvb_o[...] = cv.astype(BF16)
    mi = mi_ref[...]
    iq = _norm_rope(mi[:, 0:256], None, t64, 64, norm=False)
    iq_o[...] = (iq * (IDX_DH ** -0.5)).astype(BF16)
    mk = mi[:, MISC_IK:MISC_IK + LANES]
    tm = mk.shape[0]
    lane = lax.broadcasted_iota(I32, (tm, LANES), 1)
    lo = lane < 64
    ssq = jnp.sum(jnp.where(lo, mk * mk, 0.0), axis=1, keepdims=True)
    ikn = mk * lax.rsqrt(ssq * (1.0 / IDX_DH) + EPS) * gik_ref[...]
    ik = _rope128(ikn, t64, IDX_DH // ROPE_FRACTION // 2)
    ik32_o[...] = ik[:, 0:IDX_DH]
    ik2 = jnp.where(lo, ik, pltpu.roll(ik, 64, 1))
    ikr_o[...] = jnp.concatenate([ik2, ik2], axis=1).astype(BF16)
    iw_o[...] = jnp.where(lane < IDX_HEADS, pltpu.roll(mk, 64, 1) * (IDX_HEADS ** -0.5), 0.0)


def _post(z, tabs64, tabs128, gains, B, T):
    M = B * T
    tm = min(T, 512)
    assert T % tm == 0
    npos = T // tm

    def zspec(off):
        return pl.BlockSpec((tm, 512), lambda i: (i, off // 512))

    tab_spec = pl.BlockSpec((tm, LANES), lambda i: (i % npos, 0))

    def gspec(w):
        return pl.BlockSpec((1, w), lambda i: (0, 0))

    HR = "head rows"
    outs = [(512, BF16), (512, F32), (512, BF16), (512, BF16),
            (512, BF16), (HR, F32), (512, BF16), (512, BF16),
            (256, BF16), (IDX_DH, F32), (256, BF16), (LANES, F32), (HR, F32), (HR, F32)]

    def oshape(w, dt):
        return jax.ShapeDtypeStruct((M * 4, LANES) if w is HR else (M, w), dt)

    def ospec(w):
        return pl.BlockSpec((tm * 4, LANES) if w is HR else (tm, w), lambda i: (i, 0))

    return pl.pallas_call(
        _post_kernel,
        out_shape=[oshape(w, dt) for w, dt in outs],
        grid=(M // tm,),
        in_specs=[zspec(Z_DQ), zspec(Z_DK), zspec(Z_DV), zspec(Z_CQ), zspec(Z_CK), zspec(Z_CV), zspec(Z_MISC)]
                 + [tab_spec] * 6 + [gspec(512)] * 4 + [gspec(LANES)],
        out_specs=[ospec(w) for w, _ in outs],
        compiler_params=_cparams(("arbitrary",)),
        name="post",
    )(z, z, z, z, z, z, z, *tabs64, *tabs128, *gains)


def _split3(x):
    hi = x.astype(BF16)
    r = x - hi.astype(F32)
    mid = r.astype(BF16)
    lo = (r - mid.astype(F32)).astype(BF16)
    return hi, mid, lo


def _gla_kernel(qk_ref, v_ref, r_ref, ga_ref, wa_ref, ba_ref, g_ref, s0_ref, o_ref, st_ref, st_scr, *, nct):
    ti = pl.program_id(1)

    @pl.when(ti == 0)
    def _():
        st_scr[...] = s0_ref[...]

    C = CHUNK
    W = GLA_HEADS * GLA_DK
    row = lax.broadcasted_iota(I32, (C, C), 0)
    col = lax.broadcasted_iota(I32, (C, C), 1)
    tri = col <= row
    tri_bf = tri.astype(BF16)
    lane = lax.broadcasted_iota(I32, (1, W), 1)
    hmask = [(lane // GLA_DK) == h for h in range(GLA_HEADS)]
    rowi = lax.broadcasted_iota(I32, (C, W), 0)
    wa = wa_ref[...]
    ba = ba_ref[...]
    g = g_ref[...]

    def chunk(c, carry):
        rows = pl.ds(pl.multiple_of(c * C, C), C)
        qk = qk_ref[rows, :]
        q = qk[:, :W] * (GLA_DK ** -0.5)
        k = qk[:, W:]
        v = v_ref[rows, :].astype(BF16)
        pre = _dot(ga_ref[rows, :].astype(BF16), wa) + ba
        la = (jnp.minimum(pre, 0.0) - jnp.log(1.0 + jnp.exp(-jnp.abs(pre)))) * (1.0 / GLA_GATE_TAU)
        hi, mid, lo = _split3(la)
        b = _dot(tri_bf, hi) + _dot(tri_bf, mid) + _dot(tri_bf, lo)
        st = st_scr[...]
        st_bf = st.astype(BF16)
        qe = q * jnp.exp(b)
        b_end = b[C - 1:C, :]
        kend = k * jnp.exp(b_end - b)
        att_parts = [[] for _ in range(GLA_HEADS)]
        for s in range(C // GLA_SUB):
            r0 = s * GLA_SUB
            br = b[r0:r0 + 1, :]
            qs = q[r0:r0 + GLA_SUB, :] * jnp.exp(b[r0:r0 + GLA_SUB, :] - br)
            ks = k * jnp.exp(br - b)
            if r0 + GLA_SUB < C:
                ks = jnp.where(rowi < r0 + GLA_SUB, ks, 0.0)
            ks = ks.astype(BF16)
            for h in range(GLA_HEADS):
                att_parts[h].append(_dot_nt(jnp.where(hmask[h], qs, 0.0).astype(BF16), ks))
        upd = None
        for h in range(GLA_HEADS):
            hs = slice(h * GLA_DV, (h + 1) * GLA_DV)
            att = jnp.where(tri, jnp.concatenate(att_parts[h], axis=0), 0.0)
            o = _dot(att.astype(BF16), v[:, hs]) + _dot_nt(jnp.where(hmask[h], qe, 0.0).astype(BF16), st_bf)
            u = _dot_tn(v[:, hs], jnp.where(hmask[h], kend, 0.0).astype(BF16))
            upd = u if upd is None else upd + u
            y = o * lax.rsqrt(jnp.mean(o * o, axis=-1, keepdims=True) + EPS) * g[:, hs]
            o_ref[rows, hs] = (y * _silu(r_ref[rows, hs])).astype(BF16)
        st_scr[...] = st * jnp.exp(b_end) + upd
        return carry

    lax.fori_loop(0, nct, chunk, 0)

    @pl.when(ti == pl.num_programs(1) - 1)
    def _():
        st_ref[...] = st_scr[...]


def _gla(z, wa, ba, g, s0t, B, T):
    M = B * T
    tt = min(T, 512)
    assert T % tt == 0 and tt % CHUNK == 0
    nt = T // tt
    W = GLA_HEADS * GLA_DK
    return pl.pallas_call(
        functools.partial(_gla_kernel, nct=tt // CHUNK),
        out_shape=[jax.ShapeDtypeStruct((M, 512), BF16),
                   jax.ShapeDtypeStruct((B, GLA_DV, W), F32)],
        grid=(B, nt),
        in_specs=[pl.BlockSpec((tt, 512), lambda b, t: (b * nt + t, Z_GQK // 512)),
                  pl.BlockSpec((tt, 512), lambda b, t: (b * nt + t, Z_GV // 512)),
                  pl.BlockSpec((tt, 512), lambda b, t: (b * nt + t, Z_GR // 512)),
                  pl.BlockSpec((tt, LANES), lambda b, t: (b * nt + t, (Z_MISC + MISC_GA) // LANES)),
                  pl.BlockSpec((LANES, W), lambda b, t: (0, 0)),
                  pl.BlockSpec((1, W), lambda b, t: (0, 0)),
                  pl.BlockSpec((1, 512), lambda b, t: (0, 0)),
                  pl.BlockSpec((None, GLA_DV, W), lambda b, t: (b, 0, 0))],
        out_specs=[pl.BlockSpec((tt, 512), lambda b, t: (b * nt + t, 0)),
                   pl.BlockSpec((None, GLA_DV, W), lambda b, t: (b, 0, 0))],
        scratch_shapes=[pltpu.VMEM((GLA_DV, W), F32)],
        compiler_params=_cparams(("arbitrary", "arbitrary")),
        name="gla",
    )(z, z, z, z, wa, ba, g, s0t)


def _diff_lambda(lam_ref, lam_init):
    lv = lam_ref[...]
    a = jnp.sum(lv[0:1, :] * lv[1:2, :], axis=1, keepdims=True)
    b = jnp.sum(lv[2:3, :] * lv[3:4, :], axis=1, keepdims=True)
    return jnp.exp(a) - jnp.exp(b) + lam_init


def _diff_finish(o0, o1, lam, g, lam_init):
    o = o0 - lam * o1
    return o * lax.rsqrt(jnp.mean(o * o, axis=-1, keepdims=True) + EPS) * g * (1.0 - lam_init)


def _diff_prompt_kernel(qi_ref, kj_ref, q_ref, k_ref, v_ref, lam_ref, g_ref, o_ref,
                        q2_scr, m_scr, l_scr, acc_scr, *, lam_init, tq):
    n = pl.program_id(2)
    qi = qi_ref[n]
    kj = kj_ref[n]

    @pl.when(kj == 0)
    def _():
        q = q_ref[...]
        lane = lax.broadcasted_iota(I32, (1, LANES), 1)
        q2_scr[0:tq, :] = jnp.where(lane < DIFF_DH, q, jnp.zeros_like(q))
        q2_scr[tq:2 * tq, :] = jnp.where(lane < DIFF_DH, jnp.zeros_like(q), q)
        m_scr[...] = jnp.full(m_scr.shape, NEG_INF, F32)
        l_scr[...] = jnp.zeros(l_scr.shape, F32)
        acc_scr[...] = jnp.zeros(acc_scr.shape, F32)

    def step(diagonal):
        s = _dot_nt(q2_scr[...], k_ref[...])
        if diagonal:
            qc = lax.broadcasted_iota(I32, (tq, tq), 0) // CHUNK
            kc = lax.broadcasted_iota(I32, (tq, tq), 1) // CHUNK
            vis = kc <= qc
            s = jnp.where(jnp.concatenate([vis, vis], axis=0), s, NEG_INF)
        m_prev = m_scr[...]
        m_new = jnp.maximum(m_prev, jnp.max(s, axis=1, keepdims=True))
        alpha = jnp.exp(m_prev - m_new)
        p = jnp.exp(s - jnp.concatenate([m_new] * (tq // LANES), axis=1))
        l_scr[...] = alpha * l_scr[...] + jnp.sum(p, axis=1, keepdims=True)
        acc_scr[...] = alpha * acc_scr[...] + _dot(p.astype(BF16), v_ref[...])
        m_scr[...] = m_new

    @pl.when(kj < qi)
    def _():
        step(False)

    @pl.when(kj == qi)
    def _():
        step(True)
        lam = _diff_lambda(lam_ref, lam_init)
        o = acc_scr[...] / l_scr[...]
        o_ref[...] = _diff_finish(o[0:tq], o[tq:2 * tq], lam, g_ref[...], lam_init).astype(BF16)


def _diff_prompt(dq, dk, dv, lamv, g, B, T, lam_init):
    M = B * T
    tq = min(T, 512)
    nq = T // tq
    pairs = [(i, j) for i in range(nq) for j in range(i + 1)]
    qi = jnp.asarray([p[0] for p in pairs], I32)
    kj = jnp.asarray([p[1] for p in pairs], I32)
    grid_spec = pltpu.PrefetchScalarGridSpec(
        num_scalar_prefetch=2,
        grid=(B, DIFF_HEADS, len(pairs)),
        in_specs=[pl.BlockSpec((tq, LANES), lambda b, h, n, qi, kj: (b * nq + qi[n], h)),
                  pl.BlockSpec((tq, LANES), lambda b, h, n, qi, kj: (b * nq + kj[n], h)),
                  pl.BlockSpec((tq, LANES), lambda b, h, n, qi, kj: (b * nq + kj[n], h)),
                  pl.BlockSpec((4, DIFF_DH), lambda b, h, n, qi, kj: (0, 0)),
                  pl.BlockSpec((1, LANES), lambda b, h, n, qi, kj: (0, 0))],
        out_specs=pl.BlockSpec((tq, LANES), lambda b, h, n, qi, kj: (b * nq + qi[n], h)),
        scratch_shapes=[pltpu.VMEM((2 * tq, LANES), BF16), pltpu.VMEM((2 * tq, LANES), F32),
                        pltpu.VMEM((2 * tq, LANES), F32), pltpu.VMEM((2 * tq, LANES), F32)])
    return pl.pallas_call(
        functools.partial(_diff_prompt_kernel, lam_init=lam_init, tq=tq),
        out_shape=jax.ShapeDtypeStruct((M, 512), BF16),
        grid_spec=grid_spec,
        compiler_params=_cparams(("arbitrary", "arbitrary", "arbitrary")),
        name="diff_prompt",
    )(qi, kj, dq, dk, dv, lamv, g)


def _diff_sample_kernel(q_ref, kn_ref, vn_ref, kp_ref, vp_ref, lam_ref, g_ref, o_ref, *, lam_init, past_len):
    T = q_ref.shape[0]
    lam = _diff_lambda(lam_ref, lam_init)
    lane = lax.broadcasted_iota(I32, (1, LANES), 1)
    qpos = past_len + lax.broadcasted_iota(I32, (T, T), 0)
    kpos = past_len + lax.broadcasted_iota(I32, (T, T), 1)
    vis_new = (kpos // CHUNK) <= (qpos // CHUNK)
    P = past_len
    for h in range(DIFF_HEADS):
        hs = slice(h * LANES, (h + 1) * LANES)
        q = q_ref[:, hs]
        kpt = kp_ref[h].reshape(2 * DIFF_DH, P).astype(BF16)
        vp = vp_ref[pl.ds(h, P, stride=DIFF_HEADS), :].astype(BF16)
        kn = kn_ref[:, hs]
        vn = vn_ref[:, hs]
        outs = []
        for c in range(2):
            qc = jnp.where((lane < DIFF_DH) == (c == 0), q, jnp.zeros_like(q))
            sp = _dot(qc, kpt)
            sn = jnp.where(vis_new, _dot_nt(qc, kn), NEG_INF)
            m = jnp.maximum(jnp.max(sp, axis=1, keepdims=True), jnp.max(sn, axis=1, keepdims=True))
            pp = jnp.exp(sp - m)
            pn = jnp.exp(sn - m)
            l = jnp.sum(pp, axis=1, keepdims=True) + jnp.sum(pn, axis=1, keepdims=True)
            outs.append((_dot(pp.astype(BF16), vp) + _dot(pn.astype(BF16), vn)) / l)
        o_ref[:, hs] = _diff_finish(outs[0], outs[1], lam, g_ref[...], lam_init).astype(BF16)


def _diff_sample(dq, dk, dv, past_kt, past_v, layer, lamv, g, B, T, lam_init):
    M = B * T
    P = past_kt.shape[-1]
    return pl.pallas_call(
        functools.partial(_diff_sample_kernel, lam_init=lam_init, past_len=P),
        out_shape=jax.ShapeDtypeStruct((M, 512), BF16),
        grid=(B,),
        in_specs=[pl.BlockSpec((T, 512), lambda b: (b, 0)),
                  pl.BlockSpec((T, 512), lambda b: (b, 0)),
                  pl.BlockSpec((T, 512), lambda b: (b, 0)),
                  pl.BlockSpec((None, None, DIFF_HEADS, 2, DIFF_DH, P), lambda b: (layer, b, 0, 0, 0, 0)),
                  pl.BlockSpec((None, None, P * DIFF_HEADS, DIFF_DV), lambda b: (layer, b, 0, 0)),
                  pl.BlockSpec((4, DIFF_DH), lambda b: (0, 0)),
                  pl.BlockSpec((1, LANES), lambda b: (0, 0))],
        out_specs=pl.BlockSpec((T, 512), lambda b: (b, 0)),
        compiler_params=_cparams(("arbitrary",)),
        name="diff_sample",
    )(dq, dk, dv, past_kt, past_v, lamv, g)


INT_MIN = -2 ** 31


def _idx_score(iq, iw, ikr, transposed=False):
    lane = lax.broadcasted_iota(I32, (1, IDX_HEADS * IDX_DH), 1)
    sc = None
    for i in range(IDX_HEADS):
        iqi = jnp.where((lane // IDX_DH) == i, iq, jnp.zeros_like(iq))
        lg = _dot(iqi, ikr) if transposed else _dot_nt(iqi, ikr)
        t = iw[:, i:i + 1] * jnp.maximum(lg, 0.0)
        sc = t if sc is None else sc + t
    return jnp.where(sc == 0.0, 0.0, sc)


def _order_key(score):
    bits = pltpu.bitcast(score, I32)
    return jnp.where(bits < 0, bits ^ 0x7FFFFFFF, bits)


def _count(mask):
    return jnp.sum(mask.astype(F32), axis=1, keepdims=True)


def _f32_key(x):
    b = int(np.float32(x).view(np.int32))
    return b ^ 0x7FFFFFFF if b < 0 else b


def _key_value(k):
    return pltpu.bitcast(jnp.where(k < 0, k ^ 0x7FFFFFFF, k), F32)


def _topk_threshold(key_refs, smax, smin, n_sel, n_keys, probe_masked):
    tq = key_refs[0].shape[0]
    n = float(n_sel)

    def count_ge(cand):
        tot = None
        for kr in key_refs:
            c = _count(kr[...] >= cand)
            tot = c if tot is None else tot + c
        return tot

    def update(st, cand):
        lo, cnt_lo, hi, cnt_hi = st
        c = count_ge(cand)
        ge = c >= n
        up = ge & (cand > lo)
        dn = jnp.logical_not(ge) & (cand < hi)
        return (jnp.where(up, cand, lo), jnp.where(up, c, cnt_lo),
                jnp.where(dn, cand, hi), jnp.where(dn, c, cnt_hi))

    def settled(st):
        lo, cnt_lo, hi, _ = st
        return (cnt_lo == n) | ((hi - 1) == lo)

    def unsettled_any(st):
        return jnp.max(jnp.where(settled(st), 0.0, 1.0)) > 0.0

    col = lambda v, dt: jnp.full((tq, 1), v, dt)
    kmax = _order_key(smax)
    hi0 = jnp.where(kmax == 2 ** 31 - 1, kmax, kmax + 1)
    st = (col(INT_MIN, I32), col(float(n_keys), F32), hi0, col(0.0, F32))
    fixed = ((_f32_key(NEG_INF), _f32_key(NEG_INF) + 1) if probe_masked else ()) + (0, 1)
    for c in fixed:
        st = update(st, col(c, I32))
    st = update(st, _order_key(smin))
    log_n = math.log(n)

    def cond(carry):
        it, _, go = carry
        return jnp.logical_and(it < 100, go)

    def body(carry):
        it, st, _ = carry
        lo, cnt_lo, hi, cnt_hi = st
        a = jnp.log(cnt_lo + 0.5)
        frac = (a - log_n) / jnp.maximum(a - jnp.log(cnt_hi + 0.5), 1e-9)
        v_lo = _key_value(lo)
        guess = _order_key(v_lo + frac * (_key_value(hi) - v_lo))
        mid = (lo >> 1) + (hi >> 1) + (lo & hi & 1)
        cand = jnp.minimum(jnp.maximum(jnp.where(lax.rem(it, 3) == 2, mid, guess), lo + 1), hi - 1)
        st = update(st, jnp.where(settled(st), lo, cand))
        return it + 1, st, unsettled_any(st)

    _, st, _ = lax.while_loop(cond, body, (jnp.int32(0), st, unsettled_any(st)))
    lo, cnt_lo, hi, cnt_hi = st
    tied = ((hi - 1) == lo) & (cnt_lo > n)
    return lo, jnp.where(tied, n - cnt_hi, float(n_keys))


def _row_extremes(score, vis):
    if vis is None:
        return jnp.max(score, axis=1, keepdims=True), jnp.min(score, axis=1, keepdims=True)
    return (jnp.max(jnp.where(vis, score, -jnp.inf), axis=1, keepdims=True),
            jnp.min(jnp.where(vis, score, jnp.inf), axis=1, keepdims=True))


def _selection_bias(key_refs, bias_refs, vis_fns, smax, smin, n_sel, n_keys, probe_masked):
    t, need = _topk_threshold(key_refs, smax, smin, n_sel, n_keys, probe_masked)
    tq = key_refs[0].shape[0]
    base = jnp.zeros((tq, 1), F32)
    for kr, br, vis_fn in zip(key_refs, bias_refs, vis_fns):
        W = kr.shape[1]
        bw = min(W, LANES)
        assert W % bw == 0
        tri = (lax.broadcasted_iota(I32, (bw, bw), 0) <= lax.broadcasted_iota(I32, (bw, bw), 1)).astype(BF16)
        ones_tri = jnp.concatenate([jnp.ones((bw, bw), BF16), tri], axis=0)
        earlier = jnp.zeros((tq, bw), F32)
        for j in range(W // bw):
            cs = slice(j * bw, (j + 1) * bw)
            key = kr[:, cs]
            tie = key == t
            tb = jnp.where(tie, 1.0, 0.0)
            rank = base + _dot(jnp.concatenate([earlier, tb], axis=1).astype(BF16), ones_tri)
            sel = (key > t) | (tie & (rank <= need))
            if vis_fn is not None:
                sel = sel & vis_fn(j * bw, bw)
            br[:, cs] = jnp.where(sel, 0.0, NEG_INF)
            earlier = earlier + tb
        base = base + jnp.sum(earlier, axis=1, keepdims=True)


def _dsa_attend(q_ref, bias_refs, k_loads, v_loads, o_ref):
    for h in range(DSA_HEADS):
        hs = slice(h * DSA_DH, (h + 1) * DSA_DH)
        q = q_ref[:, hs]
        ss = [_dot_nt(q, kl(hs)) + br[...] for kl, br in zip(k_loads, bias_refs)]
        m = None
        for s in ss:
            ms = jnp.max(s, axis=1, keepdims=True)
            m = ms if m is None else jnp.maximum(m, ms)
        l = None
        o = None
        for s, vl in zip(ss, v_loads):
            p = jnp.exp(s - m)
            ls = jnp.sum(p, axis=1, keepdims=True)
            os_ = _dot(p.astype(BF16), vl(hs))
            l = ls if l is None else l + ls
            o = os_ if o is None else o + os_
        o_ref[:, hs] = (o / l).astype(BF16)


def _dsa_prompt_kernel(q_ref, iq_ref, iw_ref, k_ref, v_ref, ikr_ref, o_ref, key_scr, bias_scr,
                       *, tq, q_tile0, n_sel, probe_masked):
    tk = k_ref.shape[0]
    qt = q_tile0 + pl.program_id(1)
    score = _idx_score(iq_ref[...], iw_ref[...], ikr_ref[...])
    qpos = qt * tq + lax.broadcasted_iota(I32, (tq, tk), 0)
    kidx = lax.broadcasted_iota(I32, (tq, tk), 1)
    vis = (kidx // CHUNK) <= (qpos // CHUNK)
    key_scr[...] = _order_key(jnp.where(vis, score, NEG_INF))
    smax, smin = _row_extremes(score, vis)
    qchunk = (qt * tq + lax.broadcasted_iota(I32, (tq, 1), 0)) // CHUNK

    def vis_block(off, bw):
        return ((off + lax.broadcasted_iota(I32, (tq, bw), 1)) // CHUNK) <= qchunk

    _selection_bias([key_scr], [bias_scr], [vis_block], smax, smin, n_sel, tk, probe_masked)
    _dsa_attend(q_ref, [bias_scr], [lambda hs: k_ref[:, hs]], [lambda hs: v_ref[:, hs]], o_ref)


def _dsa_prompt(cq, iq, iw, ck, cv, ikr, B, T):
    tq = min(T, 128)
    nq = T // tq
    n_sel = min(DSA_TOPK_MAX, T // 4)
    ng = next(g for g in (16, 8, 4, 2, 1) if nq % g == 0)
    tpg = nq // ng
    outs = []
    for gi in range(ng):
        tk = (gi + 1) * tpg * tq
        nkb = T // tk if T % tk == 0 else None

        def kv_spec(w, tk=tk):
            return pl.BlockSpec((None, tk, w), lambda b, i: (b, 0, 0))

        def q_spec(w, gi=gi):
            return pl.BlockSpec((None, tq, w), lambda b, i: (b, gi * tpg + i, 0))

        out = pl.pallas_call(
            functools.partial(_dsa_prompt_kernel, tq=tq, q_tile0=gi * tpg, n_sel=n_sel,
                              probe_masked=gi * tpg * tq + CHUNK < n_sel),
            out_shape=jax.ShapeDtypeStruct((B, tpg * tq, 512), BF16),
            grid=(B, tpg),
            in_specs=[q_spec(512), q_spec(256), q_spec(LANES), kv_spec(512), kv_spec(512), kv_spec(256)],
            out_specs=pl.BlockSpec((None, tq, 512), lambda b, i: (b, i, 0)),
            scratch_shapes=[pltpu.VMEM((tq, tk), I32), pltpu.VMEM((tq, tk), F32)],
            compiler_params=_cparams(("arbitrary", "arbitrary"), 56 * 1024 * 1024),
            name=f"dsa_prompt_{gi}",
        )(cq.reshape(B, T, 512), iq.reshape(B, T, 256), iw.reshape(B, T, LANES),
          ck.reshape(B, T, 512), cv.reshape(B, T, 512), ikr.reshape(B, T, 256))
        outs.append(out)
    o = outs[0] if ng == 1 else jnp.concatenate(outs, axis=1)
    return o.reshape(B * T, 512)


def _dsa_sample_kernel(q_ref, iq_ref, iw_ref, kn_ref, vn_ref, ikrn_ref, kp_ref, vp_ref, ikp_ref, o_ref,
                       keyp_scr, keyn_scr, biasp_scr, biasn_scr, *, past_len, n_sel):
    T = q_ref.shape[0]
    P = past_len
    iq = iq_ref[...]
    iw = iw_ref[...]
    ikt = ikp_ref[...].astype(BF16)
    ikrp = jnp.concatenate([ikt] * IDX_HEADS, axis=0)
    score_p = _idx_score(iq, iw, ikrp, transposed=True)
    keyp_scr[...] = _order_key(score_p)
    pmax, pmin = _row_extremes(score_p, None)
    qpos = P + lax.broadcasted_iota(I32, (T, T), 0)
    kpos = P + lax.broadcasted_iota(I32, (T, T), 1)
    vis_n = (kpos // CHUNK) <= (qpos // CHUNK)
    score_n = _idx_score(iq, iw, ikrn_ref[...])
    keyn_scr[...] = _order_key(jnp.where(vis_n, score_n, NEG_INF))
    nmax, nmin = _row_extremes(score_n, vis_n)
    _selection_bias([keyp_scr, keyn_scr], [biasp_scr, biasn_scr], [None, lambda off, bw: vis_n[:, off:off + bw]],
                    jnp.maximum(pmax, nmax), jnp.minimum(pmin, nmin),
                    n_sel, P + T, probe_masked=P + min(T, CHUNK) < n_sel)

    def past_head(ref):
        return lambda hs: ref[pl.ds(hs.start // DSA_DH, P, stride=DSA_HEADS), :].astype(BF16)

    _dsa_attend(q_ref, [biasp_scr, biasn_scr],
                [past_head(kp_ref), lambda hs: kn_ref[:, hs]],
                [past_head(vp_ref), lambda hs: vn_ref[:, hs]], o_ref)


def _dsa_sample(cq, iq, iw, ck, cv, ikr, past_k, past_v, past_ikt, layer, B, T):
    M = B * T
    P = past_ikt.shape[-1]
    n_sel = min(DSA_TOPK_MAX, (P + T) // 4)

    def rspec(w):
        return pl.BlockSpec((T, w), lambda b: (b, 0))

    def pspec(r, w):
        return pl.BlockSpec((None, None, r, w), lambda b: (layer, b, 0, 0))

    return pl.pallas_call(
        functools.partial(_dsa_sample_kernel, past_len=P, n_sel=n_sel),
        out_shape=jax.ShapeDtypeStruct((M, 512), BF16),
        grid=(B,),
        in_specs=[rspec(512), rspec(256), rspec(LANES), rspec(512), rspec(512), rspec(256),
                  pspec(P * DSA_HEADS, DSA_DH), pspec(P * DSA_HEADS, DSA_DH), pspec(IDX_DH, P)],
        out_specs=rspec(512),
        scratch_shapes=[pltpu.VMEM((T, P), I32), pltpu.VMEM((T, T), I32),
                        pltpu.VMEM((T, P), F32), pltpu.VMEM((T, T), F32)],
        compiler_params=_cparams(("arbitrary",)),
        name="dsa_sample",
    )(cq, iq, iw, ck, cv, ikr, past_k, past_v, past_ikt)


def _merge_kernel(og_ref, od_ref, oc_ref, g0_ref, g1_ref, g2_ref, x_ref, m_ref,
                  wg_ref, wd_ref, wc_ref, wo_ref, gn_ref, wr_ref, br_ref,
                  x1_ref, h2_ref, ti_ref, tw_ref, rk_ref, cnt_ref, run_scr, *, nb):
    merged = (jax.nn.sigmoid(g0_ref[...]) * _dot(og_ref[...], wg_ref[...])
              + jax.nn.sigmoid(g1_ref[...]) * _dot(od_ref[...], wd_ref[...])
              + jax.nn.sigmoid(g2_ref[...]) * _dot(oc_ref[...], wc_ref[...]))
    mix = _dot(merged.astype(BF16), wo_ref[...])
    m = m_ref[...]
    x1 = x_ref[...] + _gate_rows(mix, m, 2, nb)
    x1_ref[...] = x1
    xn = x1 * lax.rsqrt(jnp.mean(x1 * x1, axis=-1, keepdims=True) + EPS) * gn_ref[...]
    h2 = _modulate(xn, m, 3, 4, nb)
    _rows_to_tiles(h2_ref, h2)
    lg = _dot(h2.astype(BF16), wr_ref[...]) + br_ref[...]
    tm = lg.shape[0]
    lane = lax.broadcasted_iota(I32, (tm, LANES), 1).astype(F32)
    vals, idxs = [], []
    for _ in range(TOP_K):
        mx = jnp.max(lg, axis=1, keepdims=True)
        ix = jnp.min(jnp.where(lg == mx, lane, float(LANES)), axis=1, keepdims=True)
        vals.append(mx)
        idxs.append(ix)
        lg = jnp.where(lane == ix, -jnp.inf, lg)
    es = [jnp.exp(v - vals[0]) for v in vals]
    den = es[0] + es[1] + es[2] + es[3]
    @pl.when(pl.program_id(0) == 0)
    def _():
        run_scr[...] = jnp.zeros(run_scr.shape, F32)

    hot = [lane == ix for ix in idxs]
    tot = (hot[0].astype(F32) + hot[1].astype(F32)) + (hot[2].astype(F32) + hot[3].astype(F32))
    strict = (lax.broadcasted_iota(I32, (tm, tm), 1) < lax.broadcasted_iota(I32, (tm, tm), 0)).astype(BF16)
    before = run_scr[...] + _dot(strict, tot.astype(BF16))
    ti = jnp.zeros((tm, LANES), F32)
    tw = jnp.zeros((tm, LANES), F32)
    rk = jnp.zeros((tm, LANES), F32)
    for r in range(TOP_K):
        ti = jnp.where(lane == float(r), idxs[r], ti)
        tw = jnp.where(lane == float(r), es[r] / den, tw)
        rk = jnp.where(lane == float(r), jnp.sum(jnp.where(hot[r], before, 0.0), axis=1, keepdims=True), rk)
    ti_ref[...] = ti.astype(I32)
    tw_ref[...] = tw
    rk_ref[...] = rk.astype(I32)
    run_new = run_scr[...] + jnp.sum(tot, axis=0, keepdims=True)
    run_scr[...] = run_new
    cnt_ref[...] = run_new.astype(I32)


def _merge(og, od, oc, z, x2, m, wts, B, T):
    M = B * T
    tm, nb = _row_tiling(B, T, 512)
    wg, wd, wc, wo, gn, wr, br = wts

    def rspec(w):
        return pl.BlockSpec((tm, w), lambda i: (i, 0))

    def gspec(k):
        return pl.BlockSpec((tm, D_MODEL), lambda i: (i, Z_GATES // D_MODEL + k))

    def wspec(r, c):
        return pl.BlockSpec((r, c), lambda i: (0, 0))

    return pl.pallas_call(
        functools.partial(_merge_kernel, nb=nb),
        out_shape=[jax.ShapeDtypeStruct((M, D_MODEL), F32), jax.ShapeDtypeStruct((M * ROW_TILE, LANES), F32),
                   jax.ShapeDtypeStruct((M, LANES), I32), jax.ShapeDtypeStruct((M, LANES), F32),
                   jax.ShapeDtypeStruct((M, LANES), I32), jax.ShapeDtypeStruct((1, LANES), I32)],
        grid=(M // tm,),
        in_specs=[rspec(512), rspec(512), rspec(512), gspec(0), gspec(1), gspec(2), rspec(D_MODEL),
                  pl.BlockSpec((nb, 6, D_MODEL), lambda i: ((i * tm) // (T * nb), 0, 0)),
                  wspec(512, D_MODEL), wspec(512, D_MODEL), wspec(512, D_MODEL), wspec(D_MODEL, D_MODEL),
                  wspec(1, D_MODEL), wspec(D_MODEL, LANES), wspec(1, LANES)],
        out_specs=[rspec(D_MODEL), pl.BlockSpec((tm * ROW_TILE, LANES), lambda i: (i, 0)),
                   rspec(LANES), rspec(LANES), rspec(LANES),
                   pl.BlockSpec((1, LANES), lambda i: (0, 0))],
        scratch_shapes=[pltpu.VMEM((1, LANES), F32)],
        compiler_params=_cparams(("arbitrary",)),
        name="merge_router",
    )(og, od, oc, z, z, z, x2, m, wg, wd, wc, wo, gn, wr, br)


DMA_UNROLL = 8


def _w1prep_kernel(w_ref, sel_ref, g_ref, u_ref):
    sel = sel_ref[...]
    for c in range(w_ref.shape[1] // 256):
        r = _dot(w_ref[:, c * 256:(c + 1) * 256].astype(BF16), sel)
        g_ref[:, c * LANES:(c + 1) * LANES] = r[:, :LANES].astype(BF16)
        u_ref[:, c * LANES:(c + 1) * LANES] = r[:, LANES:].astype(BF16)


def _w1prep(w1_all, layer):
    E = w1_all.shape[1]
    tr = 512
    j = jnp.arange(256, dtype=I32)
    src = jnp.where(j < LANES, 2 * j, 2 * (j - LANES) + 1)
    sel = (jnp.arange(256, dtype=I32)[:, None] == src[None, :]).astype(BF16)
    return pl.pallas_call(
        _w1prep_kernel,
        out_shape=[jax.ShapeDtypeStruct((E, D_MODEL, D_FF), BF16)] * 2,
        grid=(E, D_MODEL // tr),
        in_specs=[pl.BlockSpec((None, None, tr, 2 * D_FF), lambda e, r: (layer, e, r, 0)),
                  pl.BlockSpec((256, 256), lambda e, r: (0, 0))],
        out_specs=[pl.BlockSpec((None, tr, D_FF), lambda e, r: (e, r, 0))] * 2,
        compiler_params=_cparams(("arbitrary", "arbitrary")),
        name="w1prep",
    )(w1_all, sel)


def _rows_to_tiles(ref, x):
    n = x.shape[0]
    for c in range(ROW_TILE):
        ref[pl.ds(c, n, stride=ROW_TILE), :] = x[:, c * LANES:(c + 1) * LANES]


def _tiles_to_rows(ref, base, n):
    return jnp.concatenate([ref[pl.ds(base + c, n, stride=ROW_TILE), :] for c in range(ROW_TILE)], axis=1)


def _tile_rows(ref, r):
    return ref.at[pl.ds(pl.multiple_of(r * ROW_TILE, ROW_TILE), ROW_TILE), :]


def _dispatch_kernel(pos_ref, h_ref, xs_in, xs_out, sem, *, tm):
    del xs_in

    tokens = DMA_UNROLL // TOP_K

    def body(j, c):
        for tt in range(tokens):
            src = _tile_rows(h_ref, j * tokens + tt)
            for k in range(TOP_K):
                dst = _tile_rows(xs_out, pos_ref[0, 0, (j * tokens + tt) * TOP_K + k])
                pltpu.make_async_copy(src, dst, sem).start()
        return c

    lax.fori_loop(0, tm // tokens, body, 0)
    for _ in range(TOP_K):
        pltpu.make_async_copy(h_ref, xs_out.at[pl.ds(0, tm * ROW_TILE), :], sem).wait()


def _dispatch(h2t, pos, n_rows, tm):
    M = h2t.shape[0] // ROW_TILE
    nt = M // tm
    return pl.pallas_call(
        functools.partial(_dispatch_kernel, tm=tm),
        out_shape=jax.ShapeDtypeStruct((n_rows * ROW_TILE, LANES), F32),
        grid=(nt,),
        in_specs=[pl.BlockSpec((1, 1, TOP_K * tm), lambda i: (i, 0, 0), memory_space=pltpu.SMEM),
                  pl.BlockSpec((tm * ROW_TILE, LANES), lambda i: (i, 0)),
                  pl.BlockSpec(memory_space=pl.ANY)],
        out_specs=pl.BlockSpec(memory_space=pl.ANY),
        scratch_shapes=[pltpu.SemaphoreType.DMA],
        input_output_aliases={2: 0},
        compiler_params=_cparams(("arbitrary",)),
        name="moe_dispatch",
    )(pos.reshape(nt, 1, TOP_K * tm), h2t, jnp.zeros((n_rows * ROW_TILE, LANES), F32))


def _ffn_kernel(be_ref, nu_ref, x_ref, w1g_ref, b1g_ref, w1u_ref, b1u_ref, w2_ref, b2_ref, o_ref, *, bm):
    @pl.when(pl.program_id(0) < nu_ref[0])
    def _():
        x = _tiles_to_rows(x_ref, 0, bm).astype(BF16)
        g = jnp.minimum(_dot(x, w1g_ref[...]) + b1g_ref[...], SWIGLU_LIMIT)
        u = jnp.clip(_dot(x, w1u_ref[...]) + b1u_ref[...], -SWIGLU_LIMIT, SWIGLU_LIMIT)
        a = g * jax.nn.sigmoid(SWIGLU_ALPHA * g) * (u + 1.0)
        _rows_to_tiles(o_ref, _dot(a.astype(BF16), w2_ref[...]) + b2_ref[...])

    @pl.when(pl.program_id(0) >= nu_ref[0])
    def _():
        o_ref[...] = jnp.zeros(o_ref.shape, F32)


def _ffn(xs, block_e, n_used, w1g, b1g, w1u, b1u, w2_all, b2, layer, bm):
    n_rows = xs.shape[0] // ROW_TILE
    nblk = n_rows // bm

    def wspec(r, c):
        return pl.BlockSpec((None, r, c), lambda i, be, nu: (be[i], 0, 0))

    grid_spec = pltpu.PrefetchScalarGridSpec(
        num_scalar_prefetch=2,
        grid=(nblk,),
        in_specs=[pl.BlockSpec((bm * ROW_TILE, LANES), lambda i, be, nu: (jnp.minimum(i, nu[0] - 1), 0)),
                  wspec(D_MODEL, D_FF), wspec(1, D_FF), wspec(D_MODEL, D_FF), wspec(1, D_FF),
                  pl.BlockSpec((None, None, D_FF, D_MODEL), lambda i, be, nu: (layer, be[i], 0, 0)),
                  wspec(1, D_MODEL)],
        out_specs=pl.BlockSpec((bm * ROW_TILE, LANES), lambda i, be, nu: (i, 0)))
    return pl.pallas_call(
        functools.partial(_ffn_kernel, bm=bm),
        out_shape=jax.ShapeDtypeStruct((n_rows * ROW_TILE, LANES), F32),
        grid_spec=grid_spec,
        compiler_params=_cparams(("arbitrary",)),
        name="moe_ffn",
    )(block_e, n_used, xs, w1g, b1g, w1u, b1u, w2_all, b2)


def _combine_kernel(pos_ref, posn_ref, ys_hbm, tw_ref, x_ref, m_ref, o_ref, ybuf, sem, *, tm, nb):
    i = pl.program_id(0)
    nt = pl.num_programs(0)
    rows = TOP_K * tm * ROW_TILE
    slot = lax.rem(i, 2)

    def request(idx_ref, s):
        base = s * (TOP_K * tm)

        def body(j, c):
            for u in range(DMA_UNROLL):
                n = j * DMA_UNROLL + u
                pltpu.make_async_copy(_tile_rows(ys_hbm, idx_ref[0, 0, n]), _tile_rows(ybuf, base + n),
                                      sem.at[s]).start()
            return c

        lax.fori_loop(0, TOP_K * tm // DMA_UNROLL, body, 0)

    @pl.when(i == 0)
    def _():
        request(pos_ref, 0)

    @pl.when(i + 1 < nt)
    def _():
        request(posn_ref, 1 - slot)

    off = pl.multiple_of(slot * rows, rows)
    pltpu.make_async_copy(ys_hbm.at[pl.ds(0, rows), :], ybuf.at[pl.ds(off, rows), :], sem.at[slot]).wait()
    tw = tw_ref[...]
    y = None
    for k in range(TOP_K):
        yk = tw[:, k:k + 1] * _tiles_to_rows(ybuf, off + k * tm * ROW_TILE, tm)
        y = yk if y is None else y + yk
    o_ref[...] = x_ref[...] + _gate_rows(y, m_ref[...], 5, nb)


def _combine(pos_t, ys, top_w, x1, m, B, T, tm, nb):
    M = B * T
    nt = M // tm
    pos3 = pos_t.reshape(nt, 1, TOP_K * tm)
    return pl.pallas_call(
        functools.partial(_combine_kernel, tm=tm, nb=nb),
        out_shape=jax.ShapeDtypeStruct((M, D_MODEL), F32),
        grid=(nt,),
        in_specs=[pl.BlockSpec((1, 1, TOP_K * tm), lambda i: (i, 0, 0), memory_space=pltpu.SMEM),
                  pl.BlockSpec((1, 1, TOP_K * tm), lambda i: (jnp.minimum(i + 1, nt - 1), 0, 0),
                               memory_space=pltpu.SMEM),
                  pl.BlockSpec(memory_space=pl.ANY),
                  pl.BlockSpec((tm, LANES), lambda i: (i, 0)),
                  pl.BlockSpec((tm, D_MODEL), lambda i: (i, 0)),
                  pl.BlockSpec((nb, 6, D_MODEL), lambda i: ((i * tm) // (T * nb), 0, 0))],
        out_specs=pl.BlockSpec((tm, D_MODEL), lambda i: (i, 0)),
        scratch_shapes=[pltpu.VMEM((2 * TOP_K * tm * ROW_TILE, LANES), F32), pltpu.SemaphoreType.DMA((2,))],
        compiler_params=_cparams(("arbitrary",)),
        name="moe_combine",
    )(pos3, pos3, ys, top_w, x1, m)


def _route(top_i, rank, counts, bm):
    N = top_i.shape[0]
    NK = N * TOP_K
    padded = (counts + bm - 1) // bm * bm
    pad_end = jnp.cumsum(padded)
    pad_start = pad_end - padded
    onehot = top_i[:, :, None] == jnp.arange(N_EXPERTS, dtype=I32)[None, None, :]
    pos = rank + jnp.sum(jnp.where(onehot, pad_start[None, None, :], 0), axis=2)
    n_rows = (-(-NK // bm)) * bm + N_EXPERTS * bm
    nblk = n_rows // bm
    starts = jnp.arange(nblk, dtype=I32) * bm
    block_e = jnp.minimum(jnp.sum((pad_end[None, :] <= starts[:, None]).astype(I32), axis=1), N_EXPERTS - 1)
    n_used = (pad_end[-1:] // bm).astype(I32)
    return pos.reshape(NK).astype(I32), n_rows, block_e.astype(I32), n_used


def _moe(h2, top_i, top_w, rank, counts, x1, m, ew, B, T):
    M = B * T
    bm = 256 if M * TOP_K >= 256 * N_EXPERTS * 4 else 128
    pos, n_rows, block_e, n_used = _route(top_i[:, :TOP_K], rank[:, :TOP_K], counts[0, :N_EXPERTS], bm)
    tm, nb = _row_tiling(B, T, 256)
    xs = _dispatch(h2, pos, n_rows, tm)
    ys = _ffn(xs, block_e, n_used, *ew, bm)
    pos_t = pos.reshape(M // tm, tm, TOP_K).transpose(0, 2, 1)
    return _combine(pos_t, ys, top_w, x1, m, B, T, tm, nb)


def _prep_layer(P, l, w2_all):
    w_t = P['w_in'][l].T

    def rows(name):
        o, s = _SRC[name]
        return w_t[o:o + s]

    zeros = lambda n: jnp.zeros((n, D_MODEL), F32)
    w_r = jnp.concatenate([
        rows('gla_q'), rows('gla_k'), rows('gla_v'), rows('gla_r'),
        rows('diff_q'), rows('diff_k'), rows('diff_v'),
        rows('dsa_q'), rows('dsa_k'), rows('dsa_v'),
        rows('idx_q'), rows('idx_k'), rows('idx_w'), zeros(60), rows('gla_a'), zeros(112),
        rows('gates')], axis=0).astype(BF16)
    assert w_r.shape[0] == Z_WIDTH
    W = GLA_HEADS * GLA_DK
    wa = jnp.zeros((LANES, W), F32).at[:GLA_GATE_RANK].set(P['w_gla_a2'][l]).astype(BF16)
    w1g, w1u = _w1prep(P['w_mlp1'], l)
    b1 = P['b_mlp1'][l]
    return dict(
        w_in=w_r, wa=wa, ba=P['b_gla_a2'][l].reshape(1, W),
        g_gla=jnp.tile(P['g_gla_out'][l], GLA_HEADS).reshape(1, 512),
        gains=(jnp.tile(P['g_diff_q'][l], 8).reshape(1, 512), jnp.tile(P['g_diff_k'][l], 8).reshape(1, 512),
               jnp.tile(P['g_dsa_q'][l], 4).reshape(1, 512), jnp.tile(P['g_dsa_k'][l], 4).reshape(1, 512),
               jnp.tile(P['g_idx_k'][l], 2).reshape(1, LANES)),
        lamv=jnp.stack([P['lambda_q1'][l], P['lambda_k1'][l], P['lambda_q2'][l], P['lambda_k2'][l]]),
        g_diff=P['g_diff_out'][l].reshape(1, LANES),
        merge=(P['w_branch_gla'][l].astype(BF16), P['w_branch_diff'][l].astype(BF16),
               P['w_branch_dsa'][l].astype(BF16), P['w_out'][l].astype(BF16),
               P['g_norm2'][l].reshape(1, D_MODEL),
               jnp.zeros((D_MODEL, LANES), F32).at[:, :N_EXPERTS].set(P['w_router'][l]).astype(BF16),
               jnp.full((1, LANES), NEG_INF, F32).at[0, :N_EXPERTS].set(P['b_router'][l])),
        experts=(w1g, b1[:, None, 0::2], w1u, b1[:, None, 1::2], w2_all, P['b_mlp2'][l][:, None, :], l),
    )


def _trunk(x, c, past, P, prepped):
    B, T, _ = x.shape
    M = B * T
    x2 = x.reshape(M, D_MODEL)
    past_len = 0 if past is None else past[0].shape[2]
    pos = past_len + jnp.arange(T, dtype=I32)
    tabs64 = _rope_tables(pos, 64)
    tabs128 = _rope_tables(pos, 128)
    W = GLA_HEADS * GLA_DK
    if past is not None:
        L, _, PL = past[0].shape[:3]
        pdk = past[0].transpose(0, 1, 3, 4, 5, 2)
        pdv = past[1].reshape(L, B, PL * DIFF_HEADS, DIFF_DV)
        pck = past[2].reshape(L, B, PL * DSA_HEADS, DSA_DH)
        pcv = past[3].reshape(L, B, PL * DSA_HEADS, DSA_DH)
        pik = past[4].transpose(0, 1, 3, 2)
        s0_all = past[5].transpose(0, 1, 4, 2, 3).reshape(past[5].shape[0], B, GLA_DV, W)
    per_layer = []
    for l, pp in enumerate(prepped):
        lam_init = 0.8 - 0.6 * math.exp(-0.3 * l)
        m = _ada(c, P['w_ada'], P['b_ada'], l)
        z = _inproj(x2, m, P['g_norm1'][l], pp['w_in'], B, T)
        (dq, dk32, dkb, dvb, cq, ck32, ckb, cvb, iq, ik32, ikr, iw, dv32, cv32) = _post(
            z, tabs64, tabs128, pp['gains'], B, T)
        s0t = jnp.zeros((B, GLA_DV, W), F32) if past is None else s0_all[l]
        o_gla, st = _gla(z, pp['wa'], pp['ba'], pp['g_gla'], s0t, B, T)
        if past is None:
            o_diff = _diff_prompt(dq, dkb, dvb, pp['lamv'], pp['g_diff'], B, T, lam_init)
            o_dsa = _dsa_prompt(cq, iq, iw, ckb, cvb, ikr, B, T)
        else:
            o_diff = _diff_sample(dq, dkb, dvb, pdk, pdv, l, pp['lamv'], pp['g_diff'], B, T, lam_init)
            o_dsa = _dsa_sample(cq, iq, iw, ckb, cvb, ikr, pck, pcv, pik, l, B, T)
        x1, h2, top_i, top_w, rank, counts = _merge(o_gla, o_diff, o_dsa, z, x2, m, pp['merge'], B, T)
        x2 = _moe(h2, top_i, top_w, rank, counts, x1, m, pp['experts'], B, T)
        s_gla = st.reshape(B, GLA_DV, GLA_HEADS, GLA_DK).transpose(0, 2, 3, 1)
        per_layer.append((dk32.reshape(B, T, DIFF_HEADS, 2, DIFF_DH),
                          dv32.reshape(B, T, DIFF_HEADS, DIFF_DV),
                          ck32.reshape(B, T, DSA_HEADS, DSA_DH),
                          cv32.reshape(B, T, DSA_HEADS, DSA_DH),
                          ik32.reshape(B, T, IDX_DH),
                          s_gla))
    stacked = tuple(jnp.stack([st[i] for st in per_layer]) for i in range(6))
    return x2.reshape(B, T, D_MODEL), stacked


def kernel(x_prompt, x_sample, cache_diff_k, cache_diff_v, cache_dsa_k, cache_dsa_v, cache_dsa_idx_k,
           state_gla, c_prompt, c_sample, w_ada, b_ada, g_norm1, g_norm2, w_in, w_gla_a2, b_gla_a2,
           g_gla_out, g_diff_q, g_diff_k, lambda_q1, lambda_k1, lambda_q2, lambda_k2, g_diff_out,
           g_dsa_q, g_dsa_k, g_idx_k, w_branch_gla, w_branch_diff, w_branch_dsa, w_out, w_router,
           b_router, w_mlp1, b_mlp1, w_mlp2, b_mlp2):
    P = dict(w_ada=w_ada, b_ada=b_ada, g_norm1=g_norm1, g_norm2=g_norm2, w_in=w_in, w_gla_a2=w_gla_a2,
             b_gla_a2=b_gla_a2, g_gla_out=g_gla_out, g_diff_q=g_diff_q, g_diff_k=g_diff_k,
             lambda_q1=lambda_q1, lambda_k1=lambda_k1, lambda_q2=lambda_q2, lambda_k2=lambda_k2,
             g_diff_out=g_diff_out, g_dsa_q=g_dsa_q, g_dsa_k=g_dsa_k, g_idx_k=g_idx_k,
             w_branch_gla=w_branch_gla, w_branch_diff=w_branch_diff, w_branch_dsa=w_branch_dsa,
             w_out=w_out, w_router=w_router, b_router=b_router, w_mlp1=w_mlp1, b_mlp1=b_mlp1,
             w_mlp2=w_mlp2, b_mlp2=b_mlp2)
    depth = w_in.shape[0]
    w2_all = w_mlp2.astype(BF16)
    prepped = [_prep_layer(P, l, w2_all) for l in range(depth)]
    y_prompt, new_p = _trunk(x_prompt, c_prompt, None, P, prepped)
    y_sample, new_s = _trunk(
        x_sample, c_sample,
        (cache_diff_k, cache_diff_v, cache_dsa_k, cache_dsa_v, cache_dsa_idx_k, state_gla), P, prepped)
    return (y_prompt, y_sample) + new_p + new_s
```

```python
import functools
import math

import numpy as np
import jax
import jax.numpy as jnp
from jax import lax
from jax.experimental import pallas as pl
from jax.experimental.pallas import tpu as pltpu

F32 = jnp.float32
BF16 = jnp.bfloat16
I32 = jnp.int32

D_MODEL = 1024
CHUNK = 64
ROPE_THETA = 500000.0
ROPE_FRACTION = 4
EPS = 1e-6
NEG_INF = -1e30

GLA_HEADS = 4
GLA_DK = 64
GLA_DV = 128
GLA_GATE_RANK = 16
GLA_GATE_TAU = 16.0
GLA_SUB = 16
DIFF_HEADS = 4
DIFF_DH = 64
DIFF_DV = 128
DSA_HEADS = 4
DSA_DH = 128
IDX_HEADS = 4
IDX_DH = 64
DSA_TOPK_MAX = 256
N_EXPERTS = 32
TOP_K = 4
D_FF = 1024
SWIGLU_LIMIT = 7.0
SWIGLU_ALPHA = 1.702

LANES = 128
ROW_TILE = 8
VMEM_LIMIT = 48 * 1024 * 1024

Z_GQK, Z_GV, Z_GR = 0, 512, 1024
Z_DQ, Z_DK, Z_DV = 1536, 2048, 2560
Z_CQ, Z_CK, Z_CV = 3072, 3584, 4096
Z_MISC = 4608
Z_GATES = 5120
Z_WIDTH = 8192
MISC_IK = 256
MISC_GA = 384

_SRC = {}
_off = 0
for _name, _size in (
        ('gla_q', 256), ('gla_k', 256), ('gla_v', 512), ('gla_a', 16), ('gla_r', 512),
        ('diff_q', 512), ('diff_k', 512), ('diff_v', 512),
        ('dsa_q', 512), ('dsa_k', 512), ('dsa_v', 512),
        ('idx_q', 256), ('idx_k', 64), ('idx_w', 4), ('gates', 3072)):
    _SRC[_name] = (_off, _size)
    _off += _size


def _cparams(sem, vmem=VMEM_LIMIT):
    return pltpu.CompilerParams(dimension_semantics=sem, vmem_limit_bytes=vmem)


def _dot(a, b):
    return jnp.dot(a, b, preferred_element_type=F32)


def _dot_nt(a, b):
    return lax.dot_general(a, b, (((1,), (1,)), ((), ())), preferred_element_type=F32)


def _dot_tn(a, b):
    return lax.dot_general(a, b, (((0,), (0,)), ((), ())), preferred_element_type=F32)


def _silu(x):
    return x * jax.nn.sigmoid(x)


def _row_tiling(B, T, target):
    if T >= target:
        assert T % target == 0
        return target, 1
    nb = 1
    for cand in range(1, B + 1):
        if B % cand == 0 and cand * T <= target:
            nb = cand
    return nb * T, nb


def _modulate(xn, m, shift_i, scale_i, nb):
    tm = xn.shape[0]
    if nb == 1:
        return xn * (1.0 + m[0, scale_i:scale_i + 1, :]) + m[0, shift_i:shift_i + 1, :]
    x3 = xn.reshape(nb, tm // nb, D_MODEL)
    h = x3 * (1.0 + m[:, scale_i:scale_i + 1, :]) + m[:, shift_i:shift_i + 1, :]
    return h.reshape(tm, D_MODEL)


def _gate_rows(y, m, gate_i, nb):
    tm = y.shape[0]
    if nb == 1:
        return y * m[0, gate_i:gate_i + 1, :]
    return (y.reshape(nb, tm // nb, D_MODEL) * m[:, gate_i:gate_i + 1, :]).reshape(tm, D_MODEL)


def _ada_kernel(c_ref, w_ref, b_ref, o_ref):
    s = _silu(c_ref[...])
    o_ref[...] = _dot(s.astype(BF16), w_ref[...].astype(BF16)) + b_ref[...]


def _ada(c, w, b, layer):
    B = c.shape[0]
    out = pl.pallas_call(
        _ada_kernel,
        out_shape=jax.ShapeDtypeStruct((B, 6 * D_MODEL), F32),
        grid=(6,),
        in_specs=[pl.BlockSpec((B, D_MODEL), lambda j: (0, 0)),
                  pl.BlockSpec((None, D_MODEL, D_MODEL), lambda j: (layer, 0, j)),
                  pl.BlockSpec((None, 1, D_MODEL), lambda j: (layer, 0, j))],
        out_specs=pl.BlockSpec((B, D_MODEL), lambda j: (0, j)),
        compiler_params=_cparams(("arbitrary",)),
        name="ada",
    )(c, w, b.reshape(b.shape[0], 1, -1))
    return out.reshape(B, 6, D_MODEL)


def _inproj_kernel(x_ref, m_ref, g_ref, w_ref, o_ref, h_scr, *, nb):
    @pl.when(pl.program_id(1) == 0)
    def _():
        x = x_ref[...]
        xn = x * lax.rsqrt(jnp.mean(x * x, axis=-1, keepdims=True) + EPS) * g_ref[...]
        h_scr[...] = _modulate(xn, m_ref[...], 0, 1, nb).astype(BF16)

    o_ref[...] = _dot_nt(h_scr[...], w_ref[...])


def _inproj(x2, m, g, wt, B, T):
    M = B * T
    tm, nb = _row_tiling(B, T, 1024)
    tn = 1024
    return pl.pallas_call(
        functools.partial(_inproj_kernel, nb=nb),
        out_shape=jax.ShapeDtypeStruct((M, Z_WIDTH), F32),
        grid=(M // tm, Z_WIDTH // tn),
        in_specs=[pl.BlockSpec((tm, D_MODEL), lambda i, j: (i, 0)),
                  pl.BlockSpec((nb, 6, D_MODEL), lambda i, j: ((i * tm) // (T * nb), 0, 0)),
                  pl.BlockSpec((1, D_MODEL), lambda i, j: (0, 0)),
                  pl.BlockSpec((tn, D_MODEL), lambda i, j: (j, 0))],
        out_specs=pl.BlockSpec((tm, tn), lambda i, j: (i, j)),
        scratch_shapes=[pltpu.VMEM((tm, D_MODEL), BF16)],
        compiler_params=_cparams(("arbitrary", "arbitrary")),
        name="inproj",
    )(x2, m, g.reshape(1, -1), wt)


def _rope_tables(pos, d):
    rot = d // ROPE_FRACTION
    half = rot // 2
    T = pos.shape[0]
    inv_freq = ROPE_THETA ** (-jnp.arange(half, dtype=F32) / half)
    ang = pos.astype(F32)[:, None] * inv_freq[None, :]
    cos, sin = jnp.cos(ang), jnp.sin(ang)
    c = jnp.concatenate([cos, cos, jnp.ones((T, d - rot), F32)], axis=1)
    a = jnp.concatenate([-sin, jnp.zeros((T, d - half), F32)], axis=1)
    b = jnp.concatenate([jnp.zeros((T, half), F32), sin, jnp.zeros((T, d - rot), F32)], axis=1)
    reps = LANES // d
    return tuple(jnp.tile(t, (1, reps)) for t in (c, a, b))


def _rope128(xs, tabs, half):
    c, a, b = tabs
    return xs * c + pltpu.roll(xs, LANES - half, 1) * a + pltpu.roll(xs, half, 1) * b


def _norm_rope(x, g, tabs, d, norm=True):
    tm, W = x.shape
    half = d // ROPE_FRACTION // 2
    lo = lax.broadcasted_iota(I32, (tm, LANES), 1) < 64
    outs = []
    for s in range(W // LANES):
        xs = x[:, s * LANES:(s + 1) * LANES]
        if norm:
            sq = xs * xs
            if d == LANES:
                r = lax.rsqrt(jnp.sum(sq, axis=1, keepdims=True) * (1.0 / d) + EPS)
            else:
                s_lo = jnp.sum(jnp.where(lo, sq, 0.0), axis=1, keepdims=True)
                s_hi = jnp.sum(jnp.where(lo, 0.0, sq), axis=1, keepdims=True)
                r = jnp.where(lo, lax.rsqrt(s_lo * (1.0 / d) + EPS), lax.rsqrt(s_hi * (1.0 / d) + EPS))
            xs = xs * r * g[:, s * LANES:(s + 1) * LANES]
        outs.append(_rope128(xs, tabs, half))
    return outs[0] if len(outs) == 1 else jnp.concatenate(outs, axis=1)


def _store_head_rows(ref, x):
    tm = x.shape[0]
    for h in range(4):
        ref[pl.ds(h, tm, stride=4), :] = x[:, h * LANES:(h + 1) * LANES]


def _post_kernel(dq_ref, dk_ref, dv_ref, cq_ref, ck_ref, cv_ref, mi_ref,
                 c64_ref, a64_ref, b64_ref, c128_ref, a128_ref, b128_ref,
                 gdq_ref, gdk_ref, gcq_ref, gck_ref, gik_ref,
                 dq_o, dk32_o, dkb_o, dvb_o, cq_o, ck32_o, ckb_o, cvb_o, iq_o, ik32_o, ikr_o, iw_o,
                 dv32_o, cv32_o):
    t64 = (c64_ref[...], a64_ref[...], b64_ref[...])
    t128 = (c128_ref[...], a128_ref[...], b128_ref[...])
    dq = _norm_rope(dq_ref[...], gdq_ref[...], t64, 64)
    dq_o[...] = (dq * (DIFF_DH ** -0.5)).astype(BF16)
    dk = _norm_rope(dk_ref[...], gdk_ref[...], t64, 64)
    dk32_o[...] = dk
    dkb_o[...] = dk.astype(BF16)
    dv = dv_ref[...]
    _store_head_rows(dv32_o, dv)
    dvb_o[...] = dv.astype(BF16)
    cq = _norm_rope(cq_ref[...], gcq_ref[...], t128, 128)
    cq_o[...] = (cq * (DSA_DH ** -0.5)).astype(BF16)
    ck = _norm_rope(ck_ref[...], gck_ref[...], t128, 128)
    _store_head_rows(ck32_o, ck)
    ckb_o[...] = ck.astype(BF16)
    cv = cv_ref[...]
    _store_head_rows(cv32_o, cv)
    cvb_o[...] = cv.astype(BF16)
    mi = mi_ref[...]
    iq = _norm_rope(mi[:, 0:256], None, t64, 64, norm=False)
    iq_o[...] = (iq * (IDX_DH ** -0.5)).astype(BF16)
    mk = mi[:, MISC_IK:MISC_IK + LANES]
    tm = mk.shape[0]
    lane = lax.broadcasted_iota(I32, (tm, LANES), 1)
    lo = lane < 64
    ssq = jnp.sum(jnp.where(lo, mk * mk, 0.0), axis=1, keepdims=True)
    ikn = mk * lax.rsqrt(ssq * (1.0 / IDX_DH) + EPS) * gik_ref[...]
    ik = _rope128(ikn, t64, IDX_DH // ROPE_FRACTION // 2)
    ik32_o[...] = ik[:, 0:IDX_DH]
    ik2 = jnp.where(lo, ik, pltpu.roll(ik, 64, 1))
    ikr_o[...] = jnp.concatenate([ik2, ik2], axis=1).astype(BF16)
    iw_o[...] = jnp.where(lane < IDX_HEADS, pltpu.roll(mk, 64, 1) * (IDX_HEADS ** -0.5), 0.0)


def _post(z, tabs64, tabs128, gains, B, T):
    M = B * T
    tm = min(T, 512)
    assert T % tm == 0
    npos = T // tm

    def zspec(off):
        return pl.BlockSpec((tm, 512), lambda i: (i, off // 512))

    tab_spec = pl.BlockSpec((tm, LANES), lambda i: (i % npos, 0))

    def gspec(w):
        return pl.BlockSpec((1, w), lambda i: (0, 0))

    HR = "head rows"
    outs = [(512, BF16), (512, F32), (512, BF16), (512, BF16),
            (512, BF16), (HR, F32), (512, BF16), (512, BF16),
            (256, BF16), (IDX_DH, F32), (256, BF16), (LANES, F32), (HR, F32), (HR, F32)]

    def oshape(w, dt):
        return jax.ShapeDtypeStruct((M * 4, LANES) if w is HR else (M, w), dt)

    def ospec(w):
        return pl.BlockSpec((tm * 4, LANES) if w is HR else (tm, w), lambda i: (i, 0))

    return pl.pallas_call(
        _post_kernel,
        out_shape=[oshape(w, dt) for w, dt in outs],
        grid=(M // tm,),
        in_specs=[zspec(Z_DQ), zspec(Z_DK), zspec(Z_DV), zspec(Z_CQ), zspec(Z_CK), zspec(Z_CV), zspec(Z_MISC)]
                 + [tab_spec] * 6 + [gspec(512)] * 4 + [gspec(LANES)],
        out_specs=[ospec(w) for w, _ in outs],
        compiler_params=_cparams(("arbitrary",)),
        name="post",
    )(z, z, z, z, z, z, z, *tabs64, *tabs128, *gains)


def _split3(x):
    hi = x.astype(BF16)
    r = x - hi.astype(F32)
    mid = r.astype(BF16)
    lo = (r - mid.astype(F32)).astype(BF16)
    return hi, mid, lo


def _gla_kernel(qk_ref, v_ref, r_ref, ga_ref, wa_ref, ba_ref, g_ref, s0_ref, o_ref, st_ref, st_scr, *, nct):
    ti = pl.program_id(1)

    @pl.when(ti == 0)
    def _():
        st_scr[...] = s0_ref[...]

    C = CHUNK
    W = GLA_HEADS * GLA_DK
    row = lax.broadcasted_iota(I32, (C, C), 0)
    col = lax.broadcasted_iota(I32, (C, C), 1)
    tri = col <= row
    tri_bf = tri.astype(BF16)
    lane = lax.broadcasted_iota(I32, (1, W), 1)
    hmask = [(lane // GLA_DK) == h for h in range(GLA_HEADS)]
    rowi = lax.broadcasted_iota(I32, (C, W), 0)
    wa = wa_ref[...]
    ba = ba_ref[...]
    g = g_ref[...]

    def chunk(c, carry):
        rows = pl.ds(pl.multiple_of(c * C, C), C)
        qk = qk_ref[rows, :]
        q = qk[:, :W] * (GLA_DK ** -0.5)
        k = qk[:, W:]
        v = v_ref[rows, :].astype(BF16)
        pre = _dot(ga_ref[rows, :].astype(BF16), wa) + ba
        la = (jnp.minimum(pre, 0.0) - jnp.log(1.0 + jnp.exp(-jnp.abs(pre)))) * (1.0 / GLA_GATE_TAU)
        hi, mid, lo = _split3(la)
        b = _dot(tri_bf, hi) + _dot(tri_bf, mid) + _dot(tri_bf, lo)
        st = st_scr[...]
        st_bf = st.astype(BF16)
        qe = q * jnp.exp(b)
        b_end = b[C - 1:C, :]
        kend = k * jnp.exp(b_end - b)
        att_parts = [[] for _ in range(GLA_HEADS)]
        for s in range(C // GLA_SUB):
            r0 = s * GLA_SUB
            br = b[r0:r0 + 1, :]
            qs = q[r0:r0 + GLA_SUB, :] * jnp.exp(b[r0:r0 + GLA_SUB, :] - br)
            ks = k * jnp.exp(br - b)
            if r0 + GLA_SUB < C:
                ks = jnp.where(rowi < r0 + GLA_SUB, ks, 0.0)
            ks = ks.astype(BF16)
            for h in range(GLA_HEADS):
                att_parts[h].append(_dot_nt(jnp.where(hmask[h], qs, 0.0).astype(BF16), ks))
        upd = None
        for h in range(GLA_HEADS):
            hs = slice(h * GLA_DV, (h + 1) * GLA_DV)
            att = jnp.where(tri, jnp.concatenate(att_parts[h], axis=0), 0.0)
            o = _dot(att.astype(BF16), v[:, hs]) + _dot_nt(jnp.where(hmask[h], qe, 0.0).astype(BF16), st_bf)
            u = _dot_tn(v[:, hs], jnp.where(hmask[h], kend, 0.0).astype(BF16))
            upd = u if upd is None else upd + u
            y = o * lax.rsqrt(jnp.mean(o * o, axis=-1, keepdims=True) + EPS) * g[:, hs]
            o_ref[rows, hs] = (y * _silu(r_ref[rows, hs])).astype(BF16)
        st_scr[...] = st * jnp.exp(b_end) + upd
        return carry

    lax.fori_loop(0, nct, chunk, 0)

    @pl.when(ti == pl.num_programs(1) - 1)
    def _():
        st_ref[...] = st_scr[...]


def _gla(z, wa, ba, g, s0t, B, T):
    M = B * T
    tt = min(T, 512)
    assert T % tt == 0 and tt % CHUNK == 0
    nt = T // tt
    W = GLA_HEADS * GLA_DK
    return pl.pallas_call(
        functools.partial(_gla_kernel, nct=tt // CHUNK),
        out_shape=[jax.ShapeDtypeStruct((M, 512), BF16),
                   jax.ShapeDtypeStruct((B, GLA_DV, W), F32)],
        grid=(B, nt),
        in_specs=[pl.BlockSpec((tt, 512), lambda b, t: (b * nt + t, Z_GQK // 512)),
                  pl.BlockSpec((tt, 512), lambda b, t: (b * nt + t, Z_GV // 512)),
                  pl.BlockSpec((tt, 512), lambda b, t: (b * nt + t, Z_GR // 512)),
                  pl.BlockSpec((tt, LANES), lambda b, t: (b * nt + t, (Z_MISC + MISC_GA) // LANES)),
                  pl.BlockSpec((LANES, W), lambda b, t: (0, 0)),
                  pl.BlockSpec((1, W), lambda b, t: (0, 0)),
                  pl.BlockSpec((1, 512), lambda b, t: (0, 0)),
                  pl.BlockSpec((None, GLA_DV, W), lambda b, t: (b, 0, 0))],
        out_specs=[pl.BlockSpec((tt, 512), lambda b, t: (b * nt + t, 0)),
                   pl.BlockSpec((None, GLA_DV, W), lambda b, t: (b, 0, 0))],
        scratch_shapes=[pltpu.VMEM((GLA_DV, W), F32)],
        compiler_params=_cparams(("arbitrary", "arbitrary")),
        name="gla",
    )(z, z, z, z, wa, ba, g, s0t)


def _diff_lambda(lam_ref, lam_init):
    lv = lam_ref[...]
    a = jnp.sum(lv[0:1, :] * lv[1:2, :], axis=1, keepdims=True)
    b = jnp.sum(lv[2:3, :] * lv[3:4, :], axis=1, keepdims=True)
    return jnp.exp(a) - jnp.exp(b) + lam_init


def _diff_finish(o0, o1, lam, g, lam_init):
    o = o0 - lam * o1
    return o * lax.rsqrt(jnp.mean(o * o, axis=-1, keepdims=True) + EPS) * g * (1.0 - lam_init)


def _diff_prompt_kernel(qi_ref, kj_ref, q_ref, k_ref, v_ref, lam_ref, g_ref, o_ref,
                        q2_scr, m_scr, l_scr, acc_scr, *, lam_init, tq):
    n = pl.program_id(2)
    qi = qi_ref[n]
    kj = kj_ref[n]

    @pl.when(kj == 0)
    def _():
        q = q_ref[...]
        lane = lax.broadcasted_iota(I32, (1, LANES), 1)
        q2_scr[0:tq, :] = jnp.where(lane < DIFF_DH, q, jnp.zeros_like(q))
        q2_scr[tq:2 * tq, :] = jnp.where(lane < DIFF_DH, jnp.zeros_like(q), q)
        m_scr[...] = jnp.full(m_scr.shape, NEG_INF, F32)
        l_scr[...] = jnp.zeros(l_scr.shape, F32)
        acc_scr[...] = jnp.zeros(acc_scr.shape, F32)

    def step(diagonal):
        s = _dot_nt(q2_scr[...], k_ref[...])
        if diagonal:
            qc = lax.broadcasted_iota(I32, (tq, tq), 0) // CHUNK
            kc = lax.broadcasted_iota(I32, (tq, tq), 1) // CHUNK
            vis = kc <= qc
            s = jnp.where(jnp.concatenate([vis, vis], axis=0), s, NEG_INF)
        m_prev = m_scr[...]
        m_new = jnp.maximum(m_prev, jnp.max(s, axis=1, keepdims=True))
        alpha = jnp.exp(m_prev - m_new)
        p = jnp.exp((s - jnp.concatenate([m_new] * (tq // LANES), axis=1)).astype(BF16))
        l_scr[...] = alpha * l_scr[...] + jnp.sum(p.astype(F32), axis=1, keepdims=True)
        acc_scr[...] = alpha * acc_scr[...] + _dot(p, v_ref[...])
        m_scr[...] = m_new

    @pl.when(kj < qi)
    def _():
        step(False)

    @pl.when(kj == qi)
    def _():
        step(True)
        lam = _diff_lambda(lam_ref, lam_init)
        o = acc_scr[...] / l_scr[...]
        o_ref[...] = _diff_finish(o[0:tq], o[tq:2 * tq], lam, g_ref[...], lam_init).astype(BF16)


def _diff_prompt(dq, dk, dv, lamv, g, B, T, lam_init):
    M = B * T
    tq = min(T, 512)
    nq = T // tq
    pairs = [(i, j) for i in range(nq) for j in range(i + 1)]
    qi = jnp.asarray([p[0] for p in pairs], I32)
    kj = jnp.asarray([p[1] for p in pairs], I32)
    grid_spec = pltpu.PrefetchScalarGridSpec(
        num_scalar_prefetch=2,
        grid=(B, DIFF_HEADS, len(pairs)),
        in_specs=[pl.BlockSpec((tq, LANES), lambda b, h, n, qi, kj: (b * nq + qi[n], h)),
                  pl.BlockSpec((tq, LANES), lambda b, h, n, qi, kj: (b * nq + kj[n], h)),
                  pl.BlockSpec((tq, LANES), lambda b, h, n, qi, kj: (b * nq + kj[n], h)),
                  pl.BlockSpec((4, DIFF_DH), lambda b, h, n, qi, kj: (0, 0)),
                  pl.BlockSpec((1, LANES), lambda b, h, n, qi, kj: (0, 0))],
        out_specs=pl.BlockSpec((tq, LANES), lambda b, h, n, qi, kj: (b * nq + qi[n], h)),
        scratch_shapes=[pltpu.VMEM((2 * tq, LANES), BF16), pltpu.VMEM((2 * tq, LANES), F32),
                        pltpu.VMEM((2 * tq, LANES), F32), pltpu.VMEM((2 * tq, LANES), F32)])
    return pl.pallas_call(
        functools.partial(_diff_prompt_kernel, lam_init=lam_init, tq=tq),
        out_shape=jax.ShapeDtypeStruct((M, 512), BF16),
        grid_spec=grid_spec,
        compiler_params=_cparams(("arbitrary", "arbitrary", "arbitrary")),
        name="diff_prompt",
    )(qi, kj, dq, dk, dv, lamv, g)


def _diff_sample_kernel(q_ref, kn_ref, vn_ref, kp_ref, vp_ref, lam_ref, g_ref, o_ref, *, lam_init, past_len):
    T = q_ref.shape[0]
    lam = _diff_lambda(lam_ref, lam_init)
    lane = lax.broadcasted_iota(I32, (1, LANES), 1)
    qpos = past_len + lax.broadcasted_iota(I32, (T, T), 0)
    kpos = past_len + lax.broadcasted_iota(I32, (T, T), 1)
    vis_new = (kpos // CHUNK) <= (qpos // CHUNK)
    P = past_len
    for h in range(DIFF_HEADS):
        hs = slice(h * LANES, (h + 1) * LANES)
        q = q_ref[:, hs]
        kpt = kp_ref[h].reshape(2 * DIFF_DH, P).astype(BF16)
        vp = vp_ref[pl.ds(h, P, stride=DIFF_HEADS), :].astype(BF16)
        kn = kn_ref[:, hs]
        vn = vn_ref[:, hs]
        outs = []
        for c in range(2):
            qc = jnp.where((lane < DIFF_DH) == (c == 0), q, jnp.zeros_like(q))
            sp = _dot(qc, kpt)
            sn = jnp.where(vis_new, _dot_nt(qc, kn), NEG_INF)
            m = jnp.maximum(jnp.max(sp, axis=1, keepdims=True), jnp.max(sn, axis=1, keepdims=True))
            pp = jnp.exp((sp - m).astype(BF16))
            pn = jnp.exp((sn - m).astype(BF16))
            l = jnp.sum(pp.astype(F32), axis=1, keepdims=True) + jnp.sum(pn.astype(F32), axis=1, keepdims=True)
            outs.append((_dot(pp, vp) + _dot(pn, vn)) / l)
        o_ref[:, hs] = _diff_finish(outs[0], outs[1], lam, g_ref[...], lam_init).astype(BF16)


def _diff_sample(dq, dk, dv, past_kt, past_v, layer, lamv, g, B, T, lam_init):
    M = B * T
    P = past_kt.shape[-1]
    return pl.pallas_call(
        functools.partial(_diff_sample_kernel, lam_init=lam_init, past_len=P),
        out_shape=jax.ShapeDtypeStruct((M, 512), BF16),
        grid=(B,),
        in_specs=[pl.BlockSpec((T, 512), lambda b: (b, 0)),
                  pl.BlockSpec((T, 512), lambda b: (b, 0)),
                  pl.BlockSpec((T, 512), lambda b: (b, 0)),
                  pl.BlockSpec((None, None, DIFF_HEADS, 2, DIFF_DH, P), lambda b: (layer, b, 0, 0, 0, 0)),
                  pl.BlockSpec((None, None, P * DIFF_HEADS, DIFF_DV), lambda b: (layer, b, 0, 0)),
                  pl.BlockSpec((4, DIFF_DH), lambda b: (0, 0)),
                  pl.BlockSpec((1, LANES), lambda b: (0, 0))],
        out_specs=pl.BlockSpec((T, 512), lambda b: (b, 0)),
        compiler_params=_cparams(("arbitrary",)),
        name="diff_sample",
    )(dq, dk, dv, past_kt, past_v, lamv, g)


INT_MIN = -2 ** 31


def _idx_score(iq, iw, ikr, transposed=False):
    lane = lax.broadcasted_iota(I32, (1, IDX_HEADS * IDX_DH), 1)
    sc = None
    for i in range(IDX_HEADS):
        iqi = jnp.where((lane // IDX_DH) == i, iq, jnp.zeros_like(iq))
        lg = _dot(iqi, ikr) if transposed else _dot_nt(iqi, ikr)
        t = iw[:, i:i + 1] * jnp.maximum(lg, 0.0)
        sc = t if sc is None else sc + t
    return jnp.where(sc == 0.0, 0.0, sc)


def _order_key(score):
    bits = pltpu.bitcast(score, I32)
    return jnp.where(bits < 0, bits ^ 0x7FFFFFFF, bits)


def _count(mask):
    return jnp.sum(mask.astype(F32), axis=1, keepdims=True)


def _f32_key(x):
    b = int(np.float32(x).view(np.int32))
    return b ^ 0x7FFFFFFF if b < 0 else b


def _key_value(k):
    return pltpu.bitcast(jnp.where(k < 0, k ^ 0x7FFFFFFF, k), F32)


def _topk_threshold(key_refs, smax, smin, n_sel, n_keys, probe_masked):
    tq = key_refs[0].shape[0]
    n = float(n_sel)

    def count_ge(cand):
        tot = None
        for kr in key_refs:
            c = _count(kr[...] >= cand)
            tot = c if tot is None else tot + c
        return tot

    def update(st, cand):
        lo, cnt_lo, hi, cnt_hi = st
        c = count_ge(cand)
        ge = c >= n
        up = ge & (cand > lo)
        dn = jnp.logical_not(ge) & (cand < hi)
        return (jnp.where(up, cand, lo), jnp.where(up, c, cnt_lo),
                jnp.where(dn, cand, hi), jnp.where(dn, c, cnt_hi))

    def settled(st):
        lo, cnt_lo, hi, _ = st
        return (cnt_lo == n) | ((hi - 1) == lo)

    def unsettled_any(st):
        return jnp.max(jnp.where(settled(st), 0.0, 1.0)) > 0.0

    col = lambda v, dt: jnp.full((tq, 1), v, dt)
    kmax = _order_key(smax)
    hi0 = jnp.where(kmax == 2 ** 31 - 1, kmax, kmax + 1)
    st = (col(INT_MIN, I32), col(float(n_keys), F32), hi0, col(0.0, F32))
    fixed = ((_f32_key(NEG_INF), _f32_key(NEG_INF) + 1) if probe_masked else ()) + (0, 1)
    for c in fixed:
        st = update(st, col(c, I32))
    st = update(st, _order_key(smin))
    log_n = math.log(n)

    def cond(carry):
        it, _, go = carry
        return jnp.logical_and(it < 100, go)

    def body(carry):
        it, st, _ = carry
        lo, cnt_lo, hi, cnt_hi = st
        a = jnp.log(cnt_lo + 0.5)
        frac = (a - log_n) / jnp.maximum(a - jnp.log(cnt_hi + 0.5), 1e-9)
        v_lo = _key_value(lo)
        guess = _order_key(v_lo + frac * (_key_value(hi) - v_lo))
        mid = (lo >> 1) + (hi >> 1) + (lo & hi & 1)
        cand = jnp.minimum(jnp.maximum(jnp.where(lax.rem(it, 3) == 2, mid, guess), lo + 1), hi - 1)
        st = update(st, jnp.where(settled(st), lo, cand))
        return it + 1, st, unsettled_any(st)

    _, st, _ = lax.while_loop(cond, body, (jnp.int32(0), st, unsettled_any(st)))
    lo, cnt_lo, hi, cnt_hi = st
    tied = ((hi - 1) == lo) & (cnt_lo > n)
    return lo, jnp.where(tied, n - cnt_hi, float(n_keys))


def _row_extremes(score, vis):
    if vis is None:
        return jnp.max(score, axis=1, keepdims=True), jnp.min(score, axis=1, keepdims=True)
    return (jnp.max(jnp.where(vis, score, -jnp.inf), axis=1, keepdims=True),
            jnp.min(jnp.where(vis, score, jnp.inf), axis=1, keepdims=True))


def _selection_bias(key_refs, bias_refs, vis_fns, smax, smin, n_sel, n_keys, probe_masked):
    t, need = _topk_threshold(key_refs, smax, smin, n_sel, n_keys, probe_masked)
    tq = key_refs[0].shape[0]
    base = jnp.zeros((tq, 1), F32)
    for kr, br, vis_fn in zip(key_refs, bias_refs, vis_fns):
        W = kr.shape[1]
        bw = min(W, LANES)
        assert W % bw == 0
        tri = (lax.broadcasted_iota(I32, (bw, bw), 0) <= lax.broadcasted_iota(I32, (bw, bw), 1)).astype(BF16)
        ones_tri = jnp.concatenate([jnp.ones((bw, bw), BF16), tri], axis=0)
        earlier = jnp.zeros((tq, bw), F32)
        for j in range(W // bw):
            cs = slice(j * bw, (j + 1) * bw)
            key = kr[:, cs]
            tie = key == t
            tb = jnp.where(tie, 1.0, 0.0)
            rank = base + _dot(jnp.concatenate([earlier, tb], axis=1).astype(BF16), ones_tri)
            sel = (key > t) | (tie & (rank <= need))
            if vis_fn is not None:
                sel = sel & vis_fn(j * bw, bw)
            br[:, cs] = jnp.where(sel, 0.0, NEG_INF)
            earlier = earlier + tb
        base = base + jnp.sum(earlier, axis=1, keepdims=True)


def _dsa_attend(q_ref, bias_refs, k_loads, v_loads, o_ref):
    for h in range(DSA_HEADS):
        hs = slice(h * DSA_DH, (h + 1) * DSA_DH)
        q = q_ref[:, hs]
        ss = [_dot_nt(q, kl(hs)) + br[...] for kl, br in zip(k_loads, bias_refs)]
        m = None
        for s in ss:
            ms = jnp.max(s, axis=1, keepdims=True)
            m = ms if m is None else jnp.maximum(m, ms)
        l = None
        o = None
        for s, vl in zip(ss, v_loads):
            p = jnp.exp((s - m).astype(BF16))
            ls = jnp.sum(p.astype(F32), axis=1, keepdims=True)
            os_ = _dot(p, vl(hs))
            l = ls if l is None else l + ls
            o = os_ if o is None else o + os_
        o_ref[:, hs] = (o / l).astype(BF16)


def _dsa_prompt_kernel(q_ref, iq_ref, iw_ref, k_ref, v_ref, ikr_ref, o_ref, key_scr, bias_scr,
                       *, tq, q_tile0, n_sel, probe_masked):
    tk = k_ref.shape[0]
    qt = q_tile0 + pl.program_id(1)
    score = _idx_score(iq_ref[...], iw_ref[...], ikr_ref[...])
    qpos = qt * tq + lax.broadcasted_iota(I32, (tq, tk), 0)
    kidx = lax.broadcasted_iota(I32, (tq, tk), 1)
    vis = (kidx // CHUNK) <= (qpos // CHUNK)
    key_scr[...] = _order_key(jnp.where(vis, score, NEG_INF))
    smax, smin = _row_extremes(score, vis)
    qchunk = (qt * tq + lax.broadcasted_iota(I32, (tq, 1), 0)) // CHUNK

    def vis_block(off, bw):
        return ((off + lax.broadcasted_iota(I32, (tq, bw), 1)) // CHUNK) <= qchunk

    _selection_bias([key_scr], [bias_scr], [vis_block], smax, smin, n_sel, tk, probe_masked)
    _dsa_attend(q_ref, [bias_scr], [lambda hs: k_ref[:, hs]], [lambda hs: v_ref[:, hs]], o_ref)


def _dsa_prompt(cq, iq, iw, ck, cv, ikr, B, T):
    tq = min(T, 128)
    nq = T // tq
    n_sel = min(DSA_TOPK_MAX, T // 4)
    ng = next(g for g in (16, 8, 4, 2, 1) if nq % g == 0)
    tpg = nq // ng
    outs = []
    for gi in range(ng):
        tk = (gi + 1) * tpg * tq
        nkb = T // tk if T % tk == 0 else None

        def kv_spec(w, tk=tk):
            return pl.BlockSpec((None, tk, w), lambda b, i: (b, 0, 0))

        def q_spec(w, gi=gi):
            return pl.BlockSpec((None, tq, w), lambda b, i: (b, gi * tpg + i, 0))

        out = pl.pallas_call(
            functools.partial(_dsa_prompt_kernel, tq=tq, q_tile0=gi * tpg, n_sel=n_sel,
                              probe_masked=gi * tpg * tq + CHUNK < n_sel),
            out_shape=jax.ShapeDtypeStruct((B, tpg * tq, 512), BF16),
            grid=(B, tpg),
            in_specs=[q_spec(512), q_spec(256), q_spec(LANES), kv_spec(512), kv_spec(512), kv_spec(256)],
            out_specs=pl.BlockSpec((None, tq, 512), lambda b, i: (b, i, 0)),
            scratch_shapes=[pltpu.VMEM((tq, tk), I32), pltpu.VMEM((tq, tk), F32)],
            compiler_params=_cparams(("arbitrary", "arbitrary"), 56 * 1024 * 1024),
            name=f"dsa_prompt_{gi}",
        )(cq.reshape(B, T, 512), iq.reshape(B, T, 256), iw.reshape(B, T, LANES),
          ck.reshape(B, T, 512), cv.reshape(B, T, 512), ikr.reshape(B, T, 256))
        outs.append(out)
    o = outs[0] if ng == 1 else jnp.concatenate(outs, axis=1)
    return o.reshape(B * T, 512)


def _dsa_sample_kernel(q_ref, iq_ref, iw_ref, kn_ref, vn_ref, ikrn_ref, kp_ref, vp_ref, ikp_ref, o_ref,
                       keyp_scr, keyn_scr, biasp_scr, biasn_scr, *, past_len, n_sel):
    T = q_ref.shape[0]
    P = past_len
    iq = iq_ref[...]
    iw = iw_ref[...]
    ikt = ikp_ref[...].astype(BF16)
    ikrp = jnp.concatenate([ikt] * IDX_HEADS, axis=0)
    score_p = _idx_score(iq, iw, ikrp, transposed=True)
    keyp_scr[...] = _order_key(score_p)
    pmax, pmin = _row_extremes(score_p, None)
    qpos = P + lax.broadcasted_iota(I32, (T, T), 0)
    kpos = P + lax.broadcasted_iota(I32, (T, T), 1)
    vis_n = (kpos // CHUNK) <= (qpos // CHUNK)
    score_n = _idx_score(iq, iw, ikrn_ref[...])
    keyn_scr[...] = _order_key(jnp.where(vis_n, score_n, NEG_INF))
    nmax, nmin = _row_extremes(score_n, vis_n)
    _selection_bias([keyp_scr, keyn_scr], [biasp_scr, biasn_scr], [None, lambda off, bw: vis_n[:, off:off + bw]],
                    jnp.maximum(pmax, nmax), jnp.minimum(pmin, nmin),
                    n_sel, P + T, probe_masked=P + min(T, CHUNK) < n_sel)

    def past_head(ref):
        return lambda hs: ref[pl.ds(hs.start // DSA_DH, P, stride=DSA_HEADS), :].astype(BF16)

    _dsa_attend(q_ref, [biasp_scr, biasn_scr],
                [past_head(kp_ref), lambda hs: kn_ref[:, hs]],
                [past_head(vp_ref), lambda hs: vn_ref[:, hs]], o_ref)


def _dsa_sample(cq, iq, iw, ck, cv, ikr, past_k, past_v, past_ikt, layer, B, T):
    M = B * T
    P = past_ikt.shape[-1]
    n_sel = min(DSA_TOPK_MAX, (P + T) // 4)

    def rspec(w):
        return pl.BlockSpec((T, w), lambda b: (b, 0))

    def pspec(r, w):
        return pl.BlockSpec((None, None, r, w), lambda b: (layer, b, 0, 0))

    return pl.pallas_call(
        functools.partial(_dsa_sample_kernel, past_len=P, n_sel=n_sel),
        out_shape=jax.ShapeDtypeStruct((M, 512), BF16),
        grid=(B,),
        in_specs=[rspec(512), rspec(256), rspec(LANES), rspec(512), rspec(512), rspec(256),
                  pspec(P * DSA_HEADS, DSA_DH), pspec(P * DSA_HEADS, DSA_DH), pspec(IDX_DH, P)],
        out_specs=rspec(512),
        scratch_shapes=[pltpu.VMEM((T, P), I32), pltpu.VMEM((T, T), I32),
                        pltpu.VMEM((T, P), F32), pltpu.VMEM((T, T), F32)],
        compiler_params=_cparams(("arbitrary",)),
        name="dsa_sample",
    )(cq, iq, iw, ck, cv, ikr, past_k, past_v, past_ikt)


def _merge_kernel(og_ref, od_ref, oc_ref, g0_ref, g1_ref, g2_ref, x_ref, m_ref,
                  wg_ref, wd_ref, wc_ref, wo_ref, gn_ref, wr_ref, br_ref,
                  x1_ref, h2_ref, ti_ref, tw_ref, rk_ref, cnt_ref, run_scr, *, nb):
    merged = (jax.nn.sigmoid(g0_ref[...]) * _dot(og_ref[...], wg_ref[...])
              + jax.nn.sigmoid(g1_ref[...]) * _dot(od_ref[...], wd_ref[...])
              + jax.nn.sigmoid(g2_ref[...]) * _dot(oc_ref[...], wc_ref[...]))
    mix = _dot(merged.astype(BF16), wo_ref[...])
    m = m_ref[...]
    x1 = x_ref[...] + _gate_rows(mix, m, 2, nb)
    x1_ref[...] = x1
    xn = x1 * lax.rsqrt(jnp.mean(x1 * x1, axis=-1, keepdims=True) + EPS) * gn_ref[...]
    h2 = _modulate(xn, m, 3, 4, nb)
    _rows_to_tiles(h2_ref, h2)
    lg = _dot(h2.astype(BF16), wr_ref[...]) + br_ref[...]
    tm = lg.shape[0]
    lane = lax.broadcasted_iota(I32, (tm, LANES), 1).astype(F32)
    vals, idxs = [], []
    for _ in range(TOP_K):
        mx = jnp.max(lg, axis=1, keepdims=True)
        ix = jnp.min(jnp.where(lg == mx, lane, float(LANES)), axis=1, keepdims=True)
        vals.append(mx)
        idxs.append(ix)
        lg = jnp.where(lane == ix, -jnp.inf, lg)
    es = [jnp.exp(v - vals[0]) for v in vals]
    den = es[0] + es[1] + es[2] + es[3]
    @pl.when(pl.program_id(0) == 0)
    def _():
        run_scr[...] = jnp.zeros(run_scr.shape, F32)

    hot = [lane == ix for ix in idxs]
    tot = (hot[0].astype(F32) + hot[1].astype(F32)) + (hot[2].astype(F32) + hot[3].astype(F32))
    strict = (lax.broadcasted_iota(I32, (tm, tm), 1) < lax.broadcasted_iota(I32, (tm, tm), 0)).astype(BF16)
    before = run_scr[...] + _dot(strict, tot.astype(BF16))
    ti = jnp.zeros((tm, LANES), F32)
    tw = jnp.zeros((tm, LANES), F32)
    rk = jnp.zeros((tm, LANES), F32)
    for r in range(TOP_K):
        ti = jnp.where(lane == float(r), idxs[r], ti)
        tw = jnp.where(lane == float(r), es[r] / den, tw)
        rk = jnp.where(lane == float(r), jnp.sum(jnp.where(hot[r], before, 0.0), axis=1, keepdims=True), rk)
    ti_ref[...] = ti.astype(I32)
    tw_ref[...] = tw
    rk_ref[...] = rk.astype(I32)
    run_new = run_scr[...] + jnp.sum(tot, axis=0, keepdims=True)
    run_scr[...] = run_new
    cnt_ref[...] = run_new.astype(I32)


def _merge(og, od, oc, z, x2, m, wts, B, T):
    M = B * T
    tm, nb = _row_tiling(B, T, 512)
    wg, wd, wc, wo, gn, wr, br = wts

    def rspec(w):
        return pl.BlockSpec((tm, w), lambda i: (i, 0))

    def gspec(k):
        return pl.BlockSpec((tm, D_MODEL), lambda i: (i, Z_GATES // D_MODEL + k))

    def wspec(r, c):
        return pl.BlockSpec((r, c), lambda i: (0, 0))

    return pl.pallas_call(
        functools.partial(_merge_kernel, nb=nb),
        out_shape=[jax.ShapeDtypeStruct((M, D_MODEL), F32), jax.ShapeDtypeStruct((M * ROW_TILE, LANES), F32),
                   jax.ShapeDtypeStruct((M, LANES), I32), jax.ShapeDtypeStruct((M, LANES), F32),
                   jax.ShapeDtypeStruct((M, LANES), I32), jax.ShapeDtypeStruct((1, LANES), I32)],
        grid=(M // tm,),
        in_specs=[rspec(512), rspec(512), rspec(512), gspec(0), gspec(1), gspec(2), rspec(D_MODEL),
                  pl.BlockSpec((nb, 6, D_MODEL), lambda i: ((i * tm) // (T * nb), 0, 0)),
                  wspec(512, D_MODEL), wspec(512, D_MODEL), wspec(512, D_MODEL), wspec(D_MODEL, D_MODEL),
                  wspec(1, D_MODEL), wspec(D_MODEL, LANES), wspec(1, LANES)],
        out_specs=[rspec(D_MODEL), pl.BlockSpec((tm * ROW_TILE, LANES), lambda i: (i, 0)),
                   rspec(LANES), rspec(LANES), rspec(LANES),
                   pl.BlockSpec((1, LANES), lambda i: (0, 0))],
        scratch_shapes=[pltpu.VMEM((1, LANES), F32)],
        compiler_params=_cparams(("arbitrary",)),
        name="merge_router",
    )(og, od, oc, z, z, z, x2, m, wg, wd, wc, wo, gn, wr, br)


DMA_UNROLL = 8


def _w1prep_kernel(w_ref, sel_ref, g_ref, u_ref):
    sel = sel_ref[...]
    for c in range(w_ref.shape[1] // 256):
        r = _dot(w_ref[:, c * 256:(c + 1) * 256].astype(BF16), sel)
        g_ref[:, c * LANES:(c + 1) * LANES] = r[:, :LANES].astype(BF16)
        u_ref[:, c * LANES:(c + 1) * LANES] = r[:, LANES:].astype(BF16)


def _w1prep(w1_all, layer):
    E = w1_all.shape[1]
    tr = 512
    j = jnp.arange(256, dtype=I32)
    src = jnp.where(j < LANES, 2 * j, 2 * (j - LANES) + 1)
    sel = (jnp.arange(256, dtype=I32)[:, None] == src[None, :]).astype(BF16)
    return pl.pallas_call(
        _w1prep_kernel,
        out_shape=[jax.ShapeDtypeStruct((E, D_MODEL, D_FF), BF16)] * 2,
        grid=(E, D_MODEL // tr),
        in_specs=[pl.BlockSpec((None, None, tr, 2 * D_FF), lambda e, r: (layer, e, r, 0)),
                  pl.BlockSpec((256, 256), lambda e, r: (0, 0))],
        out_specs=[pl.BlockSpec((None, tr, D_FF), lambda e, r: (e, r, 0))] * 2,
        compiler_params=_cparams(("arbitrary", "arbitrary")),
        name="w1prep",
    )(w1_all, sel)


def _rows_to_tiles(ref, x):
    n = x.shape[0]
    for c in range(ROW_TILE):
        ref[pl.ds(c, n, stride=ROW_TILE), :] = x[:, c * LANES:(c + 1) * LANES]


def _tiles_to_rows(ref, base, n):
    return jnp.concatenate([ref[pl.ds(base + c, n, stride=ROW_TILE), :] for c in range(ROW_TILE)], axis=1)


def _tile_rows(ref, r):
    return ref.at[pl.ds(pl.multiple_of(r * ROW_TILE, ROW_TILE), ROW_TILE), :]


def _dispatch_kernel(pos_ref, h_ref, xs_in, xs_out, sem, *, tm):
    del xs_in

    tokens = DMA_UNROLL // TOP_K

    def body(j, c):
        for tt in range(tokens):
            src = _tile_rows(h_ref, j * tokens + tt)
            for k in range(TOP_K):
                dst = _tile_rows(xs_out, pos_ref[0, 0, (j * tokens + tt) * TOP_K + k])
                pltpu.make_async_copy(src, dst, sem).start()
        return c

    lax.fori_loop(0, tm // tokens, body, 0)
    for _ in range(TOP_K):
        pltpu.make_async_copy(h_ref, xs_out.at[pl.ds(0, tm * ROW_TILE), :], sem).wait()


def _dispatch(h2t, pos, n_rows, tm):
    M = h2t.shape[0] // ROW_TILE
    nt = M // tm
    return pl.pallas_call(
        functools.partial(_dispatch_kernel, tm=tm),
        out_shape=jax.ShapeDtypeStruct((n_rows * ROW_TILE, LANES), F32),
        grid=(nt,),
        in_specs=[pl.BlockSpec((1, 1, TOP_K * tm), lambda i: (i, 0, 0), memory_space=pltpu.SMEM),
                  pl.BlockSpec((tm * ROW_TILE, LANES), lambda i: (i, 0)),
                  pl.BlockSpec(memory_space=pl.ANY)],
        out_specs=pl.BlockSpec(memory_space=pl.ANY),
        scratch_shapes=[pltpu.SemaphoreType.DMA],
        input_output_aliases={2: 0},
        compiler_params=_cparams(("arbitrary",)),
        name="moe_dispatch",
    )(pos.reshape(nt, 1, TOP_K * tm), h2t, jnp.zeros((n_rows * ROW_TILE, LANES), F32))


def _ffn_kernel(be_ref, nu_ref, x_ref, w1g_ref, b1g_ref, w1u_ref, b1u_ref, w2_ref, b2_ref, o_ref, *, bm):
    @pl.when(pl.program_id(0) < nu_ref[0])
    def _():
        x = _tiles_to_rows(x_ref, 0, bm).astype(BF16)
        g = jnp.minimum(_dot(x, w1g_ref[...]) + b1g_ref[...], SWIGLU_LIMIT)
        u = jnp.clip(_dot(x, w1u_ref[...]) + b1u_ref[...], -SWIGLU_LIMIT, SWIGLU_LIMIT)
        a = g * jax.nn.sigmoid(SWIGLU_ALPHA * g) * (u + 1.0)
        _rows_to_tiles(o_ref, _dot(a.astype(BF16), w2_ref[...]) + b2_ref[...])

    @pl.when(pl.program_id(0) >= nu_ref[0])
    def _():
        o_ref[...] = jnp.zeros(o_ref.shape, F32)


def _ffn(xs, block_e, n_used, w1g, b1g, w1u, b1u, w2_all, b2, layer, bm):
    n_rows = xs.shape[0] // ROW_TILE
    nblk = n_rows // bm

    def wspec(r, c):
        return pl.BlockSpec((None, r, c), lambda i, be, nu: (be[i], 0, 0))

    grid_spec = pltpu.PrefetchScalarGridSpec(
        num_scalar_prefetch=2,
        grid=(nblk,),
        in_specs=[pl.BlockSpec((bm * ROW_TILE, LANES), lambda i, be, nu: (jnp.minimum(i, nu[0] - 1), 0)),
                  wspec(D_MODEL, D_FF), wspec(1, D_FF), wspec(D_MODEL, D_FF), wspec(1, D_FF),
                  pl.BlockSpec((None, None, D_FF, D_MODEL), lambda i, be, nu: (layer, be[i], 0, 0)),
                  wspec(1, D_MODEL)],
        out_specs=pl.BlockSpec((bm * ROW_TILE, LANES), lambda i, be, nu: (i, 0)))
    return pl.pallas_call(
        functools.partial(_ffn_kernel, bm=bm),
        out_shape=jax.ShapeDtypeStruct((n_rows * ROW_TILE, LANES), F32),
        grid_spec=grid_spec,
        compiler_params=_cparams(("arbitrary",)),
        name="moe_ffn",
    )(block_e, n_used, xs, w1g, b1g, w1u, b1u, w2_all, b2)


def _combine_kernel(pos_ref, posn_ref, ys_hbm, tw_ref, x_ref, m_ref, o_ref, ybuf, sem, *, tm, nb):
    i = pl.program_id(0)
    nt = pl.num_programs(0)
    rows = TOP_K * tm * ROW_TILE
    slot = lax.rem(i, 2)

    def request(idx_ref, s):
        base = s * (TOP_K * tm)

        def body(j, c):
            for u in range(DMA_UNROLL):
                n = j * DMA_UNROLL + u
                pltpu.make_async_copy(_tile_rows(ys_hbm, idx_ref[0, 0, n]), _tile_rows(ybuf, base + n),
                                      sem.at[s]).start()
            return c

        lax.fori_loop(0, TOP_K * tm // DMA_UNROLL, body, 0)

    @pl.when(i == 0)
    def _():
        request(pos_ref, 0)

    @pl.when(i + 1 < nt)
    def _():
        request(posn_ref, 1 - slot)

    off = pl.multiple_of(slot * rows, rows)
    pltpu.make_async_copy(ys_hbm.at[pl.ds(0, rows), :], ybuf.at[pl.ds(off, rows), :], sem.at[slot]).wait()
    tw = tw_ref[...]
    y = None
    for k in range(TOP_K):
        yk = tw[:, k:k + 1] * _tiles_to_rows(ybuf, off + k * tm * ROW_TILE, tm)
        y = yk if y is None else y + yk
    o_ref[...] = x_ref[...] + _gate_rows(y, m_ref[...], 5, nb)


def _combine(pos_t, ys, top_w, x1, m, B, T, tm, nb):
    M = B * T
    nt = M // tm
    pos3 = pos_t.reshape(nt, 1, TOP_K * tm)
    return pl.pallas_call(
        functools.partial(_combine_kernel, tm=tm, nb=nb),
        out_shape=jax.ShapeDtypeStruct((M, D_MODEL), F32),
        grid=(nt,),
        in_specs=[pl.BlockSpec((1, 1, TOP_K * tm), lambda i: (i, 0, 0), memory_space=pltpu.SMEM),
                  pl.BlockSpec((1, 1, TOP_K * tm), lambda i: (jnp.minimum(i + 1, nt - 1), 0, 0),
                               memory_space=pltpu.SMEM),
                  pl.BlockSpec(memory_space=pl.ANY),
                  pl.BlockSpec((tm, LANES), lambda i: (i, 0)),
                  pl.BlockSpec((tm, D_MODEL), lambda i: (i, 0)),
                  pl.BlockSpec((nb, 6, D_MODEL), lambda i: ((i * tm) // (T * nb), 0, 0))],
        out_specs=pl.BlockSpec((tm, D_MODEL), lambda i: (i, 0)),
        scratch_shapes=[pltpu.VMEM((2 * TOP_K * tm * ROW_TILE, LANES), F32), pltpu.SemaphoreType.DMA((2,))],
        compiler_params=_cparams(("arbitrary",)),
        name="moe_combine",
    )(pos3, pos3, ys, top_w, x1, m)


def _route(top_i, rank, counts, bm):
    N = top_i.shape[0]
    NK = N * TOP_K
    padded = (counts + bm - 1) // bm * bm
    pad_end = jnp.cumsum(padded)
    pad_start = pad_end - padded
    onehot = top_i[:, :, None] == jnp.arange(N_EXPERTS, dtype=I32)[None, None, :]
    pos = rank + jnp.sum(jnp.where(onehot, pad_start[None, None, :], 0), axis=2)
    n_rows = (-(-NK // bm)) * bm + N_EXPERTS * bm
    nblk = n_rows // bm
    starts = jnp.arange(nblk, dtype=I32) * bm
    block_e = jnp.minimum(jnp.sum((pad_end[None, :] <= starts[:, None]).astype(I32), axis=1), N_EXPERTS - 1)
    n_used = (pad_end[-1:] // bm).astype(I32)
    return pos.reshape(NK).astype(I32), n_rows, block_e.astype(I32), n_used


def _moe(h2, top_i, top_w, rank, counts, x1, m, ew, B, T):
    M = B * T
    bm = 256 if M * TOP_K >= 256 * N_EXPERTS * 4 else 128
    pos, n_rows, block_e, n_used = _route(top_i[:, :TOP_K], rank[:, :TOP_K], counts[0, :N_EXPERTS], bm)
    tm, nb = _row_tiling(B, T, 256)
    xs = _dispatch(h2, pos, n_rows, tm)
    ys = _ffn(xs, block_e, n_used, *ew, bm)
    pos_t = pos.reshape(M // tm, tm, TOP_K).transpose(0, 2, 1)
    return _combine(pos_t, ys, top_w, x1, m, B, T, tm, nb)


def _prep_layer(P, l, w2_all):
    w_t = P['w_in'][l].T

    def rows(name):
        o, s = _SRC[name]
        return w_t[o:o + s]

    zeros = lambda n: jnp.zeros((n, D_MODEL), F32)
    w_r = jnp.concatenate([
        rows('gla_q'), rows('gla_k'), rows('gla_v'), rows('gla_r'),
        rows('diff_q'), rows('diff_k'), rows('diff_v'),
        rows('dsa_q'), rows('dsa_k'), rows('dsa_v'),
        rows('idx_q'), rows('idx_k'), rows('idx_w'), zeros(60), rows('gla_a'), zeros(112),
        rows('gates')], axis=0).astype(BF16)
    assert w_r.shape[0] == Z_WIDTH
    W = GLA_HEADS * GLA_DK
    wa = jnp.zeros((LANES, W), F32).at[:GLA_GATE_RANK].set(P['w_gla_a2'][l]).astype(BF16)
    w1g, w1u = _w1prep(P['w_mlp1'], l)
    b1 = P['b_mlp1'][l]
    return dict(
        w_in=w_r, wa=wa, ba=P['b_gla_a2'][l].reshape(1, W),
        g_gla=jnp.tile(P['g_gla_out'][l], GLA_HEADS).reshape(1, 512),
        gains=(jnp.tile(P['g_diff_q'][l], 8).reshape(1, 512), jnp.tile(P['g_diff_k'][l], 8).reshape(1, 512),
               jnp.tile(P['g_dsa_q'][l], 4).reshape(1, 512), jnp.tile(P['g_dsa_k'][l], 4).reshape(1, 512),
               jnp.tile(P['g_idx_k'][l], 2).reshape(1, LANES)),
        lamv=jnp.stack([P['lambda_q1'][l], P['lambda_k1'][l], P['lambda_q2'][l], P['lambda_k2'][l]]),
        g_diff=P['g_diff_out'][l].reshape(1, LANES),
        merge=(P['w_branch_gla'][l].astype(BF16), P['w_branch_diff'][l].astype(BF16),
               P['w_branch_dsa'][l].astype(BF16), P['w_out'][l].astype(BF16),
               P['g_norm2'][l].reshape(1, D_MODEL),
               jnp.zeros((D_MODEL, LANES), F32).at[:, :N_EXPERTS].set(P['w_router'][l]).astype(BF16),
               jnp.full((1, LANES), NEG_INF, F32).at[0, :N_EXPERTS].set(P['b_router'][l])),
        experts=(w1g, b1[:, None, 0::2], w1u, b1[:, None, 1::2], w2_all, P['b_mlp2'][l][:, None, :], l),
    )


def _trunk(x, c, past, P, prepped):
    B, T, _ = x.shape
    M = B * T
    x2 = x.reshape(M, D_MODEL)
    past_len = 0 if past is None else past[0].shape[2]
    pos = past_len + jnp.arange(T, dtype=I32)
    tabs64 = _rope_tables(pos, 64)
    tabs128 = _rope_tables(pos, 128)
    W = GLA_HEADS * GLA_DK
    if past is not None:
        L, _, PL = past[0].shape[:3]
        pdk = past[0].transpose(0, 1, 3, 4, 5, 2)
        pdv = past[1].reshape(L, B, PL * DIFF_HEADS, DIFF_DV)
        pck = past[2].reshape(L, B, PL * DSA_HEADS, DSA_DH)
        pcv = past[3].reshape(L, B, PL * DSA_HEADS, DSA_DH)
        pik = past[4].transpose(0, 1, 3, 2)
        s0_all = past[5].transpose(0, 1, 4, 2, 3).reshape(past[5].shape[0], B, GLA_DV, W)
    per_layer = []
    for l, pp in enumerate(prepped):
        lam_init = 0.8 - 0.6 * math.exp(-0.3 * l)
        m = _ada(c, P['w_ada'], P['b_ada'], l)
        z = _inproj(x2, m, P['g_norm1'][l], pp['w_in'], B, T)
        (dq, dk32, dkb, dvb, cq, ck32, ckb, cvb, iq, ik32, ikr, iw, dv32, cv32) = _post(
            z, tabs64, tabs128, pp['gains'], B, T)
        s0t = jnp.zeros((B, GLA_DV, W), F32) if past is None else s0_all[l]
        o_gla, st = _gla(z, pp['wa'], pp['ba'], pp['g_gla'], s0t, B, T)
        if past is None:
            o_diff = _diff_prompt(dq, dkb, dvb, pp['lamv'], pp['g_diff'], B, T, lam_init)
            o_dsa = _dsa_prompt(cq, iq, iw, ckb, cvb, ikr, B, T)
        else:
            o_diff = _diff_sample(dq, dkb, dvb, pdk, pdv, l, pp['lamv'], pp['g_diff'], B, T, lam_init)
            o_dsa = _dsa_sample(cq, iq, iw, ckb, cvb, ikr, pck, pcv, pik, l, B, T)
        x1, h2, top_i, top_w, rank, counts = _merge(o_gla, o_diff, o_dsa, z, x2, m, pp['merge'], B, T)
        x2 = _moe(h2, top_i, top_w, rank, counts, x1, m, pp['experts'], B, T)
        s_gla = st.reshape(B, GLA_DV, GLA_HEADS, GLA_DK).transpose(0, 2, 3, 1)
        per_layer.append((dk32.reshape(B, T, DIFF_HEADS, 2, DIFF_DH),
                          dv32.reshape(B, T, DIFF_HEADS, DIFF_DV),
                          ck32.reshape(B, T, DSA_HEADS, DSA_DH),
                          cv32.reshape(B, T, DSA_HEADS, DSA_DH),
                          ik32.reshape(B, T, IDX_DH),
                          s_gla))
    stacked = tuple(jnp.stack([st[i] for st in per_layer]) for i in range(6))
    return x2.reshape(B, T, D_MODEL), stacked


def kernel(x_prompt, x_sample, cache_diff_k, cache_diff_v, cache_dsa_k, cache_dsa_v, cache_dsa_idx_k,
           state_gla, c_prompt, c_sample, w_ada, b_ada, g_norm1, g_norm2, w_in, w_gla_a2, b_gla_a2,
           g_gla_out, g_diff_q, g_diff_k, lambda_q1, lambda_k1, lambda_q2, lambda_k2, g_diff_out,
           g_dsa_q, g_dsa_k, g_idx_k, w_branch_gla, w_branch_diff, w_branch_dsa, w_out, w_router,
           b_router, w_mlp1, b_mlp1, w_mlp2, b_mlp2):
    P = dict(w_ada=w_ada, b_ada=b_ada, g_norm1=g_norm1, g_norm2=g_norm2, w_in=w_in, w_gla_a2=w_gla_a2,
             b_gla_a2=b_gla_a2, g_gla_out=g_gla_out, g_diff_q=g_diff_q, g_diff_k=g_diff_k,
             lambda_q1=lambda_q1, lambda_k1=lambda_k1, lambda_q2=lambda_q2, lambda_k2=lambda_k2,
             g_diff_out=g_diff_out, g_dsa_q=g_dsa_q, g_dsa_k=g_dsa_k, g_idx_k=g_idx_k,
             w_branch_gla=w_branch_gla, w_branch_diff=w_branch_diff, w_branch_dsa=w_branch_dsa,
             w_out=w_out, w_router=w_router, b_router=b_router, w_mlp1=w_mlp1, b_mlp1=b_mlp1,
             w_mlp2=w_mlp2, b_mlp2=b_mlp2)
    depth = w_in.shape[0]
    w2_all = w_mlp2.astype(BF16)
    prepped = [_prep_layer(P, l, w2_all) for l in range(depth)]
    y_prompt, new_p = _trunk(x_prompt, c_prompt, None, P, prepped)
    y_sample, new_s = _trunk(
        x_sample, c_sample,
        (cache_diff_k, cache_diff_v, cache_dsa_k, cache_dsa_v, cache_dsa_idx_k, state_gla), P, prepped)
    return (y_prompt, y_sample) + new_p + new_s
```

```python
import functools
import math

import numpy as np
import jax
import jax.numpy as jnp
from jax import lax
from jax.experimental import pallas as pl
from jax.experimental.pallas import tpu as pltpu

F32 = jnp.float32
BF16 = jnp.bfloat16
I32 = jnp.int32

D_MODEL = 1024
CHUNK = 64
ROPE_THETA = 500000.0
ROPE_FRACTION = 4
EPS = 1e-6
NEG_INF = -1e30

GLA_HEADS = 4
GLA_DK = 64
GLA_DV = 128
GLA_GATE_RANK = 16
GLA_GATE_TAU = 16.0
GLA_SUB = 16
DIFF_HEADS = 4
DIFF_DH = 64
DIFF_DV = 128
DSA_HEADS = 4
DSA_DH = 128
IDX_HEADS = 4
IDX_DH = 64
DSA_TOPK_MAX = 256
N_EXPERTS = 32
TOP_K = 4
D_FF = 1024
SWIGLU_LIMIT = 7.0
SWIGLU_ALPHA = 1.702

LANES = 128
ROW_TILE = 8
VMEM_LIMIT = 48 * 1024 * 1024
VMEM_LIMIT_DSA = 56 * 1024 * 1024

Z_GQK, Z_GV, Z_GR = 0, 512, 1024
Z_DQ, Z_DK, Z_DV = 1536, 2048, 2560
Z_CQ, Z_CK, Z_CV = 3072, 3584, 4096
Z_MISC = 4608
Z_GATES = 5120
Z_WIDTH = 8192
MISC_IK = 256
MISC_GA = 384

_SRC = {}
_off = 0
for _name, _size in (
        ('gla_q', 256), ('gla_k', 256), ('gla_v', 512), ('gla_a', 16), ('gla_r', 512),
        ('diff_q', 512), ('diff_k', 512), ('diff_v', 512),
        ('dsa_q', 512), ('dsa_k', 512), ('dsa_v', 512),
        ('idx_q', 256), ('idx_k', 64), ('idx_w', 4), ('gates', 3072)):
    _SRC[_name] = (_off, _size)
    _off += _size


def _cparams(sem, vmem=VMEM_LIMIT):
    return pltpu.CompilerParams(dimension_semantics=sem, vmem_limit_bytes=vmem)


def _dot(a, b):
    return jnp.dot(a, b, preferred_element_type=F32)


def _dot_nt(a, b):
    return lax.dot_general(a, b, (((1,), (1,)), ((), ())), preferred_element_type=F32)


def _dot_tn(a, b):
    return lax.dot_general(a, b, (((0,), (0,)), ((), ())), preferred_element_type=F32)


def _silu(x):
    return x * jax.nn.sigmoid(x)


def _row_tiling(B, T, target):
    if T >= target:
        assert T % target == 0
        return target, 1
    nb = 1
    for cand in range(1, B + 1):
        if B % cand == 0 and cand * T <= target:
            nb = cand
    return nb * T, nb


def _modulate(xn, m, shift_i, scale_i, nb):
    tm = xn.shape[0]
    if nb == 1:
        return xn * (1.0 + m[0, scale_i:scale_i + 1, :]) + m[0, shift_i:shift_i + 1, :]
    x3 = xn.reshape(nb, tm // nb, D_MODEL)
    h = x3 * (1.0 + m[:, scale_i:scale_i + 1, :]) + m[:, shift_i:shift_i + 1, :]
    return h.reshape(tm, D_MODEL)


def _gate_rows(y, m, gate_i, nb):
    tm = y.shape[0]
    if nb == 1:
        return y * m[0, gate_i:gate_i + 1, :]
    return (y.reshape(nb, tm // nb, D_MODEL) * m[:, gate_i:gate_i + 1, :]).reshape(tm, D_MODEL)


def _ada_kernel(c_ref, w_ref, b_ref, o_ref):
    s = _silu(c_ref[...])
    o_ref[...] = _dot(s.astype(BF16), w_ref[...].astype(BF16)) + b_ref[...]


def _ada(c, w, b, layer):
    B = c.shape[0]
    out = pl.pallas_call(
        _ada_kernel,
        out_shape=jax.ShapeDtypeStruct((B, 6 * D_MODEL), F32),
        grid=(6,),
        in_specs=[pl.BlockSpec((B, D_MODEL), lambda j: (0, 0)),
                  pl.BlockSpec((None, D_MODEL, D_MODEL), lambda j: (layer, 0, j)),
                  pl.BlockSpec((None, 1, D_MODEL), lambda j: (layer, 0, j))],
        out_specs=pl.BlockSpec((B, D_MODEL), lambda j: (0, j)),
        compiler_params=_cparams(("arbitrary",)),
        name="ada",
    )(c, w, b.reshape(b.shape[0], 1, -1))
    return out.reshape(B, 6, D_MODEL)


def _inproj_kernel(x_ref, m_ref, g_ref, w_ref, o_ref, h_scr, *, nb):
    @pl.when(pl.program_id(1) == 0)
    def _():
        x = x_ref[...]
        xn = x * lax.rsqrt(jnp.mean(x * x, axis=-1, keepdims=True) + EPS) * g_ref[...]
        h_scr[...] = _modulate(xn, m_ref[...], 0, 1, nb).astype(BF16)

    o_ref[...] = _dot_nt(h_scr[...], w_ref[...])


def _inproj(x2, m, g, wt, B, T):
    M = B * T
    tm, nb = _row_tiling(B, T, 1024)
    tn = 1024
    return pl.pallas_call(
        functools.partial(_inproj_kernel, nb=nb),
        out_shape=jax.ShapeDtypeStruct((M, Z_WIDTH), F32),
        grid=(M // tm, Z_WIDTH // tn),
        in_specs=[pl.BlockSpec((tm, D_MODEL), lambda i, j: (i, 0)),
                  pl.BlockSpec((nb, 6, D_MODEL), lambda i, j: ((i * tm) // (T * nb), 0, 0)),
                  pl.BlockSpec((1, D_MODEL), lambda i, j: (0, 0)),
                  pl.BlockSpec((tn, D_MODEL), lambda i, j: (j, 0))],
        out_specs=pl.BlockSpec((tm, tn), lambda i, j: (i, j)),
        scratch_shapes=[pltpu.VMEM((tm, D_MODEL), BF16)],
        compiler_params=_cparams(("arbitrary", "arbitrary")),
        name="inproj",
    )(x2, m, g.reshape(1, -1), wt)


def _rope_tables(pos, d):
    rot = d // ROPE_FRACTION
    half = rot // 2
    T = pos.shape[0]
    inv_freq = ROPE_THETA ** (-jnp.arange(half, dtype=F32) / half)
    ang = pos.astype(F32)[:, None] * inv_freq[None, :]
    cos, sin = jnp.cos(ang), jnp.sin(ang)
    c = jnp.concatenate([cos, cos, jnp.ones((T, d - rot), F32)], axis=1)
    a = jnp.concatenate([-sin, jnp.zeros((T, d - half), F32)], axis=1)
    b = jnp.concatenate([jnp.zeros((T, half), F32), sin, jnp.zeros((T, d - rot), F32)], axis=1)
    reps = LANES // d
    return tuple(jnp.tile(t, (1, reps)) for t in (c, a, b))


def _rope128(xs, tabs, half):
    c, a, b = tabs
    return xs * c + pltpu.roll(xs, LANES - half, 1) * a + pltpu.roll(xs, half, 1) * b


def _norm_rope(x, g, tabs, d, norm=True):
    tm, W = x.shape
    half = d // ROPE_FRACTION // 2
    lo = lax.broadcasted_iota(I32, (tm, LANES), 1) < 64
    outs = []
    for s in range(W // LANES):
        xs = x[:, s * LANES:(s + 1) * LANES]
        if norm:
            sq = xs * xs
            if d == LANES:
                r = lax.rsqrt(jnp.sum(sq, axis=1, keepdims=True) * (1.0 / d) + EPS)
            else:
                s_lo = jnp.sum(jnp.where(lo, sq, 0.0), axis=1, keepdims=True)
                s_hi = jnp.sum(jnp.where(lo, 0.0, sq), axis=1, keepdims=True)
                r = jnp.where(lo, lax.rsqrt(s_lo * (1.0 / d) + EPS), lax.rsqrt(s_hi * (1.0 / d) + EPS))
            xs = xs * r * g[:, s * LANES:(s + 1) * LANES]
        outs.append(_rope128(xs, tabs, half))
    return outs[0] if len(outs) == 1 else jnp.concatenate(outs, axis=1)


def _store_head_rows(ref, x):
    tm = x.shape[0]
    for h in range(4):
        ref[pl.ds(h, tm, stride=4), :] = x[:, h * LANES:(h + 1) * LANES]


def _post_kernel(dq_ref, dk_ref, dv_ref, cq_ref, ck_ref, cv_ref, mi_ref,
                 c64_ref, a64_ref, b64_ref, c128_ref, a128_ref, b128_ref,
                 gdq_ref, gdk_ref, gcq_ref, gck_ref, gik_ref,
                 dq_o, dk32_o, dkb_o, dvb_o, cq_o, ck32_o, ckb_o, cvb_o, iq_o, ik32_o, ikr_o, iw_o,
                 dv32_o, cv32_o):
    t64 = (c64_ref[...], a64_ref[...], b64_ref[...])
    t128 = (c128_ref[...], a128_ref[...], b128_ref[...])
    dq = _norm_rope(dq_ref[...], gdq_ref[...], t64, 64)
    dq_o[...] = (dq * (DIFF_DH ** -0.5)).astype(BF16)
    dk = _norm_rope(dk_ref[...], gdk_ref[...], t64, 64)
    dk32_o[...] = dk
    dkb_o[...] = dk.astype(BF16)
    dv = dv_ref[...]
    _store_head_rows(dv32_o, dv)
    dvb_o[...] = dv.astype(BF16)
    cq = _norm_rope(cq_ref[...], gcq_ref[...], t128, 128)
    cq_o[...] = (cq * (DSA_DH ** -0.5)).astype(BF16)
    ck = _norm_rope(ck_ref[...], gck_ref[...], t128, 128)
    _store_head_rows(ck32_o, ck)
    ckb_o[...] = ck.astype(BF16)
    cv = cv_ref[...]
    _store_head_rows(cv32_o, cv)
    cvb_o[...] = cv.astype(BF16)
    mi = mi_ref[...]
    iq = _norm_rope(mi[:, 0:256], None, t64, 64, norm=False)
    iq_o[...] = (iq * (IDX_DH ** -0.5)).astype(BF16)
    mk = mi[:, MISC_IK:MISC_IK + LANES]
    tm = mk.shape[0]
    lane = lax.broadcasted_iota(I32, (tm, LANES), 1)
    lo = lane < 64
    ssq = jnp.sum(jnp.where(lo, mk * mk, 0.0), axis=1, keepdims=True)
    ikn = mk * lax.rsqrt(ssq * (1.0 / IDX_DH) + EPS) * gik_ref[...]
    ik = _rope128(ikn, t64, IDX_DH // ROPE_FRACTION // 2)
    ik32_o[...] = ik[:, 0:IDX_DH]
    ik2 = jnp.where(lo, ik, pltpu.roll(ik, 64, 1))
    ikr_o[...] = jnp.concatenate([ik2, ik2], axis=1).astype(BF16)
    iw_o[...] = jnp.where(lane < IDX_HEADS, pltpu.roll(mk, 64, 1) * (IDX_HEADS ** -0.5), 0.0)


def _post(z, tabs64, tabs128, gains, B, T):
    M = B * T
    tm = min(T, 512)
    assert T % tm == 0
    npos = T // tm

    def zspec(off):
        return pl.BlockSpec((tm, 512), lambda i: (i, off // 512))

    tab_spec = pl.BlockSpec((tm, LANES), lambda i: (i % npos, 0))

    def gspec(w):
        return pl.BlockSpec((1, w), lambda i: (0, 0))

    HR = "head rows"
    outs = [(512, BF16), (512, F32), (512, BF16), (512, BF16),
            (512, BF16), (HR, F32), (512, BF16), (512, BF16),
            (256, BF16), (IDX_DH, F32), (256, BF16), (LANES, F32), (HR, F32), (HR, F32)]

    def oshape(w, dt):
        return jax.ShapeDtypeStruct((M * 4, LANES) if w is HR else (M, w), dt)

    def ospec(w):
        return pl.BlockSpec((tm * 4, LANES) if w is HR else (tm, w), lambda i: (i, 0))

    return pl.pallas_call(
        _post_kernel,
        out_shape=[oshape(w, dt) for w, dt in outs],
        grid=(M // tm,),
        in_specs=[zspec(Z_DQ), zspec(Z_DK), zspec(Z_DV), zspec(Z_CQ), zspec(Z_CK), zspec(Z_CV), zspec(Z_MISC)]
                 + [tab_spec] * 6 + [gspec(512)] * 4 + [gspec(LANES)],
        out_specs=[ospec(w) for w, _ in outs],
        compiler_params=_cparams(("arbitrary",)),
        name="post",
    )(z, z, z, z, z, z, z, *tabs64, *tabs128, *gains)


def _split3(x):
    hi = x.astype(BF16)
    r = x - hi.astype(F32)
    mid = r.astype(BF16)
    lo = (r - mid.astype(F32)).astype(BF16)
    return hi, mid, lo


def _gla_kernel(qk_ref, v_ref, r_ref, ga_ref, wa_ref, ba_ref, g_ref, s0_ref, o_ref, st_ref, st_scr, *, nct):
    ti = pl.program_id(1)

    @pl.when(ti == 0)
    def _():
        st_scr[...] = s0_ref[...]

    C = CHUNK
    W = GLA_HEADS * GLA_DK
    row = lax.broadcasted_iota(I32, (C, C), 0)
    col = lax.broadcasted_iota(I32, (C, C), 1)
    tri = col <= row
    tri_bf = tri.astype(BF16)
    lane = lax.broadcasted_iota(I32, (1, W), 1)
    hmask = [(lane // GLA_DK) == h for h in range(GLA_HEADS)]
    rowi = lax.broadcasted_iota(I32, (C, W), 0)
    wa = wa_ref[...]
    ba = ba_ref[...]
    g = g_ref[...]

    def chunk(c, carry):
        rows = pl.ds(pl.multiple_of(c * C, C), C)
        qk = qk_ref[rows, :]
        q = qk[:, :W] * (GLA_DK ** -0.5)
        k = qk[:, W:]
        v = v_ref[rows, :].astype(BF16)
        pre = _dot(ga_ref[rows, :].astype(BF16), wa) + ba
        la = (jnp.minimum(pre, 0.0) - jnp.log(1.0 + jnp.exp(-jnp.abs(pre)))) * (1.0 / GLA_GATE_TAU)
        hi, mid, lo = _split3(la)
        b = _dot(tri_bf, hi) + _dot(tri_bf, mid) + _dot(tri_bf, lo)
        st = st_scr[...]
        st_bf = st.astype(BF16)
        qe = q * jnp.exp(b)
        b_end = b[C - 1:C, :]
        kend = k * jnp.exp(b_end - b)
        att_parts = [[] for _ in range(GLA_HEADS)]
        for s in range(C // GLA_SUB):
            r0 = s * GLA_SUB
            br = b[r0:r0 + 1, :]
            qs = q[r0:r0 + GLA_SUB, :] * jnp.exp(b[r0:r0 + GLA_SUB, :] - br)
            ks = k * jnp.exp(br - b)
            if r0 + GLA_SUB < C:
                ks = jnp.where(rowi < r0 + GLA_SUB, ks, 0.0)
            ks = ks.astype(BF16)
            for h in range(GLA_HEADS):
                att_parts[h].append(_dot_nt(jnp.where(hmask[h], qs, 0.0).astype(BF16), ks))
        upd = None
        for h in range(GLA_HEADS):
            hs = slice(h * GLA_DV, (h + 1) * GLA_DV)
            att = jnp.where(tri, jnp.concatenate(att_parts[h], axis=0), 0.0)
            o = _dot(att.astype(BF16), v[:, hs]) + _dot_nt(jnp.where(hmask[h], qe, 0.0).astype(BF16), st_bf)
            u = _dot_tn(v[:, hs], jnp.where(hmask[h], kend, 0.0).astype(BF16))
            upd = u if upd is None else upd + u
            y = o * lax.rsqrt(jnp.mean(o * o, axis=-1, keepdims=True) + EPS) * g[:, hs]
            o_ref[rows, hs] = (y * _silu(r_ref[rows, hs])).astype(BF16)
        st_scr[...] = st * jnp.exp(b_end) + upd
        return carry

    lax.fori_loop(0, nct, chunk, 0)

    @pl.when(ti == pl.num_programs(1) - 1)
    def _():
        st_ref[...] = st_scr[...]


def _gla(z, wa, ba, g, s0t, B, T):
    M = B * T
    tt = min(T, 512)
    assert T % tt == 0 and tt % CHUNK == 0
    nt = T // tt
    W = GLA_HEADS * GLA_DK
    return pl.pallas_call(
        functools.partial(_gla_kernel, nct=tt // CHUNK),
        out_shape=[jax.ShapeDtypeStruct((M, 512), BF16),
                   jax.ShapeDtypeStruct((B, GLA_DV, W), F32)],
        grid=(B, nt),
        in_specs=[pl.BlockSpec((tt, 512), lambda b, t: (b * nt + t, Z_GQK // 512)),
                  pl.BlockSpec((tt, 512), lambda b, t: (b * nt + t, Z_GV // 512)),
                  pl.BlockSpec((tt, 512), lambda b, t: (b * nt + t, Z_GR // 512)),
                  pl.BlockSpec((tt, LANES), lambda b, t: (b * nt + t, (Z_MISC + MISC_GA) // LANES)),
                  pl.BlockSpec((LANES, W), lambda b, t: (0, 0)),
                  pl.BlockSpec((1, W), lambda b, t: (0, 0)),
                  pl.BlockSpec((1, 512), lambda b, t: (0, 0)),
                  pl.BlockSpec((None, GLA_DV, W), lambda b, t: (b, 0, 0))],
        out_specs=[pl.BlockSpec((tt, 512), lambda b, t: (b * nt + t, 0)),
                   pl.BlockSpec((None, GLA_DV, W), lambda b, t: (b, 0, 0))],
        scratch_shapes=[pltpu.VMEM((GLA_DV, W), F32)],
        compiler_params=_cparams(("arbitrary", "arbitrary")),
        name="gla",
    )(z, z, z, z, wa, ba, g, s0t)


def _diff_lambda(lam_ref, lam_init):
    lv = lam_ref[...]
    a = jnp.sum(lv[0:1, :] * lv[1:2, :], axis=1, keepdims=True)
    b = jnp.sum(lv[2:3, :] * lv[3:4, :], axis=1, keepdims=True)
    return jnp.exp(a) - jnp.exp(b) + lam_init


def _diff_finish(o0, o1, lam, g, lam_init):
    o = o0 - lam * o1
    return o * lax.rsqrt(jnp.mean(o * o, axis=-1, keepdims=True) + EPS) * g * (1.0 - lam_init)


def _diff_prompt_kernel(qi_ref, kj_ref, q_ref, k_ref, v_ref, lam_ref, g_ref, o_ref,
                        q2_scr, m_scr, l_scr, acc_scr, *, lam_init, tq):
    n = pl.program_id(2)
    qi = qi_ref[n]
    kj = kj_ref[n]

    @pl.when(kj == 0)
    def _():
        q = q_ref[...]
        lane = lax.broadcasted_iota(I32, (1, LANES), 1)
        q2_scr[0:tq, :] = jnp.where(lane < DIFF_DH, q, jnp.zeros_like(q))
        q2_scr[tq:2 * tq, :] = jnp.where(lane < DIFF_DH, jnp.zeros_like(q), q)
        m_scr[...] = jnp.full(m_scr.shape, NEG_INF, F32)
        l_scr[...] = jnp.zeros(l_scr.shape, F32)
        acc_scr[...] = jnp.zeros(acc_scr.shape, F32)

    def step(diagonal):
        s = _dot_nt(q2_scr[...], k_ref[...])
        if diagonal:
            qc = lax.broadcasted_iota(I32, (tq, tq), 0) // CHUNK
            kc = lax.broadcasted_iota(I32, (tq, tq), 1) // CHUNK
            vis = kc <= qc
            s = jnp.where(jnp.concatenate([vis, vis], axis=0), s, NEG_INF)
        m_prev = m_scr[...]
        m_new = jnp.maximum(m_prev, jnp.max(s, axis=1, keepdims=True))
        alpha = jnp.exp(m_prev - m_new)
        p = jnp.exp((s - jnp.concatenate([m_new] * (tq // LANES), axis=1)).astype(BF16))
        l_scr[...] = alpha * l_scr[...] + jnp.sum(p.astype(F32), axis=1, keepdims=True)
        acc_scr[...] = alpha * acc_scr[...] + _dot(p, v_ref[...])
        m_scr[...] = m_new

    @pl.when(kj < qi)
    def _():
        step(False)

    @pl.when(kj == qi)
    def _():
        step(True)
        lam = _diff_lambda(lam_ref, lam_init)
        o = acc_scr[...] / l_scr[...]
        o_ref[...] = _diff_finish(o[0:tq], o[tq:2 * tq], lam, g_ref[...], lam_init).astype(BF16)


def _diff_prompt(dq, dk, dv, lamv, g, B, T, lam_init):
    M = B * T
    tq = min(T, 512)
    nq = T // tq
    pairs = [(i, j) for i in range(nq) for j in range(i + 1)]
    qi = jnp.asarray([p[0] for p in pairs], I32)
    kj = jnp.asarray([p[1] for p in pairs], I32)
    grid_spec = pltpu.PrefetchScalarGridSpec(
        num_scalar_prefetch=2,
        grid=(B, DIFF_HEADS, len(pairs)),
        in_specs=[pl.BlockSpec((tq, LANES), lambda b, h, n, qi, kj: (b * nq + qi[n], h)),
                  pl.BlockSpec((tq, LANES), lambda b, h, n, qi, kj: (b * nq + kj[n], h)),
                  pl.BlockSpec((tq, LANES), lambda b, h, n, qi, kj: (b * nq + kj[n], h)),
                  pl.BlockSpec((4, DIFF_DH), lambda b, h, n, qi, kj: (0, 0)),
                  pl.BlockSpec((1, LANES), lambda b, h, n, qi, kj: (0, 0))],
        out_specs=pl.BlockSpec((tq, LANES), lambda b, h, n, qi, kj: (b * nq + qi[n], h)),
        scratch_shapes=[pltpu.VMEM((2 * tq, LANES), BF16), pltpu.VMEM((2 * tq, LANES), F32),
                        pltpu.VMEM((2 * tq, LANES), F32), pltpu.VMEM((2 * tq, LANES), F32)])
    return pl.pallas_call(
        functools.partial(_diff_prompt_kernel, lam_init=lam_init, tq=tq),
        out_shape=jax.ShapeDtypeStruct((M, 512), BF16),
        grid_spec=grid_spec,
        compiler_params=_cparams(("arbitrary", "arbitrary", "arbitrary")),
        name="diff_prompt",
    )(qi, kj, dq, dk, dv, lamv, g)


def _diff_sample_kernel(q_ref, kn_ref, vn_ref, kp_ref, vp_ref, lam_ref, g_ref, o_ref, *, lam_init, past_len):
    T = q_ref.shape[0]
    lam = _diff_lambda(lam_ref, lam_init)
    lane = lax.broadcasted_iota(I32, (1, LANES), 1)
    qpos = past_len + lax.broadcasted_iota(I32, (T, T), 0)
    kpos = past_len + lax.broadcasted_iota(I32, (T, T), 1)
    vis_new = (kpos // CHUNK) <= (qpos // CHUNK)
    P = past_len
    for h in range(DIFF_HEADS):
        hs = slice(h * LANES, (h + 1) * LANES)
        q = q_ref[:, hs]
        kpt = kp_ref[h].reshape(2 * DIFF_DH, P).astype(BF16)
        vp = vp_ref[pl.ds(h, P, stride=DIFF_HEADS), :].astype(BF16)
        kn = kn_ref[:, hs]
        vn = vn_ref[:, hs]
        outs = []
        for c in range(2):
            qc = jnp.where((lane < DIFF_DH) == (c == 0), q, jnp.zeros_like(q))
            sp = _dot(qc, kpt)
            sn = jnp.where(vis_new, _dot_nt(qc, kn), NEG_INF)
            m = jnp.maximum(jnp.max(sp, axis=1, keepdims=True), jnp.max(sn, axis=1, keepdims=True))
            pp = jnp.exp((sp - m).astype(BF16))
            pn = jnp.exp((sn - m).astype(BF16))
            l = jnp.sum(pp.astype(F32), axis=1, keepdims=True) + jnp.sum(pn.astype(F32), axis=1, keepdims=True)
            outs.append((_dot(pp, vp) + _dot(pn, vn)) / l)
        o_ref[:, hs] = _diff_finish(outs[0], outs[1], lam, g_ref[...], lam_init).astype(BF16)


def _diff_sample(dq, dk, dv, past_kt, past_v, layer, lamv, g, B, T, lam_init):
    M = B * T
    P = past_kt.shape[-1]
    return pl.pallas_call(
        functools.partial(_diff_sample_kernel, lam_init=lam_init, past_len=P),
        out_shape=jax.ShapeDtypeStruct((M, 512), BF16),
        grid=(B,),
        in_specs=[pl.BlockSpec((T, 512), lambda b: (b, 0)),
                  pl.BlockSpec((T, 512), lambda b: (b, 0)),
                  pl.BlockSpec((T, 512), lambda b: (b, 0)),
                  pl.BlockSpec((None, None, DIFF_HEADS, 2, DIFF_DH, P), lambda b: (layer, b, 0, 0, 0, 0)),
                  pl.BlockSpec((None, None, P * DIFF_HEADS, DIFF_DV), lambda b: (layer, b, 0, 0)),
                  pl.BlockSpec((4, DIFF_DH), lambda b: (0, 0)),
                  pl.BlockSpec((1, LANES), lambda b: (0, 0))],
        out_specs=pl.BlockSpec((T, 512), lambda b: (b, 0)),
        compiler_params=_cparams(("arbitrary",)),
        name="diff_sample",
    )(dq, dk, dv, past_kt, past_v, lamv, g)


INT_MIN = -2 ** 31


def _idx_score(iq, iw, ikr, transposed=False):
    lane = lax.broadcasted_iota(I32, (1, IDX_HEADS * IDX_DH), 1)
    sc = None
    for i in range(IDX_HEADS):
        iqi = jnp.where((lane // IDX_DH) == i, iq, jnp.zeros_like(iq))
        lg = _dot(iqi, ikr) if transposed else _dot_nt(iqi, ikr)
        t = iw[:, i:i + 1] * jnp.maximum(lg, 0.0)
        sc = t if sc is None else sc + t
    return jnp.where(sc == 0.0, 0.0, sc)


def _order_key(score):
    bits = pltpu.bitcast(score, I32)
    return jnp.where(bits < 0, bits ^ 0x7FFFFFFF, bits)


def _count(mask):
    return jnp.sum(mask.astype(F32), axis=1, keepdims=True)


def _f32_key(x):
    b = int(np.float32(x).view(np.int32))
    return b ^ 0x7FFFFFFF if b < 0 else b


def _key_value(k):
    return pltpu.bitcast(jnp.where(k < 0, k ^ 0x7FFFFFFF, k), F32)


def _topk_threshold(key_refs, smax, smin, n_sel, n_keys, probe_masked):
    tq = key_refs[0].shape[0]
    n = float(n_sel)

    def count_ge(cand):
        tot = None
        for kr in key_refs:
            c = _count(kr[...] >= cand)
            tot = c if tot is None else tot + c
        return tot

    def update(st, cand):
        lo, cnt_lo, hi, cnt_hi = st
        c = count_ge(cand)
        ge = c >= n
        up = ge & (cand > lo)
        dn = jnp.logical_not(ge) & (cand < hi)
        return (jnp.where(up, cand, lo), jnp.where(up, c, cnt_lo),
                jnp.where(dn, cand, hi), jnp.where(dn, c, cnt_hi))

    def settled(st):
        lo, cnt_lo, hi, _ = st
        return (cnt_lo == n) | ((hi - 1) == lo)

    def unsettled_any(st):
        return jnp.max(jnp.where(settled(st), 0.0, 1.0)) > 0.0

    col = lambda v, dt: jnp.full((tq, 1), v, dt)
    kmax = _order_key(smax)
    hi0 = jnp.where(kmax == 2 ** 31 - 1, kmax, kmax + 1)
    st = (col(INT_MIN, I32), col(float(n_keys), F32), hi0, col(0.0, F32))
    fixed = ((_f32_key(NEG_INF), _f32_key(NEG_INF) + 1) if probe_masked else ()) + (0, 1)
    for c in fixed:
        st = update(st, col(c, I32))
    st = update(st, _order_key(smin))
    log_n = math.log(n)

    def cond(carry):
        it, _, go = carry
        return jnp.logical_and(it < 100, go)

    def body(carry):
        it, st, _ = carry
        lo, cnt_lo, hi, cnt_hi = st
        a = jnp.log(cnt_lo + 0.5)
        frac = (a - log_n) / jnp.maximum(a - jnp.log(cnt_hi + 0.5), 1e-9)
        v_lo = _key_value(lo)
        guess = _order_key(v_lo + frac * (_key_value(hi) - v_lo))
        mid = (lo >> 1) + (hi >> 1) + (lo & hi & 1)
        cand = jnp.minimum(jnp.maximum(jnp.where(lax.rem(it, 3) == 2, mid, guess), lo + 1), hi - 1)
        st = update(st, jnp.where(settled(st), lo, cand))
        return it + 1, st, unsettled_any(st)

    _, st, _ = lax.while_loop(cond, body, (jnp.int32(0), st, unsettled_any(st)))
    lo, cnt_lo, hi, cnt_hi = st
    tied = ((hi - 1) == lo) & (cnt_lo > n)
    return lo, jnp.where(tied, n - cnt_hi, float(n_keys))


def _row_extremes(score, vis):
    if vis is None:
        return jnp.max(score, axis=1, keepdims=True), jnp.min(score, axis=1, keepdims=True)
    return (jnp.max(jnp.where(vis, score, -jnp.inf), axis=1, keepdims=True),
            jnp.min(jnp.where(vis, score, jnp.inf), axis=1, keepdims=True))


def _selection_bias(key_refs, bias_refs, vis_fns, smax, smin, n_sel, n_keys, probe_masked):
    t, need = _topk_threshold(key_refs, smax, smin, n_sel, n_keys, probe_masked)
    tq = key_refs[0].shape[0]
    base = jnp.zeros((tq, 1), F32)
    for kr, br, vis_fn in zip(key_refs, bias_refs, vis_fns):
        W = kr.shape[1]
        bw = min(W, LANES)
        assert W % bw == 0
        tri = (lax.broadcasted_iota(I32, (bw, bw), 0) <= lax.broadcasted_iota(I32, (bw, bw), 1)).astype(BF16)
        ones_tri = jnp.concatenate([jnp.ones((bw, bw), BF16), tri], axis=0)
        earlier = jnp.zeros((tq, bw), F32)
        for j in range(W // bw):
            cs = slice(j * bw, (j + 1) * bw)
            key = kr[:, cs]
            tie = key == t
            tb = jnp.where(tie, 1.0, 0.0)
            rank = base + _dot(jnp.concatenate([earlier, tb], axis=1).astype(BF16), ones_tri)
            sel = (key > t) | (tie & (rank <= need))
            if vis_fn is not None:
                sel = sel & vis_fn(j * bw, bw)
            br[:, cs] = jnp.where(sel, 0.0, NEG_INF)
            earlier = earlier + tb
        base = base + jnp.sum(earlier, axis=1, keepdims=True)


def _dsa_attend(q_ref, bias_refs, k_loads, v_loads, o_ref):
    for h in range(DSA_HEADS):
        hs = slice(h * DSA_DH, (h + 1) * DSA_DH)
        q = q_ref[:, hs]
        ss = [_dot_nt(q, kl(hs)) + br[...] for kl, br in zip(k_loads, bias_refs)]
        m = None
        for s in ss:
            ms = jnp.max(s, axis=1, keepdims=True)
            m = ms if m is None else jnp.maximum(m, ms)
        l = None
        o = None
        for s, vl in zip(ss, v_loads):
            p = jnp.exp(s - m)
            ls = jnp.sum(p, axis=1, keepdims=True)
            os_ = _dot(p.astype(BF16), vl(hs))
            l = ls if l is None else l + ls
            o = os_ if o is None else o + os_
        o_ref[:, hs] = (o / l).astype(BF16)


def _dsa_prompt_kernel(q_ref, iq_ref, iw_ref, k_ref, v_ref, ikr_ref, o_ref, key_scr, bias_scr,
                       *, tq, q_tile0, n_sel, probe_masked):
    tk = k_ref.shape[0]
    qt = q_tile0 + pl.program_id(1)
    score = _idx_score(iq_ref[...], iw_ref[...], ikr_ref[...])
    qpos = qt * tq + lax.broadcasted_iota(I32, (tq, tk), 0)
    kidx = lax.broadcasted_iota(I32, (tq, tk), 1)
    vis = (kidx // CHUNK) <= (qpos // CHUNK)
    key_scr[...] = _order_key(jnp.where(vis, score, NEG_INF))
    smax, smin = _row_extremes(score, vis)
    qchunk = (qt * tq + lax.broadcasted_iota(I32, (tq, 1), 0)) // CHUNK

    def vis_block(off, bw):
        return ((off + lax.broadcasted_iota(I32, (tq, bw), 1)) // CHUNK) <= qchunk

    _selection_bias([key_scr], [bias_scr], [vis_block], smax, smin, n_sel, tk, probe_masked)
    _dsa_attend(q_ref, [bias_scr], [lambda hs: k_ref[:, hs]], [lambda hs: v_ref[:, hs]], o_ref)


def _dsa_prompt(cq, iq, iw, ck, cv, ikr, B, T):
    tq = min(T, 128)
    nq = T // tq
    n_sel = min(DSA_TOPK_MAX, T // 4)
    ng = next(g for g in (32, 16, 8, 4, 2, 1) if nq % g == 0)
    tpg = nq // ng
    outs = []
    for gi in range(ng):
        tk = (gi + 1) * tpg * tq

        def kv_spec(w, tk=tk):
            return pl.BlockSpec((None, tk, w), lambda b, i: (b, 0, 0))

        def q_spec(w, gi=gi):
            return pl.BlockSpec((None, tq, w), lambda b, i: (b, gi * tpg + i, 0))

        out = pl.pallas_call(
            functools.partial(_dsa_prompt_kernel, tq=tq, q_tile0=gi * tpg, n_sel=n_sel,
                              probe_masked=gi * tpg * tq + CHUNK < n_sel),
            out_shape=jax.ShapeDtypeStruct((B, tpg * tq, 512), BF16),
            grid=(B, tpg),
            in_specs=[q_spec(512), q_spec(256), q_spec(LANES), kv_spec(512), kv_spec(512), kv_spec(256)],
            out_specs=pl.BlockSpec((None, tq, 512), lambda b, i: (b, i, 0)),
            scratch_shapes=[pltpu.VMEM((tq, tk), I32), pltpu.VMEM((tq, tk), F32)],
            compiler_params=_cparams(("arbitrary", "arbitrary"), VMEM_LIMIT_DSA),
            name=f"dsa_prompt_{gi}",
        )(cq.reshape(B, T, 512), iq.reshape(B, T, 256), iw.reshape(B, T, LANES),
          ck.reshape(B, T, 512), cv.reshape(B, T, 512), ikr.reshape(B, T, 256))
        outs.append(out)
    o = outs[0] if ng == 1 else jnp.concatenate(outs, axis=1)
    return o.reshape(B * T, 512)


def _dsa_sample_kernel(q_ref, iq_ref, iw_ref, kn_ref, vn_ref, ikrn_ref, kp_ref, vp_ref, ikp_ref, o_ref,
                       keyp_scr, keyn_scr, biasp_scr, biasn_scr, *, past_len, n_sel):
    T = q_ref.shape[0]
    P = past_len
    iq = iq_ref[...]
    iw = iw_ref[...]
    ikt = ikp_ref[...].astype(BF16)
    ikrp = jnp.concatenate([ikt] * IDX_HEADS, axis=0)
    score_p = _idx_score(iq, iw, ikrp, transposed=True)
    keyp_scr[...] = _order_key(score_p)
    pmax, pmin = _row_extremes(score_p, None)
    qpos = P + lax.broadcasted_iota(I32, (T, T), 0)
    kpos = P + lax.broadcasted_iota(I32, (T, T), 1)
    vis_n = (kpos // CHUNK) <= (qpos // CHUNK)
    score_n = _idx_score(iq, iw, ikrn_ref[...])
    keyn_scr[...] = _order_key(jnp.where(vis_n, score_n, NEG_INF))
    nmax, nmin = _row_extremes(score_n, vis_n)
    _selection_bias([keyp_scr, keyn_scr], [biasp_scr, biasn_scr], [None, lambda off, bw: vis_n[:, off:off + bw]],
                    jnp.maximum(pmax, nmax), jnp.minimum(pmin, nmin),
                    n_sel, P + T, probe_masked=P + min(T, CHUNK) < n_sel)

    def past_head(ref):
        return lambda hs: ref[pl.ds(hs.start // DSA_DH, P, stride=DSA_HEADS), :].astype(BF16)

    _dsa_attend(q_ref, [biasp_scr, biasn_scr],
                [past_head(kp_ref), lambda hs: kn_ref[:, hs]],
                [past_head(vp_ref), lambda hs: vn_ref[:, hs]], o_ref)


def _dsa_sample(cq, iq, iw, ck, cv, ikr, past_k, past_v, past_ikt, layer, B, T):
    M = B * T
    P = past_ikt.shape[-1]
    n_sel = min(DSA_TOPK_MAX, (P + T) // 4)

    def rspec(w):
        return pl.BlockSpec((T, w), lambda b: (b, 0))

    def pspec(r, w):
        return pl.BlockSpec((None, None, r, w), lambda b: (layer, b, 0, 0))

    return pl.pallas_call(
        functools.partial(_dsa_sample_kernel, past_len=P, n_sel=n_sel),
        out_shape=jax.ShapeDtypeStruct((M, 512), BF16),
        grid=(B,),
        in_specs=[rspec(512), rspec(256), rspec(LANES), rspec(512), rspec(512), rspec(256),
                  pspec(P * DSA_HEADS, DSA_DH), pspec(P * DSA_HEADS, DSA_DH), pspec(IDX_DH, P)],
        out_specs=rspec(512),
        scratch_shapes=[pltpu.VMEM((T, P), I32), pltpu.VMEM((T, T), I32),
                        pltpu.VMEM((T, P), F32), pltpu.VMEM((T, T), F32)],
        compiler_params=_cparams(("arbitrary",)),
        name="dsa_sample",
    )(cq, iq, iw, ck, cv, ikr, past_k, past_v, past_ikt)


def _merge_kernel(og_ref, od_ref, oc_ref, g0_ref, g1_ref, g2_ref, x_ref, m_ref,
                  wg_ref, wd_ref, wc_ref, wo_ref, gn_ref, wr_ref, br_ref,
                  x1_ref, h2_ref, ti_ref, tw_ref, rk_ref, cnt_ref, run_scr, *, nb):
    merged = (jax.nn.sigmoid(g0_ref[...]) * _dot(og_ref[...], wg_ref[...])
              + jax.nn.sigmoid(g1_ref[...]) * _dot(od_ref[...], wd_ref[...])
              + jax.nn.sigmoid(g2_ref[...]) * _dot(oc_ref[...], wc_ref[...]))
    mix = _dot(merged.astype(BF16), wo_ref[...])
    m = m_ref[...]
    x1 = x_ref[...] + _gate_rows(mix, m, 2, nb)
    x1_ref[...] = x1
    xn = x1 * lax.rsqrt(jnp.mean(x1 * x1, axis=-1, keepdims=True) + EPS) * gn_ref[...]
    h2 = _modulate(xn, m, 3, 4, nb)
    _rows_to_tiles(h2_ref, h2)
    lg = _dot(h2.astype(BF16), wr_ref[...]) + br_ref[...]
    tm = lg.shape[0]
    lane = lax.broadcasted_iota(I32, (tm, LANES), 1).astype(F32)
    vals, idxs = [], []
    for _ in range(TOP_K):
        mx = jnp.max(lg, axis=1, keepdims=True)
        ix = jnp.min(jnp.where(lg == mx, lane, float(LANES)), axis=1, keepdims=True)
        vals.append(mx)
        idxs.append(ix)
        lg = jnp.where(lane == ix, -jnp.inf, lg)
    es = [jnp.exp(v - vals[0]) for v in vals]
    den = es[0] + es[1] + es[2] + es[3]
    @pl.when(pl.program_id(0) == 0)
    def _():
        run_scr[...] = jnp.zeros(run_scr.shape, F32)

    hot = [lane == ix for ix in idxs]
    tot = (hot[0].astype(F32) + hot[1].astype(F32)) + (hot[2].astype(F32) + hot[3].astype(F32))
    strict = (lax.broadcasted_iota(I32, (tm, tm), 1) < lax.broadcasted_iota(I32, (tm, tm), 0)).astype(BF16)
    before = run_scr[...] + _dot(strict, tot.astype(BF16))
    ti = jnp.zeros((tm, LANES), F32)
    tw = jnp.zeros((tm, LANES), F32)
    rk = jnp.zeros((tm, LANES), F32)
    for r in range(TOP_K):
        ti = jnp.where(lane == float(r), idxs[r], ti)
        tw = jnp.where(lane == float(r), es[r] / den, tw)
        rk = jnp.where(lane == float(r), jnp.sum(jnp.where(hot[r], before, 0.0), axis=1, keepdims=True), rk)
    ti_ref[...] = ti.astype(I32)
    tw_ref[...] = tw
    rk_ref[...] = rk.astype(I32)
    run_new = run_scr[...] + jnp.sum(tot, axis=0, keepdims=True)
    run_scr[...] = run_new
    cnt_ref[...] = run_new.astype(I32)


def _merge(og, od, oc, z, x2, m, wts, B, T):
    M = B * T
    tm, nb = _row_tiling(B, T, 512)
    wg, wd, wc, wo, gn, wr, br = wts

    def rspec(w):
        return pl.BlockSpec((tm, w), lambda i: (i, 0))

    def gspec(k):
        return pl.BlockSpec((tm, D_MODEL), lambda i: (i, Z_GATES // D_MODEL + k))

    def wspec(r, c):
        return pl.BlockSpec((r, c), lambda i: (0, 0))

    return pl.pallas_call(
        functools.partial(_merge_kernel, nb=nb),
        out_shape=[jax.ShapeDtypeStruct((M, D_MODEL), F32), jax.ShapeDtypeStruct((M * ROW_TILE, LANES), F32),
                   jax.ShapeDtypeStruct((M, LANES), I32), jax.ShapeDtypeStruct((M, LANES), F32),
                   jax.ShapeDtypeStruct((M, LANES), I32), jax.ShapeDtypeStruct((1, LANES), I32)],
        grid=(M // tm,),
        in_specs=[rspec(512), rspec(512), rspec(512), gspec(0), gspec(1), gspec(2), rspec(D_MODEL),
                  pl.BlockSpec((nb, 6, D_MODEL), lambda i: ((i * tm) // (T * nb), 0, 0)),
                  wspec(512, D_MODEL), wspec(512, D_MODEL), wspec(512, D_MODEL), wspec(D_MODEL, D_MODEL),
                  wspec(1, D_MODEL), wspec(D_MODEL, LANES), wspec(1, LANES)],
        out_specs=[rspec(D_MODEL), pl.BlockSpec((tm * ROW_TILE, LANES), lambda i: (i, 0)),
                   rspec(LANES), rspec(LANES), rspec(LANES),
                   pl.BlockSpec((1, LANES), lambda i: (0, 0))],
        scratch_shapes=[pltpu.VMEM((1, LANES), F32)],
        compiler_params=_cparams(("arbitrary",)),
        name="merge_router",
    )(og, od, oc, z, z, z, x2, m, wg, wd, wc, wo, gn, wr, br)


DMA_UNROLL = 8


def _w1prep_kernel(w_ref, sel_ref, g_ref, u_ref):
    sel = sel_ref[...]
    for c in range(w_ref.shape[1] // 256):
        r = _dot(w_ref[:, c * 256:(c + 1) * 256].astype(BF16), sel)
        g_ref[:, c * LANES:(c + 1) * LANES] = r[:, :LANES].astype(BF16)
        u_ref[:, c * LANES:(c + 1) * LANES] = r[:, LANES:].astype(BF16)


def _w1prep(w1_all, layer):
    E = w1_all.shape[1]
    tr = 512
    j = jnp.arange(256, dtype=I32)
    src = jnp.where(j < LANES, 2 * j, 2 * (j - LANES) + 1)
    sel = (jnp.arange(256, dtype=I32)[:, None] == src[None, :]).astype(BF16)
    return pl.pallas_call(
        _w1prep_kernel,
        out_shape=[jax.ShapeDtypeStruct((E, D_MODEL, D_FF), BF16)] * 2,
        grid=(E, D_MODEL // tr),
        in_specs=[pl.BlockSpec((None, None, tr, 2 * D_FF), lambda e, r: (layer, e, r, 0)),
                  pl.BlockSpec((256, 256), lambda e, r: (0, 0))],
        out_specs=[pl.BlockSpec((None, tr, D_FF), lambda e, r: (e, r, 0))] * 2,
        compiler_params=_cparams(("arbitrary", "arbitrary")),
        name="w1prep",
    )(w1_all, sel)


def _rows_to_tiles(ref, x):
    n = x.shape[0]
    for c in range(ROW_TILE):
        ref[pl.ds(c, n, stride=ROW_TILE), :] = x[:, c * LANES:(c + 1) * LANES]


def _tiles_to_rows(ref, base, n):
    return jnp.concatenate([ref[pl.ds(base + c, n, stride=ROW_TILE), :] for c in range(ROW_TILE)], axis=1)


def _tile_rows(ref, r):
    return ref.at[pl.ds(pl.multiple_of(r * ROW_TILE, ROW_TILE), ROW_TILE), :]


def _dispatch_kernel(pos_ref, h_ref, xs_in, xs_out, sem, *, tm):
    del xs_in

    tokens = DMA_UNROLL // TOP_K

    def body(j, c):
        for tt in range(tokens):
            src = _tile_rows(h_ref, j * tokens + tt)
            for k in range(TOP_K):
                dst = _tile_rows(xs_out, pos_ref[0, 0, (j * tokens + tt) * TOP_K + k])
                pltpu.make_async_copy(src, dst, sem).start()
        return c

    lax.fori_loop(0, tm // tokens, body, 0)
    for _ in range(TOP_K):
        pltpu.make_async_copy(h_ref, xs_out.at[pl.ds(0, tm * ROW_TILE), :], sem).wait()


def _dispatch(h2t, pos, n_rows, tm):
    M = h2t.shape[0] // ROW_TILE
    nt = M // tm
    return pl.pallas_call(
        functools.partial(_dispatch_kernel, tm=tm),
        out_shape=jax.ShapeDtypeStruct((n_rows * ROW_TILE, LANES), F32),
        grid=(nt,),
        in_specs=[pl.BlockSpec((1, 1, TOP_K * tm), lambda i: (i, 0, 0), memory_space=pltpu.SMEM),
                  pl.BlockSpec((tm * ROW_TILE, LANES), lambda i: (i, 0)),
                  pl.BlockSpec(memory_space=pl.ANY)],
        out_specs=pl.BlockSpec(memory_space=pl.ANY),
        scratch_shapes=[pltpu.SemaphoreType.DMA],
        input_output_aliases={2: 0},
        compiler_params=_cparams(("arbitrary",)),
        name="moe_dispatch",
    )(pos.reshape(nt, 1, TOP_K * tm), h2t, jnp.zeros((n_rows * ROW_TILE, LANES), F32))


def _ffn_kernel(be_ref, nu_ref, x_ref, w1g_ref, b1g_ref, w1u_ref, b1u_ref, w2_ref, b2_ref, o_ref, *, bm):
    @pl.when(pl.program_id(0) < nu_ref[0])
    def _():
        x = _tiles_to_rows(x_ref, 0, bm).astype(BF16)
        g = jnp.minimum(_dot(x, w1g_ref[...]) + b1g_ref[...], SWIGLU_LIMIT)
        u = jnp.clip(_dot(x, w1u_ref[...]) + b1u_ref[...], -SWIGLU_LIMIT, SWIGLU_LIMIT)
        a = g * jax.nn.sigmoid(SWIGLU_ALPHA * g) * (u + 1.0)
        _rows_to_tiles(o_ref, _dot(a.astype(BF16), w2_ref[...]) + b2_ref[...])

    @pl.when(pl.program_id(0) >= nu_ref[0])
    def _():
        o_ref[...] = jnp.zeros(o_ref.shape, F32)


def _ffn(xs, block_e, n_used, w1g, b1g, w1u, b1u, w2_all, b2, layer, bm):
    n_rows = xs.shape[0] // ROW_TILE
    nblk = n_rows // bm

    def wspec(r, c):
        return pl.BlockSpec((None, r, c), lambda i, be, nu: (be[i], 0, 0))

    grid_spec = pltpu.PrefetchScalarGridSpec(
        num_scalar_prefetch=2,
        grid=(nblk,),
        in_specs=[pl.BlockSpec((bm * ROW_TILE, LANES), lambda i, be, nu: (jnp.minimum(i, nu[0] - 1), 0)),
                  wspec(D_MODEL, D_FF), wspec(1, D_FF), wspec(D_MODEL, D_FF), wspec(1, D_FF),
                  pl.BlockSpec((None, None, D_FF, D_MODEL), lambda i, be, nu: (layer, be[i], 0, 0)),
                  wspec(1, D_MODEL)],
        out_specs=pl.BlockSpec((bm * ROW_TILE, LANES), lambda i, be, nu: (i, 0)))
    return pl.pallas_call(
        functools.partial(_ffn_kernel, bm=bm),
        out_shape=jax.ShapeDtypeStruct((n_rows * ROW_TILE, LANES), F32),
        grid_spec=grid_spec,
        compiler_params=_cparams(("arbitrary",)),
        name="moe_ffn",
    )(block_e, n_used, xs, w1g, b1g, w1u, b1u, w2_all, b2)


def _combine_kernel(pos_ref, posn_ref, ys_hbm, tw_ref, x_ref, m_ref, o_ref, ybuf, sem, *, tm, nb):
    i = pl.program_id(0)
    nt = pl.num_programs(0)
    rows = TOP_K * tm * ROW_TILE
    slot = lax.rem(i, 2)

    def request(idx_ref, s):
        base = s * (TOP_K * tm)

        def body(j, c):
            for u in range(DMA_UNROLL):
                n = j * DMA_UNROLL + u
                pltpu.make_async_copy(_tile_rows(ys_hbm, idx_ref[0, 0, n]), _tile_rows(ybuf, base + n),
                                      sem.at[s]).start()
            return c

        lax.fori_loop(0, TOP_K * tm // DMA_UNROLL, body, 0)

    @pl.when(i == 0)
    def _():
        request(pos_ref, 0)

    @pl.when(i + 1 < nt)
    def _():
        request(posn_ref, 1 - slot)

    off = pl.multiple_of(slot * rows, rows)
    pltpu.make_async_copy(ys_hbm.at[pl.ds(0, rows), :], ybuf.at[pl.ds(off, rows), :], sem.at[slot]).wait()
    tw = tw_ref[...]
    y = None
    for k in range(TOP_K):
        yk = tw[:, k:k + 1] * _tiles_to_rows(ybuf, off + k * tm * ROW_TILE, tm)
        y = yk if y is None else y + yk
    o_ref[...] = x_ref[...] + _gate_rows(y, m_ref[...], 5, nb)


def _combine(pos_t, ys, top_w, x1, m, B, T, tm, nb):
    M = B * T
    nt = M // tm
    pos3 = pos_t.reshape(nt, 1, TOP_K * tm)
    return pl.pallas_call(
        functools.partial(_combine_kernel, tm=tm, nb=nb),
        out_shape=jax.ShapeDtypeStruct((M, D_MODEL), F32),
        grid=(nt,),
        in_specs=[pl.BlockSpec((1, 1, TOP_K * tm), lambda i: (i, 0, 0), memory_space=pltpu.SMEM),
                  pl.BlockSpec((1, 1, TOP_K * tm), lambda i: (jnp.minimum(i + 1, nt - 1), 0, 0),
                               memory_space=pltpu.SMEM),
                  pl.BlockSpec(memory_space=pl.ANY),
                  pl.BlockSpec((tm, LANES), lambda i: (i, 0)),
                  pl.BlockSpec((tm, D_MODEL), lambda i: (i, 0)),
                  pl.BlockSpec((nb, 6, D_MODEL), lambda i: ((i * tm) // (T * nb), 0, 0))],
        out_specs=pl.BlockSpec((tm, D_MODEL), lambda i: (i, 0)),
        scratch_shapes=[pltpu.VMEM((2 * TOP_K * tm * ROW_TILE, LANES), F32), pltpu.SemaphoreType.DMA((2,))],
        compiler_params=_cparams(("arbitrary",)),
        name="moe_combine",
    )(pos3, pos3, ys, top_w, x1, m)


def _route(top_i, rank, counts, bm):
    N = top_i.shape[0]
    NK = N * TOP_K
    padded = (counts + bm - 1) // bm * bm
    pad_end = jnp.cumsum(padded)
    pad_start = pad_end - padded
    onehot = top_i[:, :, None] == jnp.arange(N_EXPERTS, dtype=I32)[None, None, :]
    pos = rank + jnp.sum(jnp.where(onehot, pad_start[None, None, :], 0), axis=2)
    n_rows = (-(-NK // bm)) * bm + N_EXPERTS * bm
    nblk = n_rows // bm
    starts = jnp.arange(nblk, dtype=I32) * bm
    block_e = jnp.minimum(jnp.sum((pad_end[None, :] <= starts[:, None]).astype(I32), axis=1), N_EXPERTS - 1)
    n_used = (pad_end[-1:] // bm).astype(I32)
    return pos.reshape(NK).astype(I32), n_rows, block_e.astype(I32), n_used


def _moe(h2, top_i, top_w, rank, counts, x1, m, ew, B, T):
    M = B * T
    bm = 256 if M * TOP_K >= 256 * N_EXPERTS * 4 else 128
    pos, n_rows, block_e, n_used = _route(top_i[:, :TOP_K], rank[:, :TOP_K], counts[0, :N_EXPERTS], bm)
    tm, nb = _row_tiling(B, T, 256)
    xs = _dispatch(h2, pos, n_rows, tm)
    ys = _ffn(xs, block_e, n_used, *ew, bm)
    pos_t = pos.reshape(M // tm, tm, TOP_K).transpose(0, 2, 1)
    return _combine(pos_t, ys, top_w, x1, m, B, T, tm, nb)


def _prep_layer(P, l, w2_all):
    w_t = P['w_in'][l].T

    def rows(name):
        o, s = _SRC[name]
        return w_t[o:o + s]

    zeros = lambda n: jnp.zeros((n, D_MODEL), F32)
    w_r = jnp.concatenate([
        rows('gla_q'), rows('gla_k'), rows('gla_v'), rows('gla_r'),
        rows('diff_q'), rows('diff_k'), rows('diff_v'),
        rows('dsa_q'), rows('dsa_k'), rows('dsa_v'),
        rows('idx_q'), rows('idx_k'), rows('idx_w'), zeros(60), rows('gla_a'), zeros(112),
        rows('gates')], axis=0).astype(BF16)
    assert w_r.shape[0] == Z_WIDTH
    W = GLA_HEADS * GLA_DK
    wa = jnp.zeros((LANES, W), F32).at[:GLA_GATE_RANK].set(P['w_gla_a2'][l]).astype(BF16)
    w1g, w1u = _w1prep(P['w_mlp1'], l)
    b1 = P['b_mlp1'][l]
    return dict(
        w_in=w_r, wa=wa, ba=P['b_gla_a2'][l].reshape(1, W),
        g_gla=jnp.tile(P['g_gla_out'][l], GLA_HEADS).reshape(1, 512),
        gains=(jnp.tile(P['g_diff_q'][l], 8).reshape(1, 512), jnp.tile(P['g_diff_k'][l], 8).reshape(1, 512),
               jnp.tile(P['g_dsa_q'][l], 4).reshape(1, 512), jnp.tile(P['g_dsa_k'][l], 4).reshape(1, 512),
               jnp.tile(P['g_idx_k'][l], 2).reshape(1, LANES)),
        lamv=jnp.stack([P['lambda_q1'][l], P['lambda_k1'][l], P['lambda_q2'][l], P['lambda_k2'][l]]),
        g_diff=P['g_diff_out'][l].reshape(1, LANES),
        merge=(P['w_branch_gla'][l].astype(BF16), P['w_branch_diff'][l].astype(BF16),
               P['w_branch_dsa'][l].astype(BF16), P['w_out'][l].astype(BF16),
               P['g_norm2'][l].reshape(1, D_MODEL),
               jnp.zeros((D_MODEL, LANES), F32).at[:, :N_EXPERTS].set(P['w_router'][l]).astype(BF16),
               jnp.full((1, LANES), NEG_INF, F32).at[0, :N_EXPERTS].set(P['b_router'][l])),
        experts=(w1g, b1[:, None, 0::2], w1u, b1[:, None, 1::2], w2_all, P['b_mlp2'][l][:, None, :], l),
    )


def _trunk(x, c, past, P, prepped):
    B, T, _ = x.shape
    M = B * T
    x2 = x.reshape(M, D_MODEL)
    past_len = 0 if past is None else past[0].shape[2]
    pos = past_len + jnp.arange(T, dtype=I32)
    tabs64 = _rope_tables(pos, 64)
    tabs128 = _rope_tables(pos, 128)
    W = GLA_HEADS * GLA_DK
    if past is not None:
        L, _, PL = past[0].shape[:3]
        pdk = past[0].transpose(0, 1, 3, 4, 5, 2)
        pdv = past[1].reshape(L, B, PL * DIFF_HEADS, DIFF_DV)
        pck = past[2].reshape(L, B, PL * DSA_HEADS, DSA_DH)
        pcv = past[3].reshape(L, B, PL * DSA_HEADS, DSA_DH)
        pik = past[4].transpose(0, 1, 3, 2)
        s0_all = past[5].transpose(0, 1, 4, 2, 3).reshape(past[5].shape[0], B, GLA_DV, W)
    per_layer = []
    for l, pp in enumerate(prepped):
        lam_init = 0.8 - 0.6 * math.exp(-0.3 * l)
        m = _ada(c, P['w_ada'], P['b_ada'], l)
        z = _inproj(x2, m, P['g_norm1'][l], pp['w_in'], B, T)
        (dq, dk32, dkb, dvb, cq, ck32, ckb, cvb, iq, ik32, ikr, iw, dv32, cv32) = _post(
            z, tabs64, tabs128, pp['gains'], B, T)
        s0t = jnp.zeros((B, GLA_DV, W), F32) if past is None else s0_all[l]
        o_gla, st = _gla(z, pp['wa'], pp['ba'], pp['g_gla'], s0t, B, T)
        if past is None:
            o_diff = _diff_prompt(dq, dkb, dvb, pp['lamv'], pp['g_diff'], B, T, lam_init)
            o_dsa = _dsa_prompt(cq, iq, iw, ckb, cvb, ikr, B, T)
        else:
            o_diff = _diff_sample(dq, dkb, dvb, pdk, pdv, l, pp['lamv'], pp['g_diff'], B, T, lam_init)
            o_dsa = _dsa_sample(cq, iq, iw, ckb, cvb, ikr, pck, pcv, pik, l, B, T)
        x1, h2, top_i, top_w, rank, counts = _merge(o_gla, o_diff, o_dsa, z, x2, m, pp['merge'], B, T)
        x2 = _moe(h2, top_i, top_w, rank, counts, x1, m, pp['experts'], B, T)
        s_gla = st.reshape(B, GLA_DV, GLA_HEADS, GLA_DK).transpose(0, 2, 3, 1)
        per_layer.append((dk32.reshape(B, T, DIFF_HEADS, 2, DIFF_DH),
                          dv32.reshape(B, T, DIFF_HEADS, DIFF_DV),
                          ck32.reshape(B, T, DSA_HEADS, DSA_DH),
                          cv32.reshape(B, T, DSA_HEADS, DSA_DH),
                          ik32.reshape(B, T, IDX_DH),
                          s_gla))
    stacked = tuple(jnp.stack([st[i] for st in per_layer]) for i in range(6))
    return x2.reshape(B, T, D_MODEL), stacked


def kernel(x_prompt, x_sample, cache_diff_k, cache_diff_v, cache_dsa_k, cache_dsa_v, cache_dsa_idx_k,
           state_gla, c_prompt, c_sample, w_ada, b_ada, g_norm1, g_norm2, w_in, w_gla_a2, b_gla_a2,
           g_gla_out, g_diff_q, g_diff_k, lambda_q1, lambda_k1, lambda_q2, lambda_k2, g_diff_out,
           g_dsa_q, g_dsa_k, g_idx_k, w_branch_gla, w_branch_diff, w_branch_dsa, w_out, w_router,
           b_router, w_mlp1, b_mlp1, w_mlp2, b_mlp2):
    P = dict(w_ada=w_ada, b_ada=b_ada, g_norm1=g_norm1, g_norm2=g_norm2, w_in=w_in, w_gla_a2=w_gla_a2,
             b_gla_a2=b_gla_a2, g_gla_out=g_gla_out, g_diff_q=g_diff_q, g_diff_k=g_diff_k,
             lambda_q1=lambda_q1, lambda_k1=lambda_k1, lambda_q2=lambda_q2, lambda_k2=lambda_k2,
             g_diff_out=g_diff_out, g_dsa_q=g_dsa_q, g_dsa_k=g_dsa_k, g_idx_k=g_idx_k,
             w_branch_gla=w_branch_gla, w_branch_diff=w_branch_diff, w_branch_dsa=w_branch_dsa,
             w_out=w_out, w_router=w_router, b_router=b_router, w_mlp1=w_mlp1, b_mlp1=b_mlp1,
             w_mlp2=w_mlp2, b_mlp2=b_mlp2)
    depth = w_in.shape[0]
    w2_all = w_mlp2.astype(BF16)
    prepped = [_prep_layer(P, l, w2_all) for l in range(depth)]
    y_prompt, new_p = _trunk(x_prompt, c_prompt, None, P, prepped)
    y_sample, new_s = _trunk(
        x_sample, c_sample,
        (cache_diff_k, cache_diff_v, cache_dsa_k, cache_dsa_v, cache_dsa_idx_k, state_gla), P, prepped)
    return (y_prompt, y_sample) + new_p + new_s
```

```python
import functools
import math

import numpy as np
import jax
import jax.numpy as jnp
from jax import lax
from jax.experimental import pallas as pl
from jax.experimental.pallas import tpu as pltpu

F32 = jnp.float32
BF16 = jnp.bfloat16
I32 = jnp.int32

D_MODEL = 1024
CHUNK = 64
ROPE_THETA = 500000.0
ROPE_FRACTION = 4
EPS = 1e-6
NEG_INF = -1e30

GLA_HEADS = 4
GLA_DK = 64
GLA_DV = 128
GLA_GATE_RANK = 16
GLA_GATE_TAU = 16.0
GLA_SUB = 16
DIFF_HEADS = 4
DIFF_DH = 64
DIFF_DV = 128
DSA_HEADS = 4
DSA_DH = 128
IDX_HEADS = 4
IDX_DH = 64
DSA_TOPK_MAX = 256
N_EXPERTS = 32
TOP_K = 4
D_FF = 1024
SWIGLU_LIMIT = 7.0
SWIGLU_ALPHA = 1.702

LANES = 128
ROW_TILE = 8
VMEM_LIMIT = 48 * 1024 * 1024

Z_GQK, Z_GV, Z_GR = 0, 512, 1024
Z_DQ, Z_DK, Z_DV = 1536, 2048, 2560
Z_CQ, Z_CK, Z_CV = 3072, 3584, 4096
Z_MISC = 4608
Z_GATES = 5120
Z_WIDTH = 8192
MISC_IK = 256
MISC_GA = 384

_SRC = {}
_off = 0
for _name, _size in (
        ('gla_q', 256), ('gla_k', 256), ('gla_v', 512), ('gla_a', 16), ('gla_r', 512),
        ('diff_q', 512), ('diff_k', 512), ('diff_v', 512),
        ('dsa_q', 512), ('dsa_k', 512), ('dsa_v', 512),
        ('idx_q', 256), ('idx_k', 64), ('idx_w', 4), ('gates', 3072)):
    _SRC[_name] = (_off, _size)
    _off += _size


def _cparams(sem, vmem=VMEM_LIMIT):
    return pltpu.CompilerParams(dimension_semantics=sem, vmem_limit_bytes=vmem)


def _dot(a, b):
    return jnp.dot(a, b, preferred_element_type=F32)


def _dot_nt(a, b):
    return lax.dot_general(a, b, (((1,), (1,)), ((), ())), preferred_element_type=F32)


def _dot_tn(a, b):
    return lax.dot_general(a, b, (((0,), (0,)), ((), ())), preferred_element_type=F32)


def _silu(x):
    return x * jax.nn.sigmoid(x)


def _row_tiling(B, T, target):
    if T >= target:
        assert T % target == 0
        return target, 1
    nb = 1
    for cand in range(1, B + 1):
        if B % cand == 0 and cand * T <= target:
            nb = cand
    return nb * T, nb


def _modulate(xn, m, shift_i, scale_i, nb):
    tm = xn.shape[0]
    if nb == 1:
        return xn * (1.0 + m[0, scale_i:scale_i + 1, :]) + m[0, shift_i:shift_i + 1, :]
    x3 = xn.reshape(nb, tm // nb, D_MODEL)
    h = x3 * (1.0 + m[:, scale_i:scale_i + 1, :]) + m[:, shift_i:shift_i + 1, :]
    return h.reshape(tm, D_MODEL)


def _gate_rows(y, m, gate_i, nb):
    tm = y.shape[0]
    if nb == 1:
        return y * m[0, gate_i:gate_i + 1, :]
    return (y.reshape(nb, tm // nb, D_MODEL) * m[:, gate_i:gate_i + 1, :]).reshape(tm, D_MODEL)


def _ada_kernel(c_ref, w_ref, b_ref, o_ref):
    s = _silu(c_ref[...])
    o_ref[...] = _dot(s.astype(BF16), w_ref[...].astype(BF16)) + b_ref[...]


def _ada(c, w, b, layer):
    B = c.shape[0]
    out = pl.pallas_call(
        _ada_kernel,
        out_shape=jax.ShapeDtypeStruct((B, 6 * D_MODEL), F32),
        grid=(6,),
        in_specs=[pl.BlockSpec((B, D_MODEL), lambda j: (0, 0)),
                  pl.BlockSpec((None, D_MODEL, D_MODEL), lambda j: (layer, 0, j)),
                  pl.BlockSpec((None, 1, D_MODEL), lambda j: (layer, 0, j))],
        out_specs=pl.BlockSpec((B, D_MODEL), lambda j: (0, j)),
        compiler_params=_cparams(("arbitrary",)),
        name="ada",
    )(c, w, b.reshape(b.shape[0], 1, -1))
    return out.reshape(B, 6, D_MODEL)


def _inproj_kernel(x_ref, m_ref, g_ref, w_ref, o_ref, h_scr, *, nb):
    @pl.when(pl.program_id(1) == 0)
    def _():
        x = x_ref[...]
        xn = x * lax.rsqrt(jnp.mean(x * x, axis=-1, keepdims=True) + EPS) * g_ref[...]
        h_scr[...] = _modulate(xn, m_ref[...], 0, 1, nb).astype(BF16)

    o_ref[...] = _dot_nt(h_scr[...], w_ref[...])


def _inproj(x2, m, g, wt, B, T):
    M = B * T
    tm, nb = _row_tiling(B, T, 1024)
    tn = 1024
    return pl.pallas_call(
        functools.partial(_inproj_kernel, nb=nb),
        out_shape=jax.ShapeDtypeStruct((M, Z_WIDTH), F32),
        grid=(M // tm, Z_WIDTH // tn),
        in_specs=[pl.BlockSpec((tm, D_MODEL), lambda i, j: (i, 0)),
                  pl.BlockSpec((nb, 6, D_MODEL), lambda i, j: ((i * tm) // (T * nb), 0, 0)),
                  pl.BlockSpec((1, D_MODEL), lambda i, j: (0, 0)),
                  pl.BlockSpec((tn, D_MODEL), lambda i, j: (j, 0))],
        out_specs=pl.BlockSpec((tm, tn), lambda i, j: (i, j)),
        scratch_shapes=[pltpu.VMEM((tm, D_MODEL), BF16)],
        compiler_params=_cparams(("arbitrary", "arbitrary")),
        name="inproj",
    )(x2, m, g.reshape(1, -1), wt)


def _rope_tables(pos, d):
    rot = d // ROPE_FRACTION
    half = rot // 2
    T = pos.shape[0]
    inv_freq = ROPE_THETA ** (-jnp.arange(half, dtype=F32) / half)
    ang = pos.astype(F32)[:, None] * inv_freq[None, :]
    cos, sin = jnp.cos(ang), jnp.sin(ang)
    c = jnp.concatenate([cos, cos, jnp.ones((T, d - rot), F32)], axis=1)
    a = jnp.concatenate([-sin, jnp.zeros((T, d - half), F32)], axis=1)
    b = jnp.concatenate([jnp.zeros((T, half), F32), sin, jnp.zeros((T, d - rot), F32)], axis=1)
    reps = LANES // d
    return tuple(jnp.tile(t, (1, reps)) for t in (c, a, b))


def _rope128(xs, tabs, half):
    c, a, b = tabs
    return xs * c + pltpu.roll(xs, LANES - half, 1) * a + pltpu.roll(xs, half, 1) * b


def _norm_rope(x, g, tabs, d, norm=True):
    tm, W = x.shape
    half = d // ROPE_FRACTION // 2
    lo = lax.broadcasted_iota(I32, (tm, LANES), 1) < 64
    outs = []
    for s in range(W // LANES):
        xs = x[:, s * LANES:(s + 1) * LANES]
        if norm:
            sq = xs * xs
            if d == LANES:
                r = lax.rsqrt(jnp.sum(sq, axis=1, keepdims=True) * (1.0 / d) + EPS)
            else:
                s_lo = jnp.sum(jnp.where(lo, sq, 0.0), axis=1, keepdims=True)
                s_hi = jnp.sum(jnp.where(lo, 0.0, sq), axis=1, keepdims=True)
                r = jnp.where(lo, lax.rsqrt(s_lo * (1.0 / d) + EPS), lax.rsqrt(s_hi * (1.0 / d) + EPS))
            xs = xs * r * g[:, s * LANES:(s + 1) * LANES]
        outs.append(_rope128(xs, tabs, half))
    return outs[0] if len(outs) == 1 else jnp.concatenate(outs, axis=1)


def _store_head_rows(ref, x):
    tm = x.shape[0]
    for h in range(4):
        ref[pl.ds(h, tm, stride=4), :] = x[:, h * LANES:(h + 1) * LANES]


def _post_kernel(dq_ref, dk_ref, dv_ref, cq_ref, ck_ref, cv_ref, mi_ref,
                 c64_ref, a64_ref, b64_ref, c128_ref, a128_ref, b128_ref,
                 gdq_ref, gdk_ref, gcq_ref, gck_ref, gik_ref,
                 dq_o, dk32_o, dkb_o, dvb_o, cq_o, ck32_o, ckb_o, cvb_o, iq_o, ik32_o, ikr_o, iw_o,
                 dv32_o, cv32_o):
    t64 = (c64_ref[...], a64_ref[...], b64_ref[...])
    t128 = (c128_ref[...], a128_ref[...], b128_ref[...])
    dq = _norm_rope(dq_ref[...], gdq_ref[...], t64, 64)
    dq_o[...] = (dq * (DIFF_DH ** -0.5)).astype(BF16)
    dk = _norm_rope(dk_ref[...], gdk_ref[...], t64, 64)
    dk32_o[...] = dk
    dkb_o[...] = dk.astype(BF16)
    dv = dv_ref[...]
    _store_head_rows(dv32_o, dv)
    dvb_o[...] = dv.astype(BF16)
    cq = _norm_rope(cq_ref[...], gcq_ref[...], t128, 128)
    cq_o[...] = (cq * (DSA_DH ** -0.5)).astype(BF16)
    ck = _norm_rope(ck_ref[...], gck_ref[...], t128, 128)
    _store_head_rows(ck32_o, ck)
    ckb_o[...] = ck.astype(BF16)
    cv = cv_ref[...]
    _store_head_rows(cv32_o, cv)
    cvb_o[...] = cv.astype(BF16)
    mi = mi_ref[...]
    iq = _norm_rope(mi[:, 0:256], None, t64, 64, norm=False)
    iq_o[...] = (iq * (IDX_DH ** -0.5)).astype(BF16)
    mk = mi[:, MISC_IK:MISC_IK + LANES]
    tm = mk.shape[0]
    lane = lax.broadcasted_iota(I32, (tm, LANES), 1)
    lo = lane < 64
    ssq = jnp.sum(jnp.where(lo, mk * mk, 0.0), axis=1, keepdims=True)
    ikn = mk * lax.rsqrt(ssq * (1.0 / IDX_DH) + EPS) * gik_ref[...]
    ik = _rope128(ikn, t64, IDX_DH // ROPE_FRACTION // 2)
    ik32_o[...] = ik[:, 0:IDX_DH]
    ik2 = jnp.where(lo, ik, pltpu.roll(ik, 64, 1))
    ikr_o[...] = jnp.concatenate([ik2, ik2], axis=1).astype(BF16)
    iw_o[...] = jnp.where(lane < IDX_HEADS, pltpu.roll(mk, 64, 1) * (IDX_HEADS ** -0.5), 0.0)


def _post(z, tabs64, tabs128, gains, B, T):
    M = B * T
    tm = min(T, 512)
    assert T % tm == 0
    npos = T // tm

    def zspec(off):
        return pl.BlockSpec((tm, 512), lambda i: (i, off // 512))

    tab_spec = pl.BlockSpec((tm, LANES), lambda i: (i % npos, 0))

    def gspec(w):
        return pl.BlockSpec((1, w), lambda i: (0, 0))

    HR = "head rows"
    outs = [(512, BF16), (512, F32), (512, BF16), (512, BF16),
            (512, BF16), (HR, F32), (512, BF16), (512, BF16),
            (256, BF16), (IDX_DH, F32), (256, BF16), (LANES, F32), (HR, F32), (HR, F32)]

    def oshape(w, dt):
        return jax.ShapeDtypeStruct((M * 4, LANES) if w is HR else (M, w), dt)

    def ospec(w):
        return pl.BlockSpec((tm * 4, LANES) if w is HR else (tm, w), lambda i: (i, 0))

    return pl.pallas_call(
        _post_kernel,
        out_shape=[oshape(w, dt) for w, dt in outs],
        grid=(M // tm,),
        in_specs=[zspec(Z_DQ), zspec(Z_DK), zspec(Z_DV), zspec(Z_CQ), zspec(Z_CK), zspec(Z_CV), zspec(Z_MISC)]
                 + [tab_spec] * 6 + [gspec(512)] * 4 + [gspec(LANES)],
        out_specs=[ospec(w) for w, _ in outs],
        compiler_params=_cparams(("arbitrary",)),
        name="post",
    )(z, z, z, z, z, z, z, *tabs64, *tabs128, *gains)


def _split3(x):
    hi = x.astype(BF16)
    r = x - hi.astype(F32)
    mid = r.astype(BF16)
    lo = (r - mid.astype(F32)).astype(BF16)
    return hi, mid, lo


def _gla_kernel(qk_ref, v_ref, r_ref, ga_ref, wa_ref, ba_ref, g_ref, s0_ref, o_ref, st_ref, st_scr, *, nct):
    ti = pl.program_id(1)

    @pl.when(ti == 0)
    def _():
        st_scr[...] = s0_ref[...]

    C = CHUNK
    W = GLA_HEADS * GLA_DK
    row = lax.broadcasted_iota(I32, (C, C), 0)
    col = lax.broadcasted_iota(I32, (C, C), 1)
    tri = col <= row
    tri_bf = tri.astype(BF16)
    lane = lax.broadcasted_iota(I32, (1, W), 1)
    hmask = [(lane // GLA_DK) == h for h in range(GLA_HEADS)]
    rowi = lax.broadcasted_iota(I32, (C, W), 0)
    wa = wa_ref[...]
    ba = ba_ref[...]
    g = g_ref[...]

    def chunk(c, carry):
        rows = pl.ds(pl.multiple_of(c * C, C), C)
        qk = qk_ref[rows, :]
        q = qk[:, :W] * (GLA_DK ** -0.5)
        k = qk[:, W:]
        v = v_ref[rows, :].astype(BF16)
        pre = _dot(ga_ref[rows, :].astype(BF16), wa) + ba
        la = (jnp.minimum(pre, 0.0) - jnp.log(1.0 + jnp.exp(-jnp.abs(pre)))) * (1.0 / GLA_GATE_TAU)
        hi, mid, lo = _split3(la)
        b = _dot(tri_bf, hi) + _dot(tri_bf, mid) + _dot(tri_bf, lo)
        st = st_scr[...]
        st_bf = st.astype(BF16)
        qe = q * jnp.exp(b)
        b_end = b[C - 1:C, :]
        kend = k * jnp.exp(b_end - b)
        att_parts = [[] for _ in range(GLA_HEADS)]
        for s in range(C // GLA_SUB):
            r0 = s * GLA_SUB
            br = b[r0:r0 + 1, :]
            qs = q[r0:r0 + GLA_SUB, :] * jnp.exp(b[r0:r0 + GLA_SUB, :] - br)
            ks = k * jnp.exp(br - b)
            if r0 + GLA_SUB < C:
                ks = jnp.where(rowi < r0 + GLA_SUB, ks, 0.0)
            ks = ks.astype(BF16)
            for h in range(GLA_HEADS):
                att_parts[h].append(_dot_nt(jnp.where(hmask[h], qs, 0.0).astype(BF16), ks))
        upd = None
        for h in range(GLA_HEADS):
            hs = slice(h * GLA_DV, (h + 1) * GLA_DV)
            att = jnp.where(tri, jnp.concatenate(att_parts[h], axis=0), 0.0)
            o = _dot(att.astype(BF16), v[:, hs]) + _dot_nt(jnp.where(hmask[h], qe, 0.0).astype(BF16), st_bf)
            u = _dot_tn(v[:, hs], jnp.where(hmask[h], kend, 0.0).astype(BF16))
            upd = u if upd is None else upd + u
            y = o * lax.rsqrt(jnp.mean(o * o, axis=-1, keepdims=True) + EPS) * g[:, hs]
            o_ref[rows, hs] = (y * _silu(r_ref[rows, hs])).astype(BF16)
        st_scr[...] = st * jnp.exp(b_end) + upd
        return carry

    lax.fori_loop(0, nct, chunk, 0)

    @pl.when(ti == pl.num_programs(1) - 1)
    def _():
        st_ref[...] = st_scr[...]


def _gla(z, wa, ba, g, s0t, B, T):
    M = B * T
    tt = min(T, 512)
    assert T % tt == 0 and tt % CHUNK == 0
    nt = T // tt
    W = GLA_HEADS * GLA_DK
    return pl.pallas_call(
        functools.partial(_gla_kernel, nct=tt // CHUNK),
        out_shape=[jax.ShapeDtypeStruct((M, 512), BF16),
                   jax.ShapeDtypeStruct((B, GLA_DV, W), F32)],
        grid=(B, nt),
        in_specs=[pl.BlockSpec((tt, 512), lambda b, t: (b * nt + t, Z_GQK // 512)),
                  pl.BlockSpec((tt, 512), lambda b, t: (b * nt + t, Z_GV // 512)),
                  pl.BlockSpec((tt, 512), lambda b, t: (b * nt + t, Z_GR // 512)),
                  pl.BlockSpec((tt, LANES), lambda b, t: (b * nt + t, (Z_MISC + MISC_GA) // LANES)),
                  pl.BlockSpec((LANES, W), lambda b, t: (0, 0)),
                  pl.BlockSpec((1, W), lambda b, t: (0, 0)),
                  pl.BlockSpec((1, 512), lambda b, t: (0, 0)),
                  pl.BlockSpec((None, GLA_DV, W), lambda b, t: (b, 0, 0))],
        out_specs=[pl.BlockSpec((tt, 512), lambda b, t: (b * nt + t, 0)),
                   pl.BlockSpec((None, GLA_DV, W), lambda b, t: (b, 0, 0))],
        scratch_shapes=[pltpu.VMEM((GLA_DV, W), F32)],
        compiler_params=_cparams(("arbitrary", "arbitrary")),
        name="gla",
    )(z, z, z, z, wa, ba, g, s0t)


def _diff_lambda(lam_ref, lam_init):
    lv = lam_ref[...]
    a = jnp.sum(lv[0:1, :] * lv[1:2, :], axis=1, keepdims=True)
    b = jnp.sum(lv[2:3, :] * lv[3:4, :], axis=1, keepdims=True)
    return jnp.exp(a) - jnp.exp(b) + lam_init


def _diff_finish(o0, o1, lam, g, lam_init):
    o = o0 - lam * o1
    return o * lax.rsqrt(jnp.mean(o * o, axis=-1, keepdims=True) + EPS) * g * (1.0 - lam_init)


def _diff_prompt_kernel(qi_ref, kj_ref, q_ref, k_ref, v_ref, lam_ref, g_ref, o_ref,
                        q2_scr, m_scr, l_scr, acc_scr, *, lam_init, tq):
    n = pl.program_id(2)
    qi = qi_ref[n]
    kj = kj_ref[n]

    @pl.when(kj == 0)
    def _():
        q = q_ref[...]
        lane = lax.broadcasted_iota(I32, (1, LANES), 1)
        q2_scr[0:tq, :] = jnp.where(lane < DIFF_DH, q, jnp.zeros_like(q))
        q2_scr[tq:2 * tq, :] = jnp.where(lane < DIFF_DH, jnp.zeros_like(q), q)
        m_scr[...] = jnp.full(m_scr.shape, NEG_INF, F32)
        l_scr[...] = jnp.zeros(l_scr.shape, F32)
        acc_scr[...] = jnp.zeros(acc_scr.shape, F32)

    def step(diagonal):
        s = _dot_nt(q2_scr[...], k_ref[...])
        if diagonal:
            qc = lax.broadcasted_iota(I32, (tq, tq), 0) // CHUNK
            kc = lax.broadcasted_iota(I32, (tq, tq), 1) // CHUNK
            vis = kc <= qc
            s = jnp.where(jnp.concatenate([vis, vis], axis=0), s, NEG_INF)
        m_prev = m_scr[...]
        m_new = jnp.maximum(m_prev, jnp.max(s, axis=1, keepdims=True))
        alpha = jnp.exp(m_prev - m_new)
        p = jnp.exp((s - jnp.concatenate([m_new] * (tq // LANES), axis=1)).astype(BF16))
        l_scr[...] = alpha * l_scr[...] + jnp.sum(p.astype(F32), axis=1, keepdims=True)
        acc_scr[...] = alpha * acc_scr[...] + _dot(p, v_ref[...])
        m_scr[...] = m_new

    @pl.when(kj < qi)
    def _():
        step(False)

    @pl.when(kj == qi)
    def _():
        step(True)
        lam = _diff_lambda(lam_ref, lam_init)
        o = acc_scr[...] / l_scr[...]
        o_ref[...] = _diff_finish(o[0:tq], o[tq:2 * tq], lam, g_ref[...], lam_init).astype(BF16)


def _diff_prompt(dq, dk, dv, lamv, g, B, T, lam_init):
    M = B * T
    tq = min(T, 512)
    nq = T // tq
    pairs = [(i, j) for i in range(nq) for j in range(i + 1)]
    qi = jnp.asarray([p[0] for p in pairs], I32)
    kj = jnp.asarray([p[1] for p in pairs], I32)
    grid_spec = pltpu.PrefetchScalarGridSpec(
        num_scalar_prefetch=2,
        grid=(B, DIFF_HEADS, len(pairs)),
        in_specs=[pl.BlockSpec((tq, LANES), lambda b, h, n, qi, kj: (b * nq + qi[n], h)),
                  pl.BlockSpec((tq, LANES), lambda b, h, n, qi, kj: (b * nq + kj[n], h)),
                  pl.BlockSpec((tq, LANES), lambda b, h, n, qi, kj: (b * nq + kj[n], h)),
                  pl.BlockSpec((4, DIFF_DH), lambda b, h, n, qi, kj: (0, 0)),
                  pl.BlockSpec((1, LANES), lambda b, h, n, qi, kj: (0, 0))],
        out_specs=pl.BlockSpec((tq, LANES), lambda b, h, n, qi, kj: (b * nq + qi[n], h)),
        scratch_shapes=[pltpu.VMEM((2 * tq, LANES), BF16), pltpu.VMEM((2 * tq, LANES), F32),
                        pltpu.VMEM((2 * tq, LANES), F32), pltpu.VMEM((2 * tq, LANES), F32)])
    return pl.pallas_call(
        functools.partial(_diff_prompt_kernel, lam_init=lam_init, tq=tq),
        out_shape=jax.ShapeDtypeStruct((M, 512), BF16),
        grid_spec=grid_spec,
        compiler_params=_cparams(("arbitrary", "arbitrary", "arbitrary")),
        name="diff_prompt",
    )(qi, kj, dq, dk, dv, lamv, g)


def _diff_sample_kernel(q_ref, kn_ref, vn_ref, kp_ref, vp_ref, lam_ref, g_ref, o_ref, *, lam_init, past_len):
    T = q_ref.shape[0]
    lam = _diff_lambda(lam_ref, lam_init)
    lane = lax.broadcasted_iota(I32, (1, LANES), 1)
    qpos = past_len + lax.broadcasted_iota(I32, (T, T), 0)
    kpos = past_len + lax.broadcasted_iota(I32, (T, T), 1)
    vis_new = (kpos // CHUNK) <= (qpos // CHUNK)
    P = past_len
    for h in range(DIFF_HEADS):
        hs = slice(h * LANES, (h + 1) * LANES)
        q = q_ref[:, hs]
        kpt = kp_ref[h].reshape(2 * DIFF_DH, P).astype(BF16)
        vp = vp_ref[pl.ds(h, P, stride=DIFF_HEADS), :].astype(BF16)
        kn = kn_ref[:, hs]
        vn = vn_ref[:, hs]
        outs = []
        for c in range(2):
            qc = jnp.where((lane < DIFF_DH) == (c == 0), q, jnp.zeros_like(q))
            sp = _dot(qc, kpt)
            sn = jnp.where(vis_new, _dot_nt(qc, kn), NEG_INF)
            m = jnp.maximum(jnp.max(sp, axis=1, keepdims=True), jnp.max(sn, axis=1, keepdims=True))
            pp = jnp.exp((sp - m).astype(BF16))
            pn = jnp.exp((sn - m).astype(BF16))
            l = jnp.sum(pp.astype(F32), axis=1, keepdims=True) + jnp.sum(pn.astype(F32), axis=1, keepdims=True)
            outs.append((_dot(pp, vp) + _dot(pn, vn)) / l)
        o_ref[:, hs] = _diff_finish(outs[0], outs[1], lam, g_ref[...], lam_init).astype(BF16)


def _diff_sample(dq, dk, dv, past_kt, past_v, layer, lamv, g, B, T, lam_init):
    M = B * T
    P = past_kt.shape[-1]
    return pl.pallas_call(
        functools.partial(_diff_sample_kernel, lam_init=lam_init, past_len=P),
        out_shape=jax.ShapeDtypeStruct((M, 512), BF16),
        grid=(B,),
        in_specs=[pl.BlockSpec((T, 512), lambda b: (b, 0)),
                  pl.BlockSpec((T, 512), lambda b: (b, 0)),
                  pl.BlockSpec((T, 512), lambda b: (b, 0)),
                  pl.BlockSpec((None, None, DIFF_HEADS, 2, DIFF_DH, P), lambda b: (layer, b, 0, 0, 0, 0)),
                  pl.BlockSpec((None, None, P * DIFF_HEADS, DIFF_DV), lambda b: (layer, b, 0, 0)),
                  pl.BlockSpec((4, DIFF_DH), lambda b: (0, 0)),
                  pl.BlockSpec((1, LANES), lambda b: (0, 0))],
        out_specs=pl.BlockSpec((T, 512), lambda b: (b, 0)),
        compiler_params=_cparams(("arbitrary",)),
        name="diff_sample",
    )(dq, dk, dv, past_kt, past_v, lamv, g)


INT_MIN = -2 ** 31


def _idx_score(iq, iw, ikr, transposed=False):
    lane = lax.broadcasted_iota(I32, (1, IDX_HEADS * IDX_DH), 1)
    sc = None
    for i in range(IDX_HEADS):
        iqi = jnp.where((lane // IDX_DH) == i, iq, jnp.zeros_like(iq))
        lg = _dot(iqi, ikr) if transposed else _dot_nt(iqi, ikr)
        t = iw[:, i:i + 1] * jnp.maximum(lg, 0.0)
        sc = t if sc is None else sc + t
    return jnp.where(sc == 0.0, 0.0, sc)


def _order_key(score):
    bits = pltpu.bitcast(score, I32)
    return jnp.where(bits < 0, bits ^ 0x7FFFFFFF, bits)


def _count(mask):
    return jnp.sum(mask.astype(F32), axis=1, keepdims=True)


def _f32_key(x):
    b = int(np.float32(x).view(np.int32))
    return b ^ 0x7FFFFFFF if b < 0 else b


def _key_value(k):
    return pltpu.bitcast(jnp.where(k < 0, k ^ 0x7FFFFFFF, k), F32)


def _topk_threshold(key_refs, smax, smin, n_sel, n_keys, probe_masked):
    tq = key_refs[0].shape[0]
    n = float(n_sel)

    def count_ge(cand):
        tot = None
        for kr in key_refs:
            c = _count(kr[...] >= cand)
            tot = c if tot is None else tot + c
        return tot

    def update(st, cand):
        lo, cnt_lo, hi, cnt_hi = st
        c = count_ge(cand)
        ge = c >= n
        up = ge & (cand > lo)
        dn = jnp.logical_not(ge) & (cand < hi)
        return (jnp.where(up, cand, lo), jnp.where(up, c, cnt_lo),
                jnp.where(dn, cand, hi), jnp.where(dn, c, cnt_hi))

    def settled(st):
        lo, cnt_lo, hi, _ = st
        return (cnt_lo == n) | ((hi - 1) == lo)

    def unsettled_any(st):
        return jnp.max(jnp.where(settled(st), 0.0, 1.0)) > 0.0

    col = lambda v, dt: jnp.full((tq, 1), v, dt)
    kmax = _order_key(smax)
    hi0 = jnp.where(kmax == 2 ** 31 - 1, kmax, kmax + 1)
    st = (col(INT_MIN, I32), col(float(n_keys), F32), hi0, col(0.0, F32))
    fixed = ((_f32_key(NEG_INF), _f32_key(NEG_INF) + 1) if probe_masked else ()) + (0, 1)
    for c in fixed:
        st = update(st, col(c, I32))
    st = update(st, _order_key(smin))
    log_n = math.log(n)

    def cond(carry):
        it, _, go = carry
        return jnp.logical_and(it < 100, go)

    def body(carry):
        it, st, _ = carry
        lo, cnt_lo, hi, cnt_hi = st
        a = jnp.log(cnt_lo + 0.5)
        frac = (a - log_n) / jnp.maximum(a - jnp.log(cnt_hi + 0.5), 1e-9)
        v_lo = _key_value(lo)
        guess = _order_key(v_lo + frac * (_key_value(hi) - v_lo))
        mid = (lo >> 1) + (hi >> 1) + (lo & hi & 1)
        cand = jnp.minimum(jnp.maximum(jnp.where(lax.rem(it, 3) == 2, mid, guess), lo + 1), hi - 1)
        st = update(st, jnp.where(settled(st), lo, cand))
        return it + 1, st, unsettled_any(st)

    _, st, _ = lax.while_loop(cond, body, (jnp.int32(0), st, unsettled_any(st)))
    lo, cnt_lo, hi, cnt_hi = st
    tied = ((hi - 1) == lo) & (cnt_lo > n)
    return lo, jnp.where(tied, n - cnt_hi, float(n_keys))


def _row_extremes(score, vis):
    if vis is None:
        return jnp.max(score, axis=1, keepdims=True), jnp.min(score, axis=1, keepdims=True)
    return (jnp.max(jnp.where(vis, score, -jnp.inf), axis=1, keepdims=True),
            jnp.min(jnp.where(vis, score, jnp.inf), axis=1, keepdims=True))


def _selection_bias(key_refs, bias_refs, vis_fns, smax, smin, n_sel, n_keys, probe_masked):
    t, need = _topk_threshold(key_refs, smax, smin, n_sel, n_keys, probe_masked)
    tq = key_refs[0].shape[0]
    base = jnp.zeros((tq, 1), F32)
    for kr, br, vis_fn in zip(key_refs, bias_refs, vis_fns):
        W = kr.shape[1]
        bw = min(W, LANES)
        assert W % bw == 0
        tri = (lax.broadcasted_iota(I32, (bw, bw), 0) <= lax.broadcasted_iota(I32, (bw, bw), 1)).astype(BF16)
        ones_tri = jnp.concatenate([jnp.ones((bw, bw), BF16), tri], axis=0)
        earlier = jnp.zeros((tq, bw), F32)
        for j in range(W // bw):
            cs = slice(j * bw, (j + 1) * bw)
            key = kr[:, cs]
            tie = key == t
            tb = jnp.where(tie, 1.0, 0.0)
            rank = base + _dot(jnp.concatenate([earlier, tb], axis=1).astype(BF16), ones_tri)
            sel = (key > t) | (tie & (rank <= need))
            if vis_fn is not None:
                sel = sel & vis_fn(j * bw, bw)
            br[:, cs] = jnp.where(sel, 0.0, NEG_INF)
            earlier = earlier + tb
        base = base + jnp.sum(earlier, axis=1, keepdims=True)


def _dsa_attend(q_ref, bias_refs, k_loads, v_loads, o_ref):
    for h in range(DSA_HEADS):
        hs = slice(h * DSA_DH, (h + 1) * DSA_DH)
        q = q_ref[:, hs]
        ss = [_dot_nt(q, kl(hs)) + br[...] for kl, br in zip(k_loads, bias_refs)]
        m = None
        for s in ss:
            ms = jnp.max(s, axis=1, keepdims=True)
            m = ms if m is None else jnp.maximum(m, ms)
        l = None
        o = None
        for s, vl in zip(ss, v_loads):
            p = jnp.exp((s - m).astype(BF16))
            ls = jnp.sum(p.astype(F32), axis=1, keepdims=True)
            os_ = _dot(p, vl(hs))
            l = ls if l is None else l + ls
            o = os_ if o is None else o + os_
        o_ref[:, hs] = (o / l).astype(BF16)


def _dsa_prompt_kernel(q_ref, iq_ref, iw_ref, k_ref, v_ref, ikr_ref, o_ref, key_scr, bias_scr,
                       *, tq, q_tile0, n_sel, probe_masked):
    tk = k_ref.shape[0]
    qt = q_tile0 + pl.program_id(1)
    score = _idx_score(iq_ref[...], iw_ref[...], ikr_ref[...])
    qpos = qt * tq + lax.broadcasted_iota(I32, (tq, tk), 0)
    kidx = lax.broadcasted_iota(I32, (tq, tk), 1)
    vis = (kidx // CHUNK) <= (qpos // CHUNK)
    key_scr[...] = _order_key(jnp.where(vis, score, NEG_INF))
    smax, smin = _row_extremes(score, vis)
    qchunk = (qt * tq + lax.broadcasted_iota(I32, (tq, 1), 0)) // CHUNK

    def vis_block(off, bw):
        return ((off + lax.broadcasted_iota(I32, (tq, bw), 1)) // CHUNK) <= qchunk

    _selection_bias([key_scr], [bias_scr], [vis_block], smax, smin, n_sel, tk, probe_masked)
    _dsa_attend(q_ref, [bias_scr], [lambda hs: k_ref[:, hs]], [lambda hs: v_ref[:, hs]], o_ref)


def _dsa_prompt(cq, iq, iw, ck, cv, ikr, B, T):
    tq = min(T, 128)
    nq = T // tq
    n_sel = min(DSA_TOPK_MAX, T // 4)
    ng = next(g for g in (16, 8, 4, 2, 1) if nq % g == 0)
    tpg = nq // ng
    outs = []
    for gi in range(ng):
        tk = (gi + 1) * tpg * tq
        nkb = T // tk if T % tk == 0 else None

        def kv_spec(w, tk=tk):
            return pl.BlockSpec((None, tk, w), lambda b, i: (b, 0, 0))

        def q_spec(w, gi=gi):
            return pl.BlockSpec((None, tq, w), lambda b, i: (b, gi * tpg + i, 0))

        out = pl.pallas_call(
            functools.partial(_dsa_prompt_kernel, tq=tq, q_tile0=gi * tpg, n_sel=n_sel,
                              probe_masked=gi * tpg * tq + CHUNK < n_sel),
            out_shape=jax.ShapeDtypeStruct((B, tpg * tq, 512), BF16),
            grid=(B, tpg),
            in_specs=[q_spec(512), q_spec(256), q_spec(LANES), kv_spec(512), kv_spec(512), kv_spec(256)],
            out_specs=pl.BlockSpec((None, tq, 512), lambda b, i: (b, i, 0)),
            scratch_shapes=[pltpu.VMEM((tq, tk), I32), pltpu.VMEM((tq, tk), F32)],
            compiler_params=_cparams(("arbitrary", "arbitrary"), 56 * 1024 * 1024),
            name=f"dsa_prompt_{gi}",
        )(cq.reshape(B, T, 512), iq.reshape(B, T, 256), iw.reshape(B, T, LANES),
          ck.reshape(B, T, 512), cv.reshape(B, T, 512), ikr.reshape(B, T, 256))
        outs.append(out)
    o = outs[0] if ng == 1 else jnp.concatenate(outs, axis=1)
    return o.reshape(B * T, 512)


def _dsa_sample_kernel(q_ref, iq_ref, iw_ref, kn_ref, vn_ref, ikrn_ref, kp_ref, vp_ref, ikp_ref, o_ref,
                       keyp_scr, keyn_scr, biasp_scr, biasn_scr, *, past_len, n_sel):
    T = q_ref.shape[0]
    P = past_len
    iq = iq_ref[...]
    iw = iw_ref[...]
    ikt = ikp_ref[...].astype(BF16)
    ikrp = jnp.concatenate([ikt] * IDX_HEADS, axis=0)
    score_p = _idx_score(iq, iw, ikrp, transposed=True)
    keyp_scr[...] = _order_key(score_p)
    pmax, pmin = _row_extremes(score_p, None)
    qpos = P + lax.broadcasted_iota(I32, (T, T), 0)
    kpos = P + lax.broadcasted_iota(I32, (T, T), 1)
    vis_n = (kpos // CHUNK) <= (qpos // CHUNK)
    score_n = _idx_score(iq, iw, ikrn_ref[...])
    keyn_scr[...] = _order_key(jnp.where(vis_n, score_n, NEG_INF))
    nmax, nmin = _row_extremes(score_n, vis_n)
    _selection_bias([keyp_scr, keyn_scr], [biasp_scr, biasn_scr], [None, lambda off, bw: vis_n[:, off:off + bw]],
                    jnp.maximum(pmax, nmax), jnp.minimum(pmin, nmin),
                    n_sel, P + T, probe_masked=P + min(T, CHUNK) < n_sel)

    def past_head(ref):
        return lambda hs: ref[pl.ds(hs.start // DSA_DH, P, stride=DSA_HEADS), :].astype(BF16)

    _dsa_attend(q_ref, [biasp_scr, biasn_scr],
                [past_head(kp_ref), lambda hs: kn_ref[:, hs]],
                [past_head(vp_ref), lambda hs: vn_ref[:, hs]], o_ref)


def _dsa_sample(cq, iq, iw, ck, cv, ikr, past_k, past_v, past_ikt, layer, B, T):
    M = B * T
    P = past_ikt.shape[-1]
    n_sel = min(DSA_TOPK_MAX, (P + T) // 4)

    def rspec(w):
        return pl.BlockSpec((T, w), lambda b: (b, 0))

    def pspec(r, w):
        return pl.BlockSpec((None, None, r, w), lambda b: (layer, b, 0, 0))

    return pl.pallas_call(
        functools.partial(_dsa_sample_kernel, past_len=P, n_sel=n_sel),
        out_shape=jax.ShapeDtypeStruct((M, 512), BF16),
        grid=(B,),
        in_specs=[rspec(512), rspec(256), rspec(LANES), rspec(512), rspec(512), rspec(256),
                  pspec(P * DSA_HEADS, DSA_DH), pspec(P * DSA_HEADS, DSA_DH), pspec(IDX_DH, P)],
        out_specs=rspec(512),
        scratch_shapes=[pltpu.VMEM((T, P), I32), pltpu.VMEM((T, T), I32),
                        pltpu.VMEM((T, P), F32), pltpu.VMEM((T, T), F32)],
        compiler_params=_cparams(("arbitrary",)),
        name="dsa_sample",
    )(cq, iq, iw, ck, cv, ikr, past_k, past_v, past_ikt)


def _merge_kernel(og_ref, od_ref, oc_ref, g0_ref, g1_ref, g2_ref, x_ref, m_ref,
                  wg_ref, wd_ref, wc_ref, wo_ref, gn_ref, wr_ref, br_ref,
                  x1_ref, h2_ref, ti_ref, tw_ref, rk_ref, cnt_ref, run_scr, *, nb):
    merged = (jax.nn.sigmoid(g0_ref[...]) * _dot(og_ref[...], wg_ref[...])
              + jax.nn.sigmoid(g1_ref[...]) * _dot(od_ref[...], wd_ref[...])
              + jax.nn.sigmoid(g2_ref[...]) * _dot(oc_ref[...], wc_ref[...]))
    mix = _dot(merged.astype(BF16), wo_ref[...])
    m = m_ref[...]
    x1 = x_ref[...] + _gate_rows(mix, m, 2, nb)
    x1_ref[...] = x1
    xn = x1 * lax.rsqrt(jnp.mean(x1 * x1, axis=-1, keepdims=True) + EPS) * gn_ref[...]
    h2 = _modulate(xn, m, 3, 4, nb)
    _rows_to_tiles(h2_ref, h2)
    lg = _dot(h2.astype(BF16), wr_ref[...]) + br_ref[...]
    tm = lg.shape[0]
    lane = lax.broadcasted_iota(I32, (tm, LANES), 1).astype(F32)
    vals, idxs = [], []
    for _ in range(TOP_K):
        mx = jnp.max(lg, axis=1, keepdims=True)
        ix = jnp.min(jnp.where(lg == mx, lane, float(LANES)), axis=1, keepdims=True)
        vals.append(mx)
        idxs.append(ix)
        lg = jnp.where(lane == ix, -jnp.inf, lg)
    es = [jnp.exp(v - vals[0]) for v in vals]
    den = es[0] + es[1] + es[2] + es[3]
    @pl.when(pl.program_id(0) == 0)
    def _():
        run_scr[...] = jnp.zeros(run_scr.shape, F32)

    hot = [lane == ix for ix in idxs]
    tot = (hot[0].astype(F32) + hot[1].astype(F32)) + (hot[2].astype(F32) + hot[3].astype(F32))
    strict = (lax.broadcasted_iota(I32, (tm, tm), 1) < lax.broadcasted_iota(I32, (tm, tm), 0)).astype(BF16)
    before = run_scr[...] + _dot(strict, tot.astype(BF16))
    ti = jnp.zeros((tm, LANES), F32)
    tw = jnp.zeros((tm, LANES), F32)
    rk = jnp.zeros((tm, LANES), F32)
    for r in range(TOP_K):
        ti = jnp.where(lane == float(r), idxs[r], ti)
        tw = jnp.where(lane == float(r), es[r] / den, tw)
        rk = jnp.where(lane == float(r), jnp.sum(jnp.where(hot[r], before, 0.0), axis=1, keepdims=True), rk)
    ti_ref[...] = ti.astype(I32)
    tw_ref[...] = tw
    rk_ref[...] = rk.astype(I32)
    run_new = run_scr[...] + jnp.sum(tot, axis=0, keepdims=True)
    run_scr[...] = run_new
    cnt_ref[...] = run_new.astype(I32)


def _merge(og, od, oc, z, x2, m, wts, B, T):
    M = B * T
    tm, nb = _row_tiling(B, T, 512)
    wg, wd, wc, wo, gn, wr, br = wts

    def rspec(w):
        return pl.BlockSpec((tm, w), lambda i: (i, 0))

    def gspec(k):
        return pl.BlockSpec((tm, D_MODEL), lambda i: (i, Z_GATES // D_MODEL + k))

    def wspec(r, c):
        return pl.BlockSpec((r, c), lambda i: (0, 0))

    return pl.pallas_call(
        functools.partial(_merge_kernel, nb=nb),
        out_shape=[jax.ShapeDtypeStruct((M, D_MODEL), F32), jax.ShapeDtypeStruct((M * ROW_TILE, LANES), F32),
                   jax.ShapeDtypeStruct((M, LANES), I32), jax.ShapeDtypeStruct((M, LANES), F32),
                   jax.ShapeDtypeStruct((M, LANES), I32), jax.ShapeDtypeStruct((1, LANES), I32)],
        grid=(M // tm,),
        in_specs=[rspec(512), rspec(512), rspec(512), gspec(0), gspec(1), gspec(2), rspec(D_MODEL),
                  pl.BlockSpec((nb, 6, D_MODEL), lambda i: ((i * tm) // (T * nb), 0, 0)),
                  wspec(512, D_MODEL), wspec(512, D_MODEL), wspec(512, D_MODEL), wspec(D_MODEL, D_MODEL),
                  wspec(1, D_MODEL), wspec(D_MODEL, LANES), wspec(1, LANES)],
        out_specs=[rspec(D_MODEL), pl.BlockSpec((tm * ROW_TILE, LANES), lambda i: (i, 0)),
                   rspec(LANES), rspec(LANES), rspec(LANES),
                   pl.BlockSpec((1, LANES), lambda i: (0, 0))],
        scratch_shapes=[pltpu.VMEM((1, LANES), F32)],
        compiler_params=_cparams(("arbitrary",)),
        name="merge_router",
    )(og, od, oc, z, z, z, x2, m, wg, wd, wc, wo, gn, wr, br)


DMA_UNROLL = 8


def _w1prep_kernel(w_ref, sel_ref, g_ref, u_ref):
    sel = sel_ref[...]
    for c in range(w_ref.shape[1] // 256):
        r = _dot(w_ref[:, c * 256:(c + 1) * 256].astype(BF16), sel)
        g_ref[:, c * LANES:(c + 1) * LANES] = r[:, :LANES].astype(BF16)
        u_ref[:, c * LANES:(c + 1) * LANES] = r[:, LANES:].astype(BF16)


def _w1prep(w1_all, layer):
    E = w1_all.shape[1]
    tr = 512
    j = jnp.arange(256, dtype=I32)
    src = jnp.where(j < LANES, 2 * j, 2 * (j - LANES) + 1)
    sel = (jnp.arange(256, dtype=I32)[:, None] == src[None, :]).astype(BF16)
    return pl.pallas_call(
        _w1prep_kernel,
        out_shape=[jax.ShapeDtypeStruct((E, D_MODEL, D_FF), BF16)] * 2,
        grid=(E, D_MODEL // tr),
        in_specs=[pl.BlockSpec((None, None, tr, 2 * D_FF), lambda e, r: (layer, e, r, 0)),
                  pl.BlockSpec((256, 256), lambda e, r: (0, 0))],
        out_specs=[pl.BlockSpec((None, tr, D_FF), lambda e, r: (e, r, 0))] * 2,
        compiler_params=_cparams(("arbitrary", "arbitrary")),
        name="w1prep",
    )(w1_all, sel)


def _rows_to_tiles(ref, x):
    n = x.shape[0]
    for c in range(ROW_TILE):
        ref[pl.ds(c, n, stride=ROW_TILE), :] = x[:, c * LANES:(c + 1) * LANES]


def _tiles_to_rows(ref, base, n):
    return jnp.concatenate([ref[pl.ds(base + c, n, stride=ROW_TILE), :] for c in range(ROW_TILE)], axis=1)


def _tile_rows(ref, r):
    return ref.at[pl.ds(pl.multiple_of(r * ROW_TILE, ROW_TILE), ROW_TILE), :]


def _dispatch_kernel(pos_ref, h_ref, xs_in, xs_out, sem, *, tm):
    del xs_in

    tokens = DMA_UNROLL // TOP_K

    def body(j, c):
        for tt in range(tokens):
            src = _tile_rows(h_ref, j * tokens + tt)
            for k in range(TOP_K):
                dst = _tile_rows(xs_out, pos_ref[0, 0, (j * tokens + tt) * TOP_K + k])
                pltpu.make_async_copy(src, dst, sem).start(priority=k % 2)
        return c

    lax.fori_loop(0, tm // tokens, body, 0)
    for _ in range(TOP_K):
        pltpu.make_async_copy(h_ref, xs_out.at[pl.ds(0, tm * ROW_TILE), :], sem).wait()


def _dispatch(h2t, pos, n_rows, tm):
    M = h2t.shape[0] // ROW_TILE
    nt = M // tm
    return pl.pallas_call(
        functools.partial(_dispatch_kernel, tm=tm),
        out_shape=jax.ShapeDtypeStruct((n_rows * ROW_TILE, LANES), F32),
        grid=(nt,),
        in_specs=[pl.BlockSpec((1, 1, TOP_K * tm), lambda i: (i, 0, 0), memory_space=pltpu.SMEM),
                  pl.BlockSpec((tm * ROW_TILE, LANES), lambda i: (i, 0)),
                  pl.BlockSpec(memory_space=pl.ANY)],
        out_specs=pl.BlockSpec(memory_space=pl.ANY),
        scratch_shapes=[pltpu.SemaphoreType.DMA],
        input_output_aliases={2: 0},
        compiler_params=_cparams(("arbitrary",)),
        name="moe_dispatch",
    )(pos.reshape(nt, 1, TOP_K * tm), h2t, jnp.zeros((n_rows * ROW_TILE, LANES), F32))


def _ffn_kernel(be_ref, nu_ref, x_ref, w1g_ref, b1g_ref, w1u_ref, b1u_ref, w2_ref, b2_ref, o_ref, *, bm):
    @pl.when(pl.program_id(0) < nu_ref[0])
    def _():
        x = _tiles_to_rows(x_ref, 0, bm).astype(BF16)
        g = jnp.minimum(_dot(x, w1g_ref[...]) + b1g_ref[...], SWIGLU_LIMIT)
        u = jnp.clip(_dot(x, w1u_ref[...]) + b1u_ref[...], -SWIGLU_LIMIT, SWIGLU_LIMIT)
        a = g * jax.nn.sigmoid(SWIGLU_ALPHA * g) * (u + 1.0)
        _rows_to_tiles(o_ref, _dot(a.astype(BF16), w2_ref[...]) + b2_ref[...])

    @pl.when(pl.program_id(0) >= nu_ref[0])
    def _():
        o_ref[...] = jnp.zeros(o_ref.shape, F32)


def _ffn(xs, block_e, n_used, w1g, b1g, w1u, b1u, w2_all, b2, layer, bm):
    n_rows = xs.shape[0] // ROW_TILE
    nblk = n_rows // bm

    def wspec(r, c):
        return pl.BlockSpec((None, r, c), lambda i, be, nu: (be[i], 0, 0))

    grid_spec = pltpu.PrefetchScalarGridSpec(
        num_scalar_prefetch=2,
        grid=(nblk,),
        in_specs=[pl.BlockSpec((bm * ROW_TILE, LANES), lambda i, be, nu: (jnp.minimum(i, nu[0] - 1), 0)),
                  wspec(D_MODEL, D_FF), wspec(1, D_FF), wspec(D_MODEL, D_FF), wspec(1, D_FF),
                  pl.BlockSpec((None, None, D_FF, D_MODEL), lambda i, be, nu: (layer, be[i], 0, 0)),
                  wspec(1, D_MODEL)],
        out_specs=pl.BlockSpec((bm * ROW_TILE, LANES), lambda i, be, nu: (i, 0)))
    return pl.pallas_call(
        functools.partial(_ffn_kernel, bm=bm),
        out_shape=jax.ShapeDtypeStruct((n_rows * ROW_TILE, LANES), F32),
        grid_spec=grid_spec,
        compiler_params=_cparams(("arbitrary",)),
        name="moe_ffn",
    )(block_e, n_used, xs, w1g, b1g, w1u, b1u, w2_all, b2)


def _combine_kernel(pos_ref, posn_ref, ys_hbm, tw_ref, x_ref, m_ref, o_ref, ybuf, sem, *, tm, nb):
    i = pl.program_id(0)
    nt = pl.num_programs(0)
    rows = TOP_K * tm * ROW_TILE
    slot = lax.rem(i, 2)

    def request(idx_ref, s):
        base = s * (TOP_K * tm)

        def body(j, c):
            for u in range(DMA_UNROLL):
                n = j * DMA_UNROLL + u
                pltpu.make_async_copy(_tile_rows(ys_hbm, idx_ref[0, 0, n]), _tile_rows(ybuf, base + n),
                                      sem.at[s]).start(priority=u % 2)
            return c

        lax.fori_loop(0, TOP_K * tm // DMA_UNROLL, body, 0)

    @pl.when(i == 0)
    def _():
        request(pos_ref, 0)

    @pl.when(i + 1 < nt)
    def _():
        request(posn_ref, 1 - slot)

    off = pl.multiple_of(slot * rows, rows)
    pltpu.make_async_copy(ys_hbm.at[pl.ds(0, rows), :], ybuf.at[pl.ds(off, rows), :], sem.at[slot]).wait()
    tw = tw_ref[...]
    y = None
    for k in range(TOP_K):
        yk = tw[:, k:k + 1] * _tiles_to_rows(ybuf, off + k * tm * ROW_TILE, tm)
        y = yk if y is None else y + yk
    o_ref[...] = x_ref[...] + _gate_rows(y, m_ref[...], 5, nb)


def _combine(pos_t, ys, top_w, x1, m, B, T, tm, nb):
    M = B * T
    nt = M // tm
    pos3 = pos_t.reshape(nt, 1, TOP_K * tm)
    return pl.pallas_call(
        functools.partial(_combine_kernel, tm=tm, nb=nb),
        out_shape=jax.ShapeDtypeStruct((M, D_MODEL), F32),
        grid=(nt,),
        in_specs=[pl.BlockSpec((1, 1, TOP_K * tm), lambda i: (i, 0, 0), memory_space=pltpu.SMEM),
                  pl.BlockSpec((1, 1, TOP_K * tm), lambda i: (jnp.minimum(i + 1, nt - 1), 0, 0),
                               memory_space=pltpu.SMEM),
                  pl.BlockSpec(memory_space=pl.ANY),
                  pl.BlockSpec((tm, LANES), lambda i: (i, 0)),
                  pl.BlockSpec((tm, D_MODEL), lambda i: (i, 0)),
                  pl.BlockSpec((nb, 6, D_MODEL), lambda i: ((i * tm) // (T * nb), 0, 0))],
        out_specs=pl.BlockSpec((tm, D_MODEL), lambda i: (i, 0)),
        scratch_shapes=[pltpu.VMEM((2 * TOP_K * tm * ROW_TILE, LANES), F32), pltpu.SemaphoreType.DMA((2,))],
        compiler_params=_cparams(("arbitrary",)),
        name="moe_combine",
    )(pos3, pos3, ys, top_w, x1, m)


def _route(top_i, rank, counts, bm):
    N = top_i.shape[0]
    NK = N * TOP_K
    padded = (counts + bm - 1) // bm * bm
    pad_end = jnp.cumsum(padded)
    pad_start = pad_end - padded
    onehot = top_i[:, :, None] == jnp.arange(N_EXPERTS, dtype=I32)[None, None, :]
    pos = rank + jnp.sum(jnp.where(onehot, pad_start[None, None, :], 0), axis=2)
    n_rows = (-(-NK // bm)) * bm + N_EXPERTS * bm
    nblk = n_rows // bm
    starts = jnp.arange(nblk, dtype=I32) * bm
    block_e = jnp.minimum(jnp.sum((pad_end[None, :] <= starts[:, None]).astype(I32), axis=1), N_EXPERTS - 1)
    n_used = (pad_end[-1:] // bm).astype(I32)
    return pos.reshape(NK).astype(I32), n_rows, block_e.astype(I32), n_used


def _moe(h2, top_i, top_w, rank, counts, x1, m, ew, B, T):
    M = B * T
    bm = 256 if M * TOP_K >= 256 * N_EXPERTS * 4 else 128
    pos, n_rows, block_e, n_used = _route(top_i[:, :TOP_K], rank[:, :TOP_K], counts[0, :N_EXPERTS], bm)
    tm, nb = _row_tiling(B, T, 256)
    xs = _dispatch(h2, pos, n_rows, tm)
    ys = _ffn(xs, block_e, n_used, *ew, bm)
    pos_t = pos.reshape(M // tm, tm, TOP_K).transpose(0, 2, 1)
    return _combine(pos_t, ys, top_w, x1, m, B, T, tm, nb)


def _prep_layer(P, l, w2_all):
    w_t = P['w_in'][l].T

    def rows(name):
        o, s = _SRC[name]
        return w_t[o:o + s]

    zeros = lambda n: jnp.zeros((n, D_MODEL), F32)
    w_r = jnp.concatenate([
        rows('gla_q'), rows('gla_k'), rows('gla_v'), rows('gla_r'),
        rows('diff_q'), rows('diff_k'), rows('diff_v'),
        rows('dsa_q'), rows('dsa_k'), rows('dsa_v'),
        rows('idx_q'), rows('idx_k'), rows('idx_w'), zeros(60), rows('gla_a'), zeros(112),
        rows('gates')], axis=0).astype(BF16)
    assert w_r.shape[0] == Z_WIDTH
    W = GLA_HEADS * GLA_DK
    wa = jnp.zeros((LANES, W), F32).at[:GLA_GATE_RANK].set(P['w_gla_a2'][l]).astype(BF16)
    w1g, w1u = _w1prep(P['w_mlp1'], l)
    b1 = P['b_mlp1'][l]
    return dict(
        w_in=w_r, wa=wa, ba=P['b_gla_a2'][l].reshape(1, W),
        g_gla=jnp.tile(P['g_gla_out'][l], GLA_HEADS).reshape(1, 512),
        gains=(jnp.tile(P['g_diff_q'][l], 8).reshape(1, 512), jnp.tile(P['g_diff_k'][l], 8).reshape(1, 512),
               jnp.tile(P['g_dsa_q'][l], 4).reshape(1, 512), jnp.tile(P['g_dsa_k'][l], 4).reshape(1, 512),
               jnp.tile(P['g_idx_k'][l], 2).reshape(1, LANES)),
        lamv=jnp.stack([P['lambda_q1'][l], P['lambda_k1'][l], P['lambda_q2'][l], P['lambda_k2'][l]]),
        g_diff=P['g_diff_out'][l].reshape(1, LANES),
        merge=(P['w_branch_gla'][l].astype(BF16), P['w_branch_diff'][l].astype(BF16),
               P['w_branch_dsa'][l].astype(BF16), P['w_out'][l].astype(BF16),
               P['g_norm2'][l].reshape(1, D_MODEL),
               jnp.zeros((D_MODEL, LANES), F32).at[:, :N_EXPERTS].set(P['w_router'][l]).astype(BF16),
               jnp.full((1, LANES), NEG_INF, F32).at[0, :N_EXPERTS].set(P['b_router'][l])),
        experts=(w1g, b1[:, None, 0::2], w1u, b1[:, None, 1::2], w2_all, P['b_mlp2'][l][:, None, :], l),
    )


def _trunk(x, c, past, P, prepped):
    B, T, _ = x.shape
    M = B * T
    x2 = x.reshape(M, D_MODEL)
    past_len = 0 if past is None else past[0].shape[2]
    pos = past_len + jnp.arange(T, dtype=I32)
    tabs64 = _rope_tables(pos, 64)
    tabs128 = _rope_tables(pos, 128)
    W = GLA_HEADS * GLA_DK
    if past is not None:
        L, _, PL = past[0].shape[:3]
        pdk = past[0].transpose(0, 1, 3, 4, 5, 2)
        pdv = past[1].reshape(L, B, PL * DIFF_HEADS, DIFF_DV)
        pck = past[2].reshape(L, B, PL * DSA_HEADS, DSA_DH)
        pcv = past[3].reshape(L, B, PL * DSA_HEADS, DSA_DH)
        pik = past[4].transpose(0, 1, 3, 2)
        s0_all = past[5].transpose(0, 1, 4, 2, 3).reshape(past[5].shape[0], B, GLA_DV, W)
    per_layer = []
    for l, pp in enumerate(prepped):
        lam_init = 0.8 - 0.6 * math.exp(-0.3 * l)
        m = _ada(c, P['w_ada'], P['b_ada'], l)
        z = _inproj(x2, m, P['g_norm1'][l], pp['w_in'], B, T)
        (dq, dk32, dkb, dvb, cq, ck32, ckb, cvb, iq, ik32, ikr, iw, dv32, cv32) = _post(
            z, tabs64, tabs128, pp['gains'], B, T)
        s0t = jnp.zeros((B, GLA_DV, W), F32) if past is None else s0_all[l]
        o_gla, st = _gla(z, pp['wa'], pp['ba'], pp['g_gla'], s0t, B, T)
        if past is None:
            o_diff = _diff_prompt(dq, dkb, dvb, pp['lamv'], pp['g_diff'], B, T, lam_init)
            o_dsa = _dsa_prompt(cq, iq, iw, ckb, cvb, ikr, B, T)
        else:
            o_diff = _diff_sample(dq, dkb, dvb, pdk, pdv, l, pp['lamv'], pp['g_diff'], B, T, lam_init)
            o_dsa = _dsa_sample(cq, iq, iw, ckb, cvb, ikr, pck, pcv, pik, l, B, T)
        x1, h2, top_i, top_w, rank, counts = _merge(o_gla, o_diff, o_dsa, z, x2, m, pp['merge'], B, T)
        x2 = _moe(h2, top_i, top_w, rank, counts, x1, m, pp['experts'], B, T)
        s_gla = st.reshape(B, GLA_DV, GLA_HEADS, GLA_DK).transpose(0, 2, 3, 1)
        per_layer.append((dk32.reshape(B, T, DIFF_HEADS, 2, DIFF_DH),
                          dv32.reshape(B, T, DIFF_HEADS, DIFF_DV),
                          ck32.reshape(B, T, DSA_HEADS, DSA_DH),
                          cv32.reshape(B, T, DSA_HEADS, DSA_DH),
                          ik32.reshape(B, T, IDX_DH),
                          s_gla))
    stacked = tuple(jnp.stack([st[i] for st in per_layer]) for i in range(6))
    return x2.reshape(B, T, D_MODEL), stacked


def kernel(x_prompt, x_sample, cache_diff_k, cache_diff_v, cache_dsa_k, cache_dsa_v, cache_dsa_idx_k,
           state_gla, c_prompt, c_sample, w_ada, b_ada, g_norm1, g_norm2, w_in, w_gla_a2, b_gla_a2,
           g_gla_out, g_diff_q, g_diff_k, lambda_q1, lambda_k1, lambda_q2, lambda_k2, g_diff_out,
           g_dsa_q, g_dsa_k, g_idx_k, w_branch_gla, w_branch_diff, w_branch_dsa, w_out, w_router,
           b_router, w_mlp1, b_mlp1, w_mlp2, b_mlp2):
    P = dict(w_ada=w_ada, b_ada=b_ada, g_norm1=g_norm1, g_norm2=g_norm2, w_in=w_in, w_gla_a2=w_gla_a2,
             b_gla_a2=b_gla_a2, g_gla_out=g_gla_out, g_diff_q=g_diff_q, g_diff_k=g_diff_k,
             lambda_q1=lambda_q1, lambda_k1=lambda_k1, lambda_q2=lambda_q2, lambda_k2=lambda_k2,
             g_diff_out=g_diff_out, g_dsa_q=g_dsa_q, g_dsa_k=g_dsa_k, g_idx_k=g_idx_k,
             w_branch_gla=w_branch_gla, w_branch_diff=w_branch_diff, w_branch_dsa=w_branch_dsa,
             w_out=w_out, w_router=w_router, b_router=b_router, w_mlp1=w_mlp1, b_mlp1=b_mlp1,
             w_mlp2=w_mlp2, b_mlp2=b_mlp2)
    depth = w_in.shape[0]
    w2_all = w_mlp2.astype(BF16)
    prepped = [_prep_layer(P, l, w2_all) for l in range(depth)]
    y_prompt, new_p = _trunk(x_prompt, c_prompt, None, P, prepped)
    y_sample, new_s = _trunk(
        x_sample, c_sample,
        (cache_diff_k, cache_diff_v, cache_dsa_k, cache_dsa_v, cache_dsa_idx_k, state_gla), P, prepped)
    return (y_prompt, y_sample) + new_p + new_s
```
